```python
import jax, jax.numpy as jnp
from jax import lax
import numpy as np

D_MODEL = 1024
BATCH = 8
SEQ = 16384
DEPTH = 2

CHUNK = 64
PLE_DIM = 256
BRANCH_WIDTH = D_MODEL // 2
N_BRANCH = 4
SG_BLOCK = 128
SG_GROUPS = 4
SG_WIDTH = BRANCH_WIDTH
GLA_HEADS = 4
GLA_DK = 64
GLA_DV = BRANCH_WIDTH // GLA_HEADS
GLA_RANK = 16
GLA_TAU = 16.0
ATT_HEADS = 8
ATT_HD = BRANCH_WIDTH // ATT_HEADS
ATT_BAND = 9
MAX_REL = 256
REL_TABLE = CHUNK + MAX_REL
CONV_WIDTH = BRANCH_WIDTH
CONV_K = 31
D_FF = 4 * D_MODEL
EPS = 1e-6
NEG_INF = -1e30

IN_SPLITS = (SG_WIDTH, SG_WIDTH,
             GLA_HEADS * GLA_DK, GLA_HEADS * GLA_DK, GLA_HEADS * GLA_DV, GLA_HEADS * GLA_DV, GLA_RANK,
             ATT_HEADS * ATT_HD, ATT_HEADS * ATT_HD, ATT_HEADS * ATT_HD,
             CONV_WIDTH, CONV_WIDTH)
IN_COLS = 2 * SG_WIDTH + 2 * GLA_HEADS * GLA_DK + 2 * GLA_HEADS * GLA_DV + GLA_RANK + 3 * ATT_HEADS * ATT_HD + 2 * CONV_WIDTH

kernel_name = "hybrid_gated_branch_streaming_encoder"


def rms_norm(x, g):
    xf = x.astype(jnp.float32)
    y = xf * lax.rsqrt(jnp.mean(xf * xf, axis=-1, keepdims=True) + EPS)
    return (y * g.astype(jnp.float32)).astype(x.dtype)


def layer_norm(x, g, b):
    xf = x.astype(jnp.float32)
    mu = jnp.mean(xf, axis=-1, keepdims=True)
    xc = xf - mu
    y = xc * lax.rsqrt(jnp.mean(xc * xc, axis=-1, keepdims=True) + EPS)
    return (y * g.astype(jnp.float32) + b.astype(jnp.float32)).astype(x.dtype)


def spatial_gating(u, v, ln_g, ln_b, w_s, b_s):
    bsz, s, _ = u.shape
    nb = s // SG_BLOCK
    cg = SG_WIDTH // SG_GROUPS
    v = layer_norm(v, ln_g, ln_b)
    vb = v.reshape(bsz, nb, SG_BLOCK, SG_GROUPS, cg)
    pos = jnp.arange(SG_BLOCK)
    mask = (pos[None, :] // CHUNK) <= (pos[:, None] // CHUNK)
    w = jnp.where(mask[None], w_s, 0.0)
    mixed = jnp.einsum('gij,bnjgc->bnigc', w, vb) + b_s.T[None, None, :, :, None]
    return u * mixed.reshape(bsz, s, SG_WIDTH)


def gated_linear_attention(q, k, v, r, a_lr, w_a2, b_a, norm_g):
    f32 = jnp.float32
    bsz, s, _ = q.shape
    nc = s // CHUNK
    qc = q.astype(f32).reshape(bsz, nc, CHUNK, GLA_HEADS, GLA_DK) * (GLA_DK ** -0.5)
    kc = k.astype(f32).reshape(bsz, nc, CHUNK, GLA_HEADS, GLA_DK)
    vc = v.astype(f32).reshape(bsz, nc, CHUNK, GLA_HEADS, GLA_DV)
    log_a = jax.nn.log_sigmoid(jnp.einsum('bsr,rk->bsk', a_lr.astype(f32), w_a2.astype(f32))
                               + b_a.astype(f32)) / GLA_TAU
    log_a = log_a.reshape(bsz, nc, CHUNK, GLA_HEADS, GLA_DK)
    cum = jnp.cumsum(log_a, axis=2)
    total = cum[:, :, -1]
    k_dec = kc * jnp.exp(total[:, :, None] - cum)
    upd = jnp.einsum('bclhd,bclhe->cbhde', k_dec, vc)
    decay = jnp.exp(total).transpose(1, 0, 2, 3)

    def step(state, inp):
        d, u_c = inp
        state = d[..., None] * state + u_c
        return state, state

    s0 = jnp.zeros((bsz, GLA_HEADS, GLA_DK, GLA_DV), f32)
    _, states = lax.scan(step, s0, (decay, upd))
    o = jnp.einsum('bclhd,cbhde->bclhe', qc, states)
    o = o * lax.rsqrt(jnp.mean(o * o, axis=-1, keepdims=True) + EPS)
    o = o.reshape(bsz, s, GLA_HEADS * GLA_DV) * norm_g.astype(f32)
    return (o * jax.nn.silu(r.astype(f32))).astype(r.dtype)


def band_chunk_attention(q, k, v, rel_bias):
    f32 = jnp.float32
    bsz, s, _ = q.shape
    nc = s // CHUNK
    prev = ATT_BAND - 1
    band = ATT_BAND * CHUNK
    qh = q.reshape(bsz, s, ATT_HEADS, ATT_HD)
    pad = ((0, 0), (prev * CHUNK, 0), (0, 0), (0, 0))
    kp = jnp.pad(k.reshape(bsz, s, ATT_HEADS, ATT_HD), pad)
    vp = jnp.pad(v.reshape(bsz, s, ATT_HEADS, ATT_HD), pad)
    l_idx = jnp.arange(CHUNK)
    m_idx = jnp.arange(band)
    rel = l_idx[:, None] + prev * CHUNK - m_idx[None, :]
    idx = jnp.clip(rel, -(CHUNK - 1), MAX_REL) + (CHUNK - 1)
    bias = rel_bias.astype(f32)[:, idx]
    scale = ATT_HD ** -0.5

    def one_chunk(c):
        qc = lax.dynamic_slice_in_dim(qh, c * CHUNK, CHUNK, axis=1)
        kc = lax.dynamic_slice_in_dim(kp, c * CHUNK, band, axis=1)
        vc = lax.dynamic_slice_in_dim(vp, c * CHUNK, band, axis=1)
        sc = jnp.einsum('blhd,bmhd->bhlm', qc, kc, preferred_element_type=f32) * scale + bias[None]
        key_ok = m_idx >= (prev - c) * CHUNK
        sc = jnp.where(key_ok[None, None, None, :], sc, NEG_INF)
        pw = jax.nn.softmax(sc, axis=-1)
        return jnp.einsum('bhlm,bmhd->blhd', pw.astype(vc.dtype), vc)

    out = lax.map(one_chunk, jnp.arange(nc))
    return out.transpose(1, 0, 2, 3, 4).reshape(bsz, s, ATT_HEADS * ATT_HD)


def conformer_conv(a, g, dw_w, dw_b, ln_g, ln_b):
    y = a * jax.nn.sigmoid(g)
    y = lax.conv_general_dilated(y, dw_w[:, None, :], window_strides=(1,),
                                 padding=((CONV_K - 1, 0),),
                                 dimension_numbers=('NWC', 'WIO', 'NWC'),
                                 feature_group_count=CONV_WIDTH) + dw_b
    return jax.nn.silu(layer_norm(y, ln_g, ln_b))


def _fwd_setup_inputs(seed: int = 0) -> dict:
    key = jax.random.key(seed)
    ks = iter(jax.random.split(key, 40))

    def nrm(shape, scale):
        return scale * jax.random.normal(next(ks), shape, jnp.float32)

    def gain(shape):
        return 1.0 + nrm(shape, 0.05)

    L = DEPTH
    return {
        "x": nrm((BATCH, SEQ, D_MODEL), 1.0),
        "p": nrm((DEPTH, BATCH, SEQ, PLE_DIM), 1.0),
        "norm1_g": gain((L, D_MODEL)),
        "w_in": nrm((L, D_MODEL, IN_COLS), D_MODEL ** -0.5),
        "sg_ln_g": gain((L, SG_WIDTH)),
        "sg_ln_b": nrm((L, SG_WIDTH), 0.02),
        "sg_w": nrm((L, SG_GROUPS, SG_BLOCK, SG_BLOCK), SG_BLOCK ** -0.5),
        "sg_b": 1.0 + nrm((L, SG_GROUPS, SG_BLOCK), 0.1),
        "gla_w_a2": nrm((L, GLA_RANK, GLA_HEADS * GLA_DK), GLA_RANK ** -0.5),
        "gla_b_a": nrm((L, GLA_HEADS * GLA_DK), 0.1),
        "gla_norm_g": gain((L, GLA_HEADS * GLA_DV)),
        "att_rel_bias": nrm((L, ATT_HEADS, REL_TABLE), 0.5),
        "conv_dw_w": nrm((L, CONV_K, CONV_WIDTH), CONV_K ** -0.5),
        "conv_dw_b": nrm((L, CONV_WIDTH), 0.02),
        "conv_ln_g": gain((L, CONV_WIDTH)),
        "conv_ln_b": nrm((L, CONV_WIDTH), 0.02),
        "w_branch": nrm((L, N_BRANCH, BRANCH_WIDTH, D_MODEL), BRANCH_WIDTH ** -0.5),
        "w_gate": nrm((L, N_BRANCH, D_MODEL, D_MODEL), D_MODEL ** -0.5),
        "b_gate": nrm((L, N_BRANCH, D_MODEL), 0.02),
        "w_out": nrm((L, D_MODEL, D_MODEL), D_MODEL ** -0.5),
        "norm2_g": gain((L, D_MODEL)),
        "w_ff1": nrm((L, D_MODEL, D_FF), D_MODEL ** -0.5),
        "w_ff2": nrm((L, D_FF, D_MODEL), D_FF ** -0.5),
        "norm3_g": gain((L, D_MODEL)),
        "w_ple_gate": nrm((L, D_MODEL, D_MODEL), D_MODEL ** -0.5),
        "b_ple_gate": nrm((L, D_MODEL), 0.02),
        "w_ple": nrm((L, PLE_DIM, D_MODEL), PLE_DIM ** -0.5),
        "final_g": gain((D_MODEL,)),
    }


def _fwd_reference(x, p, norm1_g, w_in, sg_ln_g, sg_ln_b, sg_w, sg_b, gla_w_a2, gla_b_a, gla_norm_g,
              att_rel_bias, conv_dw_w, conv_dw_b, conv_ln_g, conv_ln_b, w_branch, w_gate, b_gate,
              w_out, norm2_g, w_ff1, w_ff2, norm3_g, w_ple_gate, b_ple_gate, w_ple, final_g):
    cuts = np.cumsum(IN_SPLITS)[:-1].tolist()
    h = x
    for i in range(DEPTH):
        xn = rms_norm(h, norm1_g[i])
        proj = jnp.einsum('bsd,dk->bsk', xn, w_in[i])
        (sg_u, sg_v, g_q, g_k, g_v, g_r, g_a, a_q, a_k, a_v, c_a, c_g) = jnp.split(proj, cuts, axis=-1)

        y_a = spatial_gating(jax.nn.gelu(sg_u), jax.nn.gelu(sg_v), sg_ln_g[i], sg_ln_b[i], sg_w[i], sg_b[i])
        y_b = gated_linear_attention(g_q, g_k, g_v, g_r, g_a, gla_w_a2[i], gla_b_a[i], gla_norm_g[i])
        y_c = band_chunk_attention(a_q, a_k, a_v, att_rel_bias[i])
        y_d = conformer_conv(c_a, c_g, conv_dw_w[i], conv_dw_b[i], conv_ln_g[i], conv_ln_b[i])

        merged = jnp.zeros_like(h)
        for n, y in enumerate((y_a, y_b, y_c, y_d)):
            gate = jax.nn.sigmoid(jnp.einsum('bsd,de->bse', xn, w_gate[i, n]) + b_gate[i, n])
            merged = merged + gate * jnp.einsum('bsk,kd->bsd', y, w_branch[i, n])
        h = h + jnp.einsum('bsd,de->bse', merged, w_out[i])

        hn = rms_norm(h, norm2_g[i])
        ff = jnp.square(jax.nn.relu(jnp.einsum('bsd,df->bsf', hn, w_ff1[i])))
        h = h + jnp.einsum('bsf,fd->bsd', ff, w_ff2[i])

        hg = rms_norm(h, norm3_g[i])
        ple_gate = jax.nn.sigmoid(jnp.einsum('bsd,de->bse', hg, w_ple_gate[i]) + b_ple_gate[i])
        h = h + ple_gate * jnp.einsum('bsq,qd->bsd', p[i], w_ple[i])
    return rms_norm(h, final_g)


import jax as _jax
import jax.numpy as _jnp

TWIN_FORMAT = 'train_step'
FWD_PARAMS = ['x', 'p', 'norm1_g', 'w_in', 'sg_ln_g', 'sg_ln_b', 'sg_w', 'sg_b', 'gla_w_a2', 'gla_b_a', 'gla_norm_g', 'att_rel_bias', 'conv_dw_w', 'conv_dw_b', 'conv_ln_g', 'conv_ln_b', 'w_branch', 'w_gate', 'b_gate', 'w_out', 'norm2_g', 'w_ff1', 'w_ff2', 'norm3_g', 'w_ple_gate', 'b_ple_gate', 'w_ple', 'final_g']
TWIN_WEIGHTS = ['norm1_g', 'w_in', 'sg_ln_g', 'sg_ln_b', 'sg_w', 'sg_b', 'gla_w_a2', 'gla_b_a', 'gla_norm_g', 'att_rel_bias', 'conv_dw_w', 'conv_dw_b', 'conv_ln_g', 'conv_ln_b', 'w_branch', 'w_gate', 'b_gate', 'w_out', 'norm2_g', 'w_ff1', 'w_ff2', 'norm3_g', 'w_ple_gate', 'b_ple_gate', 'w_ple', 'final_g']
TWIN_DIFF_INPUT = 'x'
TWIN_INPUTS = ['x', 'p', 'norm1_g', 'w_in', 'sg_ln_g', 'sg_ln_b', 'sg_w', 'sg_b', 'gla_w_a2', 'gla_b_a', 'gla_norm_g', 'att_rel_bias', 'conv_dw_w', 'conv_dw_b', 'conv_ln_g', 'conv_ln_b', 'w_branch', 'w_gate', 'b_gate', 'w_out', 'norm2_g', 'w_ff1', 'w_ff2', 'norm3_g', 'w_ple_gate', 'b_ple_gate', 'w_ple', 'final_g', 'loss_target', 'm_norm1_g', 'm_w_in', 'm_sg_ln_g', 'm_sg_ln_b', 'm_sg_w', 'm_sg_b', 'm_gla_w_a2', 'm_gla_b_a', 'm_gla_norm_g', 'm_att_rel_bias', 'm_conv_dw_w', 'm_conv_dw_b', 'm_conv_ln_g', 'm_conv_ln_b', 'm_w_branch', 'm_w_gate', 'm_b_gate', 'm_w_out', 'm_norm2_g', 'm_w_ff1', 'm_w_ff2', 'm_norm3_g', 'm_w_ple_gate', 'm_b_ple_gate', 'm_w_ple', 'm_final_g', 'v_norm1_g', 'v_w_in', 'v_sg_ln_g', 'v_sg_ln_b', 'v_sg_w', 'v_sg_b', 'v_gla_w_a2', 'v_gla_b_a', 'v_gla_norm_g', 'v_att_rel_bias', 'v_conv_dw_w', 'v_conv_dw_b', 'v_conv_ln_g', 'v_conv_ln_b', 'v_w_branch', 'v_w_gate', 'v_b_gate', 'v_w_out', 'v_norm2_g', 'v_w_ff1', 'v_w_ff2', 'v_norm3_g', 'v_w_ple_gate', 'v_b_ple_gate', 'v_w_ple', 'v_final_g']
TWIN_OUTPUTS = ['loss', 'grad_x', 'grad_norm1_g', 'grad_w_in', 'grad_sg_ln_g', 'grad_sg_ln_b', 'grad_sg_w', 'grad_sg_b', 'grad_gla_w_a2', 'grad_gla_b_a', 'grad_gla_norm_g', 'grad_att_rel_bias', 'grad_conv_dw_w', 'grad_conv_dw_b', 'grad_conv_ln_g', 'grad_conv_ln_b', 'grad_w_branch', 'grad_w_gate', 'grad_b_gate', 'grad_w_out', 'grad_norm2_g', 'grad_w_ff1', 'grad_w_ff2', 'grad_norm3_g', 'grad_w_ple_gate', 'grad_b_ple_gate', 'grad_w_ple', 'grad_final_g', 'delta_norm1_g', 'delta_w_in', 'delta_sg_ln_g', 'delta_sg_ln_b', 'delta_sg_w', 'delta_sg_b', 'delta_gla_w_a2', 'delta_gla_b_a', 'delta_gla_norm_g', 'delta_att_rel_bias', 'delta_conv_dw_w', 'delta_conv_dw_b', 'delta_conv_ln_g', 'delta_conv_ln_b', 'delta_w_branch', 'delta_w_gate', 'delta_b_gate', 'delta_w_out', 'delta_norm2_g', 'delta_w_ff1', 'delta_w_ff2', 'delta_norm3_g', 'delta_w_ple_gate', 'delta_b_ple_gate', 'delta_w_ple', 'delta_final_g', 'new_m_norm1_g', 'new_m_w_in', 'new_m_sg_ln_g', 'new_m_sg_ln_b', 'new_m_sg_w', 'new_m_sg_b', 'new_m_gla_w_a2', 'new_m_gla_b_a', 'new_m_gla_norm_g', 'new_m_att_rel_bias', 'new_m_conv_dw_w', 'new_m_conv_dw_b', 'new_m_conv_ln_g', 'new_m_conv_ln_b', 'new_m_w_branch', 'new_m_w_gate', 'new_m_b_gate', 'new_m_w_out', 'new_m_norm2_g', 'new_m_w_ff1', 'new_m_w_ff2', 'new_m_norm3_g', 'new_m_w_ple_gate', 'new_m_b_ple_gate', 'new_m_w_ple', 'new_m_final_g', 'new_v_norm1_g', 'new_v_w_in', 'new_v_sg_ln_g', 'new_v_sg_ln_b', 'new_v_sg_w', 'new_v_sg_b', 'new_v_gla_w_a2', 'new_v_gla_b_a', 'new_v_gla_norm_g', 'new_v_att_rel_bias', 'new_v_conv_dw_w', 'new_v_conv_dw_b', 'new_v_conv_ln_g', 'new_v_conv_ln_b', 'new_v_w_branch', 'new_v_w_gate', 'new_v_b_gate', 'new_v_w_out', 'new_v_norm2_g', 'new_v_w_ff1', 'new_v_w_ff2', 'new_v_norm3_g', 'new_v_w_ple_gate', 'new_v_b_ple_gate', 'new_v_w_ple', 'new_v_final_g']
TWIN_LEAF_KINDS = {'loss': 'loss', 'grad_x': 'grad_x', 'grad_norm1_g': 'grad_w', 'grad_w_in': 'grad_w', 'grad_sg_ln_g': 'grad_w', 'grad_sg_ln_b': 'grad_w', 'grad_sg_w': 'grad_w', 'grad_sg_b': 'grad_w', 'grad_gla_w_a2': 'grad_w', 'grad_gla_b_a': 'grad_w', 'grad_gla_norm_g': 'grad_w', 'grad_att_rel_bias': 'grad_w', 'grad_conv_dw_w': 'grad_w', 'grad_conv_dw_b': 'grad_w', 'grad_conv_ln_g': 'grad_w', 'grad_conv_ln_b': 'grad_w', 'grad_w_branch': 'grad_w', 'grad_w_gate': 'grad_w', 'grad_b_gate': 'grad_w', 'grad_w_out': 'grad_w', 'grad_norm2_g': 'grad_w', 'grad_w_ff1': 'grad_w', 'grad_w_ff2': 'grad_w', 'grad_norm3_g': 'grad_w', 'grad_w_ple_gate': 'grad_w', 'grad_b_ple_gate': 'grad_w', 'grad_w_ple': 'grad_w', 'grad_final_g': 'grad_w', 'delta_norm1_g': 'delta_w', 'delta_w_in': 'delta_w', 'delta_sg_ln_g': 'delta_w', 'delta_sg_ln_b': 'delta_w', 'delta_sg_w': 'delta_w', 'delta_sg_b': 'delta_w', 'delta_gla_w_a2': 'delta_w', 'delta_gla_b_a': 'delta_w', 'delta_gla_norm_g': 'delta_w', 'delta_att_rel_bias': 'delta_w', 'delta_conv_dw_w': 'delta_w', 'delta_conv_dw_b': 'delta_w', 'delta_conv_ln_g': 'delta_w', 'delta_conv_ln_b': 'delta_w', 'delta_w_branch': 'delta_w', 'delta_w_gate': 'delta_w', 'delta_b_gate': 'delta_w', 'delta_w_out': 'delta_w', 'delta_norm2_g': 'delta_w', 'delta_w_ff1': 'delta_w', 'delta_w_ff2': 'delta_w', 'delta_norm3_g': 'delta_w', 'delta_w_ple_gate': 'delta_w', 'delta_b_ple_gate': 'delta_w', 'delta_w_ple': 'delta_w', 'delta_final_g': 'delta_w', 'new_m_norm1_g': 'new_m', 'new_m_w_in': 'new_m', 'new_m_sg_ln_g': 'new_m', 'new_m_sg_ln_b': 'new_m', 'new_m_sg_w': 'new_m', 'new_m_sg_b': 'new_m', 'new_m_gla_w_a2': 'new_m', 'new_m_gla_b_a': 'new_m', 'new_m_gla_norm_g': 'new_m', 'new_m_att_rel_bias': 'new_m', 'new_m_conv_dw_w': 'new_m', 'new_m_conv_dw_b': 'new_m', 'new_m_conv_ln_g': 'new_m', 'new_m_conv_ln_b': 'new_m', 'new_m_w_branch': 'new_m', 'new_m_w_gate': 'new_m', 'new_m_b_gate': 'new_m', 'new_m_w_out': 'new_m', 'new_m_norm2_g': 'new_m', 'new_m_w_ff1': 'new_m', 'new_m_w_ff2': 'new_m', 'new_m_norm3_g': 'new_m', 'new_m_w_ple_gate': 'new_m', 'new_m_b_ple_gate': 'new_m', 'new_m_w_ple': 'new_m', 'new_m_final_g': 'new_m', 'new_v_norm1_g': 'new_v', 'new_v_w_in': 'new_v', 'new_v_sg_ln_g': 'new_v', 'new_v_sg_ln_b': 'new_v', 'new_v_sg_w': 'new_v', 'new_v_sg_b': 'new_v', 'new_v_gla_w_a2': 'new_v', 'new_v_gla_b_a': 'new_v', 'new_v_gla_norm_g': 'new_v', 'new_v_att_rel_bias': 'new_v', 'new_v_conv_dw_w': 'new_v', 'new_v_conv_dw_b': 'new_v', 'new_v_conv_ln_g': 'new_v', 'new_v_conv_ln_b': 'new_v', 'new_v_w_branch': 'new_v', 'new_v_w_gate': 'new_v', 'new_v_b_gate': 'new_v', 'new_v_w_out': 'new_v', 'new_v_norm2_g': 'new_v', 'new_v_w_ff1': 'new_v', 'new_v_w_ff2': 'new_v', 'new_v_norm3_g': 'new_v', 'new_v_w_ple_gate': 'new_v', 'new_v_b_ple_gate': 'new_v', 'new_v_w_ple': 'new_v', 'new_v_final_g': 'new_v'}


def _forward(args):
    return _fwd_reference(*[args[k] for k in FWD_PARAMS])


def _output_shape():
    def fwd():
        inp = _fwd_setup_inputs(0)
        return _fwd_reference(*[inp[k] for k in FWD_PARAMS])
    out = _jax.eval_shape(fwd)
    return out.shape, out.dtype

N_MICROBATCH = 1
ADAM_LR = 0.001
ADAM_B1 = 0.9
ADAM_B2 = 0.999
ADAM_EPS = 1e-08
ADAM_WD = 0.01
ADAM_STEP = 10
PER_EXAMPLE_BATCH_AXIS = {'x': 0, 'p': 1, 'loss_target': 0}
SHARED_INPUTS = []
_WEIGHT_DTYPES = {'norm1_g': _jnp.float32, 'w_in': _jnp.float32, 'sg_ln_g': _jnp.float32, 'sg_ln_b': _jnp.float32, 'sg_w': _jnp.float32, 'sg_b': _jnp.float32, 'gla_w_a2': _jnp.float32, 'gla_b_a': _jnp.float32, 'gla_norm_g': _jnp.float32, 'att_rel_bias': _jnp.float32, 'conv_dw_w': _jnp.float32, 'conv_dw_b': _jnp.float32, 'conv_ln_g': _jnp.float32, 'conv_ln_b': _jnp.float32, 'w_branch': _jnp.float32, 'w_gate': _jnp.float32, 'b_gate': _jnp.float32, 'w_out': _jnp.float32, 'norm2_g': _jnp.float32, 'w_ff1': _jnp.float32, 'w_ff2': _jnp.float32, 'norm3_g': _jnp.float32, 'w_ple_gate': _jnp.float32, 'b_ple_gate': _jnp.float32, 'w_ple': _jnp.float32, 'final_g': _jnp.float32}
MOMENT_SCALE = {'norm1_g': 2.914712e-01, 'w_in': 1.295513e-01, 'sg_ln_g': 1.157495e-01, 'sg_ln_b': 1.213700e-01, 'sg_w': 1.194303e-01, 'sg_b': 1.459244e-01, 'gla_w_a2': 2.467683e-02, 'gla_b_a': 1.026394e-01, 'gla_norm_g': 1.318099e-01, 'att_rel_bias': 1.427510e-02, 'conv_dw_w': 1.495619e-01, 'conv_dw_b': 9.111868e-01, 'conv_ln_g': 3.614956e-01, 'conv_ln_b': 5.716699e-01, 'w_branch': 1.434885e-01, 'w_gate': 3.622269e-02, 'b_gate': 6.089089e-02, 'w_out': 2.821700e-01, 'norm2_g': 3.247473e-01, 'w_ff1': 1.606541e-01, 'w_ff2': 6.274760e-01, 'norm3_g': 4.866703e-02, 'w_ple_gate': 5.194006e-02, 'b_ple_gate': 1.443333e-01, 'w_ple': 1.029566e-01, 'final_g': 1.298010e+02}


def _to_microbatches(a, axis):
    t = _jnp.moveaxis(a, axis, 0)
    t = t.reshape((N_MICROBATCH, t.shape[0] // N_MICROBATCH) + t.shape[1:])
    return _jnp.moveaxis(t, 1, axis + 1)


def setup_inputs(seed: int = 0) -> dict:
    inp = _fwd_setup_inputs(seed)
    key = _jax.random.fold_in(_jax.random.key(seed), 7919)
    shape, _ = _output_shape()
    out = dict(inp)
    out["loss_target"] = _jax.random.normal(_jax.random.fold_in(key, 0), shape, _jnp.float32)
    for i, name in enumerate(TWIN_WEIGHTS):
        w = inp[name].astype(_jnp.float32)
        if MOMENT_SCALE is None:
            s = _jnp.sqrt(_jnp.mean(_jnp.square(w)) + 1e-30)
        else:
            s = MOMENT_SCALE[name]
        km, kv = _jax.random.split(_jax.random.fold_in(key, i + 1))
        out[name] = w
        out["m_" + name] = s * _jax.random.normal(km, w.shape, _jnp.float32)
        out["v_" + name] = (s * s) * _jax.random.uniform(kv, w.shape, _jnp.float32, 0.5, 1.5)
    if N_MICROBATCH > 1:
        for name, axis in PER_EXAMPLE_BATCH_AXIS.items():
            out[name] = _to_microbatches(out[name], axis)
    return {'x': out['x'], 'p': out['p'], 'norm1_g': out['norm1_g'], 'w_in': out['w_in'], 'sg_ln_g': out['sg_ln_g'], 'sg_ln_b': out['sg_ln_b'], 'sg_w': out['sg_w'], 'sg_b': out['sg_b'], 'gla_w_a2': out['gla_w_a2'], 'gla_b_a': out['gla_b_a'], 'gla_norm_g': out['gla_norm_g'], 'att_rel_bias': out['att_rel_bias'], 'conv_dw_w': out['conv_dw_w'], 'conv_dw_b': out['conv_dw_b'], 'conv_ln_g': out['conv_ln_g'], 'conv_ln_b': out['conv_ln_b'], 'w_branch': out['w_branch'], 'w_gate': out['w_gate'], 'b_gate': out['b_gate'], 'w_out': out['w_out'], 'norm2_g': out['norm2_g'], 'w_ff1': out['w_ff1'], 'w_ff2': out['w_ff2'], 'norm3_g': out['norm3_g'], 'w_ple_gate': out['w_ple_gate'], 'b_ple_gate': out['b_ple_gate'], 'w_ple': out['w_ple'], 'final_g': out['final_g'], 'loss_target': out['loss_target'], 'm_norm1_g': out['m_norm1_g'], 'm_w_in': out['m_w_in'], 'm_sg_ln_g': out['m_sg_ln_g'], 'm_sg_ln_b': out['m_sg_ln_b'], 'm_sg_w': out['m_sg_w'], 'm_sg_b': out['m_sg_b'], 'm_gla_w_a2': out['m_gla_w_a2'], 'm_gla_b_a': out['m_gla_b_a'], 'm_gla_norm_g': out['m_gla_norm_g'], 'm_att_rel_bias': out['m_att_rel_bias'], 'm_conv_dw_w': out['m_conv_dw_w'], 'm_conv_dw_b': out['m_conv_dw_b'], 'm_conv_ln_g': out['m_conv_ln_g'], 'm_conv_ln_b': out['m_conv_ln_b'], 'm_w_branch': out['m_w_branch'], 'm_w_gate': out['m_w_gate'], 'm_b_gate': out['m_b_gate'], 'm_w_out': out['m_w_out'], 'm_norm2_g': out['m_norm2_g'], 'm_w_ff1': out['m_w_ff1'], 'm_w_ff2': out['m_w_ff2'], 'm_norm3_g': out['m_norm3_g'], 'm_w_ple_gate': out['m_w_ple_gate'], 'm_b_ple_gate': out['m_b_ple_gate'], 'm_w_ple': out['m_w_ple'], 'm_final_g': out['m_final_g'], 'v_norm1_g': out['v_norm1_g'], 'v_w_in': out['v_w_in'], 'v_sg_ln_g': out['v_sg_ln_g'], 'v_sg_ln_b': out['v_sg_ln_b'], 'v_sg_w': out['v_sg_w'], 'v_sg_b': out['v_sg_b'], 'v_gla_w_a2': out['v_gla_w_a2'], 'v_gla_b_a': out['v_gla_b_a'], 'v_gla_norm_g': out['v_gla_norm_g'], 'v_att_rel_bias': out['v_att_rel_bias'], 'v_conv_dw_w': out['v_conv_dw_w'], 'v_conv_dw_b': out['v_conv_dw_b'], 'v_conv_ln_g': out['v_conv_ln_g'], 'v_conv_ln_b': out['v_conv_ln_b'], 'v_w_branch': out['v_w_branch'], 'v_w_gate': out['v_w_gate'], 'v_b_gate': out['v_b_gate'], 'v_w_out': out['v_w_out'], 'v_norm2_g': out['v_norm2_g'], 'v_w_ff1': out['v_w_ff1'], 'v_w_ff2': out['v_w_ff2'], 'v_norm3_g': out['v_norm3_g'], 'v_w_ple_gate': out['v_w_ple_gate'], 'v_b_ple_gate': out['v_b_ple_gate'], 'v_w_ple': out['v_w_ple'], 'v_final_g': out['v_final_g']}


def _loss(weights, diff, rest, loss_target):
    with _jax.named_scope("forward"):
        args = {**rest, TWIN_DIFF_INPUT: diff, **{k: w.astype(_WEIGHT_DTYPES[k]) for k, w in weights.items()}}
        y = _forward(args)
    with _jax.named_scope("loss_head"):
        err = _jnp.square(y.astype(_jnp.float32) - loss_target)
        return 0.5 * _jnp.sum(_jnp.mean(err, axis=-1)) if err.ndim else 0.5 * err


def _adamw(w, g, m, v):
    m = ADAM_B1 * m + (1.0 - ADAM_B1) * g
    v = ADAM_B2 * v + (1.0 - ADAM_B2) * _jnp.square(g)
    m_hat = m / (1.0 - ADAM_B1 ** ADAM_STEP)
    v_hat = v / (1.0 - ADAM_B2 ** ADAM_STEP)
    delta = -ADAM_LR * (m_hat / (_jnp.sqrt(v_hat) + ADAM_EPS) + ADAM_WD * w)
    return delta, m, v


def reference(x, p, norm1_g, w_in, sg_ln_g, sg_ln_b, sg_w, sg_b, gla_w_a2, gla_b_a, gla_norm_g, att_rel_bias, conv_dw_w, conv_dw_b, conv_ln_g, conv_ln_b, w_branch, w_gate, b_gate, w_out, norm2_g, w_ff1, w_ff2, norm3_g, w_ple_gate, b_ple_gate, w_ple, final_g, loss_target, m_norm1_g, m_w_in, m_sg_ln_g, m_sg_ln_b, m_sg_w, m_sg_b, m_gla_w_a2, m_gla_b_a, m_gla_norm_g, m_att_rel_bias, m_conv_dw_w, m_conv_dw_b, m_conv_ln_g, m_conv_ln_b, m_w_branch, m_w_gate, m_b_gate, m_w_out, m_norm2_g, m_w_ff1, m_w_ff2, m_norm3_g, m_w_ple_gate, m_b_ple_gate, m_w_ple, m_final_g, v_norm1_g, v_w_in, v_sg_ln_g, v_sg_ln_b, v_sg_w, v_sg_b, v_gla_w_a2, v_gla_b_a, v_gla_norm_g, v_att_rel_bias, v_conv_dw_w, v_conv_dw_b, v_conv_ln_g, v_conv_ln_b, v_w_branch, v_w_gate, v_b_gate, v_w_out, v_norm2_g, v_w_ff1, v_w_ff2, v_norm3_g, v_w_ple_gate, v_b_ple_gate, v_w_ple, v_final_g):
    given = dict(x=x, p=p, norm1_g=norm1_g, w_in=w_in, sg_ln_g=sg_ln_g, sg_ln_b=sg_ln_b, sg_w=sg_w, sg_b=sg_b, gla_w_a2=gla_w_a2, gla_b_a=gla_b_a, gla_norm_g=gla_norm_g, att_rel_bias=att_rel_bias, conv_dw_w=conv_dw_w, conv_dw_b=conv_dw_b, conv_ln_g=conv_ln_g, conv_ln_b=conv_ln_b, w_branch=w_branch, w_gate=w_gate, b_gate=b_gate, w_out=w_out, norm2_g=norm2_g, w_ff1=w_ff1, w_ff2=w_ff2, norm3_g=norm3_g, w_ple_gate=w_ple_gate, b_ple_gate=b_ple_gate, w_ple=w_ple, final_g=final_g, loss_target=loss_target, m_norm1_g=m_norm1_g, m_w_in=m_w_in, m_sg_ln_g=m_sg_ln_g, m_sg_ln_b=m_sg_ln_b, m_sg_w=m_sg_w, m_sg_b=m_sg_b, m_gla_w_a2=m_gla_w_a2, m_gla_b_a=m_gla_b_a, m_gla_norm_g=m_gla_norm_g, m_att_rel_bias=m_att_rel_bias, m_conv_dw_w=m_conv_dw_w, m_conv_dw_b=m_conv_dw_b, m_conv_ln_g=m_conv_ln_g, m_conv_ln_b=m_conv_ln_b, m_w_branch=m_w_branch, m_w_gate=m_w_gate, m_b_gate=m_b_gate, m_w_out=m_w_out, m_norm2_g=m_norm2_g, m_w_ff1=m_w_ff1, m_w_ff2=m_w_ff2, m_norm3_g=m_norm3_g, m_w_ple_gate=m_w_ple_gate, m_b_ple_gate=m_b_ple_gate, m_w_ple=m_w_ple, m_final_g=m_final_g, v_norm1_g=v_norm1_g, v_w_in=v_w_in, v_sg_ln_g=v_sg_ln_g, v_sg_ln_b=v_sg_ln_b, v_sg_w=v_sg_w, v_sg_b=v_sg_b, v_gla_w_a2=v_gla_w_a2, v_gla_b_a=v_gla_b_a, v_gla_norm_g=v_gla_norm_g, v_att_rel_bias=v_att_rel_bias, v_conv_dw_w=v_conv_dw_w, v_conv_dw_b=v_conv_dw_b, v_conv_ln_g=v_conv_ln_g, v_conv_ln_b=v_conv_ln_b, v_w_branch=v_w_branch, v_w_gate=v_w_gate, v_b_gate=v_b_gate, v_w_out=v_w_out, v_norm2_g=v_norm2_g, v_w_ff1=v_w_ff1, v_w_ff2=v_w_ff2, v_norm3_g=v_norm3_g, v_w_ple_gate=v_w_ple_gate, v_b_ple_gate=v_b_ple_gate, v_w_ple=v_w_ple, v_final_g=v_final_g)
    weights = {n: given[n] for n in TWIN_WEIGHTS}
    shared = {n: given[n] for n in SHARED_INPUTS}
    per_example = {n: given[n] for n in ['x', 'p']}
    grad_fn = _jax.value_and_grad(_loss, argnums=(0, 1))

    def one_microbatch(ex, loss_target):
        ex = dict(ex)
        diff = ex.pop(TWIN_DIFF_INPUT)
        return grad_fn(weights, diff, {**shared, **ex}, loss_target)

    if N_MICROBATCH == 1:
        loss, (grad_w, grad_x) = one_microbatch(per_example, given["loss_target"])
    else:
        def body(carry, xs):
            loss_sum, grad_sum = carry
            l_k, (gw_k, gx_k) = one_microbatch(xs[0], xs[1])
            with _jax.named_scope("update"):
                return (loss_sum + l_k, _jax.tree.map(_jnp.add, grad_sum, gw_k)), gx_k

        init = (_jnp.zeros((), _jnp.float32), _jax.tree.map(_jnp.zeros_like, weights))
        (loss, grad_w), grad_x = _jax.lax.scan(body, init, (per_example, given["loss_target"]))
    with _jax.named_scope("update"):
        delta_w, new_m, new_v = {}, {}, {}
        for n in TWIN_WEIGHTS:
            delta_w[n], new_m[n], new_v[n] = _adamw(weights[n], grad_w[n], given["m_" + n], given["v_" + n])
    return (loss, grad_x, *[grad_w[n] for n in TWIN_WEIGHTS], *[delta_w[n] for n in TWIN_WEIGHTS],
            *[new_m[n] for n in TWIN_WEIGHTS], *[new_v[n] for n in TWIN_WEIGHTS])
```

```python
import functools

import jax
import jax.numpy as jnp
import numpy as np
from jax import lax
from jax.experimental import pallas as pl
from jax.experimental.pallas import tpu as pltpu

f32, bf16 = jnp.float32, jnp.bfloat16
HI = lax.Precision.HIGHEST
MESH = pl.DeviceIdType.MESH
SDS = jax.ShapeDtypeStruct
BS = pl.BlockSpec
ANY = pl.BlockSpec(memory_space=pl.ANY)

D = 1024
DEPTH = 2
CHUNK = 64
BW = 512
NP = 5120
RANK = 16
RANKP = 128
DFF = 4096
PLE = 256
CONV_K = 31
HALO = 32
TQ = 256
WIN = 768
REL_TABLE = 320
EPS = 1e-6
NEG_INF = -1e30
VMEM_LIMIT = 56 * 1024 * 1024

ADAM_LR, ADAM_B1, ADAM_B2, ADAM_EPS, ADAM_WD, ADAM_STEP = 0.001, 0.9, 0.999, 1e-08, 0.01, 10

OUR_COLS = dict(g_q=(0, 256), g_k=(256, 256), g_v=(512, 512), g_r=(1024, 512), a_q=(1536, 512), a_k=(2048, 512),
                a_v=(2560, 512), sg_u=(3072, 512), sg_v=(3584, 512), c_a=(4096, 512), c_g=(4608, 512))
REF_SPLITS = (("sg_u", 512), ("sg_v", 512), ("g_q", 256), ("g_k", 256), ("g_v", 512), ("g_r", 512), ("g_a", 16),
              ("a_q", 512), ("a_k", 512), ("a_v", 512), ("c_a", 512), ("c_g", 512))

SHARDED = dict(w_in=((1024, 5136), 1), w_branch=((4, 512, 1024), 2), w_gate=((4, 1024, 1024), 1), w_out=((1024, 1024), 0),
               w_ff1=((1024, 4096), 1), w_ff2=((4096, 1024), 0), w_ple_gate=((1024, 1024), 0), w_ple=((256, 1024), 1),
               gla_w_a2=((16, 256), 1), att_rel_bias=((8, 320), 1), conv_dw_w=((31, 512), 1), b_gate=((4, 1024), 1))
BIG = ("w_in", "w_branch", "w_gate", "w_out", "w_ff1", "w_ff2", "w_ple_gate", "w_ple")
SMALL = ("gla_w_a2", "att_rel_bias", "conv_dw_w", "b_gate")
REPL = dict(norm1_g=(2, 1024), sg_ln_g=(2, 512), sg_ln_b=(2, 512), sg_w=(2, 4, 128, 128), sg_b=(2, 4, 128), gla_b_a=(2, 256),
            gla_norm_g=(2, 512), conv_dw_b=(2, 512), conv_ln_g=(2, 512), conv_ln_b=(2, 512), norm2_g=(2, 1024),
            norm3_g=(2, 1024), b_ple_gate=(2, 1024), final_g=(1024,))
W_ORDER = ['norm1_g', 'w_in', 'sg_ln_g', 'sg_ln_b', 'sg_w', 'sg_b', 'gla_w_a2', 'gla_b_a', 'gla_norm_g', 'att_rel_bias',
           'conv_dw_w', 'conv_dw_b', 'conv_ln_g', 'conv_ln_b', 'w_branch', 'w_gate', 'b_gate', 'w_out', 'norm2_g', 'w_ff1',
           'w_ff2', 'norm3_g', 'w_ple_gate', 'b_ple_gate', 'w_ple', 'final_g']
PACK_W = 1024
PACK_RH = 5456
SMALL_RH = 8
REPL_ROWS = 152


def _tile(s):
    return 512 if s % 512 == 0 else s


def _cp(*sem):
    return pltpu.CompilerParams(dimension_semantics=sem, vmem_limit_bytes=VMEM_LIMIT)


def rms_fwd(h, g, name):
    S, Dm = h.shape
    T = _tile(S)

    def body(h_ref, g_ref, o_ref):
        x = h_ref[...]
        r = lax.rsqrt(jnp.mean(x * x, axis=-1, keepdims=True) + EPS)
        o_ref[...] = (x * r * g_ref[...]).astype(bf16)

    return pl.pallas_call(
        body, out_shape=SDS((S, Dm), bf16), grid=(S // T,),
        in_specs=[BS((T, Dm), lambda i: (i, 0)), BS((1, Dm), lambda i: (0, 0))],
        out_specs=BS((T, Dm), lambda i: (i, 0)), compiler_params=_cp("parallel"), name=name)(h, g)


def mm_nn(x, w, *, name, tn=512, bias=None, act=None, pre=None, mul=None, res=None, out_dtype=bf16):
    S = x.shape[0]
    G, K, N = w.shape
    T = _tile(S)
    tn = min(tn, N)
    nj = N // tn
    extras = [a for a in (bias, mul, res) if a is not None]

    def body(*refs):
        it = iter(refs)
        x_ref, w_ref = next(it), next(it)
        b_ref = next(it) if bias is not None else None
        m_ref = next(it) if mul is not None else None
        r_ref = next(it) if res is not None else None
        o_ref = next(it)
        xv = x_ref[...]
        if pre == "relu2":
            xf = jnp.maximum(xv.astype(f32), 0.0)
            xv = xf * xf
        acc = jnp.dot(xv.astype(bf16), w_ref[0], preferred_element_type=f32)
        if b_ref is not None:
            acc = acc + b_ref[...]
        if act == "sigmoid":
            acc = jax.nn.sigmoid(acc)
        if m_ref is not None:
            acc = acc * m_ref[...].astype(f32)
        if r_ref is not None:
            acc = r_ref[...].astype(f32) + acc
        o_ref[...] = acc.astype(out_dtype)

    in_specs = [BS((T, K), lambda i, g, j: (i, g)), BS((1, K, tn), lambda i, g, j: (g, 0, j))]
    if bias is not None:
        in_specs.append(BS((1, tn), lambda i, g, j: (0, g * nj + j)))
    for a in (mul, res):
        if a is not None:
            in_specs.append(BS((T, tn), lambda i, g, j: (i, g * nj + j)))
    return pl.pallas_call(
        body, out_shape=SDS((S, G * N), out_dtype), grid=(S // T, G, nj), in_specs=in_specs,
        out_specs=BS((T, tn), lambda i, g, j: (i, g * nj + j)),
        compiler_params=_cp("parallel", "parallel", "parallel"), name=name)(x, w, *extras)


def mm_nt(dy, w, *, name, tk=256, res=None, post_a=None, out_dtype=f32):
    S = dy.shape[0]
    G, K, N = w.shape
    T = _tile(S)
    tk = min(tk, K)
    nk = K // tk
    extras = [a for a in (res, post_a) if a is not None]

    def body(*refs):
        it = iter(refs)
        d_ref, w_ref = next(it), next(it)
        r_ref = next(it) if res is not None else None
        a_ref = next(it) if post_a is not None else None
        o_ref = next(it)
        acc = lax.dot_general(d_ref[...].astype(bf16), w_ref[0], (((1,), (1,)), ((), ())), preferred_element_type=f32)
        if r_ref is not None:
            acc = acc + r_ref[...].astype(f32)
        if a_ref is not None:
            acc = acc * (2.0 * jnp.maximum(a_ref[...].astype(f32), 0.0))
        o_ref[...] = acc.astype(out_dtype)

    in_specs = [BS((T, N), lambda i, g, j: (i, g)), BS((1, tk, N), lambda i, g, j: (g, j, 0))]
    for a in extras:
        in_specs.append(BS((T, tk), lambda i, g, j: (i, g * nk + j)))
    return pl.pallas_call(
        body, out_shape=SDS((S, G * K), out_dtype), grid=(S // T, G, nk), in_specs=in_specs,
        out_specs=BS((T, tk), lambda i, g, j: (i, g * nk + j)),
        compiler_params=_cp("parallel", "parallel", "parallel"), name=name)(dy, w, *extras)


def mm_tn(x, dy, *, name, G=1, pre=None, ts=1024):
    S = x.shape[0]
    K, N = x.shape[1] // G, dy.shape[1] // G
    tk, tn = min(K, 1024), min(N, 1024)
    nk, nn = K // tk, N // tn
    ts = min(ts, S)

    def body(x_ref, d_ref, o_ref):
        @pl.when(pl.program_id(3) == 0)
        def _():
            o_ref[...] = jnp.zeros_like(o_ref)

        xv = x_ref[...]
        if pre == "relu2":
            xf = jnp.maximum(xv.astype(f32), 0.0)
            xv = xf * xf
        o_ref[0] += lax.dot_general(xv.astype(bf16), d_ref[...].astype(bf16), (((0,), (0,)), ((), ())),
                                    preferred_element_type=f32)

    return pl.pallas_call(
        body, out_shape=SDS((G, K, N), f32), grid=(G, nk, nn, S // ts),
        in_specs=[BS((ts, tk), lambda g, a, b, s: (s, g * nk + a)), BS((ts, tn), lambda g, a, b, s: (s, g * nn + b))],
        out_specs=BS((1, tk, tn), lambda g, a, b, s: (g, a, b)),
        compiler_params=_cp("parallel", "parallel", "parallel", "arbitrary"), name=name)(x, dy)


def rms_bwd(dxn, x, g, dres, name):
    S, Dm = x.shape
    T = _tile(S)

    def body(*refs):
        if dres is not None:
            d_ref, x_ref, g_ref, r_ref, dx_ref, dg_ref = refs
        else:
            d_ref, x_ref, g_ref, dx_ref, dg_ref = refs
        xv = x_ref[...]
        d = d_ref[...].astype(f32)
        r = lax.rsqrt(jnp.mean(xv * xv, axis=-1, keepdims=True) + EPS)
        u = d * g_ref[...]
        dx = r * u - xv * ((r * r * r) * (1.0 / Dm)) * jnp.sum(u * xv, axis=-1, keepdims=True)
        if dres is not None:
            dx = r_ref[...] + dx
        dx_ref[...] = dx

        @pl.when(pl.program_id(0) == 0)
        def _():
            dg_ref[...] = jnp.zeros_like(dg_ref)

        dg_ref[...] += jnp.sum(d * xv * r, axis=0, keepdims=True)

    tok = BS((T, Dm), lambda i: (i, 0))
    vec = BS((1, Dm), lambda i: (0, 0))
    args = (dxn, x, g) + ((dres,) if dres is not None else ())
    return pl.pallas_call(
        body, out_shape=(SDS((S, Dm), f32), SDS((1, Dm), f32)), grid=(S // T,),
        in_specs=[tok, tok, vec] + ([tok] if dres is not None else []), out_specs=(tok, vec),
        compiler_params=_cp("arbitrary"), name=name)(*args)


def loss_head(h, g, target, name):
    S, Dm = h.shape
    T = _tile(S)

    def body(h_ref, g_ref, t_ref, loss_ref, dh_ref, dg_ref):
        @pl.when(pl.program_id(0) == 0)
        def _():
            loss_ref[...] = jnp.zeros_like(loss_ref)
            dg_ref[...] = jnp.zeros_like(dg_ref)

        xv = h_ref[...]
        gv = g_ref[...]
        r = lax.rsqrt(jnp.mean(xv * xv, axis=-1, keepdims=True) + EPS)
        diff = xv * r * gv - t_ref[...]
        loss_ref[...] += 0.5 * jnp.sum(jnp.mean(diff * diff, axis=-1, keepdims=True))
        d = diff * (1.0 / Dm)
        u = d * gv
        dh_ref[...] = r * u - xv * ((r * r * r) * (1.0 / Dm)) * jnp.sum(u * xv, axis=-1, keepdims=True)
        dg_ref[...] += jnp.sum(d * xv * r, axis=0, keepdims=True)

    tok = BS((T, Dm), lambda i: (i, 0))
    vec = BS((1, Dm), lambda i: (0, 0))
    return pl.pallas_call(
        body, out_shape=(SDS((1, 128), f32), SDS((S, Dm), f32), SDS((1, Dm), f32)), grid=(S // T,),
        in_specs=[tok, vec, tok], out_specs=(BS((1, 128), lambda i: (0, 0)), tok, vec),
        compiler_params=_cp("arbitrary"), name=name)(h, g, target)


def gate_merge_fwd(gate, z, name):
    S = gate.shape[0]
    T = _tile(S)

    def body(g_ref, z_ref, o_ref):
        acc = jnp.zeros((T, D), f32)
        for n in range(4):
            acc = acc + g_ref[:, n * D:(n + 1) * D].astype(f32) * z_ref[:, n * D:(n + 1) * D].astype(f32)
        o_ref[...] = acc.astype(bf16)

    wide = BS((T, 4 * D), lambda i: (i, 0))
    return pl.pallas_call(body, out_shape=SDS((S, D), bf16), grid=(S // T,), in_specs=[wide, wide],
                          out_specs=BS((T, D), lambda i: (i, 0)), compiler_params=_cp("parallel"), name=name)(gate, z)


def gate_merge_bwd(dm, gate, z, name):
    S = gate.shape[0]
    T = _tile(S)

    def body(dm_ref, g_ref, z_ref, dz_ref, dg_ref, db_ref):
        @pl.when(pl.program_id(0) == 0)
        def _():
            db_ref[...] = jnp.zeros_like(db_ref)

        dmv = dm_ref[...].astype(f32)
        for n in range(4):
            cols = slice(n * D, (n + 1) * D)
            gv = g_ref[:, cols].astype(f32)
            dz_ref[:, cols] = (dmv * gv).astype(bf16)
            dgp = dmv * z_ref[:, cols].astype(f32) * gv * (1.0 - gv)
            dg_ref[:, cols] = dgp.astype(bf16)
            db_ref[:, cols] += jnp.sum(dgp, axis=0, keepdims=True)

    wide = BS((T, 4 * D), lambda i: (i, 0))
    return pl.pallas_call(
        body, out_shape=(SDS((S, 4 * D), bf16), SDS((S, 4 * D), bf16), SDS((1, 4 * D), f32)), grid=(S // T,),
        in_specs=[BS((T, D), lambda i: (i, 0)), wide, wide], out_specs=(wide, wide, BS((1, 4 * D), lambda i: (0, 0))),
        compiler_params=_cp("arbitrary"), name=name)(dm, gate, z)


def ple_bwd_ew(dh, e, pg, name):
    S = dh.shape[0]
    T = _tile(S)

    def body(dh_ref, e_ref, pg_ref, dp_ref, de_ref, db_ref):
        @pl.when(pl.program_id(0) == 0)
        def _():
            db_ref[...] = jnp.zeros_like(db_ref)

        d = dh_ref[...]
        g = pg_ref[...].astype(f32)
        dpre = d * e_ref[...].astype(f32) * g * (1.0 - g)
        dp_ref[...] = dpre.astype(bf16)
        de_ref[...] = (d * g).astype(bf16)
        db_ref[...] += jnp.sum(dpre, axis=0, keepdims=True)

    tok = BS((T, D), lambda i: (i, 0))
    return pl.pallas_call(
        body, out_shape=(SDS((S, D), bf16), SDS((S, D), bf16), SDS((1, D), f32)), grid=(S // T,),
        in_specs=[tok, tok, tok], out_specs=(tok, tok, BS((1, D), lambda i: (0, 0))),
        compiler_params=_cp("arbitrary"), name=name)(dh, e, pg)


_GK = 0.7978845608028654
_GC = 0.044715


def _gelu(x):
    return 0.5 * x * (1.0 + jnp.tanh(_GK * (x + _GC * (x * x * x))))


def _gelu_grad(x):
    x2 = x * x
    t = jnp.tanh(_GK * (x + _GC * (x * x2)))
    return 0.5 * (1.0 + t) + 0.5 * x * (1.0 - t * t) * (_GK * (1.0 + 3.0 * _GC * x2))


def _ln_stats(v):
    mu = jnp.mean(v, axis=-1, keepdims=True)
    vc = v - mu
    rs = lax.rsqrt(jnp.mean(vc * vc, axis=-1, keepdims=True) + EPS)
    return vc * rs, rs


def _ln_bwd(dvh, vh, rs):
    return rs * (dvh - jnp.mean(dvh, axis=-1, keepdims=True) - vh * jnp.mean(dvh * vh, axis=-1, keepdims=True))


def sg_fwd(proj, lg, lb, wm, bsb, name):
    S = proj.shape[0]
    T = _tile(S)
    cu, cv = OUR_COLS["sg_u"][0] // BW, OUR_COLS["sg_v"][0] // BW

    def body(u_ref, v_ref, lg_ref, lb_ref, wm_ref, bsb_ref, o_ref):
        for b in range(T // 128):
            rows = slice(b * 128, (b + 1) * 128)
            u = _gelu(u_ref[rows, :].astype(f32))
            vh, _ = _ln_stats(_gelu(v_ref[rows, :].astype(f32)))
            vb = (vh * lg_ref[...] + lb_ref[...]).astype(bf16)
            outs = []
            for g in range(4):
                cols = slice(g * 128, (g + 1) * 128)
                mixed = jnp.dot(wm_ref[g], vb[:, cols], preferred_element_type=f32) + bsb_ref[g]
                outs.append(u[:, cols] * mixed)
            o_ref[rows, :] = jnp.concatenate(outs, axis=1).astype(bf16)

    vec = BS((1, BW), lambda i: (0, 0))
    cube = BS((4, 128, 128), lambda i: (0, 0, 0))
    return pl.pallas_call(
        body, out_shape=SDS((S, BW), bf16), grid=(S // T,),
        in_specs=[BS((T, BW), lambda i: (i, cu)), BS((T, BW), lambda i: (i, cv)), vec, vec, cube, cube],
        out_specs=BS((T, BW), lambda i: (i, 0)), compiler_params=_cp("parallel"), name=name)(proj, proj, lg, lb, wm, bsb)


def sg_bwd(proj, dy, lg, lb, wm, bsb, maskf, name):
    S = proj.shape[0]
    T = _tile(S)
    cu, cv = OUR_COLS["sg_u"][0] // BW, OUR_COLS["sg_v"][0] // BW
    creg = OUR_COLS["sg_u"][0] // (2 * BW)

    def body(u_ref, v_ref, dy_ref, lg_ref, lb_ref, wm_ref, bsb_ref, mk_ref, dp_ref, dwm_ref, dbs_ref, dlg_ref, dlb_ref):
        @pl.when(pl.program_id(0) == 0)
        def _():
            dwm_ref[...] = jnp.zeros_like(dwm_ref)
            dbs_ref[...] = jnp.zeros_like(dbs_ref)
            dlg_ref[...] = jnp.zeros_like(dlg_ref)
            dlb_ref[...] = jnp.zeros_like(dlb_ref)

        for b in range(T // 128):
            rows = slice(b * 128, (b + 1) * 128)
            su = u_ref[rows, :].astype(f32)
            sv = v_ref[rows, :].astype(f32)
            dya = dy_ref[rows, :].astype(f32)
            u = _gelu(su)
            vh, rs = _ln_stats(_gelu(sv))
            vb = (vh * lg_ref[...] + lb_ref[...]).astype(bf16)
            dus, dvls = [], []
            for g in range(4):
                cols = slice(g * 128, (g + 1) * 128)
                mixed = jnp.dot(wm_ref[g], vb[:, cols], preferred_element_type=f32) + bsb_ref[g]
                dus.append(dya[:, cols] * mixed)
                dmg = dya[:, cols] * u[:, cols]
                dmb = dmg.astype(bf16)
                dbs_ref[g] += jnp.broadcast_to(jnp.sum(dmg, axis=1, keepdims=True), (128, 128))
                dwm_ref[g] += mk_ref[...] * lax.dot_general(dmb, vb[:, cols], (((1,), (1,)), ((), ())),
                                                            preferred_element_type=f32)
                dvls.append(lax.dot_general(wm_ref[g], dmb, (((0,), (0,)), ((), ())), preferred_element_type=f32))
            du = jnp.concatenate(dus, axis=1)
            dvln = jnp.concatenate(dvls, axis=1)
            dlg_ref[...] += jnp.sum(dvln * vh, axis=0, keepdims=True)
            dlb_ref[...] += jnp.sum(dvln, axis=0, keepdims=True)
            dv = _ln_bwd(dvln * lg_ref[...], vh, rs)
            dp_ref[rows, 0:BW] = (du * _gelu_grad(su)).astype(bf16)
            dp_ref[rows, BW:2 * BW] = (dv * _gelu_grad(sv)).astype(bf16)

    vec = BS((1, BW), lambda i: (0, 0))
    cube = BS((4, 128, 128), lambda i: (0, 0, 0))
    return pl.pallas_call(
        body,
        out_shape=(SDS((S, NP), bf16), SDS((4, 128, 128), f32), SDS((4, 128, 128), f32), SDS((1, BW), f32), SDS((1, BW), f32)),
        grid=(S // T,),
        in_specs=[BS((T, BW), lambda i: (i, cu)), BS((T, BW), lambda i: (i, cv)), BS((T, BW), lambda i: (i, 0)), vec, vec,
                  cube, cube, BS((128, 128), lambda i: (0, 0))],
        out_specs=(BS((T, 2 * BW), lambda i: (i, creg)), cube, cube, vec, vec),
        compiler_params=_cp("arbitrary"), name=name)(proj, proj, dy, lg, lb, wm, bsb, maskf)


_SUB = 64


def _conv_specs(S, T):
    ca, cg = OUR_COLS["c_a"][0] // BW, OUR_COLS["c_g"][0] // BW
    hb = T // HALO
    prev = lambda i: jnp.maximum(i * hb - 1, 0)
    return [BS((T, BW), lambda i: (i, ca)), BS((T, BW), lambda i: (i, cg)),
            BS((HALO, BW), lambda i: (prev(i), ca)), BS((HALO, BW), lambda i: (prev(i), cg))]


def _conv_fill_ybuf(a_ref, g_ref, ap_ref, gp_ref, ybuf):
    T = a_ref.shape[0]
    ybuf[pl.ds(HALO, T), :] = a_ref[...].astype(f32) * jax.nn.sigmoid(g_ref[...].astype(f32))
    first = (pl.program_id(0) == 0).astype(f32)
    ybuf[pl.ds(0, HALO), :] = (1.0 - first) * (ap_ref[...].astype(f32) * jax.nn.sigmoid(gp_ref[...].astype(f32)))


def _conv_taps(w_ref, ybuf, r0):
    acc = jnp.zeros((_SUB, BW), f32)
    for k in range(CONV_K):
        acc = acc + w_ref[k:k + 1, :] * ybuf[pl.ds(r0 + HALO - (CONV_K - 1) + k, _SUB), :]
    return acc


def conv_fwd(proj, w, b, lg, lb, name):
    S = proj.shape[0]
    T = _tile(S)

    def body(a_ref, g_ref, ap_ref, gp_ref, w_ref, b_ref, lg_ref, lb_ref, o_ref, ybuf):
        _conv_fill_ybuf(a_ref, g_ref, ap_ref, gp_ref, ybuf)
        for sb in range(T // _SUB):
            z = _conv_taps(w_ref, ybuf, sb * _SUB) + b_ref[...]
            zh, _ = _ln_stats(z)
            zl = zh * lg_ref[...] + lb_ref[...]
            o_ref[pl.ds(sb * _SUB, _SUB), :] = (zl * jax.nn.sigmoid(zl)).astype(bf16)

    vec = BS((1, BW), lambda i: (0, 0))
    return pl.pallas_call(
        body, out_shape=SDS((S, BW), bf16), grid=(S // T,),
        in_specs=_conv_specs(S, T) + [BS((CONV_K, BW), lambda i: (0, 0)), vec, vec, vec],
        out_specs=BS((T, BW), lambda i: (i, 0)), scratch_shapes=[pltpu.VMEM((T + HALO, BW), f32)],
        compiler_params=_cp("parallel"), name=name)(proj, proj, proj, proj, w, b, lg, lb)


def conv_bwd_norm(proj, dy, w, b, lg, lb, name):
    S = proj.shape[0]
    T = _tile(S)

    def body(a_ref, g_ref, ap_ref, gp_ref, dy_ref, w_ref, b_ref, lg_ref, lb_ref, dz_ref, dlg_ref, dlb_ref, db_ref, ybuf):
        @pl.when(pl.program_id(0) == 0)
        def _():
            dlg_ref[...] = jnp.zeros_like(dlg_ref)
            dlb_ref[...] = jnp.zeros_like(dlb_ref)
            db_ref[...] = jnp.zeros_like(db_ref)

        _conv_fill_ybuf(a_ref, g_ref, ap_ref, gp_ref, ybuf)
        for sb in range(T // _SUB):
            rows = pl.ds(sb * _SUB, _SUB)
            z = _conv_taps(w_ref, ybuf, sb * _SUB) + b_ref[...]
            zh, rs = _ln_stats(z)
            zl = zh * lg_ref[...] + lb_ref[...]
            sg = jax.nn.sigmoid(zl)
            dzl = dy_ref[rows, :].astype(f32) * sg * (1.0 + zl * (1.0 - sg))
            dlg_ref[...] += jnp.sum(dzl * zh, axis=0, keepdims=True)
            dlb_ref[...] += jnp.sum(dzl, axis=0, keepdims=True)
            dz = _ln_bwd(dzl * lg_ref[...], zh, rs)
            db_ref[...] += jnp.sum(dz, axis=0, keepdims=True)
            dz_ref[rows, :] = dz

    vec = BS((1, BW), lambda i: (0, 0))
    tok = BS((T, BW), lambda i: (i, 0))
    return pl.pallas_call(
        body, out_shape=(SDS((S, BW), f32), SDS((1, BW), f32), SDS((1, BW), f32), SDS((1, BW), f32)), grid=(S // T,),
        in_specs=_conv_specs(S, T) + [BS((T, BW), lambda i: (i, 3)), BS((CONV_K, BW), lambda i: (0, 0)), vec, vec, vec],
        out_specs=(tok, vec, vec, vec), scratch_shapes=[pltpu.VMEM((T + HALO, BW), f32)],
        compiler_params=_cp("arbitrary"), name=name)(proj, proj, proj, proj, dy, w, b, lg, lb)


def conv_bwd_taps(proj, dz, w, dproj, name):
    S = proj.shape[0]
    T = _tile(S)
    nT = S // T
    hb = T // HALO
    creg = OUR_COLS["c_a"][0] // (2 * BW)

    def body(a_ref, g_ref, ap_ref, gp_ref, dz_ref, dzn_ref, w_ref, dp_in, dp_ref, dw_ref, ybuf, dzbuf, dwacc):
        del dp_in
        i = pl.program_id(0)

        @pl.when(i == 0)
        def _():
            dwacc[...] = jnp.zeros_like(dwacc)

        _conv_fill_ybuf(a_ref, g_ref, ap_ref, gp_ref, ybuf)
        dzbuf[pl.ds(0, T), :] = dz_ref[...]
        dzbuf[pl.ds(T, HALO), :] = (i < nT - 1).astype(f32) * dzn_ref[...]
        for sb in range(T // _SUB):
            r0 = sb * _SUB
            rows = pl.ds(r0, _SUB)
            dzs = dz_ref[rows, :]
            dyg = jnp.zeros((_SUB, BW), f32)
            for k in range(CONV_K):
                ysl = ybuf[pl.ds(r0 + HALO - (CONV_K - 1) + k, _SUB), :]
                dwacc[pl.ds(k * 8, 8), :] += jnp.sum((dzs * ysl).reshape(_SUB // 8, 8, BW), axis=0)
                dyg = dyg + w_ref[k:k + 1, :] * dzbuf[pl.ds(r0 + (CONV_K - 1) - k, _SUB), :]
            av = a_ref[rows, :].astype(f32)
            sg = jax.nn.sigmoid(g_ref[rows, :].astype(f32))
            dp_ref[rows, 0:BW] = (dyg * sg).astype(bf16)
            dp_ref[rows, BW:2 * BW] = (dyg * av * sg * (1.0 - sg)).astype(bf16)

        @pl.when(i == nT - 1)
        def _():
            for k in range(CONV_K):
                dw_ref[k:k + 1, :] = jnp.sum(dwacc[pl.ds(k * 8, 8), :], axis=0, keepdims=True)

    nxt = lambda i: jnp.minimum((i + 1) * hb, S // HALO - 1)
    return pl.pallas_call(
        body, out_shape=(SDS((S, NP), bf16), SDS((CONV_K, BW), f32)), grid=(nT,),
        in_specs=_conv_specs(S, T) + [BS((T, BW), lambda i: (i, 0)), BS((HALO, BW), lambda i: (nxt(i), 0)),
                                      BS((CONV_K, BW), lambda i: (0, 0)), ANY],
        out_specs=(BS((T, 2 * BW), lambda i: (i, creg)), BS((CONV_K, BW), lambda i: (0, 0))),
        scratch_shapes=[pltpu.VMEM((T + HALO, BW), f32), pltpu.VMEM((T + HALO, BW), f32), pltpu.VMEM((CONV_K * 8, BW), f32)],
        input_output_aliases={7: 0}, compiler_params=_cp("arbitrary"), name=name)(proj, proj, proj, proj, dz, dz, w, dproj)


def _toeplitz_index():
    j = lax.broadcasted_iota(jnp.int32, (REL_TABLE, 1024), 1)
    t = lax.broadcasted_iota(jnp.int32, (REL_TABLE, 1024), 0)
    e = ((WIN - 1) - j) & 1023
    tidx = jnp.clip(e - (TQ - 1), -(CHUNK - 1), 256) + (CHUNK - 1)
    return (tidx == t).astype(f32)


def att_bias_build(table, name):
    H = table.shape[0]

    def body(t_ref, o_ref):
        u = jnp.dot(t_ref[...], _toeplitz_index(), precision=HI, preferred_element_type=f32)
        row = lax.broadcasted_iota(jnp.int32, (TQ, 1024), 0)
        for h in range(H):
            x = jnp.broadcast_to(u[h:h + 1, :], (TQ, 1024))
            for b in range(8):
                x = jnp.where(((row >> b) & 1) == 1, pltpu.roll(x, 1 << b, 1), x)
            o_ref[h] = x[:, :WIN]

    return pl.pallas_call(body, out_shape=SDS((H, TQ, WIN), f32), compiler_params=pltpu.CompilerParams(vmem_limit_bytes=VMEM_LIMIT),
                          name=name)(table)


def att_bias_grad(dbias, name):
    H = dbias.shape[0]

    def body(d_ref, o_ref):
        row = lax.broadcasted_iota(jnp.int32, (TQ, 1024), 0)
        rows = []
        for h in range(H):
            x = jnp.concatenate([d_ref[h], jnp.zeros((TQ, 1024 - WIN), f32)], axis=1)
            for b in range(8):
                x = jnp.where(((row >> b) & 1) == 1, pltpu.roll(x, 1024 - (1 << b), 1), x)
            rows.append(jnp.sum(x, axis=0, keepdims=True))
        du = jnp.concatenate(rows, axis=0)
        o_ref[...] = lax.dot_general(du, _toeplitz_index(), (((1,), (1,)), ((), ())), precision=HI,
                                     preferred_element_type=f32)

    return pl.pallas_call(body, out_shape=SDS((H, REL_TABLE), f32), compiler_params=pltpu.CompilerParams(vmem_limit_bytes=VMEM_LIMIT),
                          name=name)(dbias)


def _att_specs():
    cq, ck, cv = (OUR_COLS[n][0] // BW for n in ("a_q", "a_k", "a_v"))
    specs = [BS((TQ, BW), lambda i: (i, cq))]
    for col in (ck, cv):
        for back in (2, 1, 0):
            specs.append(BS((TQ, BW), functools.partial(lambda i, back, col: (jnp.maximum(i - back, 0), col), back=back, col=col)))
    return specs


def _att_valid(i):
    r = lax.broadcasted_iota(jnp.int32, (TQ, WIN), 0)
    n = lax.broadcasted_iota(jnp.int32, (TQ, WIN), 1)
    dchunk = (r // CHUNK + 8) - n // CHUNK
    return (dchunk >= 0) & (dchunk <= 8) & (n + (i - 2) * TQ >= 0)


def _att_probs(qa, kp, bias_h, valid):
    s = lax.dot_general(qa, kp, (((1,), (1,)), ((), ())), preferred_element_type=f32) * 0.125 + bias_h
    s = jnp.where(valid, s, NEG_INF)
    e = jnp.exp(s - jnp.max(s, axis=-1, keepdims=True))
    return e / jnp.sum(e, axis=-1, keepdims=True)


def att_fwd(proj, bias, name):
    S = proj.shape[0]

    def body(q_ref, k2, k1, k0, v2, v1, v0, b_ref, o_ref, kwin, vwin):
        i = pl.program_id(0)
        for w, (kr, vr) in enumerate(((k2, v2), (k1, v1), (k0, v0))):
            kwin[pl.ds(w * TQ, TQ), :] = kr[...]
            vwin[pl.ds(w * TQ, TQ), :] = vr[...]
        valid = _att_valid(i)
        lo = lax.broadcasted_iota(jnp.int32, (TQ, 128), 1) < 64
        for hp in range(4):
            cols = slice(hp * 128, (hp + 1) * 128)
            qp, kp, vp = q_ref[:, cols], kwin[:, cols], vwin[:, cols]
            outs = []
            for a in range(2):
                qa = jnp.where(lo if a == 0 else ~lo, qp, jnp.zeros_like(qp))
                p = _att_probs(qa, kp, b_ref[2 * hp + a], valid)
                outs.append(jnp.dot(p.astype(bf16), vp, preferred_element_type=f32))
            o_ref[:, cols] = jnp.where(lo, outs[0], outs[1]).astype(bf16)

    return pl.pallas_call(
        body, out_shape=SDS((S, BW), bf16), grid=(S // TQ,),
        in_specs=_att_specs() + [BS((8, TQ, WIN), lambda i: (0, 0, 0), pipeline_mode=pl.Buffered(1))],
        out_specs=BS((TQ, BW), lambda i: (i, 0)),
        scratch_shapes=[pltpu.VMEM((WIN, BW), bf16), pltpu.VMEM((WIN, BW), bf16)],
        compiler_params=_cp("parallel"), name=name)(proj, proj, proj, proj, proj, proj, proj, bias)


def att_bwd(proj, y, dy, bias, dproj, name):
    S = proj.shape[0]
    cq = OUR_COLS["a_q"][0] // BW

    def body(q_ref, k2, k1, k0, v2, v1, v0, b_ref, o_ref, do_ref, dp_in, dq_ref, dkp_ref, dvp_ref, db_ref, kwin, vwin):
        del dp_in
        i = pl.program_id(0)

        @pl.when(i == 0)
        def _():
            db_ref[...] = jnp.zeros_like(db_ref)

        for w, (kr, vr) in enumerate(((k2, v2), (k1, v1), (k0, v0))):
            kwin[pl.ds(w * TQ, TQ), :] = kr[...]
            vwin[pl.ds(w * TQ, TQ), :] = vr[...]
        valid = _att_valid(i)
        lo = lax.broadcasted_iota(jnp.int32, (TQ, 128), 1) < 64
        for hp in range(4):
            cols = slice(hp * 128, (hp + 1) * 128)
            qp, kp, vp = q_ref[:, cols], kwin[:, cols], vwin[:, cols]
            dop, op = do_ref[:, cols], o_ref[:, cols]
            dqs = []
            dk = jnp.zeros((WIN, 128), f32)
            dv = jnp.zeros((WIN, 128), f32)
            for a in range(2):
                sel = lo if a == 0 else ~lo
                qa = jnp.where(sel, qp, jnp.zeros_like(qp))
                doa = jnp.where(sel, dop, jnp.zeros_like(dop))
                p = _att_probs(qa, kp, b_ref[2 * hp + a], valid)
                dpv = lax.dot_general(doa, vp, (((1,), (1,)), ((), ())), preferred_element_type=f32)
                delta = jnp.sum(doa.astype(f32) * op.astype(f32), axis=-1, keepdims=True)
                ds = p * (dpv - delta)
                db_ref[2 * hp + a] += ds
                dsb = ds.astype(bf16)
                dqs.append(jnp.dot(dsb, kp, preferred_element_type=f32))
                dk = dk + lax.dot_general(dsb, qa, (((0,), (0,)), ((), ())), preferred_element_type=f32)
                dv = dv + lax.dot_general(p.astype(bf16), doa, (((0,), (0,)), ((), ())), preferred_element_type=f32)
            dq_ref[:, cols] = (jnp.where(lo, dqs[0], dqs[1]) * 0.125).astype(bf16)
            for w in range(3):
                dkp_ref[w, :, cols] = (dk[w * TQ:(w + 1) * TQ] * 0.125).astype(bf16)
                dvp_ref[w, :, cols] = dv[w * TQ:(w + 1) * TQ].astype(bf16)

    tok = BS((TQ, BW), lambda i: (i, 2))
    part = BS((3, TQ, BW), lambda i: (0, i, 0))
    full = BS((8, TQ, WIN), lambda i: (0, 0, 0))
    return pl.pallas_call(
        body, out_shape=(SDS((S, NP), bf16), SDS((3, S, BW), bf16), SDS((3, S, BW), bf16), SDS((8, TQ, WIN), f32)),
        grid=(S // TQ,),
        in_specs=_att_specs() + [BS((8, TQ, WIN), lambda i: (0, 0, 0), pipeline_mode=pl.Buffered(1)), tok, tok, ANY],
        out_specs=(BS((TQ, BW), lambda i: (i, cq)), part, part, full),
        scratch_shapes=[pltpu.VMEM((WIN, BW), bf16), pltpu.VMEM((WIN, BW), bf16)],
        input_output_aliases={10: 0}, compiler_params=_cp("arbitrary"), name=name)(
            proj, proj, proj, proj, proj, proj, proj, bias, y, dy, dproj)


def att_shift_add(dkp, dvp, dproj, name):
    S = dkp.shape[1]
    nT = S // TQ
    creg = OUR_COLS["a_k"][0] // (2 * BW)

    def body(k2, k1, k0, v2, v1, v0, dp_in, dp_ref):
        del dp_in
        j = pl.program_id(0)
        m1 = (j + 1 < nT).astype(f32)
        m0 = (j + 2 < nT).astype(f32)
        dp_ref[:, 0:BW] = (k2[0].astype(f32) + m1 * k1[0].astype(f32) + m0 * k0[0].astype(f32)).astype(bf16)
        dp_ref[:, BW:2 * BW] = (v2[0].astype(f32) + m1 * v1[0].astype(f32) + m0 * v0[0].astype(f32)).astype(bf16)

    def spec(w):
        return BS((1, TQ, BW), functools.partial(lambda j, w: (w, jnp.minimum(j + 2 - w, nT - 1), 0), w=w))

    return pl.pallas_call(
        body, out_shape=SDS(dproj.shape, bf16), grid=(nT,),
        in_specs=[spec(2), spec(1), spec(0), spec(2), spec(1), spec(0), ANY],
        out_specs=BS((TQ, 2 * BW), lambda j: (j, creg)),
        input_output_aliases={6: 0}, compiler_params=_cp("parallel"), name=name)(dkp, dkp, dkp, dvp, dvp, dvp, dproj)


GQ, GV = 256, 512
TGC = 8


def _bd_mask():
    r = lax.broadcasted_iota(jnp.int32, (GQ, GV), 0) // 64
    c = lax.broadcasted_iota(jnp.int32, (GQ, GV), 1) // 128
    return (r == c).astype(f32)


def _tri(strict):
    r = lax.broadcasted_iota(jnp.int32, (CHUNK, CHUNK), 0)
    c = lax.broadcasted_iota(jnp.int32, (CHUNK, CHUNK), 1)
    return ((c < r) if strict else (c <= r)).astype(f32)


def _compact(s_bd):
    return jnp.concatenate([s_bd[h * 64:(h + 1) * 64, h * 128:(h + 1) * 128] for h in range(4)], axis=0)


def _expand(comp, mask):
    return jnp.tile(comp, (1, 4)) * mask


def _gla_gates(alr, wa_ref, ba_ref, tri_incl, ones_col):
    a = jnp.dot(alr, wa_ref[...], preferred_element_type=f32) + ba_ref[...]
    la = (jnp.minimum(a, 0.0) - jnp.log(1.0 + jnp.exp(-jnp.abs(a)))) * (1.0 / 16.0)
    cum = jnp.dot(tri_incl, la, precision=HI, preferred_element_type=f32)
    tot_row = cum[CHUNK - 1:CHUNK, :]
    tot_col = lax.dot_general(la, ones_col, (((0,), (0,)), ((), ())), precision=HI, preferred_element_type=f32)
    return a, cum, tot_row, jnp.tile(jnp.exp(tot_col), (1, 4))


def _head_norm(o):
    rns, ons = [], []
    for h in range(4):
        oh = o[:, h * 128:(h + 1) * 128]
        rn = lax.rsqrt(jnp.mean(oh * oh, axis=-1, keepdims=True) + EPS)
        rns.append(rn)
        ons.append(oh * rn)
    return rns, ons


def _gla_in_specs(T, imap):
    cq, ck = OUR_COLS["g_q"][0] // GQ, OUR_COLS["g_k"][0] // GQ
    cv, cr = OUR_COLS["g_v"][0] // GV, OUR_COLS["g_r"][0] // GV
    return [BS((T, GQ), lambda i: (imap(i), cq)), BS((T, GQ), lambda i: (imap(i), ck)), BS((T, GV), lambda i: (imap(i), cv)),
            BS((T, GV), lambda i: (imap(i), cr)), BS((T, RANKP), lambda i: (imap(i), 0))]


def gla_fwd(proj, pa, wa, ba, ng, name):
    S = proj.shape[0]
    T = min(TGC * CHUNK, S)
    nch = T // CHUNK

    def body(q_ref, k_ref, v_ref, r_ref, a_ref, wa_ref, ba_ref, ng_ref, y_ref, st_ref, s_scr):
        @pl.when(pl.program_id(0) == 0)
        def _():
            s_scr[...] = jnp.zeros_like(s_scr)

        mask = _bd_mask()
        tri = _tri(False)
        ones_col = jnp.ones((CHUNK, 128), f32)

        def chunk(ci, carry):
            rows = pl.ds(pl.multiple_of(ci * CHUNK, CHUNK), CHUNK)
            _, cum, tot_row, dec4 = _gla_gates(a_ref[rows, :], wa_ref, ba_ref, tri, ones_col)
            kd = (k_ref[rows, :].astype(f32) * jnp.exp(tot_row - cum)).astype(bf16)
            upd = lax.dot_general(kd, v_ref[rows, :], (((0,), (0,)), ((), ())), preferred_element_type=f32) * mask
            s_new = dec4 * s_scr[...] + upd
            s_scr[...] = s_new
            st_ref[pl.ds(pl.multiple_of(ci * GQ, GQ), GQ), :] = _compact(s_new)
            qs = (q_ref[rows, :].astype(f32) * 0.125).astype(bf16)
            o = jnp.dot(qs, s_new.astype(bf16), preferred_element_type=f32)
            _, ons = _head_norm(o)
            rv = r_ref[rows, :].astype(f32)
            y_ref[rows, :] = (jnp.concatenate(ons, axis=1) * ng_ref[...] * (rv * jax.nn.sigmoid(rv))).astype(bf16)
            return carry

        lax.fori_loop(0, nch, chunk, 0)

    return pl.pallas_call(
        body, out_shape=(SDS((S, GV), bf16), SDS((S // CHUNK * GQ, 128), f32)), grid=(S // T,),
        in_specs=_gla_in_specs(T, lambda i: i) + [BS((RANKP, GQ), lambda i: (0, 0)), BS((1, GQ), lambda i: (0, 0)),
                                                  BS((1, GV), lambda i: (0, 0))],
        out_specs=(BS((T, GV), lambda i: (i, 0)), BS((nch * GQ, 128), lambda i: (i, 0))),
        scratch_shapes=[pltpu.VMEM((GQ, GV), f32)], compiler_params=_cp("arbitrary"), name=name)(
            proj, proj, proj, proj, pa, wa, ba, ng)


def gla_bwd(proj, pa, states, dy, wa, ba, ng, dproj, name):
    S = proj.shape[0]
    T = min(TGC * CHUNK, S)
    nch = T // CHUNK
    nT = S // T
    rev = lambda i: nT - 1 - i

    def body(q_ref, k_ref, v_ref, r_ref, a_ref, st_ref, sp_ref, dy_ref, wa_ref, ba_ref, ng_ref, dp_in,
             dp_ref, da_ref, dwa_ref, dba_ref, dng_ref, g_scr):
        del dp_in
        i = pl.program_id(0)

        @pl.when(i == 0)
        def _():
            g_scr[...] = jnp.zeros_like(g_scr)
            dwa_ref[...] = jnp.zeros_like(dwa_ref)
            dba_ref[...] = jnp.zeros_like(dba_ref)
            dng_ref[...] = jnp.zeros_like(dng_ref)

        mask = _bd_mask()
        tri = _tri(False)
        tri_strict = _tri(True)
        ones_col = jnp.ones((CHUNK, 128), f32)
        ones_row = jnp.ones((8, GV), f32)
        first_tile = (i == nT - 1).astype(f32)

        def chunk(t, carry):
            ci = nch - 1 - t
            rows = pl.ds(pl.multiple_of(ci * CHUNK, CHUNK), CHUNK)
            alr = a_ref[rows, :]
            a, cum, tot_row, dec4 = _gla_gates(alr, wa_ref, ba_ref, tri, ones_col)
            wdec = jnp.exp(tot_row - cum)
            kdf = k_ref[rows, :].astype(f32) * wdec
            kd = kdf.astype(bf16)
            s_c = _expand(st_ref[pl.ds(pl.multiple_of(ci * GQ, GQ), GQ), :], mask)
            prev_in = st_ref[pl.ds(pl.multiple_of(jnp.maximum(ci - 1, 0) * GQ, GQ), GQ), :]
            prev = jnp.where(ci > 0, prev_in, sp_ref[...] * (1.0 - first_tile))
            s_prev = _expand(prev, mask)
            qs = (q_ref[rows, :].astype(f32) * 0.125).astype(bf16)
            s_cb = s_c.astype(bf16)
            o = jnp.dot(qs, s_cb, preferred_element_type=f32)
            rns, ons = _head_norm(o)
            on = jnp.concatenate(ons, axis=1)
            rv = r_ref[rows, :].astype(f32)
            sg = jax.nn.sigmoid(rv)
            sr = rv * sg
            dyv = dy_ref[rows, :].astype(f32)
            ngv = ng_ref[...]
            dng_ref[...] += jnp.sum(dyv * on * sr, axis=0, keepdims=True)
            d_on = dyv * ngv * sr
            dr = dyv * on * ngv * (sg * (1.0 + rv * (1.0 - sg)))
            dos = []
            for h in range(4):
                cols = slice(h * 128, (h + 1) * 128)
                dh_ = d_on[:, cols]
                dos.append(rns[h] * (dh_ - ons[h] * jnp.mean(dh_ * ons[h], axis=-1, keepdims=True)))
            do = jnp.concatenate(dos, axis=1).astype(bf16)
            dq = lax.dot_general(do, s_cb, (((1,), (1,)), ((), ())), preferred_element_type=f32) * 0.125
            ds = lax.dot_general(qs, do, (((0,), (0,)), ((), ())), preferred_element_type=f32) * mask + g_scr[...]
            ddec_row = lax.dot_general(ones_row, ds * s_prev, (((1,), (1,)), ((), ())), precision=HI,
                                       preferred_element_type=f32)[0:1, :]
            dsb = ds.astype(bf16)
            dkd = lax.dot_general(v_ref[rows, :], dsb, (((1,), (1,)), ((), ())), preferred_element_type=f32)
            dv = jnp.dot(kd, dsb, preferred_element_type=f32)
            g_scr[...] = dec4 * ds
            dk = dkd * wdec
            dwlog = dkd * kdf
            dla = ddec_row * jnp.exp(tot_row) + jnp.dot(tri_strict, dwlog, precision=HI, preferred_element_type=f32)
            da = dla * (1.0 - jax.nn.sigmoid(a)) * (1.0 / 16.0)
            dab = da.astype(bf16)
            da_ref[rows, :] = lax.dot_general(dab, wa_ref[...], (((1,), (1,)), ((), ())),
                                              preferred_element_type=f32).astype(bf16)
            dwa_ref[...] += lax.dot_general(alr, dab, (((0,), (0,)), ((), ())), preferred_element_type=f32)
            dba_ref[...] += jnp.sum(da, axis=0, keepdims=True)
            dp_ref[rows, 0:GQ] = dq.astype(bf16)
            dp_ref[rows, GQ:2 * GQ] = dk.astype(bf16)
            dp_ref[rows, 2 * GQ:2 * GQ + GV] = dv.astype(bf16)
            dp_ref[rows, 2 * GQ + GV:2 * GQ + 2 * GV] = dr.astype(bf16)
            return carry

        lax.fori_loop(0, nch, chunk, 0)

    REG = 2 * GQ + 2 * GV
    return pl.pallas_call(
        body,
        out_shape=(SDS((S, NP), bf16), SDS((S, RANKP), bf16), SDS((RANKP, GQ), f32), SDS((1, GQ), f32), SDS((1, GV), f32)),
        grid=(nT,),
        in_specs=_gla_in_specs(T, rev) + [
            BS((nch * GQ, 128), lambda i: (rev(i), 0)),
            BS((GQ, 128), lambda i: (jnp.maximum(rev(i) * nch - 1, 0), 0)),
            BS((T, GV), lambda i: (rev(i), 1)),
            BS((RANKP, GQ), lambda i: (0, 0)), BS((1, GQ), lambda i: (0, 0)), BS((1, GV), lambda i: (0, 0)), ANY],
        out_specs=(BS((T, REG), lambda i: (rev(i), 0)), BS((T, RANKP), lambda i: (rev(i), 0)),
                   BS((RANKP, GQ), lambda i: (0, 0)), BS((1, GQ), lambda i: (0, 0)), BS((1, GV), lambda i: (0, 0))),
        scratch_shapes=[pltpu.VMEM((GQ, GV), f32)],
        input_output_aliases={11: 0}, compiler_params=_cp("arbitrary"), name=name)(
            proj, proj, proj, proj, pa, states, states, dy, wa, ba, ng, dproj)


def _as2d(a):
    if a.ndim == 1:
        return a.reshape(1, a.shape[0])
    return a.reshape(-1, a.shape[-1])


def adamw(w, g, m, v, name):
    shape = w.shape
    w2, g2, m2, v2 = (_as2d(a) for a in (w, g, m, v))
    R, C = w2.shape
    tr = R
    for cand in (512, 256, 128, 64, 32, 16, 8):
        if R % cand == 0 and cand * C * 4 * 7 * 2 <= 40 * 1024 * 1024:
            tr = cand
            break

    def body(w_ref, g_ref, m_ref, v_ref, d_ref, mo_ref, vo_ref):
        gv = g_ref[...]
        mn = ADAM_B1 * m_ref[...] + (1.0 - ADAM_B1) * gv
        vn = ADAM_B2 * v_ref[...] + (1.0 - ADAM_B2) * (gv * gv)
        m_hat = mn / (1.0 - ADAM_B1 ** ADAM_STEP)
        v_hat = vn / (1.0 - ADAM_B2 ** ADAM_STEP)
        d_ref[...] = -ADAM_LR * (m_hat / (jnp.sqrt(v_hat) + ADAM_EPS) + ADAM_WD * w_ref[...])
        mo_ref[...] = mn
        vo_ref[...] = vn

    blk = BS((tr, C), lambda i: (i, 0))
    outs = pl.pallas_call(body, out_shape=tuple(SDS((R, C), f32) for _ in range(3)), grid=(R // tr,),
                          in_specs=[blk] * 4, out_specs=(blk,) * 3, compiler_params=_cp("parallel"), name=name)(w2, g2, m2, v2)
    return tuple(o.reshape(shape) for o in outs)


def add_halves(gp, ra, c, name):
    _, nchip, R, W = gp.shape
    tr = R // 11

    grid_spec = pltpu.PrefetchScalarGridSpec(
        num_scalar_prefetch=1, grid=(nchip, R // tr),
        in_specs=[BS((1, 1, tr, W), lambda j, i, c_ref: (c_ref[0], j, i, 0)), BS((1, tr, W), lambda j, i, c_ref: (j, i, 0))],
        out_specs=BS((1, tr, W), lambda j, i, c_ref: (j, i, 0)))

    def body(c_ref, a_ref, b_ref, o_ref):
        del c_ref
        o_ref[0] = (a_ref[0, 0].astype(f32) + b_ref[0].astype(f32)).astype(bf16)

    return pl.pallas_call(body, out_shape=SDS((nchip, R, W), bf16), grid_spec=grid_spec,
                          compiler_params=_cp("parallel", "parallel"), name=name)(jnp.reshape(c, (1,)).astype(jnp.int32), gp, ra)


def sum_slots(x, name, tr):
    N, R, W = x.shape

    def body(x_ref, o_ref):
        acc = x_ref[0].astype(f32)
        for n in range(1, N):
            acc = acc + x_ref[n].astype(f32)
        o_ref[...] = acc

    return pl.pallas_call(body, out_shape=SDS((R, W), f32), grid=(R // tr,),
                          in_specs=[BS((N, tr, W), lambda i: (0, i, 0))], out_specs=BS((tr, W), lambda i: (i, 0)),
                          compiler_params=_cp("parallel"), name=name)(x)


def _me():
    return lax.axis_index("x"), lax.axis_index("y"), lax.axis_index("c")


def _rcopy(src, dst, send_sems, recv_sems, k, dev):
    return pltpu.make_async_remote_copy(src_ref=src, dst_ref=dst, send_sem=send_sems.at[k], recv_sem=recv_sems.at[k],
                                        device_id=dev, device_id_type=MESH)


def ag4(buf, name):
    _, R, W = buf.shape

    def body(x_ref, o_ref, send_sems, recv_sems, local_sem):
        x, y, c = _me()
        j = 2 * x + y
        sib = (x, y, 1 - c)
        chips = [(1 - x, y), (x, 1 - y), (1 - x, 1 - y)]
        mine = pltpu.make_async_copy(x_ref, o_ref.at[j], local_sem)
        mine.start()
        first = [_rcopy(x_ref.at[c], o_ref.at[j, c], send_sems, recv_sems, k, (cx, cy, c)) for k, (cx, cy) in enumerate(chips)]
        for cp in first:
            cp.start()
        passed = []
        for k, (cx, cy) in enumerate(chips):
            land = o_ref.at[2 * cx + cy, c]
            _rcopy(land, land, send_sems, recv_sems, k, (x, y, c)).wait_recv()
            fwd = _rcopy(land, land, send_sems, recv_sems, 3 + k, sib)
            fwd.start()
            passed.append(fwd)
        for k, (cx, cy) in enumerate(chips):
            land = o_ref.at[2 * cx + cy, 1 - c]
            _rcopy(land, land, send_sems, recv_sems, 3 + k, (x, y, c)).wait_recv()
        for cp in first + passed:
            cp.wait_send()
        mine.wait()

    return pl.pallas_call(
        body, out_shape=SDS((4, 2, R, W), buf.dtype), in_specs=[ANY], out_specs=ANY,
        scratch_shapes=[pltpu.SemaphoreType.DMA((6,)), pltpu.SemaphoreType.DMA((6,)), pltpu.SemaphoreType.DMA],
        name=name)(buf)


def sib_other_half(gp, name):
    def body(x_ref, o_ref, send_sems, recv_sems):
        x, y, c = _me()
        cp = _rcopy(x_ref.at[1 - c], o_ref, send_sems, recv_sems, 0, (x, y, 1 - c))
        cp.start()
        cp.wait()

    return pl.pallas_call(
        body, out_shape=SDS(gp.shape[1:], gp.dtype), in_specs=[ANY], out_specs=ANY,
        scratch_shapes=[pltpu.SemaphoreType.DMA((1,)), pltpu.SemaphoreType.DMA((1,))], name=name)(gp)


def a2a4(p, name):
    def body(x_ref, o_ref, send_sems, recv_sems, local_sem):
        x, y, c = _me()
        j = 2 * x + y
        chips = [(1 - x, y), (x, 1 - y), (1 - x, 1 - y)]
        mine = pltpu.make_async_copy(x_ref.at[j], o_ref.at[j], local_sem)
        mine.start()
        sends = [_rcopy(x_ref.at[2 * cx + cy], o_ref.at[j], send_sems, recv_sems, k, (cx, cy, c))
                 for k, (cx, cy) in enumerate(chips)]
        for cp in sends:
            cp.start()
        for k, (cx, cy) in enumerate(chips):
            land = o_ref.at[2 * cx + cy]
            _rcopy(land, land, send_sems, recv_sems, k, (x, y, c)).wait_recv()
        for cp in sends:
            cp.wait_send()
        mine.wait()

    return pl.pallas_call(
        body, out_shape=SDS(p.shape, p.dtype), in_specs=[ANY], out_specs=ANY,
        scratch_shapes=[pltpu.SemaphoreType.DMA((3,)), pltpu.SemaphoreType.DMA((3,)), pltpu.SemaphoreType.DMA],
        name=name)(p)


def ag2(r, name):
    def body(x_ref, o_ref, send_sems, recv_sems, local_sem):
        x, y, c = _me()
        mine = pltpu.make_async_copy(x_ref, o_ref.at[c], local_sem)
        mine.start()
        cp = _rcopy(x_ref, o_ref.at[c], send_sems, recv_sems, 0, (x, y, 1 - c))
        cp.start()
        land = o_ref.at[1 - c]
        _rcopy(land, land, send_sems, recv_sems, 0, (x, y, c)).wait_recv()
        cp.wait_send()
        mine.wait()

    return pl.pallas_call(
        body, out_shape=SDS((2,) + r.shape, r.dtype), in_specs=[ANY], out_specs=ANY,
        scratch_shapes=[pltpu.SemaphoreType.DMA((1,)), pltpu.SemaphoreType.DMA((1,)), pltpu.SemaphoreType.DMA],
        name=name)(r)


def ag8(blk, name):
    m_per, n = blk.shape

    def body(x_ref, out_ref, send_sems, recv_sems, local_sem):
        x, y, c = _me()
        me, sibling = (x, y, c), (x, y, 1 - c)
        chips = [(1 - x, y), (x, 1 - y), (1 - x, 1 - y)]

        def rows(px, py, pc):
            return out_ref.at[pl.ds((4 * px + 2 * py + pc) * m_per, m_per), :]

        def copy(k, block, to, src=None):
            return pltpu.make_async_remote_copy(
                src_ref=rows(*block) if src is None else src, dst_ref=rows(*block), send_sem=send_sems.at[k],
                recv_sem=recv_sems.at[k], device_id=to, device_id_type=MESH)

        mine = pltpu.make_async_copy(x_ref, rows(*me), local_sem)
        mine.start()
        first = [copy(0, me, sibling, src=x_ref)]
        first += [copy(1 + j, me, (*chip, c), src=x_ref) for j, chip in enumerate(chips)]
        for cp in first:
            cp.start()
        passed = [copy(4 + j, (*chip, c), sibling) for j, chip in enumerate(chips)]
        for j, chip in enumerate(chips):
            copy(1 + j, (*chip, c), me).wait_recv()
            passed[j].start()
        copy(0, sibling, me).wait_recv()
        for j, chip in enumerate(chips):
            copy(4 + j, (*chip, 1 - c), me).wait_recv()
        for cp in first + passed:
            cp.wait_send()
        mine.wait()

    return pl.pallas_call(
        body, out_shape=SDS((8 * m_per, n), blk.dtype), in_specs=[pl.BlockSpec(memory_space=pltpu.VMEM)],
        out_specs=pl.BlockSpec(memory_space=pltpu.VMEM),
        scratch_shapes=[pltpu.SemaphoreType.DMA((7,)), pltpu.SemaphoreType.DMA((7,)), pltpu.SemaphoreType.DMA],
        name=name)(blk)


def _split_shards(full, axis):
    ax = axis + 1
    n = full.shape[ax] // 4
    parts = full.reshape(full.shape[:ax] + (4, n) + full.shape[ax + 1:])
    return jnp.moveaxis(parts, ax, 0)


def _merge_shards(gathered, axis):
    ax = axis + 1
    parts = jnp.moveaxis(gathered, 0, ax)
    return parts.reshape(parts.shape[:ax] + (parts.shape[ax] * parts.shape[ax + 1],) + parts.shape[ax + 2:])


def _shard_shape(name):
    full, axis = SHARDED[name]
    s = list(full)
    s[axis] //= 4
    return (2,) + tuple(s)


def _pack(parts, names, rows):
    lead = parts[names[0]].shape[:parts[names[0]].ndim - len(_shard_shape(names[0]))]
    flat = jnp.concatenate([parts[n].reshape(lead + (-1,)) for n in names], axis=-1)
    pad = rows * PACK_W - flat.shape[-1]
    flat = jnp.pad(flat, [(0, 0)] * len(lead) + [(0, pad)])
    return flat.reshape(lead + (rows, PACK_W))


def _unpack(packed, names):
    lead = packed.shape[:-2]
    flat = packed.reshape(lead + (-1,))
    out, off = {}, 0
    for n in names:
        shp = _shard_shape(n)
        size = int(np.prod(shp))
        out[n] = lax.slice_in_dim(flat, off, off + size, axis=len(lead)).reshape(lead + shp)
        off += size
    return out


def _to_ref_cols(main, rank):
    pieces = []
    for n, width in REF_SPLITS:
        if n == "g_a":
            pieces.append(rank[..., :RANK])
        else:
            off = OUR_COLS[n][0]
            pieces.append(main[..., off:off + width])
    return jnp.concatenate(pieces, axis=-1)


def _from_ref_cols(w):
    offs, o = {}, 0
    for n, width in REF_SPLITS:
        offs[n] = (o, width)
        o += width
    main = jnp.concatenate([w[..., offs[n][0]:offs[n][0] + offs[n][1]] for n in sorted(OUR_COLS, key=lambda k: OUR_COLS[k][0])],
                           axis=-1)
    ro = offs["g_a"][0]
    rank = jnp.pad(w[..., ro:ro + RANK], [(0, 0)] * (w.ndim - 1) + [(0, RANKP - RANK)])
    return main, rank


def _layer_fwd(h, p_i, W, li):
    t = f"l{li}_"
    sv = {"h0": h}
    xn = rms_fwd(h, W["norm1_g"], t + "rms1")
    proj = mm_nn(xn, W["w_in_main"], name=t + "inproj")
    pa = mm_nn(xn, W["w_in_rank"], name=t + "inproj_rank")
    ya = sg_fwd(proj, W["sg_ln_g"], W["sg_ln_b"], W["sg_wm"], W["sg_bsb"], t + "sg_fwd")
    yb, states = gla_fwd(proj, pa, W["gla_wa"], W["gla_b_a"], W["gla_norm_g"], t + "gla_fwd")
    yc = att_fwd(proj, W["att_bias"], t + "att_fwd")
    yd = conv_fwd(proj, W["conv_dw_w"], W["conv_dw_b"], W["conv_ln_g"], W["conv_ln_b"], t + "conv_fwd")
    y = jnp.concatenate([ya, yb, yc, yd], axis=1)
    gate = mm_nn(xn, W["w_gate_all"], bias=W["b_gate_all"], act="sigmoid", name=t + "gate")
    z = mm_nn(y, W["w_branch"], name=t + "branch")
    m = gate_merge_fwd(gate, z, t + "merge")
    h1 = mm_nn(m, W["w_out"], res=h, out_dtype=f32, name=t + "outproj")
    hn = rms_fwd(h1, W["norm2_g"], t + "rms2")
    a = mm_nn(hn, W["w_ff1"], name=t + "ff1")
    h2 = mm_nn(a, W["w_ff2"], pre="relu2", res=h1, out_dtype=f32, name=t + "ff2")
    hg = rms_fwd(h2, W["norm3_g"], t + "rms3")
    pg = mm_nn(hg, W["w_ple_gate"], bias=W["b_ple_gate"], act="sigmoid", name=t + "ple_gate")
    e = mm_nn(p_i, W["w_ple"], name=t + "ple_emb")
    h3 = mm_nn(p_i, W["w_ple"], mul=pg, res=h2, out_dtype=f32, name=t + "ple_out")
    sv.update(xn=xn, proj=proj, pa=pa, states=states, y=y, gate=gate, z=z, m=m, h1=h1, hn=hn, a=a, h2=h2, hg=hg, pg=pg, e=e)
    return h3, sv


def _layer_bwd(dh3, sv, p_i, W, li):
    t = f"l{li}_b_"
    G = {}
    dpg, de, G["b_ple_gate"] = ple_bwd_ew(dh3, sv["e"], sv["pg"], t + "ple_ew")
    G["w_ple_gate"] = mm_tn(sv["hg"], dpg, name=t + "dw_ple_gate")[0]
    G["w_ple"] = mm_tn(p_i, de, name=t + "dw_ple")[0]
    dhg = mm_nt(dpg, W["w_ple_gate"], name=t + "dhg")
    dh2, G["norm3_g"] = rms_bwd(dhg, sv["h2"], W["norm3_g"], dh3, t + "rms3")
    da = mm_nt(dh2, W["w_ff2"], post_a=sv["a"], out_dtype=bf16, name=t + "da")
    G["w_ff2"] = mm_tn(sv["a"], dh2, pre="relu2", name=t + "dw_ff2")[0]
    G["w_ff1"] = mm_tn(sv["hn"], da, name=t + "dw_ff1")[0]
    dhn = mm_nt(da, W["w_ff1"], name=t + "dhn")
    dh1, G["norm2_g"] = rms_bwd(dhn, sv["h1"], W["norm2_g"], dh2, t + "rms2")
    dm = mm_nt(dh1, W["w_out"], out_dtype=bf16, name=t + "dm")
    G["w_out"] = mm_tn(sv["m"], dh1, name=t + "dw_out")[0]
    dz, dgp, G["b_gate_all"] = gate_merge_bwd(dm, sv["gate"], sv["z"], t + "merge")
    G["w_branch"] = mm_tn(sv["y"], dz, G=4, name=t + "dw_branch")
    dy = mm_nt(dz, W["w_branch"], tk=512, out_dtype=bf16, name=t + "dy")
    G["w_gate_all"] = mm_tn(sv["xn"], dgp, name=t + "dw_gate")[0]
    dxn = mm_nt(dgp, W["w_gate_all"], name=t + "dxn_gate")
    proj = sv["proj"]
    dproj, dwm, dbs, G["sg_ln_g"], G["sg_ln_b"] = sg_bwd(proj, dy, W["sg_ln_g"], W["sg_ln_b"], W["sg_wm"], W["sg_bsb"],
                                                          W["sg_maskf"], t + "sg")
    G["sg_w"], G["sg_b"] = dwm, dbs[:, :, 0]
    dproj, dpa, dwa, G["gla_b_a"], G["gla_norm_g"] = gla_bwd(proj, sv["pa"], sv["states"], dy, W["gla_wa"], W["gla_b_a"],
                                                             W["gla_norm_g"], dproj, t + "gla")
    G["gla_w_a2"] = dwa[:RANK]
    dproj, dkp, dvp, dbias = att_bwd(proj, sv["y"], dy, W["att_bias"], dproj, t + "att")
    dproj = att_shift_add(dkp, dvp, dproj, t + "att_kv")
    G["att_rel_bias"] = att_bias_grad(dbias, t + "att_bias")
    dz_c, G["conv_ln_g"], G["conv_ln_b"], G["conv_dw_b"] = conv_bwd_norm(proj, dy, W["conv_dw_w"], W["conv_dw_b"],
                                                                        W["conv_ln_g"], W["conv_ln_b"], t + "conv_norm")
    dproj, G["conv_dw_w"] = conv_bwd_taps(proj, dz_c, W["conv_dw_w"], dproj, t + "conv_taps")
    G["w_in_main"] = mm_tn(sv["xn"], dproj, name=t + "dw_in")[0]
    G["w_in_rank"] = mm_tn(sv["xn"], dpa, name=t + "dw_in_rank")[0]
    dxn = mm_nt(dpa, W["w_in_rank"], res=dxn, name=t + "dxn_rank")
    dxn = mm_nt(dproj, W["w_in_main"], res=dxn, name=t + "dxn_main")
    dh0, G["norm1_g"] = rms_bwd(dxn, sv["h0"], W["norm1_g"], dh1, t + "rms1")
    return dh0, G


def _prep_layer_weights(full, small, repl, li):
    W = {}
    row = lambda a: a.reshape(1, -1)
    main, rank = _from_ref_cols(full["w_in"][li])
    W["w_in_main"], W["w_in_rank"] = main[None], rank[None]
    W["w_branch"] = full["w_branch"][li]
    W["w_gate_all"] = jnp.transpose(full["w_gate"][li], (1, 0, 2)).reshape(1, D, 4 * D)
    W["b_gate_all"] = small["b_gate"][li].reshape(1, 4 * D)
    for n in ("w_out", "w_ff1", "w_ff2", "w_ple_gate", "w_ple"):
        W[n] = full[n][li][None]
    for n in ("norm1_g", "norm2_g", "norm3_g", "b_ple_gate", "sg_ln_g", "sg_ln_b", "gla_b_a", "gla_norm_g", "conv_dw_b",
              "conv_ln_g", "conv_ln_b"):
        W[n] = row(repl[n][li])
    pos = np.arange(128)
    mask = (pos[None, :] // CHUNK) <= (pos[:, None] // CHUNK)
    W["sg_maskf"] = jnp.asarray(mask, f32)
    W["sg_wm"] = jnp.where(mask[None], repl["sg_w"][li], 0.0).astype(bf16)
    W["sg_bsb"] = jnp.broadcast_to(repl["sg_b"][li][:, :, None], (4, 128, 128))
    W["gla_wa"] = jnp.pad(small["gla_w_a2"][li], ((0, RANKP - RANK), (0, 0))).astype(bf16)
    W["att_bias"] = att_bias_build(small["att_rel_bias"][li], f"l{li}_att_bias")
    W["conv_dw_w"] = small["conv_dw_w"][li]
    return W


def _layer_grads_to_ref(G):
    out = {}
    out["w_in"] = _to_ref_cols(G["w_in_main"], G["w_in_rank"])
    out["w_gate"] = jnp.transpose(G["w_gate_all"].reshape(D, 4, D), (1, 0, 2))
    out["b_gate"] = G["b_gate_all"].reshape(4, D)
    for n in ("w_branch", "w_out", "w_ff1", "w_ff2", "w_ple_gate", "w_ple", "gla_w_a2", "att_rel_bias", "conv_dw_w", "sg_w",
              "sg_b"):
        out[n] = G[n]
    for n in ("norm1_g", "norm2_g", "norm3_g", "b_ple_gate", "sg_ln_g", "sg_ln_b", "gla_b_a", "gla_norm_g", "conv_dw_b",
              "conv_ln_g", "conv_ln_b"):
        out[n] = G[n].reshape(-1)
    return out


def kernel(x, p, norm1_g, w_in, sg_ln_g, sg_ln_b, sg_w, sg_b, gla_w_a2, gla_b_a, gla_norm_g, att_rel_bias, conv_dw_w, conv_dw_b, conv_ln_g, conv_ln_b, w_branch, w_gate, b_gate, w_out, norm2_g, w_ff1, w_ff2, norm3_g, w_ple_gate, b_ple_gate, w_ple, final_g, loss_target, m_norm1_g, m_w_in, m_sg_ln_g, m_sg_ln_b, m_sg_w, m_sg_b, m_gla_w_a2, m_gla_b_a, m_gla_norm_g, m_att_rel_bias, m_conv_dw_w, m_conv_dw_b, m_conv_ln_g, m_conv_ln_b, m_w_branch, m_w_gate, m_b_gate, m_w_out, m_norm2_g, m_w_ff1, m_w_ff2, m_norm3_g, m_w_ple_gate, m_b_ple_gate, m_w_ple, m_final_g, v_norm1_g, v_w_in, v_sg_ln_g, v_sg_ln_b, v_sg_w, v_sg_b, v_gla_w_a2, v_gla_b_a, v_gla_norm_g, v_att_rel_bias, v_conv_dw_w, v_conv_dw_b, v_conv_ln_g, v_conv_ln_b, v_w_branch, v_w_gate, v_b_gate, v_w_out, v_norm2_g, v_w_ff1, v_w_ff2, v_norm3_g, v_w_ple_gate, v_b_ple_gate, v_w_ple, v_final_g):
    args = dict(locals())
    weights = {n: args[n] for n in W_ORDER}
    moments_m = {n: args["m_" + n] for n in W_ORDER}
    moments_v = {n: args["v_" + n] for n in W_ORDER}
    c = lax.axis_index("c")
    sharded_names = BIG + SMALL

    wpk = _pack({n: weights[n] for n in sharded_names}, sharded_names, 2 * PACK_RH).astype(bf16)
    allw = ag4(wpk.reshape(2, PACK_RH, PACK_W), "ag_weights")
    full = {n: _merge_shards(a, SHARDED[n][1]) for n, a in _unpack(allw.reshape(4, 2 * PACK_RH, PACK_W), BIG).items()}
    spk = _pack({n: weights[n] for n in SMALL}, SMALL, 2 * SMALL_RH)
    alls = ag4(spk.reshape(2, SMALL_RH, PACK_W), "ag_small")
    small = {n: _merge_shards(a, SHARDED[n][1]) for n, a in _unpack(alls.reshape(4, 2 * SMALL_RH, PACK_W), SMALL).items()}
    repl = {n: weights[n] for n in REPL}

    h = x[0]
    Ws, saved = [], []
    for li in range(DEPTH):
        W = _prep_layer_weights(full, small, repl, li)
        h, sv = _layer_fwd(h, p[li, 0], W, li)
        Ws.append(W)
        saved.append(sv)
    loss_part, dh, dfinal = loss_head(h, final_g.reshape(1, D), loss_target[0], "loss_head")
    loss = lax.psum(loss_part[0, 0], ("x", "y", "c"))

    layer_grads = [None] * DEPTH
    for li in reversed(range(DEPTH)):
        dh, G = _layer_bwd(dh, saved[li], p[li, 0], Ws[li], li)
        layer_grads[li] = _layer_grads_to_ref(G)
    grad_x = dh[None]
    local = {n: jnp.stack([layer_grads[0][n], layer_grads[1][n]]) for n in layer_grads[0]}
    local["final_g"] = dfinal.reshape(D)

    parts = {n: _split_shards(local[n], SHARDED[n][1]) for n in sharded_names}
    gp = _pack(parts, sharded_names, 2 * PACK_RH).astype(bf16)
    gp = jnp.transpose(gp.reshape(4, 2, PACK_RH, PACK_W), (1, 0, 2, 3))
    ra = sib_other_half(gp, "rs_sibling_half")
    psum = add_halves(gp, ra, c, "rs_add_halves")
    rb = a2a4(psum, "rs_all_to_all")
    red = sum_slots(rb, "rs_sum_chips", PACK_RH // 11)
    gsh = _unpack(ag2(red, "rs_sibling_gather").reshape(2 * PACK_RH, PACK_W), sharded_names)

    rnames = tuple(REPL)
    rflat = jnp.concatenate([local[n].reshape(-1) for n in rnames])
    rflat = jnp.pad(rflat, (0, REPL_ROWS * PACK_W - rflat.shape[0])).reshape(REPL_ROWS, PACK_W)
    rall = ag8(rflat, "ar_gather").reshape(8, REPL_ROWS, PACK_W)
    rsum = sum_slots(rall, "ar_sum", REPL_ROWS).reshape(-1)
    grads, off = dict(gsh), 0
    for n in rnames:
        size = int(np.prod(REPL[n]))
        grads[n] = rsum[off:off + size].reshape(REPL[n])
        off += size

    deltas, new_m, new_v = {}, {}, {}
    for n in W_ORDER:
        deltas[n], new_m[n], new_v[n] = adamw(weights[n], grads[n], moments_m[n], moments_v[n], "adamw_" + n)
    return (loss, grad_x, *[grads[n] for n in W_ORDER], *[deltas[n] for n in W_ORDER], *[new_m[n] for n in W_ORDER],
            *[new_v[n] for n in W_ORDER])
```

```python
import functools

import jax
import jax.numpy as jnp
import numpy as np
from jax import lax
from jax.experimental import pallas as pl
from jax.experimental.pallas import tpu as pltpu

f32, bf16 = jnp.float32, jnp.bfloat16
HI = lax.Precision.HIGHEST
MESH = pl.DeviceIdType.MESH
SDS = jax.ShapeDtypeStruct
BS = pl.BlockSpec
ANY = pl.BlockSpec(memory_space=pl.ANY)

D = 1024
DEPTH = 2
CHUNK = 64
BW = 512
NP = 5120
RANK = 16
RANKP = 128
DFF = 4096
PLE = 256
CONV_K = 31
HALO = 32
TQ = 256
WIN = 768
REL_TABLE = 320
EPS = 1e-6
NEG_INF = -1e30
VMEM_LIMIT = 56 * 1024 * 1024

ADAM_LR, ADAM_B1, ADAM_B2, ADAM_EPS, ADAM_WD, ADAM_STEP = 0.001, 0.9, 0.999, 1e-08, 0.01, 10

OUR_COLS = dict(g_q=(0, 256), g_k=(256, 256), g_v=(512, 512), g_r=(1024, 512), a_q=(1536, 512), a_k=(2048, 512),
                a_v=(2560, 512), sg_u=(3072, 512), sg_v=(3584, 512), c_a=(4096, 512), c_g=(4608, 512))
REF_SPLITS = (("sg_u", 512), ("sg_v", 512), ("g_q", 256), ("g_k", 256), ("g_v", 512), ("g_r", 512), ("g_a", 16),
              ("a_q", 512), ("a_k", 512), ("a_v", 512), ("c_a", 512), ("c_g", 512))

SHARDED = dict(w_in=((1024, 5136), 1), w_branch=((4, 512, 1024), 2), w_gate=((4, 1024, 1024), 1), w_out=((1024, 1024), 0),
               w_ff1=((1024, 4096), 1), w_ff2=((4096, 1024), 0), w_ple_gate=((1024, 1024), 0), w_ple=((256, 1024), 1),
               gla_w_a2=((16, 256), 1), att_rel_bias=((8, 320), 1), conv_dw_w=((31, 512), 1), b_gate=((4, 1024), 1))
BIG = ("w_in", "w_branch", "w_gate", "w_out", "w_ff1", "w_ff2", "w_ple_gate", "w_ple")
SMALL = ("gla_w_a2", "att_rel_bias", "conv_dw_w", "b_gate")
REPL = dict(norm1_g=(2, 1024), sg_ln_g=(2, 512), sg_ln_b=(2, 512), sg_w=(2, 4, 128, 128), sg_b=(2, 4, 128), gla_b_a=(2, 256),
            gla_norm_g=(2, 512), conv_dw_b=(2, 512), conv_ln_g=(2, 512), conv_ln_b=(2, 512), norm2_g=(2, 1024),
            norm3_g=(2, 1024), b_ple_gate=(2, 1024), final_g=(1024,))
W_ORDER = ['norm1_g', 'w_in', 'sg_ln_g', 'sg_ln_b', 'sg_w', 'sg_b', 'gla_w_a2', 'gla_b_a', 'gla_norm_g', 'att_rel_bias',
           'conv_dw_w', 'conv_dw_b', 'conv_ln_g', 'conv_ln_b', 'w_branch', 'w_gate', 'b_gate', 'w_out', 'norm2_g', 'w_ff1',
           'w_ff2', 'norm3_g', 'w_ple_gate', 'b_ple_gate', 'w_ple', 'final_g']
PACK_W = 1024
REPL_ROWS = 200
TM = 1024


def _tile(s):
    return 512 if s % 512 == 0 else s


def _cp(*sem):
    return pltpu.CompilerParams(dimension_semantics=sem, vmem_limit_bytes=VMEM_LIMIT)


def rms_fwd(h, g, name):
    S, Dm = h.shape
    T = _tile(S)

    def body(h_ref, g_ref, o_ref):
        x = h_ref[...]
        r = lax.rsqrt(jnp.mean(x * x, axis=-1, keepdims=True) + EPS)
        o_ref[...] = (x * r * g_ref[...]).astype(bf16)

    return pl.pallas_call(
        body, out_shape=SDS((S, Dm), bf16), grid=(S // T,),
        in_specs=[BS((T, Dm), lambda i: (i, 0)), BS((1, Dm), lambda i: (0, 0))],
        out_specs=BS((T, Dm), lambda i: (i, 0)), compiler_params=_cp("parallel"), name=name)(h, g)


def mm_nn(x, w, *, name, bias=None, act=None, pre=None, mul=None, res=None, out_dtype=bf16, raw_out=False):
    S = x.shape[0]
    G, K, N = w.shape
    T = min(TM, S)
    tn = min(1024 if K <= 1024 else 512, N)
    nj = N // tn
    extras = [a for a in (bias, mul, res) if a is not None]

    def body(*refs):
        it = iter(refs)
        x_ref, w_ref = next(it), next(it)
        b_ref = next(it) if bias is not None else None
        m_ref = next(it) if mul is not None else None
        r_ref = next(it) if res is not None else None
        o_ref = next(it)
        xv = x_ref[...]
        if pre == "relu2":
            xf = jnp.maximum(xv.astype(f32), 0.0)
            xv = xf * xf
        acc = jnp.dot(xv.astype(bf16), w_ref[0], preferred_element_type=f32)
        if raw_out:
            next(it)[...] = acc.astype(bf16)
        if b_ref is not None:
            acc = acc + b_ref[...]
        if act == "sigmoid":
            acc = jax.nn.sigmoid(acc)
        if m_ref is not None:
            acc = acc * m_ref[...].astype(f32)
        if r_ref is not None:
            acc = r_ref[...].astype(f32) + acc
        o_ref[...] = acc.astype(out_dtype)

    in_specs = [BS((T, K), lambda i, g, j: (i, g)), BS((1, K, tn), lambda i, g, j: (g, 0, j))]
    if bias is not None:
        in_specs.append(BS((1, tn), lambda i, g, j: (0, g * nj + j)))
    for a in (mul, res):
        if a is not None:
            in_specs.append(BS((T, tn), lambda i, g, j: (i, g * nj + j)))
    ospec = BS((T, tn), lambda i, g, j: (i, g * nj + j))
    out_shape = SDS((S, G * N), out_dtype)
    if raw_out:
        out_shape, ospec = (out_shape, SDS((S, G * N), bf16)), (ospec, ospec)
    return pl.pallas_call(
        body, out_shape=out_shape, grid=(S // T, G, nj), in_specs=in_specs, out_specs=ospec,
        compiler_params=_cp("parallel", "parallel", "parallel"), name=name)(x, w, *extras)


def mm_nt(dy, w, *, name, res=None, post_a=None, out_dtype=f32):
    S = dy.shape[0]
    G, K, N = w.shape
    T = min(TM, S)
    tk = min(1024 if N <= 1024 else 512, K)
    nk = K // tk
    extras = [a for a in (res, post_a) if a is not None]

    def body(*refs):
        it = iter(refs)
        d_ref, w_ref = next(it), next(it)
        r_ref = next(it) if res is not None else None
        a_ref = next(it) if post_a is not None else None
        o_ref = next(it)
        acc = lax.dot_general(d_ref[...].astype(bf16), w_ref[0], (((1,), (1,)), ((), ())), preferred_element_type=f32)
        if r_ref is not None:
            acc = acc + r_ref[...].astype(f32)
        if a_ref is not None:
            acc = acc * (2.0 * jnp.maximum(a_ref[...].astype(f32), 0.0))
        o_ref[...] = acc.astype(out_dtype)

    in_specs = [BS((T, N), lambda i, g, j: (i, g)), BS((1, tk, N), lambda i, g, j: (g, j, 0))]
    for a in extras:
        in_specs.append(BS((T, tk), lambda i, g, j: (i, g * nk + j)))
    return pl.pallas_call(
        body, out_shape=SDS((S, G * K), out_dtype), grid=(S // T, G, nk), in_specs=in_specs,
        out_specs=BS((T, tk), lambda i, g, j: (i, g * nk + j)),
        compiler_params=_cp("parallel", "parallel", "parallel"), name=name)(dy, w, *extras)


def mm_tn(x, dy, *, name, G=1, pre=None, ts=1024):
    S = x.shape[0]
    K, N = x.shape[1] // G, dy.shape[1] // G
    tk, tn = min(K, 1024), min(N, 1024)
    nk, nn = K // tk, N // tn
    ts = min(ts, S)

    def body(x_ref, d_ref, o_ref):
        @pl.when(pl.program_id(3) == 0)
        def _():
            o_ref[...] = jnp.zeros_like(o_ref)

        xv = x_ref[...]
        if pre == "relu2":
            xf = jnp.maximum(xv.astype(f32), 0.0)
            xv = xf * xf
        o_ref[0] += lax.dot_general(xv.astype(bf16), d_ref[...].astype(bf16), (((0,), (0,)), ((), ())),
                                    preferred_element_type=f32)

    return pl.pallas_call(
        body, out_shape=SDS((G, K, N), f32), grid=(G, nk, nn, S // ts),
        in_specs=[BS((ts, tk), lambda g, a, b, s: (s, g * nk + a)), BS((ts, tn), lambda g, a, b, s: (s, g * nn + b))],
        out_specs=BS((1, tk, tn), lambda g, a, b, s: (g, a, b)),
        compiler_params=_cp("parallel", "parallel", "parallel", "arbitrary"), name=name)(x, dy)


def rms_bwd(dxn, x, g, dres, name):
    S, Dm = x.shape
    T = _tile(S)

    def body(*refs):
        if dres is not None:
            d_ref, x_ref, g_ref, r_ref, dx_ref, dg_ref = refs
        else:
            d_ref, x_ref, g_ref, dx_ref, dg_ref = refs
        xv = x_ref[...]
        d = d_ref[...].astype(f32)
        r = lax.rsqrt(jnp.mean(xv * xv, axis=-1, keepdims=True) + EPS)
        u = d * g_ref[...]
        dx = r * u - xv * ((r * r * r) * (1.0 / Dm)) * jnp.sum(u * xv, axis=-1, keepdims=True)
        if dres is not None:
            dx = r_ref[...] + dx
        dx_ref[...] = dx

        @pl.when(pl.program_id(0) == 0)
        def _():
            dg_ref[...] = jnp.zeros_like(dg_ref)

        dg_ref[...] += jnp.sum(d * xv * r, axis=0, keepdims=True)

    tok = BS((T, Dm), lambda i: (i, 0))
    vec = BS((1, Dm), lambda i: (0, 0))
    args = (dxn, x, g) + ((dres,) if dres is not None else ())
    return pl.pallas_call(
        body, out_shape=(SDS((S, Dm), f32), SDS((1, Dm), f32)), grid=(S // T,),
        in_specs=[tok, tok, vec] + ([tok] if dres is not None else []), out_specs=(tok, vec),
        compiler_params=_cp("arbitrary"), name=name)(*args)


def loss_head(h, g, target, name):
    S, Dm = h.shape
    T = _tile(S)

    def body(h_ref, g_ref, t_ref, loss_ref, dh_ref, dg_ref):
        @pl.when(pl.program_id(0) == 0)
        def _():
            loss_ref[...] = jnp.zeros_like(loss_ref)
            dg_ref[...] = jnp.zeros_like(dg_ref)

        xv = h_ref[...]
        gv = g_ref[...]
        r = lax.rsqrt(jnp.mean(xv * xv, axis=-1, keepdims=True) + EPS)
        diff = xv * r * gv - t_ref[...]
        loss_ref[...] += 0.5 * jnp.sum(jnp.mean(diff * diff, axis=-1, keepdims=True))
        d = diff * (1.0 / Dm)
        u = d * gv
        dh_ref[...] = r * u - xv * ((r * r * r) * (1.0 / Dm)) * jnp.sum(u * xv, axis=-1, keepdims=True)
        dg_ref[...] += jnp.sum(d * xv * r, axis=0, keepdims=True)

    tok = BS((T, Dm), lambda i: (i, 0))
    vec = BS((1, Dm), lambda i: (0, 0))
    return pl.pallas_call(
        body, out_shape=(SDS((1, 128), f32), SDS((S, Dm), f32), SDS((1, Dm), f32)), grid=(S // T,),
        in_specs=[tok, vec, tok], out_specs=(BS((1, 128), lambda i: (0, 0)), tok, vec),
        compiler_params=_cp("arbitrary"), name=name)(h, g, target)


def gate_merge_fwd(gate, z, name):
    S = gate.shape[0]
    T = _tile(S)

    def body(g_ref, z_ref, o_ref):
        acc = jnp.zeros((T, D), f32)
        for n in range(4):
            acc = acc + g_ref[:, n * D:(n + 1) * D].astype(f32) * z_ref[:, n * D:(n + 1) * D].astype(f32)
        o_ref[...] = acc.astype(bf16)

    wide = BS((T, 4 * D), lambda i: (i, 0))
    return pl.pallas_call(body, out_shape=SDS((S, D), bf16), grid=(S // T,), in_specs=[wide, wide],
                          out_specs=BS((T, D), lambda i: (i, 0)), compiler_params=_cp("parallel"), name=name)(gate, z)


def gate_merge_bwd(dm, gate, z, name):
    S = gate.shape[0]
    T = _tile(S)

    def body(dm_ref, g_ref, z_ref, dz_ref, dg_ref, db_ref):
        @pl.when(pl.program_id(0) == 0)
        def _():
            db_ref[...] = jnp.zeros_like(db_ref)

        dmv = dm_ref[...].astype(f32)
        for n in range(4):
            cols = slice(n * D, (n + 1) * D)
            gv = g_ref[:, cols].astype(f32)
            dz_ref[:, cols] = (dmv * gv).astype(bf16)
            dgp = dmv * z_ref[:, cols].astype(f32) * gv * (1.0 - gv)
            dg_ref[:, cols] = dgp.astype(bf16)
            db_ref[:, cols] += jnp.sum(dgp, axis=0, keepdims=True)

    wide = BS((T, 4 * D), lambda i: (i, 0))
    return pl.pallas_call(
        body, out_shape=(SDS((S, 4 * D), bf16), SDS((S, 4 * D), bf16), SDS((1, 4 * D), f32)), grid=(S // T,),
        in_specs=[BS((T, D), lambda i: (i, 0)), wide, wide], out_specs=(wide, wide, BS((1, 4 * D), lambda i: (0, 0))),
        compiler_params=_cp("arbitrary"), name=name)(dm, gate, z)


def ple_bwd_ew(dh, e, pg, name):
    S = dh.shape[0]
    T = _tile(S)

    def body(dh_ref, e_ref, pg_ref, dp_ref, de_ref, db_ref):
        @pl.when(pl.program_id(0) == 0)
        def _():
            db_ref[...] = jnp.zeros_like(db_ref)

        d = dh_ref[...]
        g = pg_ref[...].astype(f32)
        dpre = d * e_ref[...].astype(f32) * g * (1.0 - g)
        dp_ref[...] = dpre.astype(bf16)
        de_ref[...] = (d * g).astype(bf16)
        db_ref[...] += jnp.sum(dpre, axis=0, keepdims=True)

    tok = BS((T, D), lambda i: (i, 0))
    return pl.pallas_call(
        body, out_shape=(SDS((S, D), bf16), SDS((S, D), bf16), SDS((1, D), f32)), grid=(S // T,),
        in_specs=[tok, tok, tok], out_specs=(tok, tok, BS((1, D), lambda i: (0, 0))),
        compiler_params=_cp("arbitrary"), name=name)(dh, e, pg)


_GK = 0.7978845608028654
_GC = 0.044715


def _gelu(x):
    return 0.5 * x * (1.0 + jnp.tanh(_GK * (x + _GC * (x * x * x))))


def _gelu_grad(x):
    x2 = x * x
    t = jnp.tanh(_GK * (x + _GC * (x * x2)))
    return 0.5 * (1.0 + t) + 0.5 * x * (1.0 - t * t) * (_GK * (1.0 + 3.0 * _GC * x2))


def _ln_stats(v):
    mu = jnp.mean(v, axis=-1, keepdims=True)
    vc = v - mu
    rs = lax.rsqrt(jnp.mean(vc * vc, axis=-1, keepdims=True) + EPS)
    return vc * rs, rs


def _ln_bwd(dvh, vh, rs):
    return rs * (dvh - jnp.mean(dvh, axis=-1, keepdims=True) - vh * jnp.mean(dvh * vh, axis=-1, keepdims=True))


def sg_fwd(proj, lg, lb, wm, bsb, name):
    S = proj.shape[0]
    T = _tile(S)
    cu, cv = OUR_COLS["sg_u"][0] // BW, OUR_COLS["sg_v"][0] // BW

    def body(u_ref, v_ref, lg_ref, lb_ref, wm_ref, bsb_ref, o_ref):
        for b in range(T // 128):
            rows = slice(b * 128, (b + 1) * 128)
            u = _gelu(u_ref[rows, :].astype(f32))
            vh, _ = _ln_stats(_gelu(v_ref[rows, :].astype(f32)))
            vb = (vh * lg_ref[...] + lb_ref[...]).astype(bf16)
            outs = []
            for g in range(4):
                cols = slice(g * 128, (g + 1) * 128)
                mixed = jnp.dot(wm_ref[g], vb[:, cols], preferred_element_type=f32) + bsb_ref[g]
                outs.append(u[:, cols] * mixed)
            o_ref[rows, :] = jnp.concatenate(outs, axis=1).astype(bf16)

    vec = BS((1, BW), lambda i: (0, 0))
    cube = BS((4, 128, 128), lambda i: (0, 0, 0))
    return pl.pallas_call(
        body, out_shape=SDS((S, 4 * BW), bf16), grid=(S // T,),
        in_specs=[BS((T, BW), lambda i: (i, cu)), BS((T, BW), lambda i: (i, cv)), vec, vec, cube, cube],
        out_specs=BS((T, BW), lambda i: (i, 0)), compiler_params=_cp("parallel"), name=name)(proj, proj, lg, lb, wm, bsb)


def sg_bwd(proj, dy, lg, lb, wm, bsb, maskf, name):
    S = proj.shape[0]
    T = _tile(S)
    cu, cv = OUR_COLS["sg_u"][0] // BW, OUR_COLS["sg_v"][0] // BW
    creg = OUR_COLS["sg_u"][0] // (2 * BW)

    def body(u_ref, v_ref, dy_ref, lg_ref, lb_ref, wm_ref, bsb_ref, mk_ref, dp_ref, dwm_ref, dbs_ref, dlg_ref, dlb_ref):
        @pl.when(pl.program_id(0) == 0)
        def _():
            dwm_ref[...] = jnp.zeros_like(dwm_ref)
            dbs_ref[...] = jnp.zeros_like(dbs_ref)
            dlg_ref[...] = jnp.zeros_like(dlg_ref)
            dlb_ref[...] = jnp.zeros_like(dlb_ref)

        for b in range(T // 128):
            rows = slice(b * 128, (b + 1) * 128)
            su = u_ref[rows, :].astype(f32)
            sv = v_ref[rows, :].astype(f32)
            dya = dy_ref[rows, :].astype(f32)
            u = _gelu(su)
            vh, rs = _ln_stats(_gelu(sv))
            vb = (vh * lg_ref[...] + lb_ref[...]).astype(bf16)
            dus, dvls = [], []
            for g in range(4):
                cols = slice(g * 128, (g + 1) * 128)
                mixed = jnp.dot(wm_ref[g], vb[:, cols], preferred_element_type=f32) + bsb_ref[g]
                dus.append(dya[:, cols] * mixed)
                dmg = dya[:, cols] * u[:, cols]
                dmb = dmg.astype(bf16)
                dbs_ref[g] += jnp.broadcast_to(jnp.sum(dmg, axis=1, keepdims=True), (128, 128))
                dwm_ref[g] += mk_ref[...] * lax.dot_general(dmb, vb[:, cols], (((1,), (1,)), ((), ())),
                                                            preferred_element_type=f32)
                dvls.append(lax.dot_general(wm_ref[g], dmb, (((0,), (0,)), ((), ())), preferred_element_type=f32))
            du = jnp.concatenate(dus, axis=1)
            dvln = jnp.concatenate(dvls, axis=1)
            dlg_ref[...] += jnp.sum(dvln * vh, axis=0, keepdims=True)
            dlb_ref[...] += jnp.sum(dvln, axis=0, keepdims=True)
            dv = _ln_bwd(dvln * lg_ref[...], vh, rs)
            dp_ref[rows, 0:BW] = (du * _gelu_grad(su)).astype(bf16)
            dp_ref[rows, BW:2 * BW] = (dv * _gelu_grad(sv)).astype(bf16)

    vec = BS((1, BW), lambda i: (0, 0))
    cube = BS((4, 128, 128), lambda i: (0, 0, 0))
    return pl.pallas_call(
        body,
        out_shape=(SDS((S, NP), bf16), SDS((4, 128, 128), f32), SDS((4, 128, 128), f32), SDS((1, BW), f32), SDS((1, BW), f32)),
        grid=(S // T,),
        in_specs=[BS((T, BW), lambda i: (i, cu)), BS((T, BW), lambda i: (i, cv)), BS((T, BW), lambda i: (i, 0)), vec, vec,
                  cube, cube, BS((128, 128), lambda i: (0, 0))],
        out_specs=(BS((T, 2 * BW), lambda i: (i, creg)), cube, cube, vec, vec),
        compiler_params=_cp("arbitrary"), name=name)(proj, proj, dy, lg, lb, wm, bsb, maskf)


_SUB = 64


def _conv_specs(S, T):
    ca, cg = OUR_COLS["c_a"][0] // BW, OUR_COLS["c_g"][0] // BW
    hb = T // HALO
    prev = lambda i: jnp.maximum(i * hb - 1, 0)
    return [BS((T, BW), lambda i: (i, ca)), BS((T, BW), lambda i: (i, cg)),
            BS((HALO, BW), lambda i: (prev(i), ca)), BS((HALO, BW), lambda i: (prev(i), cg))]


def _conv_fill_ybuf(a_ref, g_ref, ap_ref, gp_ref, ybuf):
    T = a_ref.shape[0]
    ybuf[pl.ds(HALO, T), :] = a_ref[...].astype(f32) * jax.nn.sigmoid(g_ref[...].astype(f32))
    first = (pl.program_id(0) == 0).astype(f32)
    ybuf[pl.ds(0, HALO), :] = (1.0 - first) * (ap_ref[...].astype(f32) * jax.nn.sigmoid(gp_ref[...].astype(f32)))


def _conv_taps(w_ref, ybuf, r0):
    acc = jnp.zeros((_SUB, BW), f32)
    for k in range(CONV_K):
        acc = acc + w_ref[k:k + 1, :] * ybuf[pl.ds(r0 + HALO - (CONV_K - 1) + k, _SUB), :]
    return acc


def conv_fwd(proj, w, b, lg, lb, y, name):
    S = proj.shape[0]
    T = _tile(S)

    def body(a_ref, g_ref, ap_ref, gp_ref, w_ref, b_ref, lg_ref, lb_ref, y_in, o_ref, ybuf):
        del y_in
        _conv_fill_ybuf(a_ref, g_ref, ap_ref, gp_ref, ybuf)
        for sb in range(T // _SUB):
            z = _conv_taps(w_ref, ybuf, sb * _SUB) + b_ref[...]
            zh, _ = _ln_stats(z)
            zl = zh * lg_ref[...] + lb_ref[...]
            o_ref[pl.ds(sb * _SUB, _SUB), :] = (zl * jax.nn.sigmoid(zl)).astype(bf16)

    vec = BS((1, BW), lambda i: (0, 0))
    return pl.pallas_call(
        body, out_shape=SDS(y.shape, bf16), grid=(S // T,),
        in_specs=_conv_specs(S, T) + [BS((CONV_K, BW), lambda i: (0, 0)), vec, vec, vec, ANY],
        out_specs=BS((T, BW), lambda i: (i, 3)), scratch_shapes=[pltpu.VMEM((T + HALO, BW), f32)],
        input_output_aliases={8: 0}, compiler_params=_cp("parallel"), name=name)(proj, proj, proj, proj, w, b, lg, lb, y)


def conv_bwd_norm(proj, dy, w, b, lg, lb, name):
    S = proj.shape[0]
    T = _tile(S)

    def body(a_ref, g_ref, ap_ref, gp_ref, dy_ref, w_ref, b_ref, lg_ref, lb_ref, dz_ref, dlg_ref, dlb_ref, db_ref, ybuf):
        @pl.when(pl.program_id(0) == 0)
        def _():
            dlg_ref[...] = jnp.zeros_like(dlg_ref)
            dlb_ref[...] = jnp.zeros_like(dlb_ref)
            db_ref[...] = jnp.zeros_like(db_ref)

        _conv_fill_ybuf(a_ref, g_ref, ap_ref, gp_ref, ybuf)
        for sb in range(T // _SUB):
            rows = pl.ds(sb * _SUB, _SUB)
            z = _conv_taps(w_ref, ybuf, sb * _SUB) + b_ref[...]
            zh, rs = _ln_stats(z)
            zl = zh * lg_ref[...] + lb_ref[...]
            sg = jax.nn.sigmoid(zl)
            dzl = dy_ref[rows, :].astype(f32) * sg * (1.0 + zl * (1.0 - sg))
            dlg_ref[...] += jnp.sum(dzl * zh, axis=0, keepdims=True)
            dlb_ref[...] += jnp.sum(dzl, axis=0, keepdims=True)
            dz = _ln_bwd(dzl * lg_ref[...], zh, rs)
            db_ref[...] += jnp.sum(dz, axis=0, keepdims=True)
            dz_ref[rows, :] = dz

    vec = BS((1, BW), lambda i: (0, 0))
    tok = BS((T, BW), lambda i: (i, 0))
    return pl.pallas_call(
        body, out_shape=(SDS((S, BW), f32), SDS((1, BW), f32), SDS((1, BW), f32), SDS((1, BW), f32)), grid=(S // T,),
        in_specs=_conv_specs(S, T) + [BS((T, BW), lambda i: (i, 3)), BS((CONV_K, BW), lambda i: (0, 0)), vec, vec, vec],
        out_specs=(tok, vec, vec, vec), scratch_shapes=[pltpu.VMEM((T + HALO, BW), f32)],
        compiler_params=_cp("arbitrary"), name=name)(proj, proj, proj, proj, dy, w, b, lg, lb)


def conv_bwd_taps(proj, dz, w, dproj, name):
    S = proj.shape[0]
    T = _tile(S)
    nT = S // T
    hb = T // HALO
    creg = OUR_COLS["c_a"][0] // (2 * BW)

    def body(a_ref, g_ref, ap_ref, gp_ref, dz_ref, dzn_ref, w_ref, dp_in, dp_ref, dw_ref, ybuf, dzbuf, dwacc):
        del dp_in
        i = pl.program_id(0)

        @pl.when(i == 0)
        def _():
            dwacc[...] = jnp.zeros_like(dwacc)

        _conv_fill_ybuf(a_ref, g_ref, ap_ref, gp_ref, ybuf)
        dzbuf[pl.ds(0, T), :] = dz_ref[...]
        dzbuf[pl.ds(T, HALO), :] = (i < nT - 1).astype(f32) * dzn_ref[...]
        for sb in range(T // _SUB):
            r0 = sb * _SUB
            rows = pl.ds(r0, _SUB)
            dzs = dz_ref[rows, :]
            dyg = jnp.zeros((_SUB, BW), f32)
            for k in range(CONV_K):
                ysl = ybuf[pl.ds(r0 + HALO - (CONV_K - 1) + k, _SUB), :]
                dwacc[pl.ds(k * 8, 8), :] += jnp.sum((dzs * ysl).reshape(_SUB // 8, 8, BW), axis=0)
                dyg = dyg + w_ref[k:k + 1, :] * dzbuf[pl.ds(r0 + (CONV_K - 1) - k, _SUB), :]
            av = a_ref[rows, :].astype(f32)
            sg = jax.nn.sigmoid(g_ref[rows, :].astype(f32))
            dp_ref[rows, 0:BW] = (dyg * sg).astype(bf16)
            dp_ref[rows, BW:2 * BW] = (dyg * av * sg * (1.0 - sg)).astype(bf16)

        @pl.when(i == nT - 1)
        def _():
            for k in range(CONV_K):
                dw_ref[k:k + 1, :] = jnp.sum(dwacc[pl.ds(k * 8, 8), :], axis=0, keepdims=True)

    nxt = lambda i: jnp.minimum((i + 1) * hb, S // HALO - 1)
    return pl.pallas_call(
        body, out_shape=(SDS((S, NP), bf16), SDS((CONV_K, BW), f32)), grid=(nT,),
        in_specs=_conv_specs(S, T) + [BS((T, BW), lambda i: (i, 0)), BS((HALO, BW), lambda i: (nxt(i), 0)),
                                      BS((CONV_K, BW), lambda i: (0, 0)), ANY],
        out_specs=(BS((T, 2 * BW), lambda i: (i, creg)), BS((CONV_K, BW), lambda i: (0, 0))),
        scratch_shapes=[pltpu.VMEM((T + HALO, BW), f32), pltpu.VMEM((T + HALO, BW), f32), pltpu.VMEM((CONV_K * 8, BW), f32)],
        input_output_aliases={7: 0}, compiler_params=_cp("arbitrary"), name=name)(proj, proj, proj, proj, dz, dz, w, dproj)


def _toeplitz_index():
    j = lax.broadcasted_iota(jnp.int32, (REL_TABLE, 1024), 1)
    t = lax.broadcasted_iota(jnp.int32, (REL_TABLE, 1024), 0)
    e = ((WIN - 1) - j) & 1023
    tidx = jnp.clip(e - (TQ - 1), -(CHUNK - 1), 256) + (CHUNK - 1)
    return (tidx == t).astype(f32)


def att_bias_build(table, name):
    H = table.shape[0]

    def body(t_ref, o_ref):
        u = jnp.dot(t_ref[...], _toeplitz_index(), precision=HI, preferred_element_type=f32)
        row = lax.broadcasted_iota(jnp.int32, (TQ, 1024), 0)
        for h in range(H):
            x = jnp.broadcast_to(u[h:h + 1, :], (TQ, 1024))
            for b in range(8):
                x = jnp.where(((row >> b) & 1) == 1, pltpu.roll(x, 1 << b, 1), x)
            o_ref[h] = x[:, :WIN]

    return pl.pallas_call(body, out_shape=SDS((H, TQ, WIN), f32), compiler_params=pltpu.CompilerParams(vmem_limit_bytes=VMEM_LIMIT),
                          name=name)(table)


def att_bias_grad(dbias, name):
    H = dbias.shape[0]

    def body(d_ref, o_ref):
        row = lax.broadcasted_iota(jnp.int32, (TQ, 1024), 0)
        rows = []
        for h in range(H):
            x = jnp.concatenate([d_ref[h], jnp.zeros((TQ, 1024 - WIN), f32)], axis=1)
            for b in range(8):
                x = jnp.where(((row >> b) & 1) == 1, pltpu.roll(x, 1024 - (1 << b), 1), x)
            rows.append(jnp.sum(x, axis=0, keepdims=True))
        du = jnp.concatenate(rows, axis=0)
        o_ref[...] = lax.dot_general(du, _toeplitz_index(), (((1,), (1,)), ((), ())), precision=HI,
                                     preferred_element_type=f32)

    return pl.pallas_call(body, out_shape=SDS((H, REL_TABLE), f32), compiler_params=pltpu.CompilerParams(vmem_limit_bytes=VMEM_LIMIT),
                          name=name)(dbias)


def _att_specs():
    cq, ck, cv = (OUR_COLS[n][0] // BW for n in ("a_q", "a_k", "a_v"))
    specs = [BS((TQ, BW), lambda i: (i, cq))]
    for col in (ck, cv):
        for back in (2, 1, 0):
            specs.append(BS((TQ, BW), functools.partial(lambda i, back, col: (jnp.maximum(i - back, 0), col), back=back, col=col)))
    return specs


def _att_valid(i):
    r = lax.broadcasted_iota(jnp.int32, (TQ, WIN), 0)
    n = lax.broadcasted_iota(jnp.int32, (TQ, WIN), 1)
    dchunk = (r // CHUNK + 8) - n // CHUNK
    return (dchunk >= 0) & (dchunk <= 8) & (n + (i - 2) * TQ >= 0)


def _att_probs(qa, kp, bias_h, valid):
    s = lax.dot_general(qa, kp, (((1,), (1,)), ((), ())), preferred_element_type=f32) * 0.125 + bias_h
    s = jnp.where(valid, s, NEG_INF)
    e = jnp.exp(s - jnp.max(s, axis=-1, keepdims=True))
    return e / jnp.sum(e, axis=-1, keepdims=True)


def att_fwd(proj, bias, y, name):
    S = proj.shape[0]

    def body(q_ref, k2, k1, k0, v2, v1, v0, b_ref, y_in, o_ref, kwin, vwin):
        del y_in
        i = pl.program_id(0)
        for w, (kr, vr) in enumerate(((k2, v2), (k1, v1), (k0, v0))):
            kwin[pl.ds(w * TQ, TQ), :] = kr[...]
            vwin[pl.ds(w * TQ, TQ), :] = vr[...]
        valid = _att_valid(i)
        lo = lax.broadcasted_iota(jnp.int32, (TQ, 128), 1) < 64
        for hp in range(4):
            cols = slice(hp * 128, (hp + 1) * 128)
            qp, kp, vp = q_ref[:, cols], kwin[:, cols], vwin[:, cols]
            outs = []
            for a in range(2):
                qa = jnp.where(lo if a == 0 else ~lo, qp, jnp.zeros_like(qp))
                p = _att_probs(qa, kp, b_ref[2 * hp + a], valid)
                outs.append(jnp.dot(p.astype(bf16), vp, preferred_element_type=f32))
            o_ref[:, cols] = jnp.where(lo, outs[0], outs[1]).astype(bf16)

    return pl.pallas_call(
        body, out_shape=SDS(y.shape, bf16), grid=(S // TQ,),
        in_specs=_att_specs() + [BS((8, TQ, WIN), lambda i: (0, 0, 0), pipeline_mode=pl.Buffered(1)), ANY],
        out_specs=BS((TQ, BW), lambda i: (i, 2)),
        scratch_shapes=[pltpu.VMEM((WIN, BW), bf16), pltpu.VMEM((WIN, BW), bf16)],
        input_output_aliases={8: 0}, compiler_params=_cp("parallel"), name=name)(
            proj, proj, proj, proj, proj, proj, proj, bias, y)


def att_bwd(proj, y, dy, bias, dproj, name):
    S = proj.shape[0]
    cq = OUR_COLS["a_q"][0] // BW

    def body(q_ref, k2, k1, k0, v2, v1, v0, b_ref, o_ref, do_ref, dp_in, dq_ref, dkp_ref, dvp_ref, db_ref, kwin, vwin):
        del dp_in
        i = pl.program_id(0)

        @pl.when(i == 0)
        def _():
            db_ref[...] = jnp.zeros_like(db_ref)

        for w, (kr, vr) in enumerate(((k2, v2), (k1, v1), (k0, v0))):
            kwin[pl.ds(w * TQ, TQ), :] = kr[...]
            vwin[pl.ds(w * TQ, TQ), :] = vr[...]
        valid = _att_valid(i)
        lo = lax.broadcasted_iota(jnp.int32, (TQ, 128), 1) < 64
        for hp in range(4):
            cols = slice(hp * 128, (hp + 1) * 128)
            qp, kp, vp = q_ref[:, cols], kwin[:, cols], vwin[:, cols]
            dop, op = do_ref[:, cols], o_ref[:, cols]
            dqs = []
            dk = jnp.zeros((WIN, 128), f32)
            dv = jnp.zeros((WIN, 128), f32)
            for a in range(2):
                sel = lo if a == 0 else ~lo
                qa = jnp.where(sel, qp, jnp.zeros_like(qp))
                doa = jnp.where(sel, dop, jnp.zeros_like(dop))
                p = _att_probs(qa, kp, b_ref[2 * hp + a], valid)
                dpv = lax.dot_general(doa, vp, (((1,), (1,)), ((), ())), preferred_element_type=f32)
                delta = jnp.sum(doa.astype(f32) * op.astype(f32), axis=-1, keepdims=True)
                ds = p * (dpv - delta)
                db_ref[2 * hp + a] += ds
                dsb = ds.astype(bf16)
                dqs.append(jnp.dot(dsb, kp, preferred_element_type=f32))
                dk = dk + lax.dot_general(dsb, qa, (((0,), (0,)), ((), ())), preferred_element_type=f32)
                dv = dv + lax.dot_general(p.astype(bf16), doa, (((0,), (0,)), ((), ())), preferred_element_type=f32)
            dq_ref[:, cols] = (jnp.where(lo, dqs[0], dqs[1]) * 0.125).astype(bf16)
            for w in range(3):
                dkp_ref[w, :, cols] = (dk[w * TQ:(w + 1) * TQ] * 0.125).astype(bf16)
                dvp_ref[w, :, cols] = dv[w * TQ:(w + 1) * TQ].astype(bf16)

    tok = BS((TQ, BW), lambda i: (i, 2))
    part = BS((3, TQ, BW), lambda i: (0, i, 0))
    full = BS((8, TQ, WIN), lambda i: (0, 0, 0))
    return pl.pallas_call(
        body, out_shape=(SDS((S, NP), bf16), SDS((3, S, BW), bf16), SDS((3, S, BW), bf16), SDS((8, TQ, WIN), f32)),
        grid=(S // TQ,),
        in_specs=_att_specs() + [BS((8, TQ, WIN), lambda i: (0, 0, 0), pipeline_mode=pl.Buffered(1)), tok, tok, ANY],
        out_specs=(BS((TQ, BW), lambda i: (i, cq)), part, part, full),
        scratch_shapes=[pltpu.VMEM((WIN, BW), bf16), pltpu.VMEM((WIN, BW), bf16)],
        input_output_aliases={10: 0}, compiler_params=_cp("arbitrary"), name=name)(
            proj, proj, proj, proj, proj, proj, proj, bias, y, dy, dproj)


def att_shift_add(dkp, dvp, dproj, name):
    S = dkp.shape[1]
    nT = S // TQ
    creg = OUR_COLS["a_k"][0] // (2 * BW)

    def body(k2, k1, k0, v2, v1, v0, dp_in, dp_ref):
        del dp_in
        j = pl.program_id(0)
        m1 = (j + 1 < nT).astype(f32)
        m0 = (j + 2 < nT).astype(f32)
        dp_ref[:, 0:BW] = (k2[0].astype(f32) + m1 * k1[0].astype(f32) + m0 * k0[0].astype(f32)).astype(bf16)
        dp_ref[:, BW:2 * BW] = (v2[0].astype(f32) + m1 * v1[0].astype(f32) + m0 * v0[0].astype(f32)).astype(bf16)

    def spec(w):
        return BS((1, TQ, BW), functools.partial(lambda j, w: (w, jnp.minimum(j + 2 - w, nT - 1), 0), w=w))

    return pl.pallas_call(
        body, out_shape=SDS(dproj.shape, bf16), grid=(nT,),
        in_specs=[spec(2), spec(1), spec(0), spec(2), spec(1), spec(0), ANY],
        out_specs=BS((TQ, 2 * BW), lambda j: (j, creg)),
        input_output_aliases={6: 0}, compiler_params=_cp("parallel"), name=name)(dkp, dkp, dkp, dvp, dvp, dvp, dproj)


GQ, GV = 256, 512
TGC = 8


def _bd_mask():
    r = lax.broadcasted_iota(jnp.int32, (GQ, GV), 0) // 64
    c = lax.broadcasted_iota(jnp.int32, (GQ, GV), 1) // 128
    return (r == c).astype(f32)


def _tri(strict):
    r = lax.broadcasted_iota(jnp.int32, (CHUNK, CHUNK), 0)
    c = lax.broadcasted_iota(jnp.int32, (CHUNK, CHUNK), 1)
    return ((c < r) if strict else (c <= r)).astype(f32)


def _compact(s_bd):
    return jnp.concatenate([s_bd[h * 64:(h + 1) * 64, h * 128:(h + 1) * 128] for h in range(4)], axis=0)


def _expand(comp, mask):
    return jnp.tile(comp, (1, 4)) * mask


def _gla_gates(alr, wa_ref, ba_ref, tri_incl, ones_col):
    a = jnp.dot(alr, wa_ref[...], preferred_element_type=f32) + ba_ref[...]
    la = (jnp.minimum(a, 0.0) - jnp.log(1.0 + jnp.exp(-jnp.abs(a)))) * (1.0 / 16.0)
    cum = jnp.dot(tri_incl, la, precision=HI, preferred_element_type=f32)
    tot_row = cum[CHUNK - 1:CHUNK, :]
    tot_col = lax.dot_general(la, ones_col, (((0,), (0,)), ((), ())), precision=HI, preferred_element_type=f32)
    return a, cum, tot_row, jnp.tile(jnp.exp(tot_col), (1, 4))


def _head_norm(o):
    rns, ons = [], []
    for h in range(4):
        oh = o[:, h * 128:(h + 1) * 128]
        rn = lax.rsqrt(jnp.mean(oh * oh, axis=-1, keepdims=True) + EPS)
        rns.append(rn)
        ons.append(oh * rn)
    return rns, ons


def _gla_in_specs(T, imap):
    cq, ck = OUR_COLS["g_q"][0] // GQ, OUR_COLS["g_k"][0] // GQ
    cv, cr = OUR_COLS["g_v"][0] // GV, OUR_COLS["g_r"][0] // GV
    return [BS((T, GQ), lambda i: (imap(i), cq)), BS((T, GQ), lambda i: (imap(i), ck)), BS((T, GV), lambda i: (imap(i), cv)),
            BS((T, GV), lambda i: (imap(i), cr)), BS((T, RANKP), lambda i: (imap(i), 0))]


def gla_fwd(proj, pa, wa, ba, ng, y, name):
    S = proj.shape[0]
    T = min(TGC * CHUNK, S)
    nch = T // CHUNK

    def body(q_ref, k_ref, v_ref, r_ref, a_ref, wa_ref, ba_ref, ng_ref, y_in, y_ref, st_ref, s_scr):
        del y_in

        @pl.when(pl.program_id(0) == 0)
        def _():
            s_scr[...] = jnp.zeros_like(s_scr)

        mask = _bd_mask()
        tri = _tri(False)
        ones_col = jnp.ones((CHUNK, 128), f32)

        def chunk(ci, carry):
            rows = pl.ds(pl.multiple_of(ci * CHUNK, CHUNK), CHUNK)
            _, cum, tot_row, dec4 = _gla_gates(a_ref[rows, :], wa_ref, ba_ref, tri, ones_col)
            kd = (k_ref[rows, :].astype(f32) * jnp.exp(tot_row - cum)).astype(bf16)
            upd = lax.dot_general(kd, v_ref[rows, :], (((0,), (0,)), ((), ())), preferred_element_type=f32) * mask
            s_new = dec4 * s_scr[...] + upd
            s_scr[...] = s_new
            st_ref[pl.ds(pl.multiple_of(ci * GQ, GQ), GQ), :] = _compact(s_new)
            qs = (q_ref[rows, :].astype(f32) * 0.125).astype(bf16)
            o = jnp.dot(qs, s_new.astype(bf16), preferred_element_type=f32)
            _, ons = _head_norm(o)
            rv = r_ref[rows, :].astype(f32)
            y_ref[rows, :] = (jnp.concatenate(ons, axis=1) * ng_ref[...] * (rv * jax.nn.sigmoid(rv))).astype(bf16)
            return carry

        lax.fori_loop(0, nch, chunk, 0)

    return pl.pallas_call(
        body, out_shape=(SDS(y.shape, bf16), SDS((S // CHUNK * GQ, 128), f32)), grid=(S // T,),
        in_specs=_gla_in_specs(T, lambda i: i) + [BS((RANKP, GQ), lambda i: (0, 0)), BS((1, GQ), lambda i: (0, 0)),
                                                  BS((1, GV), lambda i: (0, 0)), ANY],
        out_specs=(BS((T, GV), lambda i: (i, 1)), BS((nch * GQ, 128), lambda i: (i, 0))),
        scratch_shapes=[pltpu.VMEM((GQ, GV), f32)], input_output_aliases={8: 0}, compiler_params=_cp("arbitrary"),
        name=name)(proj, proj, proj, proj, pa, wa, ba, ng, y)


def gla_bwd(proj, pa, states, dy, wa, ba, ng, dproj, name):
    S = proj.shape[0]
    T = min(TGC * CHUNK, S)
    nch = T // CHUNK
    nT = S // T
    rev = lambda i: nT - 1 - i

    def body(q_ref, k_ref, v_ref, r_ref, a_ref, st_ref, sp_ref, dy_ref, wa_ref, ba_ref, ng_ref, dp_in,
             dp_ref, da_ref, dwa_ref, dba_ref, dng_ref, g_scr):
        del dp_in
        i = pl.program_id(0)

        @pl.when(i == 0)
        def _():
            g_scr[...] = jnp.zeros_like(g_scr)
            dwa_ref[...] = jnp.zeros_like(dwa_ref)
            dba_ref[...] = jnp.zeros_like(dba_ref)
            dng_ref[...] = jnp.zeros_like(dng_ref)

        mask = _bd_mask()
        tri = _tri(False)
        tri_strict = _tri(True)
        ones_col = jnp.ones((CHUNK, 128), f32)
        ones_row = jnp.ones((8, GV), f32)
        first_tile = (i == nT - 1).astype(f32)

        def chunk(t, carry):
            ci = nch - 1 - t
            rows = pl.ds(pl.multiple_of(ci * CHUNK, CHUNK), CHUNK)
            alr = a_ref[rows, :]
            a, cum, tot_row, dec4 = _gla_gates(alr, wa_ref, ba_ref, tri, ones_col)
            wdec = jnp.exp(tot_row - cum)
            kdf = k_ref[rows, :].astype(f32) * wdec
            kd = kdf.astype(bf16)
            s_c = _expand(st_ref[pl.ds(pl.multiple_of(ci * GQ, GQ), GQ), :], mask)
            prev_in = st_ref[pl.ds(pl.multiple_of(jnp.maximum(ci - 1, 0) * GQ, GQ), GQ), :]
            prev = jnp.where(ci > 0, prev_in, sp_ref[...] * (1.0 - first_tile))
            s_prev = _expand(prev, mask)
            qs = (q_ref[rows, :].astype(f32) * 0.125).astype(bf16)
            s_cb = s_c.astype(bf16)
            o = jnp.dot(qs, s_cb, preferred_element_type=f32)
            rns, ons = _head_norm(o)
            on = jnp.concatenate(ons, axis=1)
            rv = r_ref[rows, :].astype(f32)
            sg = jax.nn.sigmoid(rv)
            sr = rv * sg
            dyv = dy_ref[rows, :].astype(f32)
            ngv = ng_ref[...]
            dng_ref[...] += jnp.sum(dyv * on * sr, axis=0, keepdims=True)
            d_on = dyv * ngv * sr
            dr = dyv * on * ngv * (sg * (1.0 + rv * (1.0 - sg)))
            dos = []
            for h in range(4):
                cols = slice(h * 128, (h + 1) * 128)
                dh_ = d_on[:, cols]
                dos.append(rns[h] * (dh_ - ons[h] * jnp.mean(dh_ * ons[h], axis=-1, keepdims=True)))
            do = jnp.concatenate(dos, axis=1).astype(bf16)
            dq = lax.dot_general(do, s_cb, (((1,), (1,)), ((), ())), preferred_element_type=f32) * 0.125
            ds = lax.dot_general(qs, do, (((0,), (0,)), ((), ())), preferred_element_type=f32) * mask + g_scr[...]
            ddec_row = lax.dot_general(ones_row, ds * s_prev, (((1,), (1,)), ((), ())), precision=HI,
                                       preferred_element_type=f32)[0:1, :]
            dsb = ds.astype(bf16)
            dkd = lax.dot_general(v_ref[rows, :], dsb, (((1,), (1,)), ((), ())), preferred_element_type=f32)
            dv = jnp.dot(kd, dsb, preferred_element_type=f32)
            g_scr[...] = dec4 * ds
            dk = dkd * wdec
            dwlog = dkd * kdf
            dla = ddec_row * jnp.exp(tot_row) + jnp.dot(tri_strict, dwlog, precision=HI, preferred_element_type=f32)
            da = dla * (1.0 - jax.nn.sigmoid(a)) * (1.0 / 16.0)
            dab = da.astype(bf16)
            da_ref[rows, :] = lax.dot_general(dab, wa_ref[...], (((1,), (1,)), ((), ())),
                                              preferred_element_type=f32).astype(bf16)
            dwa_ref[...] += lax.dot_general(alr, dab, (((0,), (0,)), ((), ())), preferred_element_type=f32)
            dba_ref[...] += jnp.sum(da, axis=0, keepdims=True)
            dp_ref[rows, 0:GQ] = dq.astype(bf16)
            dp_ref[rows, GQ:2 * GQ] = dk.astype(bf16)
            dp_ref[rows, 2 * GQ:2 * GQ + GV] = dv.astype(bf16)
            dp_ref[rows, 2 * GQ + GV:2 * GQ + 2 * GV] = dr.astype(bf16)
            return carry

        lax.fori_loop(0, nch, chunk, 0)

    REG = 2 * GQ + 2 * GV
    return pl.pallas_call(
        body,
        out_shape=(SDS((S, NP), bf16), SDS((S, RANKP), bf16), SDS((RANKP, GQ), f32), SDS((1, GQ), f32), SDS((1, GV), f32)),
        grid=(nT,),
        in_specs=_gla_in_specs(T, rev) + [
            BS((nch * GQ, 128), lambda i: (rev(i), 0)),
            BS((GQ, 128), lambda i: (jnp.maximum(rev(i) * nch - 1, 0), 0)),
            BS((T, GV), lambda i: (rev(i), 1)),
            BS((RANKP, GQ), lambda i: (0, 0)), BS((1, GQ), lambda i: (0, 0)), BS((1, GV), lambda i: (0, 0)), ANY],
        out_specs=(BS((T, REG), lambda i: (rev(i), 0)), BS((T, RANKP), lambda i: (rev(i), 0)),
                   BS((RANKP, GQ), lambda i: (0, 0)), BS((1, GQ), lambda i: (0, 0)), BS((1, GV), lambda i: (0, 0))),
        scratch_shapes=[pltpu.VMEM((GQ, GV), f32)],
        input_output_aliases={11: 0}, compiler_params=_cp("arbitrary"), name=name)(
            proj, proj, proj, proj, pa, states, states, dy, wa, ba, ng, dproj)


def _as2d(a):
    if a.ndim == 1:
        return a.reshape(1, a.shape[0])
    return a.reshape(-1, a.shape[-1])


def adamw(w, g, m, v, name):
    shape = w.shape
    w2, g2, m2, v2 = (_as2d(a) for a in (w, g, m, v))
    R, C = w2.shape
    tr = R
    for cand in (512, 256, 128, 64, 32, 16, 8):
        if R % cand == 0 and cand * C * 4 * 7 * 2 <= 40 * 1024 * 1024:
            tr = cand
            break

    def body(w_ref, g_ref, m_ref, v_ref, d_ref, mo_ref, vo_ref):
        gv = g_ref[...]
        mn = ADAM_B1 * m_ref[...] + (1.0 - ADAM_B1) * gv
        vn = ADAM_B2 * v_ref[...] + (1.0 - ADAM_B2) * (gv * gv)
        m_hat = mn / (1.0 - ADAM_B1 ** ADAM_STEP)
        v_hat = vn / (1.0 - ADAM_B2 ** ADAM_STEP)
        d_ref[...] = -ADAM_LR * (m_hat / (jnp.sqrt(v_hat) + ADAM_EPS) + ADAM_WD * w_ref[...])
        mo_ref[...] = mn
        vo_ref[...] = vn

    blk = BS((tr, C), lambda i: (i, 0))
    outs = pl.pallas_call(body, out_shape=tuple(SDS((R, C), f32) for _ in range(3)), grid=(R // tr,),
                          in_specs=[blk] * 4, out_specs=(blk,) * 3, compiler_params=_cp("parallel"), name=name)(w2, g2, m2, v2)
    return tuple(o.reshape(shape) for o in outs)


def _row_tile(rows, row_bytes, budget=4 * 1024 * 1024):
    best = None
    for t in range(16, rows + 1, 16):
        if rows % t == 0 and t * row_bytes <= budget:
            best = t
    return best or rows


def add_halves(gp, ra, c, name):
    shape = ra.shape
    cols = shape[-1]
    rows = int(np.prod(shape[:-1]))
    g3, r2 = gp.reshape(2, rows, cols), ra.reshape(rows, cols)
    tr = _row_tile(rows, cols * 2)

    grid_spec = pltpu.PrefetchScalarGridSpec(
        num_scalar_prefetch=1, grid=(rows // tr,),
        in_specs=[BS((1, tr, cols), lambda i, c_ref: (c_ref[0], i, 0)), BS((tr, cols), lambda i, c_ref: (i, 0))],
        out_specs=BS((tr, cols), lambda i, c_ref: (i, 0)))

    def body(c_ref, a_ref, b_ref, o_ref):
        del c_ref
        o_ref[...] = (a_ref[0].astype(f32) + b_ref[...].astype(f32)).astype(bf16)

    out = pl.pallas_call(body, out_shape=SDS((rows, cols), bf16), grid_spec=grid_spec, compiler_params=_cp("parallel"),
                         name=name)(jnp.reshape(c, (1,)).astype(jnp.int32), g3, r2)
    return out.reshape(shape)


def sum_slots(x, name):
    N, shape = x.shape[0], x.shape[1:]
    cols = shape[-1]
    rows = int(np.prod(shape[:-1]))
    tr = _row_tile(rows, cols * x.dtype.itemsize * N)

    def body(x_ref, o_ref):
        acc = x_ref[0].astype(f32)
        for n in range(1, N):
            acc = acc + x_ref[n].astype(f32)
        o_ref[...] = acc

    out = pl.pallas_call(body, out_shape=SDS((rows, cols), f32), grid=(rows // tr,),
                         in_specs=[BS((N, tr, cols), lambda i: (0, i, 0))], out_specs=BS((tr, cols), lambda i: (i, 0)),
                         compiler_params=_cp("parallel"), name=name)(x.reshape(N, rows, cols))
    return out.reshape(shape)


def _me():
    return lax.axis_index("x"), lax.axis_index("y"), lax.axis_index("c")


def _rcopy(src, dst, send_sems, recv_sems, k, dev):
    return pltpu.make_async_remote_copy(src_ref=src, dst_ref=dst, send_sem=send_sems.at[k], recv_sem=recv_sems.at[k],
                                        device_id=dev, device_id_type=MESH)


def _comm_call(body, ins, out_shapes, n_remote, n_local, name):
    return pl.pallas_call(
        body, out_shape=tuple(out_shapes), in_specs=[ANY] * len(ins), out_specs=tuple(ANY for _ in out_shapes),
        scratch_shapes=[pltpu.SemaphoreType.DMA((n_remote,)), pltpu.SemaphoreType.DMA((n_remote,)),
                        pltpu.SemaphoreType.DMA((n_local,))], name=name)(*ins)


def ag4(bufs, name):
    n = len(bufs)

    def body(*refs):
        xs, os = refs[:n], refs[n:2 * n]
        send_sems, recv_sems, local_sems = refs[2 * n:]
        x, y, c = _me()
        j = 2 * x + y
        sib = (x, y, 1 - c)
        chips = [(1 - x, y), (x, 1 - y), (1 - x, 1 - y)]
        mine = []
        for t in range(n):
            for l in range(2):
                mine.append(pltpu.make_async_copy(xs[t].at[l], os[t].at[l, j], local_sems.at[2 * t + l]))
        for cp in mine:
            cp.start()
        first = [_rcopy(xs[t].at[c], os[t].at[c, j], send_sems, recv_sems, 6 * t + k, (cx, cy, c))
                 for t in range(n) for k, (cx, cy) in enumerate(chips)]
        for cp in first:
            cp.start()
        passed = []
        for k, (cx, cy) in enumerate(chips):
            for t in range(n):
                land = os[t].at[c, 2 * cx + cy]
                _rcopy(land, land, send_sems, recv_sems, 6 * t + k, (x, y, c)).wait_recv()
                fwd = _rcopy(land, land, send_sems, recv_sems, 6 * t + 3 + k, sib)
                fwd.start()
                passed.append(fwd)
        for k, (cx, cy) in enumerate(chips):
            for t in range(n):
                land = os[t].at[1 - c, 2 * cx + cy]
                _rcopy(land, land, send_sems, recv_sems, 6 * t + 3 + k, (x, y, c)).wait_recv()
        for cp in first + passed:
            cp.wait_send()
        for cp in mine:
            cp.wait()

    outs = [SDS((2, 4) + b.shape[1:], b.dtype) for b in bufs]
    return _comm_call(body, bufs, outs, 6 * n, 2 * n, name)


def sib_other_layer(gps, name):
    n = len(gps)

    def body(*refs):
        xs, os = refs[:n], refs[n:2 * n]
        send_sems, recv_sems, _ = refs[2 * n:]
        x, y, c = _me()
        cps = [_rcopy(xs[t].at[1 - c], os[t], send_sems, recv_sems, t, (x, y, 1 - c)) for t in range(n)]
        for cp in cps:
            cp.start()
        for cp in cps:
            cp.wait()

    return _comm_call(body, gps, [SDS(g.shape[1:], g.dtype) for g in gps], n, 1, name)


def a2a4(ps, name):
    n = len(ps)

    def body(*refs):
        xs, os = refs[:n], refs[n:2 * n]
        send_sems, recv_sems, local_sems = refs[2 * n:]
        x, y, c = _me()
        j = 2 * x + y
        chips = [(1 - x, y), (x, 1 - y), (1 - x, 1 - y)]
        mine = [pltpu.make_async_copy(xs[t].at[j], os[t].at[j], local_sems.at[t]) for t in range(n)]
        for cp in mine:
            cp.start()
        sends = [_rcopy(xs[t].at[2 * cx + cy], os[t].at[j], send_sems, recv_sems, 3 * t + k, (cx, cy, c))
                 for t in range(n) for k, (cx, cy) in enumerate(chips)]
        for cp in sends:
            cp.start()
        for t in range(n):
            for k, (cx, cy) in enumerate(chips):
                land = os[t].at[2 * cx + cy]
                _rcopy(land, land, send_sems, recv_sems, 3 * t + k, (x, y, c)).wait_recv()
        for cp in sends:
            cp.wait_send()
        for cp in mine:
            cp.wait()

    return _comm_call(body, ps, [SDS(p.shape, p.dtype) for p in ps], 3 * n, n, name)


def ag2(rs, name):
    n = len(rs)

    def body(*refs):
        xs, os = refs[:n], refs[n:2 * n]
        send_sems, recv_sems, local_sems = refs[2 * n:]
        x, y, c = _me()
        mine = [pltpu.make_async_copy(xs[t], os[t].at[c], local_sems.at[t]) for t in range(n)]
        for cp in mine:
            cp.start()
        cps = [_rcopy(xs[t], os[t].at[c], send_sems, recv_sems, t, (x, y, 1 - c)) for t in range(n)]
        for cp in cps:
            cp.start()
        for t in range(n):
            land = os[t].at[1 - c]
            _rcopy(land, land, send_sems, recv_sems, t, (x, y, c)).wait_recv()
        for cp in cps:
            cp.wait_send()
        for cp in mine:
            cp.wait()

    return _comm_call(body, rs, [SDS((2,) + r.shape, r.dtype) for r in rs], n, n, name)


def ag8(blk, name):
    m_per, n = blk.shape

    def body(x_ref, out_ref, send_sems, recv_sems, local_sem):
        x, y, c = _me()
        me, sibling = (x, y, c), (x, y, 1 - c)
        chips = [(1 - x, y), (x, 1 - y), (1 - x, 1 - y)]

        def rows(px, py, pc):
            return out_ref.at[pl.ds((4 * px + 2 * py + pc) * m_per, m_per), :]

        def copy(k, block, to, src=None):
            return pltpu.make_async_remote_copy(
                src_ref=rows(*block) if src is None else src, dst_ref=rows(*block), send_sem=send_sems.at[k],
                recv_sem=recv_sems.at[k], device_id=to, device_id_type=MESH)

        mine = pltpu.make_async_copy(x_ref, rows(*me), local_sem)
        mine.start()
        first = [copy(0, me, sibling, src=x_ref)]
        first += [copy(1 + j, me, (*chip, c), src=x_ref) for j, chip in enumerate(chips)]
        for cp in first:
            cp.start()
        passed = [copy(4 + j, (*chip, c), sibling) for j, chip in enumerate(chips)]
        for j, chip in enumerate(chips):
            copy(1 + j, (*chip, c), me).wait_recv()
            passed[j].start()
        copy(0, sibling, me).wait_recv()
        for j, chip in enumerate(chips):
            copy(4 + j, (*chip, 1 - c), me).wait_recv()
        for cp in first + passed:
            cp.wait_send()
        mine.wait()

    return pl.pallas_call(
        body, out_shape=SDS((8 * m_per, n), blk.dtype), in_specs=[pl.BlockSpec(memory_space=pltpu.VMEM)],
        out_specs=pl.BlockSpec(memory_space=pltpu.VMEM),
        scratch_shapes=[pltpu.SemaphoreType.DMA((7,)), pltpu.SemaphoreType.DMA((7,)), pltpu.SemaphoreType.DMA],
        name=name)(blk)


def _split_chips(full, axis):
    n = full.shape[axis] // 4
    parts = full.reshape(full.shape[:axis] + (4, n) + full.shape[axis + 1:])
    return jnp.moveaxis(parts, axis, 0)


def _merge_chips(gathered, axis):
    parts = jnp.moveaxis(gathered, 0, axis)
    return parts.reshape(parts.shape[:axis] + (parts.shape[axis] * parts.shape[axis + 1],) + parts.shape[axis + 2:])


def _to_ref_cols(main, rank):
    pieces = []
    for n, width in REF_SPLITS:
        if n == "g_a":
            pieces.append(rank[..., :RANK])
        else:
            off = OUR_COLS[n][0]
            pieces.append(main[..., off:off + width])
    return jnp.concatenate(pieces, axis=-1)


def _from_ref_cols(w):
    offs, o = {}, 0
    for n, width in REF_SPLITS:
        offs[n] = (o, width)
        o += width
    main = jnp.concatenate([w[..., offs[n][0]:offs[n][0] + offs[n][1]] for n in sorted(OUR_COLS, key=lambda k: OUR_COLS[k][0])],
                           axis=-1)
    ro = offs["g_a"][0]
    rank = jnp.pad(w[..., ro:ro + RANK], [(0, 0)] * (w.ndim - 1) + [(0, RANKP - RANK)])
    return main, rank


def _layer_fwd(h, p_i, W, li):
    t = f"l{li}_"
    sv = {"h0": h}
    xn = rms_fwd(h, W["norm1_g"], t + "rms1")
    proj = mm_nn(xn, W["w_in_main"], name=t + "inproj")
    pa = mm_nn(xn, W["w_in_rank"], name=t + "inproj_rank")
    y = sg_fwd(proj, W["sg_ln_g"], W["sg_ln_b"], W["sg_wm"], W["sg_bsb"], t + "sg_fwd")
    y, states = gla_fwd(proj, pa, W["gla_wa"], W["gla_b_a"], W["gla_norm_g"], y, t + "gla_fwd")
    y = att_fwd(proj, W["att_bias"], y, t + "att_fwd")
    y = conv_fwd(proj, W["conv_dw_w"], W["conv_dw_b"], W["conv_ln_g"], W["conv_ln_b"], y, t + "conv_fwd")
    gate = mm_nn(xn, W["w_gate_all"], bias=W["b_gate_all"], act="sigmoid", name=t + "gate")
    z = mm_nn(y, W["w_branch"], name=t + "branch")
    m = gate_merge_fwd(gate, z, t + "merge")
    h1 = mm_nn(m, W["w_out"], res=h, out_dtype=f32, name=t + "outproj")
    hn = rms_fwd(h1, W["norm2_g"], t + "rms2")
    a = mm_nn(hn, W["w_ff1"], name=t + "ff1")
    h2 = mm_nn(a, W["w_ff2"], pre="relu2", res=h1, out_dtype=f32, name=t + "ff2")
    hg = rms_fwd(h2, W["norm3_g"], t + "rms3")
    pg = mm_nn(hg, W["w_ple_gate"], bias=W["b_ple_gate"], act="sigmoid", name=t + "ple_gate")
    h3, e = mm_nn(p_i, W["w_ple"], mul=pg, res=h2, out_dtype=f32, raw_out=True, name=t + "ple_out")
    sv.update(xn=xn, proj=proj, pa=pa, states=states, y=y, gate=gate, z=z, m=m, h1=h1, hn=hn, a=a, h2=h2, hg=hg, pg=pg, e=e)
    return h3, sv


def _layer_bwd(dh3, sv, p_i, W, li):
    t = f"l{li}_b_"
    G = {}
    dpg, de, G["b_ple_gate"] = ple_bwd_ew(dh3, sv["e"], sv["pg"], t + "ple_ew")
    G["w_ple_gate"] = mm_tn(sv["hg"], dpg, name=t + "dw_ple_gate")[0]
    G["w_ple"] = mm_tn(p_i, de, name=t + "dw_ple")[0]
    dhg = mm_nt(dpg, W["w_ple_gate"], name=t + "dhg")
    dh2, G["norm3_g"] = rms_bwd(dhg, sv["h2"], W["norm3_g"], dh3, t + "rms3")
    da = mm_nt(dh2, W["w_ff2"], post_a=sv["a"], out_dtype=bf16, name=t + "da")
    G["w_ff2"] = mm_tn(sv["a"], dh2, pre="relu2", name=t + "dw_ff2")[0]
    G["w_ff1"] = mm_tn(sv["hn"], da, name=t + "dw_ff1")[0]
    dhn = mm_nt(da, W["w_ff1"], name=t + "dhn")
    dh1, G["norm2_g"] = rms_bwd(dhn, sv["h1"], W["norm2_g"], dh2, t + "rms2")
    dm = mm_nt(dh1, W["w_out"], out_dtype=bf16, name=t + "dm")
    G["w_out"] = mm_tn(sv["m"], dh1, name=t + "dw_out")[0]
    dz, dgp, G["b_gate_all"] = gate_merge_bwd(dm, sv["gate"], sv["z"], t + "merge")
    G["w_branch"] = mm_tn(sv["y"], dz, G=4, name=t + "dw_branch")
    dy = mm_nt(dz, W["w_branch"], out_dtype=bf16, name=t + "dy")
    G["w_gate_all"] = mm_tn(sv["xn"], dgp, name=t + "dw_gate")[0]
    dxn = mm_nt(dgp, W["w_gate_all"], name=t + "dxn_gate")
    proj = sv["proj"]
    dproj, dwm, dbs, G["sg_ln_g"], G["sg_ln_b"] = sg_bwd(proj, dy, W["sg_ln_g"], W["sg_ln_b"], W["sg_wm"], W["sg_bsb"],
                                                          W["sg_maskf"], t + "sg")
    G["sg_w"], G["sg_b"] = dwm, dbs[:, :, 0]
    dproj, dpa, dwa, G["gla_b_a"], G["gla_norm_g"] = gla_bwd(proj, sv["pa"], sv["states"], dy, W["gla_wa"], W["gla_b_a"],
                                                             W["gla_norm_g"], dproj, t + "gla")
    G["gla_w_a2"] = dwa[:RANK]
    dproj, dkp, dvp, dbias = att_bwd(proj, sv["y"], dy, W["att_bias"], dproj, t + "att")
    dproj = att_shift_add(dkp, dvp, dproj, t + "att_kv")
    G["att_rel_bias"] = att_bias_grad(dbias, t + "att_bias")
    dz_c, G["conv_ln_g"], G["conv_ln_b"], G["conv_dw_b"] = conv_bwd_norm(proj, dy, W["conv_dw_w"], W["conv_dw_b"],
                                                                        W["conv_ln_g"], W["conv_ln_b"], t + "conv_norm")
    dproj, G["conv_dw_w"] = conv_bwd_taps(proj, dz_c, W["conv_dw_w"], dproj, t + "conv_taps")
    G["w_in_main"] = mm_tn(sv["xn"], dproj, name=t + "dw_in")[0]
    G["w_in_rank"] = mm_tn(sv["xn"], dpa, name=t + "dw_in_rank")[0]
    dxn = mm_nt(dpa, W["w_in_rank"], res=dxn, name=t + "dxn_rank")
    dxn = mm_nt(dproj, W["w_in_main"], res=dxn, name=t + "dxn_main")
    dh0, G["norm1_g"] = rms_bwd(dxn, sv["h0"], W["norm1_g"], dh1, t + "rms1")
    return dh0, G


def _prep_layer_weights(gathered, repl, li):
    W = {}
    row = lambda a: a.reshape(1, -1)
    full = {n: _merge_chips(gathered[n][li], SHARDED[n][1]) for n in BIG}
    small = {n: _merge_chips(gathered[n][li], SHARDED[n][1]) for n in SMALL}
    main, rank = _from_ref_cols(full["w_in"])
    W["w_in_main"], W["w_in_rank"] = main[None], rank[None]
    W["w_branch"] = full["w_branch"]
    W["w_gate_all"] = jnp.transpose(full["w_gate"], (1, 0, 2)).reshape(1, D, 4 * D)
    W["b_gate_all"] = small["b_gate"].reshape(1, 4 * D)
    for n in ("w_out", "w_ff1", "w_ff2", "w_ple_gate", "w_ple"):
        W[n] = full[n][None]
    for n in ("norm1_g", "norm2_g", "norm3_g", "b_ple_gate", "sg_ln_g", "sg_ln_b", "gla_b_a", "gla_norm_g", "conv_dw_b",
              "conv_ln_g", "conv_ln_b"):
        W[n] = row(repl[n][li])
    pos = np.arange(128)
    mask = (pos[None, :] // CHUNK) <= (pos[:, None] // CHUNK)
    W["sg_maskf"] = jnp.asarray(mask, f32)
    W["sg_wm"] = jnp.where(mask[None], repl["sg_w"][li], 0.0).astype(bf16)
    W["sg_bsb"] = jnp.broadcast_to(repl["sg_b"][li][:, :, None], (4, 128, 128))
    W["gla_wa"] = jnp.pad(small["gla_w_a2"], ((0, RANKP - RANK), (0, 0))).astype(bf16)
    W["att_bias"] = att_bias_build(small["att_rel_bias"], f"l{li}_att_bias")
    W["conv_dw_w"] = small["conv_dw_w"]
    return W


def _layer_grads_to_ref(G):
    out = {}
    out["w_in"] = _to_ref_cols(G["w_in_main"], G["w_in_rank"])
    out["w_gate"] = jnp.transpose(G["w_gate_all"].reshape(D, 4, D), (1, 0, 2))
    out["b_gate"] = G["b_gate_all"].reshape(4, D)
    for n in ("w_branch", "w_out", "w_ff1", "w_ff2", "w_ple_gate", "w_ple", "gla_w_a2", "att_rel_bias", "conv_dw_w", "sg_w",
              "sg_b"):
        out[n] = G[n]
    for n in ("norm1_g", "norm2_g", "norm3_g", "b_ple_gate", "sg_ln_g", "sg_ln_b", "gla_b_a", "gla_norm_g", "conv_dw_b",
              "conv_ln_g", "conv_ln_b"):
        out[n] = G[n].reshape(-1)
    return out


def kernel(x, p, norm1_g, w_in, sg_ln_g, sg_ln_b, sg_w, sg_b, gla_w_a2, gla_b_a, gla_norm_g, att_rel_bias, conv_dw_w, conv_dw_b, conv_ln_g, conv_ln_b, w_branch, w_gate, b_gate, w_out, norm2_g, w_ff1, w_ff2, norm3_g, w_ple_gate, b_ple_gate, w_ple, final_g, loss_target, m_norm1_g, m_w_in, m_sg_ln_g, m_sg_ln_b, m_sg_w, m_sg_b, m_gla_w_a2, m_gla_b_a, m_gla_norm_g, m_att_rel_bias, m_conv_dw_w, m_conv_dw_b, m_conv_ln_g, m_conv_ln_b, m_w_branch, m_w_gate, m_b_gate, m_w_out, m_norm2_g, m_w_ff1, m_w_ff2, m_norm3_g, m_w_ple_gate, m_b_ple_gate, m_w_ple, m_final_g, v_norm1_g, v_w_in, v_sg_ln_g, v_sg_ln_b, v_sg_w, v_sg_b, v_gla_w_a2, v_gla_b_a, v_gla_norm_g, v_att_rel_bias, v_conv_dw_w, v_conv_dw_b, v_conv_ln_g, v_conv_ln_b, v_w_branch, v_w_gate, v_b_gate, v_w_out, v_norm2_g, v_w_ff1, v_w_ff2, v_norm3_g, v_w_ple_gate, v_b_ple_gate, v_w_ple, v_final_g):
    args = dict(locals())
    weights = {n: args[n] for n in W_ORDER}
    moments_m = {n: args["m_" + n] for n in W_ORDER}
    moments_v = {n: args["v_" + n] for n in W_ORDER}
    c = lax.axis_index("c")
    sharded_names = BIG + SMALL

    shards = [weights[n].astype(bf16) for n in BIG] + [weights[n] for n in SMALL]
    gathered = dict(zip(sharded_names, ag4(shards, "ag_weights")))
    repl = {n: weights[n] for n in REPL}

    h = x[0]
    Ws, saved = [], []
    for li in range(DEPTH):
        W = _prep_layer_weights(gathered, repl, li)
        h, sv = _layer_fwd(h, p[li, 0], W, li)
        Ws.append(W)
        saved.append(sv)
    loss_part, dh, dfinal = loss_head(h, final_g.reshape(1, D), loss_target[0], "loss_head")
    loss = lax.psum(loss_part[0, 0], ("x", "y", "c"))

    layer_grads = [None] * DEPTH
    for li in reversed(range(DEPTH)):
        dh, G = _layer_bwd(dh, saved[li], p[li, 0], Ws[li], li)
        layer_grads[li] = _layer_grads_to_ref(G)
    grad_x = dh[None]

    gps = [jnp.stack([_split_chips(layer_grads[li][n], SHARDED[n][1]) for li in range(DEPTH)]).astype(bf16) for n in BIG]
    ras = sib_other_layer(gps, "rs_sibling_layer")
    psums = [add_halves(g, r, c, "rs_add_" + n) for n, g, r in zip(BIG, gps, ras)]
    rbs = a2a4(psums, "rs_all_to_all")
    reds = [sum_slots(r, "rs_sum_" + n) for n, r in zip(BIG, rbs)]
    grads = dict(zip(BIG, ag2(reds, "rs_sibling_gather")))

    local = {n: jnp.stack([layer_grads[li][n] for li in range(DEPTH)]) for n in tuple(REPL)[:-1] + SMALL}
    local["final_g"] = dfinal.reshape(D)
    rnames = tuple(REPL) + SMALL
    rflat = jnp.concatenate([local[n].reshape(-1) for n in rnames])
    rflat = jnp.pad(rflat, (0, REPL_ROWS * PACK_W - rflat.shape[0])).reshape(REPL_ROWS, PACK_W)
    rall = ag8(rflat, "ar_gather").reshape(8, REPL_ROWS, PACK_W)
    rsum = sum_slots(rall, "ar_sum").reshape(-1)
    chip = 2 * lax.axis_index("x") + lax.axis_index("y")
    off = 0
    for n in rnames:
        shape = local[n].shape
        size = int(np.prod(shape))
        g = rsum[off:off + size].reshape(shape)
        off += size
        if n in SMALL:
            ax = SHARDED[n][1] + 1
            g = lax.dynamic_slice_in_dim(g, chip * (shape[ax] // 4), shape[ax] // 4, axis=ax)
        grads[n] = g

    deltas, new_m, new_v = {}, {}, {}
    for n in W_ORDER:
        deltas[n], new_m[n], new_v[n] = adamw(weights[n], grads[n], moments_m[n], moments_v[n], "adamw_" + n)
    return (loss, grad_x, *[grads[n] for n in W_ORDER], *[deltas[n] for n in W_ORDER], *[new_m[n] for n in W_ORDER],
            *[new_v[n] for n in W_ORDER])
```

```python
import functools

import jax
import jax.numpy as jnp
import numpy as np
from jax import lax
from jax.experimental import pallas as pl
from jax.experimental.pallas import tpu as pltpu

f32, bf16 = jnp.float32, jnp.bfloat16
HI = lax.Precision.HIGHEST
MESH = pl.DeviceIdType.MESH
SDS = jax.ShapeDtypeStruct
BS = pl.BlockSpec
ANY = pl.BlockSpec(memory_space=pl.ANY)

D = 1024
DEPTH = 2
CHUNK = 64
BW = 512
NP = 5120
RANK = 16
RANKP = 128
DFF = 4096
PLE = 256
CONV_K = 31
HALO = 32
TQ = 256
WIN = 768
REL_TABLE = 320
EPS = 1e-6
NEG_INF = -1e30
VMEM_LIMIT = 56 * 1024 * 1024

ADAM_LR, ADAM_B1, ADAM_B2, ADAM_EPS, ADAM_WD, ADAM_STEP = 0.001, 0.9, 0.999, 1e-08, 0.01, 10

OUR_COLS = dict(g_q=(0, 256), g_k=(256, 256), g_v=(512, 512), g_r=(1024, 512), a_q=(1536, 512), a_k=(2048, 512),
                a_v=(2560, 512), sg_u=(3072, 512), sg_v=(3584, 512), c_a=(4096, 512), c_g=(4608, 512))
REF_SPLITS = (("sg_u", 512), ("sg_v", 512), ("g_q", 256), ("g_k", 256), ("g_v", 512), ("g_r", 512), ("g_a", 16),
              ("a_q", 512), ("a_k", 512), ("a_v", 512), ("c_a", 512), ("c_g", 512))

SHARDED = dict(w_in=((1024, 5136), 1), w_branch=((4, 512, 1024), 2), w_gate=((4, 1024, 1024), 1), w_out=((1024, 1024), 0),
               w_ff1=((1024, 4096), 1), w_ff2=((4096, 1024), 0), w_ple_gate=((1024, 1024), 0), w_ple=((256, 1024), 1),
               gla_w_a2=((16, 256), 1), att_rel_bias=((8, 320), 1), conv_dw_w=((31, 512), 1), b_gate=((4, 1024), 1))
BIG = ("w_in", "w_branch", "w_gate", "w_out", "w_ff1", "w_ff2", "w_ple_gate", "w_ple")
SMALL = ("gla_w_a2", "att_rel_bias", "conv_dw_w", "b_gate")
REPL = dict(norm1_g=(2, 1024), sg_ln_g=(2, 512), sg_ln_b=(2, 512), sg_w=(2, 4, 128, 128), sg_b=(2, 4, 128), gla_b_a=(2, 256),
            gla_norm_g=(2, 512), conv_dw_b=(2, 512), conv_ln_g=(2, 512), conv_ln_b=(2, 512), norm2_g=(2, 1024),
            norm3_g=(2, 1024), b_ple_gate=(2, 1024), final_g=(1024,))
W_ORDER = ['norm1_g', 'w_in', 'sg_ln_g', 'sg_ln_b', 'sg_w', 'sg_b', 'gla_w_a2', 'gla_b_a', 'gla_norm_g', 'att_rel_bias',
           'conv_dw_w', 'conv_dw_b', 'conv_ln_g', 'conv_ln_b', 'w_branch', 'w_gate', 'b_gate', 'w_out', 'norm2_g', 'w_ff1',
           'w_ff2', 'norm3_g', 'w_ple_gate', 'b_ple_gate', 'w_ple', 'final_g']
PACK_W = 1024
REPL_ROWS = 200
TM = 1024


def _tile(s):
    return 512 if s % 512 == 0 else s


def _cp(*sem):
    return pltpu.CompilerParams(dimension_semantics=sem, vmem_limit_bytes=VMEM_LIMIT)


def rms_fwd(h, g, name):
    S, Dm = h.shape
    T = _tile(S)

    def body(h_ref, g_ref, o_ref):
        x = h_ref[...]
        r = lax.rsqrt(jnp.mean(x * x, axis=-1, keepdims=True) + EPS)
        o_ref[...] = (x * r * g_ref[...]).astype(bf16)

    return pl.pallas_call(
        body, out_shape=SDS((S, Dm), bf16), grid=(S // T,),
        in_specs=[BS((T, Dm), lambda i: (i, 0)), BS((1, Dm), lambda i: (0, 0))],
        out_specs=BS((T, Dm), lambda i: (i, 0)), compiler_params=_cp("parallel"), name=name)(h, g)


def mm_nn(x, w, *, name, bias=None, act=None, pre=None, mul=None, res=None, out_dtype=bf16, raw_out=False):
    S = x.shape[0]
    G, K, N = w.shape
    T = min(TM, S)
    tn = min(1024 if K <= 1024 else 512, N)
    nj = N // tn
    extras = [a for a in (bias, mul, res) if a is not None]

    def body(*refs):
        it = iter(refs)
        x_ref, w_ref = next(it), next(it)
        b_ref = next(it) if bias is not None else None
        m_ref = next(it) if mul is not None else None
        r_ref = next(it) if res is not None else None
        o_ref = next(it)
        xv = x_ref[...]
        if pre == "relu2":
            xf = jnp.maximum(xv.astype(f32), 0.0)
            xv = xf * xf
        acc = jnp.dot(xv.astype(bf16), w_ref[0], preferred_element_type=f32)
        if raw_out:
            next(it)[...] = acc.astype(bf16)
        if b_ref is not None:
            acc = acc + b_ref[...]
        if act == "sigmoid":
            acc = jax.nn.sigmoid(acc)
        if m_ref is not None:
            acc = acc * m_ref[...].astype(f32)
        if r_ref is not None:
            acc = r_ref[...].astype(f32) + acc
        o_ref[...] = acc.astype(out_dtype)

    in_specs = [BS((T, K), lambda i, g, j: (i, g)), BS((1, K, tn), lambda i, g, j: (g, 0, j))]
    if bias is not None:
        in_specs.append(BS((1, tn), lambda i, g, j: (0, g * nj + j)))
    for a in (mul, res):
        if a is not None:
            in_specs.append(BS((T, tn), lambda i, g, j: (i, g * nj + j)))
    ospec = BS((T, tn), lambda i, g, j: (i, g * nj + j))
    out_shape = SDS((S, G * N), out_dtype)
    if raw_out:
        out_shape, ospec = (out_shape, SDS((S, G * N), bf16)), (ospec, ospec)
    return pl.pallas_call(
        body, out_shape=out_shape, grid=(S // T, G, nj), in_specs=in_specs, out_specs=ospec,
        compiler_params=_cp("parallel", "parallel", "parallel"), name=name)(x, w, *extras)


def mm_nt(dy, w, *, name, res=None, post_a=None, out_dtype=f32):
    S = dy.shape[0]
    G, K, N = w.shape
    T = min(TM, S)
    tk = min(1024 if N <= 1024 else 512, K)
    nk = K // tk
    extras = [a for a in (res, post_a) if a is not None]

    def body(*refs):
        it = iter(refs)
        d_ref, w_ref = next(it), next(it)
        r_ref = next(it) if res is not None else None
        a_ref = next(it) if post_a is not None else None
        o_ref = next(it)
        acc = lax.dot_general(d_ref[...].astype(bf16), w_ref[0], (((1,), (1,)), ((), ())), preferred_element_type=f32)
        if r_ref is not None:
            acc = acc + r_ref[...].astype(f32)
        if a_ref is not None:
            acc = acc * (2.0 * jnp.maximum(a_ref[...].astype(f32), 0.0))
        o_ref[...] = acc.astype(out_dtype)

    in_specs = [BS((T, N), lambda i, g, j: (i, g)), BS((1, tk, N), lambda i, g, j: (g, j, 0))]
    for a in extras:
        in_specs.append(BS((T, tk), lambda i, g, j: (i, g * nk + j)))
    return pl.pallas_call(
        body, out_shape=SDS((S, G * K), out_dtype), grid=(S // T, G, nk), in_specs=in_specs,
        out_specs=BS((T, tk), lambda i, g, j: (i, g * nk + j)),
        compiler_params=_cp("parallel", "parallel", "parallel"), name=name)(dy, w, *extras)


def mm_tn(x, dy, *, name, G=1, pre=None, ts=1024):
    S = x.shape[0]
    K, N = x.shape[1] // G, dy.shape[1] // G
    tk, tn = min(K, 1024), min(N, 1024)
    nk, nn = K // tk, N // tn
    ts = min(ts, S)

    def body(x_ref, d_ref, o_ref):
        @pl.when(pl.program_id(3) == 0)
        def _():
            o_ref[...] = jnp.zeros_like(o_ref)

        xv = x_ref[...]
        if pre == "relu2":
            xf = jnp.maximum(xv.astype(f32), 0.0)
            xv = xf * xf
        o_ref[0] += lax.dot_general(xv.astype(bf16), d_ref[...].astype(bf16), (((0,), (0,)), ((), ())),
                                    preferred_element_type=f32)

    return pl.pallas_call(
        body, out_shape=SDS((G, K, N), f32), grid=(G, nk, nn, S // ts),
        in_specs=[BS((ts, tk), lambda g, a, b, s: (s, g * nk + a)), BS((ts, tn), lambda g, a, b, s: (s, g * nn + b))],
        out_specs=BS((1, tk, tn), lambda g, a, b, s: (g, a, b)),
        compiler_params=_cp("parallel", "parallel", "parallel", "arbitrary"), name=name)(x, dy)


def rms_bwd(dxn, x, g, dres, name):
    S, Dm = x.shape
    T = _tile(S)

    def body(*refs):
        if dres is not None:
            d_ref, x_ref, g_ref, r_ref, dx_ref, dg_ref = refs
        else:
            d_ref, x_ref, g_ref, dx_ref, dg_ref = refs
        xv = x_ref[...]
        d = d_ref[...].astype(f32)
        r = lax.rsqrt(jnp.mean(xv * xv, axis=-1, keepdims=True) + EPS)
        u = d * g_ref[...]
        dx = r * u - xv * ((r * r * r) * (1.0 / Dm)) * jnp.sum(u * xv, axis=-1, keepdims=True)
        if dres is not None:
            dx = r_ref[...] + dx
        dx_ref[...] = dx

        @pl.when(pl.program_id(0) == 0)
        def _():
            dg_ref[...] = jnp.zeros_like(dg_ref)

        dg_ref[...] += jnp.sum(d * xv * r, axis=0, keepdims=True)

    tok = BS((T, Dm), lambda i: (i, 0))
    vec = BS((1, Dm), lambda i: (0, 0))
    args = (dxn, x, g) + ((dres,) if dres is not None else ())
    return pl.pallas_call(
        body, out_shape=(SDS((S, Dm), f32), SDS((1, Dm), f32)), grid=(S // T,),
        in_specs=[tok, tok, vec] + ([tok] if dres is not None else []), out_specs=(tok, vec),
        compiler_params=_cp("arbitrary"), name=name)(*args)


def loss_head(h, g, target, name):
    S, Dm = h.shape
    T = _tile(S)

    def body(h_ref, g_ref, t_ref, loss_ref, dh_ref, dg_ref):
        @pl.when(pl.program_id(0) == 0)
        def _():
            loss_ref[...] = jnp.zeros_like(loss_ref)
            dg_ref[...] = jnp.zeros_like(dg_ref)

        xv = h_ref[...]
        gv = g_ref[...]
        r = lax.rsqrt(jnp.mean(xv * xv, axis=-1, keepdims=True) + EPS)
        diff = xv * r * gv - t_ref[...]
        loss_ref[...] += 0.5 * jnp.sum(jnp.mean(diff * diff, axis=-1, keepdims=True))
        d = diff * (1.0 / Dm)
        u = d * gv
        dh_ref[...] = r * u - xv * ((r * r * r) * (1.0 / Dm)) * jnp.sum(u * xv, axis=-1, keepdims=True)
        dg_ref[...] += jnp.sum(d * xv * r, axis=0, keepdims=True)

    tok = BS((T, Dm), lambda i: (i, 0))
    vec = BS((1, Dm), lambda i: (0, 0))
    return pl.pallas_call(
        body, out_shape=(SDS((1, 128), f32), SDS((S, Dm), f32), SDS((1, Dm), f32)), grid=(S // T,),
        in_specs=[tok, vec, tok], out_specs=(BS((1, 128), lambda i: (0, 0)), tok, vec),
        compiler_params=_cp("arbitrary"), name=name)(h, g, target)


def gate_merge_fwd(gate, z, name):
    S = gate.shape[0]
    T = _tile(S)

    def body(g_ref, z_ref, o_ref):
        acc = jnp.zeros((T, D), f32)
        for n in range(4):
            acc = acc + g_ref[:, n * D:(n + 1) * D].astype(f32) * z_ref[:, n * D:(n + 1) * D].astype(f32)
        o_ref[...] = acc.astype(bf16)

    wide = BS((T, 4 * D), lambda i: (i, 0))
    return pl.pallas_call(body, out_shape=SDS((S, D), bf16), grid=(S // T,), in_specs=[wide, wide],
                          out_specs=BS((T, D), lambda i: (i, 0)), compiler_params=_cp("parallel"), name=name)(gate, z)


def gate_merge_bwd(dm, gate, z, name):
    S = gate.shape[0]
    T = _tile(S)

    def body(dm_ref, g_ref, z_ref, dz_ref, dg_ref, db_ref):
        @pl.when(pl.program_id(0) == 0)
        def _():
            db_ref[...] = jnp.zeros_like(db_ref)

        dmv = dm_ref[...].astype(f32)
        for n in range(4):
            cols = slice(n * D, (n + 1) * D)
            gv = g_ref[:, cols].astype(f32)
            dz_ref[:, cols] = (dmv * gv).astype(bf16)
            dgp = dmv * z_ref[:, cols].astype(f32) * gv * (1.0 - gv)
            dg_ref[:, cols] = dgp.astype(bf16)
            db_ref[:, cols] += jnp.sum(dgp, axis=0, keepdims=True)

    wide = BS((T, 4 * D), lambda i: (i, 0))
    return pl.pallas_call(
        body, out_shape=(SDS((S, 4 * D), bf16), SDS((S, 4 * D), bf16), SDS((1, 4 * D), f32)), grid=(S // T,),
        in_specs=[BS((T, D), lambda i: (i, 0)), wide, wide], out_specs=(wide, wide, BS((1, 4 * D), lambda i: (0, 0))),
        compiler_params=_cp("arbitrary"), name=name)(dm, gate, z)


def ple_bwd_ew(dh, e, pg, name):
    S = dh.shape[0]
    T = _tile(S)

    def body(dh_ref, e_ref, pg_ref, dp_ref, de_ref, db_ref):
        @pl.when(pl.program_id(0) == 0)
        def _():
            db_ref[...] = jnp.zeros_like(db_ref)

        d = dh_ref[...]
        g = pg_ref[...].astype(f32)
        dpre = d * e_ref[...].astype(f32) * g * (1.0 - g)
        dp_ref[...] = dpre.astype(bf16)
        de_ref[...] = (d * g).astype(bf16)
        db_ref[...] += jnp.sum(dpre, axis=0, keepdims=True)

    tok = BS((T, D), lambda i: (i, 0))
    return pl.pallas_call(
        body, out_shape=(SDS((S, D), bf16), SDS((S, D), bf16), SDS((1, D), f32)), grid=(S // T,),
        in_specs=[tok, tok, tok], out_specs=(tok, tok, BS((1, D), lambda i: (0, 0))),
        compiler_params=_cp("arbitrary"), name=name)(dh, e, pg)


_GK = 0.7978845608028654
_GC = 0.044715


def _gelu(x):
    return 0.5 * x * (1.0 + jnp.tanh(_GK * (x + _GC * (x * x * x))))


def _gelu_grad(x):
    x2 = x * x
    t = jnp.tanh(_GK * (x + _GC * (x * x2)))
    return 0.5 * (1.0 + t) + 0.5 * x * (1.0 - t * t) * (_GK * (1.0 + 3.0 * _GC * x2))


def _ln_stats(v):
    mu = jnp.mean(v, axis=-1, keepdims=True)
    vc = v - mu
    rs = lax.rsqrt(jnp.mean(vc * vc, axis=-1, keepdims=True) + EPS)
    return vc * rs, rs


def _ln_bwd(dvh, vh, rs):
    return rs * (dvh - jnp.mean(dvh, axis=-1, keepdims=True) - vh * jnp.mean(dvh * vh, axis=-1, keepdims=True))


def sg_fwd(proj, lg, lb, wm, bsb, name):
    S = proj.shape[0]
    T = _tile(S)
    cu, cv = OUR_COLS["sg_u"][0] // BW, OUR_COLS["sg_v"][0] // BW

    def body(u_ref, v_ref, lg_ref, lb_ref, wm_ref, bsb_ref, o_ref):
        for b in range(T // 128):
            rows = slice(b * 128, (b + 1) * 128)
            u = _gelu(u_ref[rows, :].astype(f32))
            vh, _ = _ln_stats(_gelu(v_ref[rows, :].astype(f32)))
            vb = (vh * lg_ref[...] + lb_ref[...]).astype(bf16)
            outs = []
            for g in range(4):
                cols = slice(g * 128, (g + 1) * 128)
                mixed = jnp.dot(wm_ref[g], vb[:, cols], preferred_element_type=f32) + bsb_ref[g]
                outs.append(u[:, cols] * mixed)
            o_ref[rows, :] = jnp.concatenate(outs, axis=1).astype(bf16)

    vec = BS((1, BW), lambda i: (0, 0))
    cube = BS((4, 128, 128), lambda i: (0, 0, 0))
    return pl.pallas_call(
        body, out_shape=SDS((S, 4 * BW), bf16), grid=(S // T,),
        in_specs=[BS((T, BW), lambda i: (i, cu)), BS((T, BW), lambda i: (i, cv)), vec, vec, cube, cube],
        out_specs=BS((T, BW), lambda i: (i, 0)), compiler_params=_cp("parallel"), name=name)(proj, proj, lg, lb, wm, bsb)


def sg_bwd(proj, dy, lg, lb, wm, bsb, maskf, name):
    S = proj.shape[0]
    T = _tile(S)
    cu, cv = OUR_COLS["sg_u"][0] // BW, OUR_COLS["sg_v"][0] // BW
    creg = OUR_COLS["sg_u"][0] // (2 * BW)

    def body(u_ref, v_ref, dy_ref, lg_ref, lb_ref, wm_ref, bsb_ref, mk_ref, dp_ref, dwm_ref, dbs_ref, dlg_ref, dlb_ref):
        @pl.when(pl.program_id(0) == 0)
        def _():
            dwm_ref[...] = jnp.zeros_like(dwm_ref)
            dbs_ref[...] = jnp.zeros_like(dbs_ref)
            dlg_ref[...] = jnp.zeros_like(dlg_ref)
            dlb_ref[...] = jnp.zeros_like(dlb_ref)

        for b in range(T // 128):
            rows = slice(b * 128, (b + 1) * 128)
            su = u_ref[rows, :].astype(f32)
            sv = v_ref[rows, :].astype(f32)
            dya = dy_ref[rows, :].astype(f32)
            u = _gelu(su)
            vh, rs = _ln_stats(_gelu(sv))
            vb = (vh * lg_ref[...] + lb_ref[...]).astype(bf16)
            dus, dvls = [], []
            for g in range(4):
                cols = slice(g * 128, (g + 1) * 128)
                mixed = jnp.dot(wm_ref[g], vb[:, cols], preferred_element_type=f32) + bsb_ref[g]
                dus.append(dya[:, cols] * mixed)
                dmg = dya[:, cols] * u[:, cols]
                dmb = dmg.astype(bf16)
                dbs_ref[g] += jnp.broadcast_to(jnp.sum(dmg, axis=1, keepdims=True), (128, 128))
                dwm_ref[g] += mk_ref[...] * lax.dot_general(dmb, vb[:, cols], (((1,), (1,)), ((), ())),
                                                            preferred_element_type=f32)
                dvls.append(lax.dot_general(wm_ref[g], dmb, (((0,), (0,)), ((), ())), preferred_element_type=f32))
            du = jnp.concatenate(dus, axis=1)
            dvln = jnp.concatenate(dvls, axis=1)
            dlg_ref[...] += jnp.sum(dvln * vh, axis=0, keepdims=True)
            dlb_ref[...] += jnp.sum(dvln, axis=0, keepdims=True)
            dv = _ln_bwd(dvln * lg_ref[...], vh, rs)
            dp_ref[rows, 0:BW] = (du * _gelu_grad(su)).astype(bf16)
            dp_ref[rows, BW:2 * BW] = (dv * _gelu_grad(sv)).astype(bf16)

    vec = BS((1, BW), lambda i: (0, 0))
    cube = BS((4, 128, 128), lambda i: (0, 0, 0))
    return pl.pallas_call(
        body,
        out_shape=(SDS((S, NP), bf16), SDS((4, 128, 128), f32), SDS((4, 128, 128), f32), SDS((1, BW), f32), SDS((1, BW), f32)),
        grid=(S // T,),
        in_specs=[BS((T, BW), lambda i: (i, cu)), BS((T, BW), lambda i: (i, cv)), BS((T, BW), lambda i: (i, 0)), vec, vec,
                  cube, cube, BS((128, 128), lambda i: (0, 0))],
        out_specs=(BS((T, 2 * BW), lambda i: (i, creg)), cube, cube, vec, vec),
        compiler_params=_cp("arbitrary"), name=name)(proj, proj, dy, lg, lb, wm, bsb, maskf)


_SUB = 64


def _conv_specs(S, T):
    ca, cg = OUR_COLS["c_a"][0] // BW, OUR_COLS["c_g"][0] // BW
    hb = T // HALO
    prev = lambda i: jnp.maximum(i * hb - 1, 0)
    return [BS((T, BW), lambda i: (i, ca)), BS((T, BW), lambda i: (i, cg)),
            BS((HALO, BW), lambda i: (prev(i), ca)), BS((HALO, BW), lambda i: (prev(i), cg))]


def _conv_fill_ybuf(a_ref, g_ref, ap_ref, gp_ref, ybuf):
    T = a_ref.shape[0]
    ybuf[pl.ds(HALO, T), :] = a_ref[...].astype(f32) * jax.nn.sigmoid(g_ref[...].astype(f32))
    first = (pl.program_id(0) == 0).astype(f32)
    ybuf[pl.ds(0, HALO), :] = (1.0 - first) * (ap_ref[...].astype(f32) * jax.nn.sigmoid(gp_ref[...].astype(f32)))


def _conv_taps(w_ref, ybuf, r0):
    acc = jnp.zeros((_SUB, BW), f32)
    for k in range(CONV_K):
        acc = acc + w_ref[k:k + 1, :] * ybuf[pl.ds(r0 + HALO - (CONV_K - 1) + k, _SUB), :]
    return acc


def conv_fwd(proj, w, b, lg, lb, y, name):
    S = proj.shape[0]
    T = _tile(S)

    def body(a_ref, g_ref, ap_ref, gp_ref, w_ref, b_ref, lg_ref, lb_ref, y_in, o_ref, ybuf):
        del y_in
        _conv_fill_ybuf(a_ref, g_ref, ap_ref, gp_ref, ybuf)
        for sb in range(T // _SUB):
            z = _conv_taps(w_ref, ybuf, sb * _SUB) + b_ref[...]
            zh, _ = _ln_stats(z)
            zl = zh * lg_ref[...] + lb_ref[...]
            o_ref[pl.ds(sb * _SUB, _SUB), :] = (zl * jax.nn.sigmoid(zl)).astype(bf16)

    vec = BS((1, BW), lambda i: (0, 0))
    return pl.pallas_call(
        body, out_shape=SDS(y.shape, bf16), grid=(S // T,),
        in_specs=_conv_specs(S, T) + [BS((CONV_K, BW), lambda i: (0, 0)), vec, vec, vec, ANY],
        out_specs=BS((T, BW), lambda i: (i, 3)), scratch_shapes=[pltpu.VMEM((T + HALO, BW), f32)],
        input_output_aliases={8: 0}, compiler_params=_cp("parallel"), name=name)(proj, proj, proj, proj, w, b, lg, lb, y)


def conv_bwd_norm(proj, dy, w, b, lg, lb, name):
    S = proj.shape[0]
    T = _tile(S)

    def body(a_ref, g_ref, ap_ref, gp_ref, dy_ref, w_ref, b_ref, lg_ref, lb_ref, dz_ref, dlg_ref, dlb_ref, db_ref, ybuf):
        @pl.when(pl.program_id(0) == 0)
        def _():
            dlg_ref[...] = jnp.zeros_like(dlg_ref)
            dlb_ref[...] = jnp.zeros_like(dlb_ref)
            db_ref[...] = jnp.zeros_like(db_ref)

        _conv_fill_ybuf(a_ref, g_ref, ap_ref, gp_ref, ybuf)
        for sb in range(T // _SUB):
            rows = pl.ds(sb * _SUB, _SUB)
            z = _conv_taps(w_ref, ybuf, sb * _SUB) + b_ref[...]
            zh, rs = _ln_stats(z)
            zl = zh * lg_ref[...] + lb_ref[...]
            sg = jax.nn.sigmoid(zl)
            dzl = dy_ref[rows, :].astype(f32) * sg * (1.0 + zl * (1.0 - sg))
            dlg_ref[...] += jnp.sum(dzl * zh, axis=0, keepdims=True)
            dlb_ref[...] += jnp.sum(dzl, axis=0, keepdims=True)
            dz = _ln_bwd(dzl * lg_ref[...], zh, rs)
            db_ref[...] += jnp.sum(dz, axis=0, keepdims=True)
            dz_ref[rows, :] = dz

    vec = BS((1, BW), lambda i: (0, 0))
    tok = BS((T, BW), lambda i: (i, 0))
    return pl.pallas_call(
        body, out_shape=(SDS((S, BW), f32), SDS((1, BW), f32), SDS((1, BW), f32), SDS((1, BW), f32)), grid=(S // T,),
        in_specs=_conv_specs(S, T) + [BS((T, BW), lambda i: (i, 3)), BS((CONV_K, BW), lambda i: (0, 0)), vec, vec, vec],
        out_specs=(tok, vec, vec, vec), scratch_shapes=[pltpu.VMEM((T + HALO, BW), f32)],
        compiler_params=_cp("arbitrary"), name=name)(proj, proj, proj, proj, dy, w, b, lg, lb)


def conv_bwd_taps(proj, dz, w, dproj, name):
    S = proj.shape[0]
    T = _tile(S)
    nT = S // T
    hb = T // HALO
    creg = OUR_COLS["c_a"][0] // (2 * BW)

    def body(a_ref, g_ref, ap_ref, gp_ref, dz_ref, dzn_ref, w_ref, dp_in, dp_ref, dw_ref, ybuf, dzbuf, dwacc):
        del dp_in
        i = pl.program_id(0)

        @pl.when(i == 0)
        def _():
            dwacc[...] = jnp.zeros_like(dwacc)

        _conv_fill_ybuf(a_ref, g_ref, ap_ref, gp_ref, ybuf)
        dzbuf[pl.ds(0, T), :] = dz_ref[...]
        dzbuf[pl.ds(T, HALO), :] = (i < nT - 1).astype(f32) * dzn_ref[...]
        for sb in range(T // _SUB):
            r0 = sb * _SUB
            rows = pl.ds(r0, _SUB)
            dzs = dz_ref[rows, :]
            dyg = jnp.zeros((_SUB, BW), f32)
            for k in range(CONV_K):
                ysl = ybuf[pl.ds(r0 + HALO - (CONV_K - 1) + k, _SUB), :]
                dwacc[pl.ds(k * 8, 8), :] += jnp.sum((dzs * ysl).reshape(_SUB // 8, 8, BW), axis=0)
                dyg = dyg + w_ref[k:k + 1, :] * dzbuf[pl.ds(r0 + (CONV_K - 1) - k, _SUB), :]
            av = a_ref[rows, :].astype(f32)
            sg = jax.nn.sigmoid(g_ref[rows, :].astype(f32))
            dp_ref[rows, 0:BW] = (dyg * sg).astype(bf16)
            dp_ref[rows, BW:2 * BW] = (dyg * av * sg * (1.0 - sg)).astype(bf16)

        @pl.when(i == nT - 1)
        def _():
            for k in range(CONV_K):
                dw_ref[k:k + 1, :] = jnp.sum(dwacc[pl.ds(k * 8, 8), :], axis=0, keepdims=True)

    nxt = lambda i: jnp.minimum((i + 1) * hb, S // HALO - 1)
    return pl.pallas_call(
        body, out_shape=(SDS((S, NP), bf16), SDS((CONV_K, BW), f32)), grid=(nT,),
        in_specs=_conv_specs(S, T) + [BS((T, BW), lambda i: (i, 0)), BS((HALO, BW), lambda i: (nxt(i), 0)),
                                      BS((CONV_K, BW), lambda i: (0, 0)), ANY],
        out_specs=(BS((T, 2 * BW), lambda i: (i, creg)), BS((CONV_K, BW), lambda i: (0, 0))),
        scratch_shapes=[pltpu.VMEM((T + HALO, BW), f32), pltpu.VMEM((T + HALO, BW), f32), pltpu.VMEM((CONV_K * 8, BW), f32)],
        input_output_aliases={7: 0}, compiler_params=_cp("arbitrary"), name=name)(proj, proj, proj, proj, dz, dz, w, dproj)


def _toeplitz_index():
    j = lax.broadcasted_iota(jnp.int32, (REL_TABLE, 1024), 1)
    t = lax.broadcasted_iota(jnp.int32, (REL_TABLE, 1024), 0)
    e = ((WIN - 1) - j) & 1023
    tidx = jnp.clip(e - (TQ - 1), -(CHUNK - 1), 256) + (CHUNK - 1)
    return (tidx == t).astype(f32)


def att_bias_build(table, name):
    H = table.shape[0]

    def body(t_ref, o_ref):
        u = jnp.dot(t_ref[...], _toeplitz_index(), precision=HI, preferred_element_type=f32)
        row = lax.broadcasted_iota(jnp.int32, (TQ, 1024), 0)
        r = lax.broadcasted_iota(jnp.int32, (TQ, WIN), 0)
        n = lax.broadcasted_iota(jnp.int32, (TQ, WIN), 1)
        dchunk = (r // CHUNK + 8) - n // CHUNK
        band = (dchunk >= 0) & (dchunk <= 8)
        for h in range(H):
            x = jnp.broadcast_to(u[h:h + 1, :], (TQ, 1024))
            for b in range(8):
                x = jnp.where(((row >> b) & 1) == 1, pltpu.roll(x, 1 << b, 1), x)
            o_ref[h] = jnp.where(band, x[:, :WIN], NEG_INF)

    return pl.pallas_call(body, out_shape=SDS((H, TQ, WIN), f32), compiler_params=pltpu.CompilerParams(vmem_limit_bytes=VMEM_LIMIT),
                          name=name)(table)


def att_bias_grad(dbias, name):
    H = dbias.shape[0]

    def body(d_ref, o_ref):
        row = lax.broadcasted_iota(jnp.int32, (TQ, 1024), 0)
        rows = []
        for h in range(H):
            x = jnp.concatenate([d_ref[h], jnp.zeros((TQ, 1024 - WIN), f32)], axis=1)
            for b in range(8):
                x = jnp.where(((row >> b) & 1) == 1, pltpu.roll(x, 1024 - (1 << b), 1), x)
            rows.append(jnp.sum(x, axis=0, keepdims=True))
        du = jnp.concatenate(rows, axis=0)
        o_ref[...] = lax.dot_general(du, _toeplitz_index(), (((1,), (1,)), ((), ())), precision=HI,
                                     preferred_element_type=f32)

    return pl.pallas_call(body, out_shape=SDS((H, REL_TABLE), f32), compiler_params=pltpu.CompilerParams(vmem_limit_bytes=VMEM_LIMIT),
                          name=name)(dbias)


def _att_specs():
    cq, ck, cv = (OUR_COLS[n][0] // BW for n in ("a_q", "a_k", "a_v"))
    specs = [BS((TQ, BW), lambda i: (i, cq))]
    for col in (ck, cv):
        for back in (2, 1, 0):
            specs.append(BS((TQ, BW), functools.partial(lambda i, back, col: (jnp.maximum(i - back, 0), col), back=back, col=col)))
    return specs


def _att_pen(i):
    n = lax.broadcasted_iota(jnp.int32, (1, WIN), 1)
    return jnp.where(n + (i - 2) * TQ >= 0, 0.0, NEG_INF).astype(f32)


def _att_probs(qa, kp, bias_h, pen):
    s = lax.dot_general(qa, kp, (((1,), (1,)), ((), ())), preferred_element_type=f32) + bias_h + pen
    e = jnp.exp(s - jnp.max(s, axis=-1, keepdims=True))
    return e * (1.0 / jnp.sum(e, axis=-1, keepdims=True))


def att_fwd(proj, bias, y, name):
    S = proj.shape[0]

    def body(q_ref, k2, k1, k0, v2, v1, v0, b_ref, y_in, o_ref, kwin, vwin):
        del y_in
        i = pl.program_id(0)
        for w, (kr, vr) in enumerate(((k2, v2), (k1, v1), (k0, v0))):
            kwin[pl.ds(w * TQ, TQ), :] = kr[...]
            vwin[pl.ds(w * TQ, TQ), :] = vr[...]
        pen = _att_pen(i)
        lo = lax.broadcasted_iota(jnp.int32, (TQ, 128), 1) < 64
        for hp in range(4):
            cols = slice(hp * 128, (hp + 1) * 128)
            qp, kp, vp = q_ref[:, cols] * jnp.asarray(0.125, bf16), kwin[:, cols], vwin[:, cols]
            outs = []
            for a in range(2):
                qa = jnp.where(lo if a == 0 else ~lo, qp, jnp.zeros_like(qp))
                p = _att_probs(qa, kp, b_ref[2 * hp + a], pen)
                outs.append(jnp.dot(p.astype(bf16), vp, preferred_element_type=f32))
            o_ref[:, cols] = jnp.where(lo, outs[0], outs[1]).astype(bf16)

    return pl.pallas_call(
        body, out_shape=SDS(y.shape, bf16), grid=(S // TQ,),
        in_specs=_att_specs() + [BS((8, TQ, WIN), lambda i: (0, 0, 0), pipeline_mode=pl.Buffered(1)), ANY],
        out_specs=BS((TQ, BW), lambda i: (i, 2)),
        scratch_shapes=[pltpu.VMEM((WIN, BW), bf16), pltpu.VMEM((WIN, BW), bf16)],
        input_output_aliases={8: 0}, compiler_params=_cp("parallel"), name=name)(
            proj, proj, proj, proj, proj, proj, proj, bias, y)


def att_bwd(proj, y, dy, bias, dproj, name):
    S = proj.shape[0]
    cq = OUR_COLS["a_q"][0] // BW

    def body(q_ref, k2, k1, k0, v2, v1, v0, b_ref, o_ref, do_ref, dp_in, dq_ref, dkp_ref, dvp_ref, db_ref, kwin, vwin):
        del dp_in
        i = pl.program_id(0)

        @pl.when(i == 0)
        def _():
            db_ref[...] = jnp.zeros_like(db_ref)

        for w, (kr, vr) in enumerate(((k2, v2), (k1, v1), (k0, v0))):
            kwin[pl.ds(w * TQ, TQ), :] = kr[...]
            vwin[pl.ds(w * TQ, TQ), :] = vr[...]
        pen = _att_pen(i)
        lo = lax.broadcasted_iota(jnp.int32, (TQ, 128), 1) < 64
        for hp in range(4):
            cols = slice(hp * 128, (hp + 1) * 128)
            qp, kp, vp = q_ref[:, cols] * jnp.asarray(0.125, bf16), kwin[:, cols], vwin[:, cols]
            dop, op = do_ref[:, cols], o_ref[:, cols]
            dqs = []
            dk = jnp.zeros((WIN, 128), f32)
            dv = jnp.zeros((WIN, 128), f32)
            for a in range(2):
                sel = lo if a == 0 else ~lo
                qa = jnp.where(sel, qp, jnp.zeros_like(qp))
                doa = jnp.where(sel, dop, jnp.zeros_like(dop))
                p = _att_probs(qa, kp, b_ref[2 * hp + a], pen)
                dpv = lax.dot_general(doa, vp, (((1,), (1,)), ((), ())), preferred_element_type=f32)
                delta = jnp.sum(doa.astype(f32) * op.astype(f32), axis=-1, keepdims=True)
                ds = p * (dpv - delta)
                db_ref[2 * hp + a] += ds
                dsb = ds.astype(bf16)
                dqs.append(jnp.dot(dsb, kp, preferred_element_type=f32))
                dk = dk + lax.dot_general(dsb, qa, (((0,), (0,)), ((), ())), preferred_element_type=f32)
                dv = dv + lax.dot_general(p.astype(bf16), doa, (((0,), (0,)), ((), ())), preferred_element_type=f32)
            dq_ref[:, cols] = (jnp.where(lo, dqs[0], dqs[1]) * 0.125).astype(bf16)
            for w in range(3):
                dkp_ref[w, :, cols] = dk[w * TQ:(w + 1) * TQ].astype(bf16)
                dvp_ref[w, :, cols] = dv[w * TQ:(w + 1) * TQ].astype(bf16)

    tok = BS((TQ, BW), lambda i: (i, 2))
    part = BS((3, TQ, BW), lambda i: (0, i, 0))
    full = BS((8, TQ, WIN), lambda i: (0, 0, 0))
    return pl.pallas_call(
        body, out_shape=(SDS((S, NP), bf16), SDS((3, S, BW), bf16), SDS((3, S, BW), bf16), SDS((8, TQ, WIN), f32)),
        grid=(S // TQ,),
        in_specs=_att_specs() + [BS((8, TQ, WIN), lambda i: (0, 0, 0), pipeline_mode=pl.Buffered(1)), tok, tok, ANY],
        out_specs=(BS((TQ, BW), lambda i: (i, cq)), part, part, full),
        scratch_shapes=[pltpu.VMEM((WIN, BW), bf16), pltpu.VMEM((WIN, BW), bf16)],
        input_output_aliases={10: 0}, compiler_params=_cp("arbitrary"), name=name)(
            proj, proj, proj, proj, proj, proj, proj, bias, y, dy, dproj)


def att_shift_add(dkp, dvp, dproj, name):
    S = dkp.shape[1]
    nT = S // TQ
    creg = OUR_COLS["a_k"][0] // (2 * BW)

    def body(k2, k1, k0, v2, v1, v0, dp_in, dp_ref):
        del dp_in
        j = pl.program_id(0)
        m1 = (j + 1 < nT).astype(f32)
        m0 = (j + 2 < nT).astype(f32)
        dp_ref[:, 0:BW] = (k2[0].astype(f32) + m1 * k1[0].astype(f32) + m0 * k0[0].astype(f32)).astype(bf16)
        dp_ref[:, BW:2 * BW] = (v2[0].astype(f32) + m1 * v1[0].astype(f32) + m0 * v0[0].astype(f32)).astype(bf16)

    def spec(w):
        return BS((1, TQ, BW), functools.partial(lambda j, w: (w, jnp.minimum(j + 2 - w, nT - 1), 0), w=w))

    return pl.pallas_call(
        body, out_shape=SDS(dproj.shape, bf16), grid=(nT,),
        in_specs=[spec(2), spec(1), spec(0), spec(2), spec(1), spec(0), ANY],
        out_specs=BS((TQ, 2 * BW), lambda j: (j, creg)),
        input_output_aliases={6: 0}, compiler_params=_cp("parallel"), name=name)(dkp, dkp, dkp, dvp, dvp, dvp, dproj)


GQ, GV = 256, 512
TGC = 8


def _bd_mask():
    r = lax.broadcasted_iota(jnp.int32, (GQ, GV), 0) // 64
    c = lax.broadcasted_iota(jnp.int32, (GQ, GV), 1) // 128
    return (r == c).astype(f32)


def _tri(strict):
    r = lax.broadcasted_iota(jnp.int32, (CHUNK, CHUNK), 0)
    c = lax.broadcasted_iota(jnp.int32, (CHUNK, CHUNK), 1)
    return ((c < r) if strict else (c <= r)).astype(f32)


def _compact(s_bd):
    return jnp.concatenate([s_bd[h * 64:(h + 1) * 64, h * 128:(h + 1) * 128] for h in range(4)], axis=0)


def _expand(comp, mask):
    return jnp.tile(comp, (1, 4)) * mask


def _gla_gates(alr, wa_ref, ba_ref, tri_incl, ones_col):
    a = jnp.dot(alr, wa_ref[...], preferred_element_type=f32) + ba_ref[...]
    la = (jnp.minimum(a, 0.0) - jnp.log(1.0 + jnp.exp(-jnp.abs(a)))) * (1.0 / 16.0)
    cum = jnp.dot(tri_incl, la, precision=HI, preferred_element_type=f32)
    tot_row = cum[CHUNK - 1:CHUNK, :]
    tot_col = lax.dot_general(la, ones_col, (((0,), (0,)), ((), ())), precision=HI, preferred_element_type=f32)
    return a, cum, tot_row, jnp.tile(jnp.exp(tot_col), (1, 4))


def _head_norm(o):
    rns, ons = [], []
    for h in range(4):
        oh = o[:, h * 128:(h + 1) * 128]
        rn = lax.rsqrt(jnp.mean(oh * oh, axis=-1, keepdims=True) + EPS)
        rns.append(rn)
        ons.append(oh * rn)
    return rns, ons


def _gla_in_specs(T, imap):
    cq, ck = OUR_COLS["g_q"][0] // GQ, OUR_COLS["g_k"][0] // GQ
    cv, cr = OUR_COLS["g_v"][0] // GV, OUR_COLS["g_r"][0] // GV
    return [BS((T, GQ), lambda i: (imap(i), cq)), BS((T, GQ), lambda i: (imap(i), ck)), BS((T, GV), lambda i: (imap(i), cv)),
            BS((T, GV), lambda i: (imap(i), cr)), BS((T, RANKP), lambda i: (imap(i), 0))]


def gla_fwd(proj, pa, wa, ba, ng, y, name):
    S = proj.shape[0]
    T = min(TGC * CHUNK, S)
    nch = T // CHUNK

    def body(q_ref, k_ref, v_ref, r_ref, a_ref, wa_ref, ba_ref, ng_ref, y_in, y_ref, st_ref, s_scr):
        del y_in

        @pl.when(pl.program_id(0) == 0)
        def _():
            s_scr[...] = jnp.zeros_like(s_scr)

        mask = _bd_mask()
        tri = _tri(False)
        ones_col = jnp.ones((CHUNK, 128), f32)

        s_bd = s_scr[...]
        for ci in range(nch):
            rows = pl.ds(ci * CHUNK, CHUNK)
            _, cum, tot_row, dec4 = _gla_gates(a_ref[rows, :], wa_ref, ba_ref, tri, ones_col)
            kd = (k_ref[rows, :].astype(f32) * jnp.exp(tot_row - cum)).astype(bf16)
            upd = lax.dot_general(kd, v_ref[rows, :], (((0,), (0,)), ((), ())), preferred_element_type=f32) * mask
            s_bd = dec4 * s_bd + upd
            st_ref[pl.ds(ci * GQ, GQ), :] = _compact(s_bd)
            qs = (q_ref[rows, :].astype(f32) * 0.125).astype(bf16)
            o = jnp.dot(qs, s_bd.astype(bf16), preferred_element_type=f32)
            _, ons = _head_norm(o)
            rv = r_ref[rows, :].astype(f32)
            y_ref[rows, :] = (jnp.concatenate(ons, axis=1) * ng_ref[...] * (rv * jax.nn.sigmoid(rv))).astype(bf16)
        s_scr[...] = s_bd

    return pl.pallas_call(
        body, out_shape=(SDS(y.shape, bf16), SDS((S // CHUNK * GQ, 128), f32)), grid=(S // T,),
        in_specs=_gla_in_specs(T, lambda i: i) + [BS((RANKP, GQ), lambda i: (0, 0)), BS((1, GQ), lambda i: (0, 0)),
                                                  BS((1, GV), lambda i: (0, 0)), ANY],
        out_specs=(BS((T, GV), lambda i: (i, 1)), BS((nch * GQ, 128), lambda i: (i, 0))),
        scratch_shapes=[pltpu.VMEM((GQ, GV), f32)], input_output_aliases={8: 0}, compiler_params=_cp("arbitrary"),
        name=name)(proj, proj, proj, proj, pa, wa, ba, ng, y)


def gla_bwd(proj, pa, states, dy, wa, ba, ng, dproj, name):
    S = proj.shape[0]
    T = min(TGC * CHUNK, S)
    nch = T // CHUNK
    nT = S // T
    rev = lambda i: nT - 1 - i

    def body(q_ref, k_ref, v_ref, r_ref, a_ref, st_ref, sp_ref, dy_ref, wa_ref, ba_ref, ng_ref, dp_in,
             dp_ref, da_ref, dwa_ref, dba_ref, dng_ref, g_scr):
        del dp_in
        i = pl.program_id(0)

        @pl.when(i == 0)
        def _():
            g_scr[...] = jnp.zeros_like(g_scr)
            dwa_ref[...] = jnp.zeros_like(dwa_ref)
            dba_ref[...] = jnp.zeros_like(dba_ref)
            dng_ref[...] = jnp.zeros_like(dng_ref)

        mask = _bd_mask()
        tri = _tri(False)
        tri_strict = _tri(True)
        ones_col = jnp.ones((CHUNK, 128), f32)
        ones_row = jnp.ones((8, GV), f32)
        first_tile = (i == nT - 1).astype(f32)

        g_carry = g_scr[...]
        for ci in reversed(range(nch)):
            rows = pl.ds(ci * CHUNK, CHUNK)
            alr = a_ref[rows, :]
            a, cum, tot_row, dec4 = _gla_gates(alr, wa_ref, ba_ref, tri, ones_col)
            wdec = jnp.exp(tot_row - cum)
            kdf = k_ref[rows, :].astype(f32) * wdec
            kd = kdf.astype(bf16)
            s_c = _expand(st_ref[pl.ds(ci * GQ, GQ), :], mask)
            prev = st_ref[pl.ds((ci - 1) * GQ, GQ), :] if ci > 0 else sp_ref[...] * (1.0 - first_tile)
            s_prev = _expand(prev, mask)
            qs = (q_ref[rows, :].astype(f32) * 0.125).astype(bf16)
            s_cb = s_c.astype(bf16)
            o = jnp.dot(qs, s_cb, preferred_element_type=f32)
            rns, ons = _head_norm(o)
            on = jnp.concatenate(ons, axis=1)
            rv = r_ref[rows, :].astype(f32)
            sg = jax.nn.sigmoid(rv)
            sr = rv * sg
            dyv = dy_ref[rows, :].astype(f32)
            ngv = ng_ref[...]
            dng_ref[...] += jnp.sum(dyv * on * sr, axis=0, keepdims=True)
            d_on = dyv * ngv * sr
            dr = dyv * on * ngv * (sg * (1.0 + rv * (1.0 - sg)))
            dos = []
            for h in range(4):
                cols = slice(h * 128, (h + 1) * 128)
                dh_ = d_on[:, cols]
                dos.append(rns[h] * (dh_ - ons[h] * jnp.mean(dh_ * ons[h], axis=-1, keepdims=True)))
            do = jnp.concatenate(dos, axis=1).astype(bf16)
            dq = lax.dot_general(do, s_cb, (((1,), (1,)), ((), ())), preferred_element_type=f32) * 0.125
            ds = lax.dot_general(qs, do, (((0,), (0,)), ((), ())), preferred_element_type=f32) * mask + g_carry
            ddec_row = lax.dot_general(ones_row, ds * s_prev, (((1,), (1,)), ((), ())), precision=HI,
                                       preferred_element_type=f32)[0:1, :]
            dsb = ds.astype(bf16)
            dkd = lax.dot_general(v_ref[rows, :], dsb, (((1,), (1,)), ((), ())), preferred_element_type=f32)
            dv = jnp.dot(kd, dsb, preferred_element_type=f32)
            g_carry = dec4 * ds
            dk = dkd * wdec
            dwlog = dkd * kdf
            dla = ddec_row * jnp.exp(tot_row) + jnp.dot(tri_strict, dwlog, precision=HI, preferred_element_type=f32)
            da = dla * (1.0 - jax.nn.sigmoid(a)) * (1.0 / 16.0)
            dab = da.astype(bf16)
            da_ref[rows, :] = lax.dot_general(dab, wa_ref[...], (((1,), (1,)), ((), ())),
                                              preferred_element_type=f32).astype(bf16)
            dwa_ref[...] += lax.dot_general(alr, dab, (((0,), (0,)), ((), ())), preferred_element_type=f32)
            dba_ref[...] += jnp.sum(da, axis=0, keepdims=True)
            dp_ref[rows, 0:GQ] = dq.astype(bf16)
            dp_ref[rows, GQ:2 * GQ] = dk.astype(bf16)
            dp_ref[rows, 2 * GQ:2 * GQ + GV] = dv.astype(bf16)
            dp_ref[rows, 2 * GQ + GV:2 * GQ + 2 * GV] = dr.astype(bf16)
        g_scr[...] = g_carry

    REG = 2 * GQ + 2 * GV
    return pl.pallas_call(
        body,
        out_shape=(SDS((S, NP), bf16), SDS((S, RANKP), bf16), SDS((RANKP, GQ), f32), SDS((1, GQ), f32), SDS((1, GV), f32)),
        grid=(nT,),
        in_specs=_gla_in_specs(T, rev) + [
            BS((nch * GQ, 128), lambda i: (rev(i), 0)),
            BS((GQ, 128), lambda i: (jnp.maximum(rev(i) * nch - 1, 0), 0)),
            BS((T, GV), lambda i: (rev(i), 1)),
            BS((RANKP, GQ), lambda i: (0, 0)), BS((1, GQ), lambda i: (0, 0)), BS((1, GV), lambda i: (0, 0)), ANY],
        out_specs=(BS((T, REG), lambda i: (rev(i), 0)), BS((T, RANKP), lambda i: (rev(i), 0)),
                   BS((RANKP, GQ), lambda i: (0, 0)), BS((1, GQ), lambda i: (0, 0)), BS((1, GV), lambda i: (0, 0))),
        scratch_shapes=[pltpu.VMEM((GQ, GV), f32)],
        input_output_aliases={11: 0}, compiler_params=_cp("arbitrary"), name=name)(
            proj, proj, proj, proj, pa, states, states, dy, wa, ba, ng, dproj)


def _as2d(a):
    if a.ndim == 1:
        return a.reshape(1, a.shape[0])
    return a.reshape(-1, a.shape[-1])


def adamw(w, g, m, v, name):
    shape = w.shape
    w2, g2, m2, v2 = (_as2d(a) for a in (w, g, m, v))
    R, C = w2.shape
    tr = R
    for cand in (512, 256, 128, 64, 32, 16, 8):
        if R % cand == 0 and cand * C * 4 * 7 * 2 <= 40 * 1024 * 1024:
            tr = cand
            break

    def body(w_ref, g_ref, m_ref, v_ref, d_ref, mo_ref, vo_ref):
        gv = g_ref[...]
        mn = ADAM_B1 * m_ref[...] + (1.0 - ADAM_B1) * gv
        vn = ADAM_B2 * v_ref[...] + (1.0 - ADAM_B2) * (gv * gv)
        m_hat = mn / (1.0 - ADAM_B1 ** ADAM_STEP)
        v_hat = vn / (1.0 - ADAM_B2 ** ADAM_STEP)
        d_ref[...] = -ADAM_LR * (m_hat / (jnp.sqrt(v_hat) + ADAM_EPS) + ADAM_WD * w_ref[...])
        mo_ref[...] = mn
        vo_ref[...] = vn

    blk = BS((tr, C), lambda i: (i, 0))
    outs = pl.pallas_call(body, out_shape=tuple(SDS((R, C), f32) for _ in range(3)), grid=(R // tr,),
                          in_specs=[blk] * 4, out_specs=(blk,) * 3, compiler_params=_cp("parallel"), name=name)(w2, g2, m2, v2)
    return tuple(o.reshape(shape) for o in outs)


def _row_tile(rows, row_bytes, budget=4 * 1024 * 1024):
    best = None
    for t in range(16, rows + 1, 16):
        if rows % t == 0 and t * row_bytes <= budget:
            best = t
    return best or rows


def add_halves(gp, ra, c, name):
    shape = ra.shape
    cols = shape[-1]
    rows = int(np.prod(shape[:-1]))
    g3, r2 = gp.reshape(2, rows, cols), ra.reshape(rows, cols)
    tr = _row_tile(rows, cols * 2)

    grid_spec = pltpu.PrefetchScalarGridSpec(
        num_scalar_prefetch=1, grid=(rows // tr,),
        in_specs=[BS((1, tr, cols), lambda i, c_ref: (c_ref[0], i, 0)), BS((tr, cols), lambda i, c_ref: (i, 0))],
        out_specs=BS((tr, cols), lambda i, c_ref: (i, 0)))

    def body(c_ref, a_ref, b_ref, o_ref):
        del c_ref
        o_ref[...] = (a_ref[0].astype(f32) + b_ref[...].astype(f32)).astype(bf16)

    out = pl.pallas_call(body, out_shape=SDS((rows, cols), bf16), grid_spec=grid_spec, compiler_params=_cp("parallel"),
                         name=name)(jnp.reshape(c, (1,)).astype(jnp.int32), g3, r2)
    return out.reshape(shape)


def reduce_chips(rb, own, c, chip, name):
    shape = rb.shape[1:]
    cols = shape[-1]
    rows = int(np.prod(shape[:-1]))
    tr = _row_tile(rows, cols * 2 * 4)
    rb3, own3 = rb.reshape(4, rows, cols), own.reshape(4, rows, cols)

    def body(s_ref, own_ref, r1, r2, r3, o_ref):
        del s_ref
        o_ref[0] = ((own_ref[0].astype(f32) + r1[0].astype(f32)) + r2[0].astype(f32)) + r3[0].astype(f32)

    def slot(k):
        return BS((1, tr, cols), functools.partial(lambda i, s, k: ((s[1] + k) % 4, i, 0), k=k))

    grid_spec = pltpu.PrefetchScalarGridSpec(
        num_scalar_prefetch=1, grid=(rows // tr,), in_specs=[slot(0), slot(1), slot(2), slot(3)],
        out_specs=BS((1, tr, cols), lambda i, s: (s[0], i, 0)))
    out = pl.pallas_call(body, out_shape=SDS((2, rows, cols), f32), grid_spec=grid_spec, compiler_params=_cp("parallel"),
                         name=name)(jnp.stack([c, chip]).astype(jnp.int32), own3, rb3, rb3, rb3)
    return out.reshape((2,) + shape)


def sum_slots(x, name):
    N, shape = x.shape[0], x.shape[1:]
    cols = shape[-1]
    rows = int(np.prod(shape[:-1]))
    tr = _row_tile(rows, cols * x.dtype.itemsize * N)

    def body(x_ref, o_ref):
        acc = x_ref[0].astype(f32)
        for n in range(1, N):
            acc = acc + x_ref[n].astype(f32)
        o_ref[...] = acc

    out = pl.pallas_call(body, out_shape=SDS((rows, cols), f32), grid=(rows // tr,),
                         in_specs=[BS((N, tr, cols), lambda i: (0, i, 0))], out_specs=BS((tr, cols), lambda i: (i, 0)),
                         compiler_params=_cp("parallel"), name=name)(x.reshape(N, rows, cols))
    return out.reshape(shape)


def _me():
    return lax.axis_index("x"), lax.axis_index("y"), lax.axis_index("c")


def _rcopy(src, dst, send_sems, recv_sems, k, dev):
    return pltpu.make_async_remote_copy(src_ref=src, dst_ref=dst, send_sem=send_sems.at[k], recv_sem=recv_sems.at[k],
                                        device_id=dev, device_id_type=MESH)


def _comm_call(body, ins, out_shapes, n_remote, name, aliases=None):
    return pl.pallas_call(
        body, out_shape=tuple(out_shapes), in_specs=[ANY] * len(ins), out_specs=tuple(ANY for _ in out_shapes),
        scratch_shapes=[pltpu.SemaphoreType.DMA((n_remote,)), pltpu.SemaphoreType.DMA((n_remote,))],
        input_output_aliases=aliases or {}, name=name)(*ins)


def ag4(bufs, name):
    n = len(bufs)

    def body(*refs):
        xs, os = refs[:n], refs[n:2 * n]
        send_sems, recv_sems = refs[2 * n:]
        x, y, c = _me()
        j = 2 * x + y
        sib = (x, y, 1 - c)
        chips = [(1 - x, y), (x, 1 - y), (1 - x, 1 - y)]
        first = [_rcopy(xs[t].at[c], os[t].at[c, j], send_sems, recv_sems, 8 * t + k, (cx, cy, c))
                 for t in range(n) for k, (cx, cy) in enumerate(chips)]
        own = [_rcopy(xs[t].at[l], os[t].at[l, j], send_sems, recv_sems, 8 * t + 6 + l, sib) for t in range(n) for l in range(2)]
        for cp in first + own:
            cp.start()
        passed = []
        for k, (cx, cy) in enumerate(chips):
            for t in range(n):
                land = os[t].at[c, 2 * cx + cy]
                _rcopy(land, land, send_sems, recv_sems, 8 * t + k, (x, y, c)).wait_recv()
                fwd = _rcopy(land, land, send_sems, recv_sems, 8 * t + 3 + k, sib)
                fwd.start()
                passed.append(fwd)
        for t in range(n):
            for l in range(2):
                land = os[t].at[l, j]
                _rcopy(land, land, send_sems, recv_sems, 8 * t + 6 + l, (x, y, c)).wait_recv()
        for k, (cx, cy) in enumerate(chips):
            for t in range(n):
                land = os[t].at[1 - c, 2 * cx + cy]
                _rcopy(land, land, send_sems, recv_sems, 8 * t + 3 + k, (x, y, c)).wait_recv()
        for cp in first + own + passed:
            cp.wait_send()

    outs = [SDS((2, 4) + b.shape[1:], b.dtype) for b in bufs]
    return _comm_call(body, bufs, outs, 8 * n, name)


def sib_other_layer(gps, name):
    n = len(gps)

    def body(*refs):
        xs, os = refs[:n], refs[n:2 * n]
        send_sems, recv_sems = refs[2 * n:]
        x, y, c = _me()
        cps = [_rcopy(xs[t].at[1 - c], os[t], send_sems, recv_sems, t, (x, y, 1 - c)) for t in range(n)]
        for cp in cps:
            cp.start()
        for cp in cps:
            cp.wait()

    return _comm_call(body, gps, [SDS(g.shape[1:], g.dtype) for g in gps], n, name)


def a2a4(ps, name):
    n = len(ps)

    def body(*refs):
        xs, os = refs[:n], refs[n:2 * n]
        send_sems, recv_sems = refs[2 * n:]
        x, y, c = _me()
        j = 2 * x + y
        chips = [(1 - x, y), (x, 1 - y), (1 - x, 1 - y)]
        sends = [_rcopy(xs[t].at[2 * cx + cy], os[t].at[j], send_sems, recv_sems, 3 * t + k, (cx, cy, c))
                 for t in range(n) for k, (cx, cy) in enumerate(chips)]
        for cp in sends:
            cp.start()
        for t in range(n):
            for k, (cx, cy) in enumerate(chips):
                land = os[t].at[2 * cx + cy]
                _rcopy(land, land, send_sems, recv_sems, 3 * t + k, (x, y, c)).wait_recv()
        for cp in sends:
            cp.wait_send()

    return _comm_call(body, ps, [SDS(p.shape, p.dtype) for p in ps], 3 * n, name)


def ag2(bufs, name):
    n = len(bufs)

    def body(*refs):
        xs, os = refs[:n], refs[n:2 * n]
        send_sems, recv_sems = refs[2 * n:]
        x, y, c = _me()
        cps = [_rcopy(xs[t].at[c], os[t].at[c], send_sems, recv_sems, t, (x, y, 1 - c)) for t in range(n)]
        for cp in cps:
            cp.start()
        for t in range(n):
            land = os[t].at[1 - c]
            _rcopy(land, land, send_sems, recv_sems, t, (x, y, c)).wait_recv()
        for cp in cps:
            cp.wait_send()

    return _comm_call(body, bufs, [SDS(b.shape, b.dtype) for b in bufs], n, name, aliases={t: t for t in range(n)})


def ag8(blk, name):
    m_per, n = blk.shape

    def body(x_ref, out_ref, send_sems, recv_sems, local_sem):
        x, y, c = _me()
        me, sibling = (x, y, c), (x, y, 1 - c)
        chips = [(1 - x, y), (x, 1 - y), (1 - x, 1 - y)]

        def rows(px, py, pc):
            return out_ref.at[pl.ds((4 * px + 2 * py + pc) * m_per, m_per), :]

        def copy(k, block, to, src=None):
            return pltpu.make_async_remote_copy(
                src_ref=rows(*block) if src is None else src, dst_ref=rows(*block), send_sem=send_sems.at[k],
                recv_sem=recv_sems.at[k], device_id=to, device_id_type=MESH)

        mine = pltpu.make_async_copy(x_ref, rows(*me), local_sem)
        mine.start()
        first = [copy(0, me, sibling, src=x_ref)]
        first += [copy(1 + j, me, (*chip, c), src=x_ref) for j, chip in enumerate(chips)]
        for cp in first:
            cp.start()
        passed = [copy(4 + j, (*chip, c), sibling) for j, chip in enumerate(chips)]
        for j, chip in enumerate(chips):
            copy(1 + j, (*chip, c), me).wait_recv()
            passed[j].start()
        copy(0, sibling, me).wait_recv()
        for j, chip in enumerate(chips):
            copy(4 + j, (*chip, 1 - c), me).wait_recv()
        for cp in first + passed:
            cp.wait_send()
        mine.wait()

    return pl.pallas_call(
        body, out_shape=SDS((8 * m_per, n), blk.dtype), in_specs=[pl.BlockSpec(memory_space=pltpu.VMEM)],
        out_specs=pl.BlockSpec(memory_space=pltpu.VMEM),
        scratch_shapes=[pltpu.SemaphoreType.DMA((7,)), pltpu.SemaphoreType.DMA((7,)), pltpu.SemaphoreType.DMA],
        name=name)(blk)


def _split_chips(full, axis):
    n = full.shape[axis] // 4
    parts = full.reshape(full.shape[:axis] + (4, n) + full.shape[axis + 1:])
    return jnp.moveaxis(parts, axis, 0)


def _merge_chips(gathered, axis):
    parts = jnp.moveaxis(gathered, 0, axis)
    return parts.reshape(parts.shape[:axis] + (parts.shape[axis] * parts.shape[axis + 1],) + parts.shape[axis + 2:])


def _to_ref_cols(main, rank):
    pieces = []
    for n, width in REF_SPLITS:
        if n == "g_a":
            pieces.append(rank[..., :RANK])
        else:
            off = OUR_COLS[n][0]
            pieces.append(main[..., off:off + width])
    return jnp.concatenate(pieces, axis=-1)


def _from_ref_cols(w):
    offs, o = {}, 0
    for n, width in REF_SPLITS:
        offs[n] = (o, width)
        o += width
    main = jnp.concatenate([w[..., offs[n][0]:offs[n][0] + offs[n][1]] for n in sorted(OUR_COLS, key=lambda k: OUR_COLS[k][0])],
                           axis=-1)
    ro = offs["g_a"][0]
    rank = jnp.pad(w[..., ro:ro + RANK], [(0, 0)] * (w.ndim - 1) + [(0, RANKP - RANK)])
    return main, rank


def _layer_fwd(h, p_i, W, li):
    t = f"l{li}_"
    sv = {"h0": h}
    xn = rms_fwd(h, W["norm1_g"], t + "rms1")
    proj = mm_nn(xn, W["w_in_main"], name=t + "inproj")
    pa = mm_nn(xn, W["w_in_rank"], name=t + "inproj_rank")
    y = sg_fwd(proj, W["sg_ln_g"], W["sg_ln_b"], W["sg_wm"], W["sg_bsb"], t + "sg_fwd")
    y, states = gla_fwd(proj, pa, W["gla_wa"], W["gla_b_a"], W["gla_norm_g"], y, t + "gla_fwd")
    y = att_fwd(proj, W["att_bias"], y, t + "att_fwd")
    y = conv_fwd(proj, W["conv_dw_w"], W["conv_dw_b"], W["conv_ln_g"], W["conv_ln_b"], y, t + "conv_fwd")
    gate = mm_nn(xn, W["w_gate_all"], bias=W["b_gate_all"], act="sigmoid", name=t + "gate")
    z = mm_nn(y, W["w_branch"], name=t + "branch")
    m = gate_merge_fwd(gate, z, t + "merge")
    h1 = mm_nn(m, W["w_out"], res=h, out_dtype=f32, name=t + "outproj")
    hn = rms_fwd(h1, W["norm2_g"], t + "rms2")
    a = mm_nn(hn, W["w_ff1"], name=t + "ff1")
    h2 = mm_nn(a, W["w_ff2"], pre="relu2", res=h1, out_dtype=f32, name=t + "ff2")
    hg = rms_fwd(h2, W["norm3_g"], t + "rms3")
    pg = mm_nn(hg, W["w_ple_gate"], bias=W["b_ple_gate"], act="sigmoid", name=t + "ple_gate")
    h3, e = mm_nn(p_i, W["w_ple"], mul=pg, res=h2, out_dtype=f32, raw_out=True, name=t + "ple_out")
    sv.update(xn=xn, proj=proj, pa=pa, states=states, y=y, gate=gate, z=z, m=m, h1=h1, hn=hn, a=a, h2=h2, hg=hg, pg=pg, e=e)
    return h3, sv


def _layer_bwd(dh3, sv, p_i, W, li):
    t = f"l{li}_b_"
    G = {}
    dpg, de, G["b_ple_gate"] = ple_bwd_ew(dh3, sv["e"], sv["pg"], t + "ple_ew")
    G["w_ple_gate"] = mm_tn(sv["hg"], dpg, name=t + "dw_ple_gate")[0]
    G["w_ple"] = mm_tn(p_i, de, name=t + "dw_ple")[0]
    dhg = mm_nt(dpg, W["w_ple_gate"], name=t + "dhg")
    dh2, G["norm3_g"] = rms_bwd(dhg, sv["h2"], W["norm3_g"], dh3, t + "rms3")
    da = mm_nt(dh2, W["w_ff2"], post_a=sv["a"], out_dtype=bf16, name=t + "da")
    G["w_ff2"] = mm_tn(sv["a"], dh2, pre="relu2", name=t + "dw_ff2")[0]
    G["w_ff1"] = mm_tn(sv["hn"], da, name=t + "dw_ff1")[0]
    dhn = mm_nt(da, W["w_ff1"], name=t + "dhn")
    dh1, G["norm2_g"] = rms_bwd(dhn, sv["h1"], W["norm2_g"], dh2, t + "rms2")
    dm = mm_nt(dh1, W["w_out"], out_dtype=bf16, name=t + "dm")
    G["w_out"] = mm_tn(sv["m"], dh1, name=t + "dw_out")[0]
    dz, dgp, G["b_gate_all"] = gate_merge_bwd(dm, sv["gate"], sv["z"], t + "merge")
    G["w_branch"] = mm_tn(sv["y"], dz, G=4, name=t + "dw_branch")
    dy = mm_nt(dz, W["w_branch"], out_dtype=bf16, name=t + "dy")
    G["w_gate_all"] = mm_tn(sv["xn"], dgp, name=t + "dw_gate")[0]
    dxn = mm_nt(dgp, W["w_gate_all"], name=t + "dxn_gate")
    proj = sv["proj"]
    dproj, dwm, dbs, G["sg_ln_g"], G["sg_ln_b"] = sg_bwd(proj, dy, W["sg_ln_g"], W["sg_ln_b"], W["sg_wm"], W["sg_bsb"],
                                                          W["sg_maskf"], t + "sg")
    G["sg_w"], G["sg_b"] = dwm, dbs[:, :, 0]
    dproj, dpa, dwa, G["gla_b_a"], G["gla_norm_g"] = gla_bwd(proj, sv["pa"], sv["states"], dy, W["gla_wa"], W["gla_b_a"],
                                                             W["gla_norm_g"], dproj, t + "gla")
    G["gla_w_a2"] = dwa[:RANK]
    dproj, dkp, dvp, dbias = att_bwd(proj, sv["y"], dy, W["att_bias"], dproj, t + "att")
    dproj = att_shift_add(dkp, dvp, dproj, t + "att_kv")
    G["att_rel_bias"] = att_bias_grad(dbias, t + "att_bias")
    dz_c, G["conv_ln_g"], G["conv_ln_b"], G["conv_dw_b"] = conv_bwd_norm(proj, dy, W["conv_dw_w"], W["conv_dw_b"],
                                                                        W["conv_ln_g"], W["conv_ln_b"], t + "conv_norm")
    dproj, G["conv_dw_w"] = conv_bwd_taps(proj, dz_c, W["conv_dw_w"], dproj, t + "conv_taps")
    G["w_in_main"] = mm_tn(sv["xn"], dproj, name=t + "dw_in")[0]
    G["w_in_rank"] = mm_tn(sv["xn"], dpa, name=t + "dw_in_rank")[0]
    dxn = mm_nt(dpa, W["w_in_rank"], res=dxn, name=t + "dxn_rank")
    dxn = mm_nt(dproj, W["w_in_main"], res=dxn, name=t + "dxn_main")
    dh0, G["norm1_g"] = rms_bwd(dxn, sv["h0"], W["norm1_g"], dh1, t + "rms1")
    return dh0, G


def _prep_layer_weights(gathered, repl, li):
    W = {}
    row = lambda a: a.reshape(1, -1)
    full = {n: _merge_chips(gathered[n][li], SHARDED[n][1]) for n in BIG}
    small = {n: _merge_chips(gathered[n][li], SHARDED[n][1]) for n in SMALL}
    main, rank = _from_ref_cols(full["w_in"])
    W["w_in_main"], W["w_in_rank"] = main[None], rank[None]
    W["w_branch"] = full["w_branch"]
    W["w_gate_all"] = jnp.transpose(full["w_gate"], (1, 0, 2)).reshape(1, D, 4 * D)
    W["b_gate_all"] = small["b_gate"].reshape(1, 4 * D)
    for n in ("w_out", "w_ff1", "w_ff2", "w_ple_gate", "w_ple"):
        W[n] = full[n][None]
    for n in ("norm1_g", "norm2_g", "norm3_g", "b_ple_gate", "sg_ln_g", "sg_ln_b", "gla_b_a", "gla_norm_g", "conv_dw_b",
              "conv_ln_g", "conv_ln_b"):
        W[n] = row(repl[n][li])
    pos = np.arange(128)
    mask = (pos[None, :] // CHUNK) <= (pos[:, None] // CHUNK)
    W["sg_maskf"] = jnp.asarray(mask, f32)
    W["sg_wm"] = jnp.where(mask[None], repl["sg_w"][li], 0.0).astype(bf16)
    W["sg_bsb"] = jnp.broadcast_to(repl["sg_b"][li][:, :, None], (4, 128, 128))
    W["gla_wa"] = jnp.pad(small["gla_w_a2"], ((0, RANKP - RANK), (0, 0))).astype(bf16)
    W["att_bias"] = att_bias_build(small["att_rel_bias"], f"l{li}_att_bias")
    W["conv_dw_w"] = small["conv_dw_w"]
    return W


def _layer_grads_to_ref(G):
    out = {}
    out["w_in"] = _to_ref_cols(G["w_in_main"], G["w_in_rank"])
    out["w_gate"] = jnp.transpose(G["w_gate_all"].reshape(D, 4, D), (1, 0, 2))
    out["b_gate"] = G["b_gate_all"].reshape(4, D)
    for n in ("w_branch", "w_out", "w_ff1", "w_ff2", "w_ple_gate", "w_ple", "gla_w_a2", "att_rel_bias", "conv_dw_w", "sg_w",
              "sg_b"):
        out[n] = G[n]
    for n in ("norm1_g", "norm2_g", "norm3_g", "b_ple_gate", "sg_ln_g", "sg_ln_b", "gla_b_a", "gla_norm_g", "conv_dw_b",
              "conv_ln_g", "conv_ln_b"):
        out[n] = G[n].reshape(-1)
    return out


def kernel(x, p, norm1_g, w_in, sg_ln_g, sg_ln_b, sg_w, sg_b, gla_w_a2, gla_b_a, gla_norm_g, att_rel_bias, conv_dw_w, conv_dw_b, conv_ln_g, conv_ln_b, w_branch, w_gate, b_gate, w_out, norm2_g, w_ff1, w_ff2, norm3_g, w_ple_gate, b_ple_gate, w_ple, final_g, loss_target, m_norm1_g, m_w_in, m_sg_ln_g, m_sg_ln_b, m_sg_w, m_sg_b, m_gla_w_a2, m_gla_b_a, m_gla_norm_g, m_att_rel_bias, m_conv_dw_w, m_conv_dw_b, m_conv_ln_g, m_conv_ln_b, m_w_branch, m_w_gate, m_b_gate, m_w_out, m_norm2_g, m_w_ff1, m_w_ff2, m_norm3_g, m_w_ple_gate, m_b_ple_gate, m_w_ple, m_final_g, v_norm1_g, v_w_in, v_sg_ln_g, v_sg_ln_b, v_sg_w, v_sg_b, v_gla_w_a2, v_gla_b_a, v_gla_norm_g, v_att_rel_bias, v_conv_dw_w, v_conv_dw_b, v_conv_ln_g, v_conv_ln_b, v_w_branch, v_w_gate, v_b_gate, v_w_out, v_norm2_g, v_w_ff1, v_w_ff2, v_norm3_g, v_w_ple_gate, v_b_ple_gate, v_w_ple, v_final_g):
    args = dict(locals())
    weights = {n: args[n] for n in W_ORDER}
    moments_m = {n: args["m_" + n] for n in W_ORDER}
    moments_v = {n: args["v_" + n] for n in W_ORDER}
    c = lax.axis_index("c")
    sharded_names = BIG + SMALL

    shards = [weights[n].astype(bf16) for n in BIG] + [weights[n] for n in SMALL]
    gathered = dict(zip(sharded_names, ag4(shards, "ag_weights")))
    repl = {n: weights[n] for n in REPL}

    h = x[0]
    Ws, saved = [], []
    for li in range(DEPTH):
        W = _prep_layer_weights(gathered, repl, li)
        h, sv = _layer_fwd(h, p[li, 0], W, li)
        Ws.append(W)
        saved.append(sv)
    loss_part, dh, dfinal = loss_head(h, final_g.reshape(1, D), loss_target[0], "loss_head")
    loss = lax.psum(loss_part[0, 0], ("x", "y", "c"))

    layer_grads = [None] * DEPTH
    for li in reversed(range(DEPTH)):
        dh, G = _layer_bwd(dh, saved[li], p[li, 0], Ws[li], li)
        layer_grads[li] = _layer_grads_to_ref(G)
    grad_x = dh[None]

    gps = [jnp.stack([_split_chips(layer_grads[li][n], SHARDED[n][1]) for li in range(DEPTH)]).astype(bf16) for n in BIG]
    ras = sib_other_layer(gps, "rs_sibling_layer")
    psums = [add_halves(g, r, c, "rs_add_" + n) for n, g, r in zip(BIG, gps, ras)]
    rbs = a2a4(psums, "rs_all_to_all")
    chip = 2 * lax.axis_index("x") + lax.axis_index("y")
    reds = [reduce_chips(r, ps, c, chip, "rs_sum_" + n) for n, r, ps in zip(BIG, rbs, psums)]
    grads = dict(zip(BIG, ag2(reds, "rs_sibling_gather")))

    local = {n: jnp.stack([layer_grads[li][n] for li in range(DEPTH)]) for n in tuple(REPL)[:-1] + SMALL}
    local["final_g"] = dfinal.reshape(D)
    rnames = tuple(REPL) + SMALL
    rflat = jnp.concatenate([local[n].reshape(-1) for n in rnames])
    rflat = jnp.pad(rflat, (0, REPL_ROWS * PACK_W - rflat.shape[0])).reshape(REPL_ROWS, PACK_W)
    rall = ag8(rflat, "ar_gather").reshape(8, REPL_ROWS, PACK_W)
    rsum = sum_slots(rall, "ar_sum").reshape(-1)
    off = 0
    for n in rnames:
        shape = local[n].shape
        size = int(np.prod(shape))
        g = rsum[off:off + size].reshape(shape)
        off += size
        if n in SMALL:
            ax = SHARDED[n][1] + 1
            g = lax.dynamic_slice_in_dim(g, chip * (shape[ax] // 4), shape[ax] // 4, axis=ax)
        grads[n] = g

    deltas, new_m, new_v = {}, {}, {}
    for n in W_ORDER:
        deltas[n], new_m[n], new_v[n] = adamw(weights[n], grads[n], moments_m[n], moments_v[n], "adamw_" + n)
    return (loss, grad_x, *[grads[n] for n in W_ORDER], *[deltas[n] for n in W_ORDER], *[new_m[n] for n in W_ORDER],
            *[new_v[n] for n in W_ORDER])
```

```python
import functools

import jax
import jax.numpy as jnp
import numpy as np
from jax import lax
from jax.experimental import pallas as pl
from jax.experimental.pallas import tpu as pltpu

f32, bf16 = jnp.float32, jnp.bfloat16
HI = lax.Precision.HIGHEST
MESH = pl.DeviceIdType.MESH
SDS = jax.ShapeDtypeStruct
BS = pl.BlockSpec
ANY = pl.BlockSpec(memory_space=pl.ANY)

D = 1024
DEPTH = 2
CHUNK = 64
BW = 512
NP = 5120
RANK = 16
RANKP = 128
DFF = 4096
PLE = 256
CONV_K = 31
HALO = 32
TQ = 256
WIN = 768
REL_TABLE = 320
EPS = 1e-6
NEG_INF = -1e30
VMEM_LIMIT = 56 * 1024 * 1024

ADAM_LR, ADAM_B1, ADAM_B2, ADAM_EPS, ADAM_WD, ADAM_STEP = 0.001, 0.9, 0.999, 1e-08, 0.01, 10

OUR_COLS = dict(g_q=(0, 256), g_k=(256, 256), g_v=(512, 512), g_r=(1024, 512), a_q=(1536, 512), a_k=(2048, 512),
                a_v=(2560, 512), sg_u=(3072, 512), sg_v=(3584, 512), c_a=(4096, 512), c_g=(4608, 512))
REF_SPLITS = (("sg_u", 512), ("sg_v", 512), ("g_q", 256), ("g_k", 256), ("g_v", 512), ("g_r", 512), ("g_a", 16),
              ("a_q", 512), ("a_k", 512), ("a_v", 512), ("c_a", 512), ("c_g", 512))

SHARDED = dict(w_in=((1024, 5136), 1), w_branch=((4, 512, 1024), 2), w_gate=((4, 1024, 1024), 1), w_out=((1024, 1024), 0),
               w_ff1=((1024, 4096), 1), w_ff2=((4096, 1024), 0), w_ple_gate=((1024, 1024), 0), w_ple=((256, 1024), 1),
               gla_w_a2=((16, 256), 1), att_rel_bias=((8, 320), 1), conv_dw_w=((31, 512), 1), b_gate=((4, 1024), 1))
BIG = ("w_in", "w_branch", "w_gate", "w_out", "w_ff1", "w_ff2", "w_ple_gate", "w_ple")
SMALL = ("gla_w_a2", "att_rel_bias", "conv_dw_w", "b_gate")
REPL = dict(norm1_g=(2, 1024), sg_ln_g=(2, 512), sg_ln_b=(2, 512), sg_w=(2, 4, 128, 128), sg_b=(2, 4, 128), gla_b_a=(2, 256),
            gla_norm_g=(2, 512), conv_dw_b=(2, 512), conv_ln_g=(2, 512), conv_ln_b=(2, 512), norm2_g=(2, 1024),
            norm3_g=(2, 1024), b_ple_gate=(2, 1024), final_g=(1024,))
W_ORDER = ['norm1_g', 'w_in', 'sg_ln_g', 'sg_ln_b', 'sg_w', 'sg_b', 'gla_w_a2', 'gla_b_a', 'gla_norm_g', 'att_rel_bias',
           'conv_dw_w', 'conv_dw_b', 'conv_ln_g', 'conv_ln_b', 'w_branch', 'w_gate', 'b_gate', 'w_out', 'norm2_g', 'w_ff1',
           'w_ff2', 'norm3_g', 'w_ple_gate', 'b_ple_gate', 'w_ple', 'final_g']
PACK_W = 1024
REPL_ROWS = 200
TM = 1024
MM_VMEM_BUDGET = 44 * 1024 * 1024


def _tile(s):
    return 512 if s % 512 == 0 else s


def _token_tile(S, row_bytes, fixed_bytes):
    for t in (2048, 1024):
        if S % t == 0 and 2 * (t * row_bytes + fixed_bytes) <= MM_VMEM_BUDGET:
            return t
    return min(TM, S)


def _cp(*sem):
    return pltpu.CompilerParams(dimension_semantics=sem, vmem_limit_bytes=VMEM_LIMIT)


def rms_fwd(h, g, name):
    S, Dm = h.shape
    T = _tile(S)

    def body(h_ref, g_ref, o_ref):
        x = h_ref[...]
        r = lax.rsqrt(jnp.mean(x * x, axis=-1, keepdims=True) + EPS)
        o_ref[...] = (x * r * g_ref[...]).astype(bf16)

    return pl.pallas_call(
        body, out_shape=SDS((S, Dm), bf16), grid=(S // T,),
        in_specs=[BS((T, Dm), lambda i: (i, 0)), BS((1, Dm), lambda i: (0, 0))],
        out_specs=BS((T, Dm), lambda i: (i, 0)), compiler_params=_cp("parallel"), name=name)(h, g)


def mm_nn(x, w, *, name, bias=None, act=None, pre=None, mul=None, res=None, out_dtype=bf16, raw_out=False):
    S = x.shape[0]
    G, K, N = w.shape
    tn = min(1024 if K <= 1024 else 512, N)
    nj = N // tn
    row_bytes = K * x.dtype.itemsize + tn * (jnp.dtype(out_dtype).itemsize + (2 if raw_out else 0)
                                             + sum(a.dtype.itemsize for a in (mul, res) if a is not None))
    T = _token_tile(S, row_bytes, K * tn * 2)
    extras = [a for a in (bias, mul, res) if a is not None]

    def body(*refs):
        it = iter(refs)
        x_ref, w_ref = next(it), next(it)
        b_ref = next(it) if bias is not None else None
        m_ref = next(it) if mul is not None else None
        r_ref = next(it) if res is not None else None
        o_ref = next(it)
        xv = x_ref[...]
        if pre == "relu2":
            xf = jnp.maximum(xv.astype(f32), 0.0)
            xv = xf * xf
        acc = jnp.dot(xv.astype(bf16), w_ref[0], preferred_element_type=f32)
        if raw_out:
            next(it)[...] = acc.astype(bf16)
        if b_ref is not None:
            acc = acc + b_ref[...]
        if act == "sigmoid":
            acc = jax.nn.sigmoid(acc)
        if m_ref is not None:
            acc = acc * m_ref[...].astype(f32)
        if r_ref is not None:
            acc = r_ref[...].astype(f32) + acc
        o_ref[...] = acc.astype(out_dtype)

    in_specs = [BS((T, K), lambda i, g, j: (i, g)), BS((1, K, tn), lambda i, g, j: (g, 0, j))]
    if bias is not None:
        in_specs.append(BS((1, tn), lambda i, g, j: (0, g * nj + j)))
    for a in (mul, res):
        if a is not None:
            in_specs.append(BS((T, tn), lambda i, g, j: (i, g * nj + j)))
    ospec = BS((T, tn), lambda i, g, j: (i, g * nj + j))
    out_shape = SDS((S, G * N), out_dtype)
    if raw_out:
        out_shape, ospec = (out_shape, SDS((S, G * N), bf16)), (ospec, ospec)
    return pl.pallas_call(
        body, out_shape=out_shape, grid=(S // T, G, nj), in_specs=in_specs, out_specs=ospec,
        compiler_params=_cp("parallel", "parallel", "parallel"), name=name)(x, w, *extras)


def mm_nt(dy, w, *, name, res=None, post_a=None, out_dtype=f32):
    S = dy.shape[0]
    G, K, N = w.shape
    tk = min(1024 if N <= 1024 else 512, K)
    nk = K // tk
    row_bytes = N * dy.dtype.itemsize + tk * (jnp.dtype(out_dtype).itemsize
                                              + sum(a.dtype.itemsize for a in (res, post_a) if a is not None))
    T = _token_tile(S, row_bytes, tk * N * 2)
    extras = [a for a in (res, post_a) if a is not None]

    def body(*refs):
        it = iter(refs)
        d_ref, w_ref = next(it), next(it)
        r_ref = next(it) if res is not None else None
        a_ref = next(it) if post_a is not None else None
        o_ref = next(it)
        acc = lax.dot_general(d_ref[...].astype(bf16), w_ref[0], (((1,), (1,)), ((), ())), preferred_element_type=f32)
        if r_ref is not None:
            acc = acc + r_ref[...].astype(f32)
        if a_ref is not None:
            acc = acc * (2.0 * jnp.maximum(a_ref[...].astype(f32), 0.0))
        o_ref[...] = acc.astype(out_dtype)

    in_specs = [BS((T, N), lambda i, g, j: (i, g)), BS((1, tk, N), lambda i, g, j: (g, j, 0))]
    for a in extras:
        in_specs.append(BS((T, tk), lambda i, g, j: (i, g * nk + j)))
    return pl.pallas_call(
        body, out_shape=SDS((S, G * K), out_dtype), grid=(S // T, G, nk), in_specs=in_specs,
        out_specs=BS((T, tk), lambda i, g, j: (i, g * nk + j)),
        compiler_params=_cp("parallel", "parallel", "parallel"), name=name)(dy, w, *extras)


def mm_tn(x, dy, *, name, G=1, pre=None, ts=1024):
    S = x.shape[0]
    K, N = x.shape[1] // G, dy.shape[1] // G
    tk, tn = min(K, 1024), min(N, 1024)
    nk, nn = K // tk, N // tn
    ts = min(ts, S)

    def body(x_ref, d_ref, o_ref):
        @pl.when(pl.program_id(3) == 0)
        def _():
            o_ref[...] = jnp.zeros_like(o_ref)

        xv = x_ref[...]
        if pre == "relu2":
            xf = jnp.maximum(xv.astype(f32), 0.0)
            xv = xf * xf
        o_ref[0] += lax.dot_general(xv.astype(bf16), d_ref[...].astype(bf16), (((0,), (0,)), ((), ())),
                                    preferred_element_type=f32)

    return pl.pallas_call(
        body, out_shape=SDS((G, K, N), f32), grid=(G, nk, nn, S // ts),
        in_specs=[BS((ts, tk), lambda g, a, b, s: (s, g * nk + a)), BS((ts, tn), lambda g, a, b, s: (s, g * nn + b))],
        out_specs=BS((1, tk, tn), lambda g, a, b, s: (g, a, b)),
        compiler_params=_cp("parallel", "parallel", "parallel", "arbitrary"), name=name)(x, dy)


def rms_bwd(dxn, x, g, dres, name):
    S, Dm = x.shape
    T = _tile(S)

    def body(*refs):
        if dres is not None:
            d_ref, x_ref, g_ref, r_ref, dx_ref, dg_ref = refs
        else:
            d_ref, x_ref, g_ref, dx_ref, dg_ref = refs
        xv = x_ref[...]
        d = d_ref[...].astype(f32)
        r = lax.rsqrt(jnp.mean(xv * xv, axis=-1, keepdims=True) + EPS)
        u = d * g_ref[...]
        dx = r * u - xv * ((r * r * r) * (1.0 / Dm)) * jnp.sum(u * xv, axis=-1, keepdims=True)
        if dres is not None:
            dx = r_ref[...] + dx
        dx_ref[...] = dx

        @pl.when(pl.program_id(0) == 0)
        def _():
            dg_ref[...] = jnp.zeros_like(dg_ref)

        dg_ref[...] += jnp.sum(d * xv * r, axis=0, keepdims=True)

    tok = BS((T, Dm), lambda i: (i, 0))
    vec = BS((1, Dm), lambda i: (0, 0))
    args = (dxn, x, g) + ((dres,) if dres is not None else ())
    return pl.pallas_call(
        body, out_shape=(SDS((S, Dm), f32), SDS((1, Dm), f32)), grid=(S // T,),
        in_specs=[tok, tok, vec] + ([tok] if dres is not None else []), out_specs=(tok, vec),
        compiler_params=_cp("arbitrary"), name=name)(*args)


def loss_head(h, g, target, name):
    S, Dm = h.shape
    T = _tile(S)

    def body(h_ref, g_ref, t_ref, loss_ref, dh_ref, dg_ref):
        @pl.when(pl.program_id(0) == 0)
        def _():
            loss_ref[...] = jnp.zeros_like(loss_ref)
            dg_ref[...] = jnp.zeros_like(dg_ref)

        xv = h_ref[...]
        gv = g_ref[...]
        r = lax.rsqrt(jnp.mean(xv * xv, axis=-1, keepdims=True) + EPS)
        diff = xv * r * gv - t_ref[...]
        loss_ref[...] += 0.5 * jnp.sum(jnp.mean(diff * diff, axis=-1, keepdims=True))
        d = diff * (1.0 / Dm)
        u = d * gv
        dh_ref[...] = r * u - xv * ((r * r * r) * (1.0 / Dm)) * jnp.sum(u * xv, axis=-1, keepdims=True)
        dg_ref[...] += jnp.sum(d * xv * r, axis=0, keepdims=True)

    tok = BS((T, Dm), lambda i: (i, 0))
    vec = BS((1, Dm), lambda i: (0, 0))
    return pl.pallas_call(
        body, out_shape=(SDS((1, 128), f32), SDS((S, Dm), f32), SDS((1, Dm), f32)), grid=(S // T,),
        in_specs=[tok, vec, tok], out_specs=(BS((1, 128), lambda i: (0, 0)), tok, vec),
        compiler_params=_cp("arbitrary"), name=name)(h, g, target)


def gate_merge_fwd(gate, z, name):
    S = gate.shape[0]
    T = _tile(S)

    def body(g_ref, z_ref, o_ref):
        acc = jnp.zeros((T, D), f32)
        for n in range(4):
            acc = acc + g_ref[:, n * D:(n + 1) * D].astype(f32) * z_ref[:, n * D:(n + 1) * D].astype(f32)
        o_ref[...] = acc.astype(bf16)

    wide = BS((T, 4 * D), lambda i: (i, 0))
    return pl.pallas_call(body, out_shape=SDS((S, D), bf16), grid=(S // T,), in_specs=[wide, wide],
                          out_specs=BS((T, D), lambda i: (i, 0)), compiler_params=_cp("parallel"), name=name)(gate, z)


def gate_merge_bwd(dm, gate, z, name):
    S = gate.shape[0]
    T = _tile(S)

    def body(dm_ref, g_ref, z_ref, dz_ref, dg_ref, db_ref):
        @pl.when(pl.program_id(0) == 0)
        def _():
            db_ref[...] = jnp.zeros_like(db_ref)

        dmv = dm_ref[...].astype(f32)
        for n in range(4):
            cols = slice(n * D, (n + 1) * D)
            gv = g_ref[:, cols].astype(f32)
            dz_ref[:, cols] = (dmv * gv).astype(bf16)
            dgp = dmv * z_ref[:, cols].astype(f32) * gv * (1.0 - gv)
            dg_ref[:, cols] = dgp.astype(bf16)
            db_ref[:, cols] += jnp.sum(dgp, axis=0, keepdims=True)

    wide = BS((T, 4 * D), lambda i: (i, 0))
    return pl.pallas_call(
        body, out_shape=(SDS((S, 4 * D), bf16), SDS((S, 4 * D), bf16), SDS((1, 4 * D), f32)), grid=(S // T,),
        in_specs=[BS((T, D), lambda i: (i, 0)), wide, wide], out_specs=(wide, wide, BS((1, 4 * D), lambda i: (0, 0))),
        compiler_params=_cp("arbitrary"), name=name)(dm, gate, z)


def ple_bwd_ew(dh, e, pg, name):
    S = dh.shape[0]
    T = _tile(S)

    def body(dh_ref, e_ref, pg_ref, dp_ref, de_ref, db_ref):
        @pl.when(pl.program_id(0) == 0)
        def _():
            db_ref[...] = jnp.zeros_like(db_ref)

        d = dh_ref[...]
        g = pg_ref[...].astype(f32)
        dpre = d * e_ref[...].astype(f32) * g * (1.0 - g)
        dp_ref[...] = dpre.astype(bf16)
        de_ref[...] = (d * g).astype(bf16)
        db_ref[...] += jnp.sum(dpre, axis=0, keepdims=True)

    tok = BS((T, D), lambda i: (i, 0))
    return pl.pallas_call(
        body, out_shape=(SDS((S, D), bf16), SDS((S, D), bf16), SDS((1, D), f32)), grid=(S // T,),
        in_specs=[tok, tok, tok], out_specs=(tok, tok, BS((1, D), lambda i: (0, 0))),
        compiler_params=_cp("arbitrary"), name=name)(dh, e, pg)


_GK = 0.7978845608028654
_GC = 0.044715


def _gelu(x):
    return 0.5 * x * (1.0 + jnp.tanh(_GK * (x + _GC * (x * x * x))))


def _gelu_grad(x):
    x2 = x * x
    t = jnp.tanh(_GK * (x + _GC * (x * x2)))
    return 0.5 * (1.0 + t) + 0.5 * x * (1.0 - t * t) * (_GK * (1.0 + 3.0 * _GC * x2))


def _ln_stats(v):
    mu = jnp.mean(v, axis=-1, keepdims=True)
    vc = v - mu
    rs = lax.rsqrt(jnp.mean(vc * vc, axis=-1, keepdims=True) + EPS)
    return vc * rs, rs


def _ln_bwd(dvh, vh, rs):
    return rs * (dvh - jnp.mean(dvh, axis=-1, keepdims=True) - vh * jnp.mean(dvh * vh, axis=-1, keepdims=True))


def sg_fwd(proj, lg, lb, wm, bsb, name):
    S = proj.shape[0]
    T = _tile(S)
    cu, cv = OUR_COLS["sg_u"][0] // BW, OUR_COLS["sg_v"][0] // BW

    def body(u_ref, v_ref, lg_ref, lb_ref, wm_ref, bsb_ref, o_ref):
        for b in range(T // 128):
            rows = slice(b * 128, (b + 1) * 128)
            u = _gelu(u_ref[rows, :].astype(f32))
            vh, _ = _ln_stats(_gelu(v_ref[rows, :].astype(f32)))
            vb = (vh * lg_ref[...] + lb_ref[...]).astype(bf16)
            outs = []
            for g in range(4):
                cols = slice(g * 128, (g + 1) * 128)
                mixed = jnp.dot(wm_ref[g], vb[:, cols], preferred_element_type=f32) + bsb_ref[g]
                outs.append(u[:, cols] * mixed)
            o_ref[rows, :] = jnp.concatenate(outs, axis=1).astype(bf16)

    vec = BS((1, BW), lambda i: (0, 0))
    cube = BS((4, 128, 128), lambda i: (0, 0, 0))
    return pl.pallas_call(
        body, out_shape=SDS((S, 4 * BW), bf16), grid=(S // T,),
        in_specs=[BS((T, BW), lambda i: (i, cu)), BS((T, BW), lambda i: (i, cv)), vec, vec, cube, cube],
        out_specs=BS((T, BW), lambda i: (i, 0)), compiler_params=_cp("parallel"), name=name)(proj, proj, lg, lb, wm, bsb)


def sg_bwd(proj, dy, lg, lb, wm, bsb, maskf, name):
    S = proj.shape[0]
    T = _tile(S)
    cu, cv = OUR_COLS["sg_u"][0] // BW, OUR_COLS["sg_v"][0] // BW
    creg = OUR_COLS["sg_u"][0] // (2 * BW)

    def body(u_ref, v_ref, dy_ref, lg_ref, lb_ref, wm_ref, bsb_ref, mk_ref, dp_ref, dwm_ref, dbs_ref, dlg_ref, dlb_ref):
        @pl.when(pl.program_id(0) == 0)
        def _():
            dwm_ref[...] = jnp.zeros_like(dwm_ref)
            dbs_ref[...] = jnp.zeros_like(dbs_ref)
            dlg_ref[...] = jnp.zeros_like(dlg_ref)
            dlb_ref[...] = jnp.zeros_like(dlb_ref)

        for b in range(T // 128):
            rows = slice(b * 128, (b + 1) * 128)
            su = u_ref[rows, :].astype(f32)
            sv = v_ref[rows, :].astype(f32)
            dya = dy_ref[rows, :].astype(f32)
            u = _gelu(su)
            vh, rs = _ln_stats(_gelu(sv))
            vb = (vh * lg_ref[...] + lb_ref[...]).astype(bf16)
            dus, dvls = [], []
            for g in range(4):
                cols = slice(g * 128, (g + 1) * 128)
                mixed = jnp.dot(wm_ref[g], vb[:, cols], preferred_element_type=f32) + bsb_ref[g]
                dus.append(dya[:, cols] * mixed)
                dmg = dya[:, cols] * u[:, cols]
                dmb = dmg.astype(bf16)
                dbs_ref[g] += jnp.broadcast_to(jnp.sum(dmg, axis=1, keepdims=True), (128, 128))
                dwm_ref[g] += mk_ref[...] * lax.dot_general(dmb, vb[:, cols], (((1,), (1,)), ((), ())),
                                                            preferred_element_type=f32)
                dvls.append(lax.dot_general(wm_ref[g], dmb, (((0,), (0,)), ((), ())), preferred_element_type=f32))
            du = jnp.concatenate(dus, axis=1)
            dvln = jnp.concatenate(dvls, axis=1)
            dlg_ref[...] += jnp.sum(dvln * vh, axis=0, keepdims=True)
            dlb_ref[...] += jnp.sum(dvln, axis=0, keepdims=True)
            dv = _ln_bwd(dvln * lg_ref[...], vh, rs)
            dp_ref[rows, 0:BW] = (du * _gelu_grad(su)).astype(bf16)
            dp_ref[rows, BW:2 * BW] = (dv * _gelu_grad(sv)).astype(bf16)

    vec = BS((1, BW), lambda i: (0, 0))
    cube = BS((4, 128, 128), lambda i: (0, 0, 0))
    return pl.pallas_call(
        body,
        out_shape=(SDS((S, NP), bf16), SDS((4, 128, 128), f32), SDS((4, 128, 128), f32), SDS((1, BW), f32), SDS((1, BW), f32)),
        grid=(S // T,),
        in_specs=[BS((T, BW), lambda i: (i, cu)), BS((T, BW), lambda i: (i, cv)), BS((T, BW), lambda i: (i, 0)), vec, vec,
                  cube, cube, BS((128, 128), lambda i: (0, 0))],
        out_specs=(BS((T, 2 * BW), lambda i: (i, creg)), cube, cube, vec, vec),
        compiler_params=_cp("arbitrary"), name=name)(proj, proj, dy, lg, lb, wm, bsb, maskf)


_SUB = 64


def _conv_specs(S, T):
    ca, cg = OUR_COLS["c_a"][0] // BW, OUR_COLS["c_g"][0] // BW
    hb = T // HALO
    prev = lambda i: jnp.maximum(i * hb - 1, 0)
    return [BS((T, BW), lambda i: (i, ca)), BS((T, BW), lambda i: (i, cg)),
            BS((HALO, BW), lambda i: (prev(i), ca)), BS((HALO, BW), lambda i: (prev(i), cg))]


def _fill_shifts(sh):
    n = sh.shape[1] - 8
    for s in range(1, 8):
        sh[s, pl.ds(0, n), :] = sh[0, pl.ds(s, n), :]


def _shifted(sh, off, rows):
    s = off % 8
    return sh[s, pl.ds(off - s, rows), :]


def _conv_fill_ybuf(a_ref, g_ref, ap_ref, gp_ref, ysh):
    T = a_ref.shape[0]
    ysh[0, pl.ds(HALO, T), :] = a_ref[...].astype(f32) * jax.nn.sigmoid(g_ref[...].astype(f32))
    first = (pl.program_id(0) == 0).astype(f32)
    ysh[0, pl.ds(0, HALO), :] = (1.0 - first) * (ap_ref[...].astype(f32) * jax.nn.sigmoid(gp_ref[...].astype(f32)))
    _fill_shifts(ysh)


def _conv_taps(w_ref, ysh, r0):
    acc = jnp.zeros((_SUB, BW), f32)
    for k in range(CONV_K):
        acc = acc + w_ref[k:k + 1, :] * _shifted(ysh, r0 + HALO - (CONV_K - 1) + k, _SUB)
    return acc


def conv_fwd(proj, w, b, lg, lb, y, name):
    S = proj.shape[0]
    T = _tile(S)

    def body(a_ref, g_ref, ap_ref, gp_ref, w_ref, b_ref, lg_ref, lb_ref, y_in, o_ref, ybuf):
        del y_in
        _conv_fill_ybuf(a_ref, g_ref, ap_ref, gp_ref, ybuf)
        for sb in range(T // _SUB):
            z = _conv_taps(w_ref, ybuf, sb * _SUB) + b_ref[...]
            zh, _ = _ln_stats(z)
            zl = zh * lg_ref[...] + lb_ref[...]
            o_ref[pl.ds(sb * _SUB, _SUB), :] = (zl * jax.nn.sigmoid(zl)).astype(bf16)

    vec = BS((1, BW), lambda i: (0, 0))
    return pl.pallas_call(
        body, out_shape=SDS(y.shape, bf16), grid=(S // T,),
        in_specs=_conv_specs(S, T) + [BS((CONV_K, BW), lambda i: (0, 0)), vec, vec, vec, ANY],
        out_specs=BS((T, BW), lambda i: (i, 3)), scratch_shapes=[pltpu.VMEM((8, T + HALO, BW), f32)],
        input_output_aliases={8: 0}, compiler_params=_cp("parallel"), name=name)(proj, proj, proj, proj, w, b, lg, lb, y)


def conv_bwd_norm(proj, dy, w, b, lg, lb, name):
    S = proj.shape[0]
    T = _tile(S)

    def body(a_ref, g_ref, ap_ref, gp_ref, dy_ref, w_ref, b_ref, lg_ref, lb_ref, dz_ref, dlg_ref, dlb_ref, db_ref, ybuf):
        @pl.when(pl.program_id(0) == 0)
        def _():
            dlg_ref[...] = jnp.zeros_like(dlg_ref)
            dlb_ref[...] = jnp.zeros_like(dlb_ref)
            db_ref[...] = jnp.zeros_like(db_ref)

        _conv_fill_ybuf(a_ref, g_ref, ap_ref, gp_ref, ybuf)
        for sb in range(T // _SUB):
            rows = pl.ds(sb * _SUB, _SUB)
            z = _conv_taps(w_ref, ybuf, sb * _SUB) + b_ref[...]
            zh, rs = _ln_stats(z)
            zl = zh * lg_ref[...] + lb_ref[...]
            sg = jax.nn.sigmoid(zl)
            dzl = dy_ref[rows, :].astype(f32) * sg * (1.0 + zl * (1.0 - sg))
            dlg_ref[...] += jnp.sum(dzl * zh, axis=0, keepdims=True)
            dlb_ref[...] += jnp.sum(dzl, axis=0, keepdims=True)
            dz = _ln_bwd(dzl * lg_ref[...], zh, rs)
            db_ref[...] += jnp.sum(dz, axis=0, keepdims=True)
            dz_ref[rows, :] = dz

    vec = BS((1, BW), lambda i: (0, 0))
    tok = BS((T, BW), lambda i: (i, 0))
    return pl.pallas_call(
        body, out_shape=(SDS((S, BW), f32), SDS((1, BW), f32), SDS((1, BW), f32), SDS((1, BW), f32)), grid=(S // T,),
        in_specs=_conv_specs(S, T) + [BS((T, BW), lambda i: (i, 3)), BS((CONV_K, BW), lambda i: (0, 0)), vec, vec, vec],
        out_specs=(tok, vec, vec, vec), scratch_shapes=[pltpu.VMEM((8, T + HALO, BW), f32)],
        compiler_params=_cp("arbitrary"), name=name)(proj, proj, proj, proj, dy, w, b, lg, lb)


def conv_bwd_taps(proj, dz, w, dproj, name):
    S = proj.shape[0]
    T = _tile(S)
    nT = S // T
    hb = T // HALO
    creg = OUR_COLS["c_a"][0] // (2 * BW)

    def body(a_ref, g_ref, ap_ref, gp_ref, dz_ref, dzn_ref, w_ref, dp_in, dp_ref, dw_ref, ybuf, dzbuf, dwacc):
        del dp_in
        i = pl.program_id(0)

        @pl.when(i == 0)
        def _():
            dwacc[...] = jnp.zeros_like(dwacc)

        _conv_fill_ybuf(a_ref, g_ref, ap_ref, gp_ref, ybuf)
        dzbuf[0, pl.ds(0, T), :] = dz_ref[...]
        dzbuf[0, pl.ds(T, HALO), :] = (i < nT - 1).astype(f32) * dzn_ref[...]
        _fill_shifts(dzbuf)
        for sb in range(T // _SUB):
            r0 = sb * _SUB
            rows = pl.ds(r0, _SUB)
            dzs = dz_ref[rows, :]
            dyg = jnp.zeros((_SUB, BW), f32)
            for k in range(CONV_K):
                ysl = _shifted(ybuf, r0 + HALO - (CONV_K - 1) + k, _SUB)
                dwacc[pl.ds(k * 8, 8), :] += jnp.sum((dzs * ysl).reshape(_SUB // 8, 8, BW), axis=0)
                dyg = dyg + w_ref[k:k + 1, :] * _shifted(dzbuf, r0 + (CONV_K - 1) - k, _SUB)
            av = a_ref[rows, :].astype(f32)
            sg = jax.nn.sigmoid(g_ref[rows, :].astype(f32))
            dp_ref[rows, 0:BW] = (dyg * sg).astype(bf16)
            dp_ref[rows, BW:2 * BW] = (dyg * av * sg * (1.0 - sg)).astype(bf16)

        @pl.when(i == nT - 1)
        def _():
            for k in range(CONV_K):
                dw_ref[k:k + 1, :] = jnp.sum(dwacc[pl.ds(k * 8, 8), :], axis=0, keepdims=True)

    nxt = lambda i: jnp.minimum((i + 1) * hb, S // HALO - 1)
    return pl.pallas_call(
        body, out_shape=(SDS((S, NP), bf16), SDS((CONV_K, BW), f32)), grid=(nT,),
        in_specs=_conv_specs(S, T) + [BS((T, BW), lambda i: (i, 0)), BS((HALO, BW), lambda i: (nxt(i), 0)),
                                      BS((CONV_K, BW), lambda i: (0, 0)), ANY],
        out_specs=(BS((T, 2 * BW), lambda i: (i, creg)), BS((CONV_K, BW), lambda i: (0, 0))),
        scratch_shapes=[pltpu.VMEM((8, T + HALO, BW), f32), pltpu.VMEM((8, T + HALO, BW), f32),
                        pltpu.VMEM((CONV_K * 8, BW), f32)],
        input_output_aliases={7: 0}, compiler_params=_cp("arbitrary"), name=name)(proj, proj, proj, proj, dz, dz, w, dproj)


def _toeplitz_index():
    j = lax.broadcasted_iota(jnp.int32, (REL_TABLE, 1024), 1)
    t = lax.broadcasted_iota(jnp.int32, (REL_TABLE, 1024), 0)
    e = ((WIN - 1) - j) & 1023
    tidx = jnp.clip(e - (TQ - 1), -(CHUNK - 1), 256) + (CHUNK - 1)
    return (tidx == t).astype(f32)


def att_bias_build(table, name):
    H = table.shape[0]

    def body(t_ref, o_ref):
        u = jnp.dot(t_ref[...], _toeplitz_index(), precision=HI, preferred_element_type=f32)
        row = lax.broadcasted_iota(jnp.int32, (TQ, 1024), 0)
        r = lax.broadcasted_iota(jnp.int32, (TQ, WIN), 0)
        n = lax.broadcasted_iota(jnp.int32, (TQ, WIN), 1)
        dchunk = (r // CHUNK + 8) - n // CHUNK
        band = (dchunk >= 0) & (dchunk <= 8)
        for h in range(H):
            x = jnp.broadcast_to(u[h:h + 1, :], (TQ, 1024))
            for b in range(8):
                x = jnp.where(((row >> b) & 1) == 1, pltpu.roll(x, 1 << b, 1), x)
            o_ref[h] = jnp.where(band, x[:, :WIN], NEG_INF)

    return pl.pallas_call(body, out_shape=SDS((H, TQ, WIN), f32), compiler_params=pltpu.CompilerParams(vmem_limit_bytes=VMEM_LIMIT),
                          name=name)(table)


def att_bias_grad(dbias, name):
    H = dbias.shape[0]

    def body(d_ref, o_ref):
        row = lax.broadcasted_iota(jnp.int32, (TQ, 1024), 0)
        rows = []
        for h in range(H):
            x = jnp.concatenate([d_ref[h], jnp.zeros((TQ, 1024 - WIN), f32)], axis=1)
            for b in range(8):
                x = jnp.where(((row >> b) & 1) == 1, pltpu.roll(x, 1024 - (1 << b), 1), x)
            rows.append(jnp.sum(x, axis=0, keepdims=True))
        du = jnp.concatenate(rows, axis=0)
        o_ref[...] = lax.dot_general(du, _toeplitz_index(), (((1,), (1,)), ((), ())), precision=HI,
                                     preferred_element_type=f32)

    return pl.pallas_call(body, out_shape=SDS((H, REL_TABLE), f32), compiler_params=pltpu.CompilerParams(vmem_limit_bytes=VMEM_LIMIT),
                          name=name)(dbias)


def _att_specs():
    cq, ck, cv = (OUR_COLS[n][0] // BW for n in ("a_q", "a_k", "a_v"))
    specs = [BS((TQ, BW), lambda i: (i, cq))]
    for col in (ck, cv):
        for back in (2, 1, 0):
            specs.append(BS((TQ, BW), functools.partial(lambda i, back, col: (jnp.maximum(i - back, 0), col), back=back, col=col)))
    return specs


def _att_pen(i):
    n = lax.broadcasted_iota(jnp.int32, (1, WIN), 1)
    return jnp.where(n + (i - 2) * TQ >= 0, 0.0, NEG_INF).astype(f32)


def _att_probs(qa, kp, bias_h, pen):
    s = lax.dot_general(qa, kp, (((1,), (1,)), ((), ())), preferred_element_type=f32) + bias_h + pen
    e = jnp.exp(s - jnp.max(s, axis=-1, keepdims=True))
    return e * (1.0 / jnp.sum(e, axis=-1, keepdims=True))


def att_fwd(proj, bias, y, name):
    S = proj.shape[0]

    def body(q_ref, k2, k1, k0, v2, v1, v0, b_ref, y_in, o_ref, kwin, vwin):
        del y_in
        i = pl.program_id(0)
        for w, (kr, vr) in enumerate(((k2, v2), (k1, v1), (k0, v0))):
            kwin[pl.ds(w * TQ, TQ), :] = kr[...]
            vwin[pl.ds(w * TQ, TQ), :] = vr[...]
        pen = _att_pen(i)
        lo = lax.broadcasted_iota(jnp.int32, (TQ, 128), 1) < 64
        for hp in range(4):
            cols = slice(hp * 128, (hp + 1) * 128)
            qp, kp, vp = q_ref[:, cols] * jnp.asarray(0.125, bf16), kwin[:, cols], vwin[:, cols]
            outs = []
            for a in range(2):
                qa = jnp.where(lo if a == 0 else ~lo, qp, jnp.zeros_like(qp))
                p = _att_probs(qa, kp, b_ref[2 * hp + a], pen)
                outs.append(jnp.dot(p.astype(bf16), vp, preferred_element_type=f32))
            o_ref[:, cols] = jnp.where(lo, outs[0], outs[1]).astype(bf16)

    return pl.pallas_call(
        body, out_shape=SDS(y.shape, bf16), grid=(S // TQ,),
        in_specs=_att_specs() + [BS((8, TQ, WIN), lambda i: (0, 0, 0), pipeline_mode=pl.Buffered(1)), ANY],
        out_specs=BS((TQ, BW), lambda i: (i, 2)),
        scratch_shapes=[pltpu.VMEM((WIN, BW), bf16), pltpu.VMEM((WIN, BW), bf16)],
        input_output_aliases={8: 0}, compiler_params=_cp("parallel"), name=name)(
            proj, proj, proj, proj, proj, proj, proj, bias, y)


def att_bwd(proj, y, dy, bias, dproj, name):
    S = proj.shape[0]
    cq = OUR_COLS["a_q"][0] // BW

    def body(q_ref, k2, k1, k0, v2, v1, v0, b_ref, o_ref, do_ref, dp_in, dq_ref, dkp_ref, dvp_ref, db_ref, kwin, vwin):
        del dp_in
        i = pl.program_id(0)

        @pl.when(i == 0)
        def _():
            db_ref[...] = jnp.zeros_like(db_ref)

        for w, (kr, vr) in enumerate(((k2, v2), (k1, v1), (k0, v0))):
            kwin[pl.ds(w * TQ, TQ), :] = kr[...]
            vwin[pl.ds(w * TQ, TQ), :] = vr[...]
        pen = _att_pen(i)
        lo = lax.broadcasted_iota(jnp.int32, (TQ, 128), 1) < 64
        for hp in range(4):
            cols = slice(hp * 128, (hp + 1) * 128)
            qp, kp, vp = q_ref[:, cols] * jnp.asarray(0.125, bf16), kwin[:, cols], vwin[:, cols]
            dop, op = do_ref[:, cols], o_ref[:, cols]
            dqs = []
            dk = jnp.zeros((WIN, 128), f32)
            dv = jnp.zeros((WIN, 128), f32)
            for a in range(2):
                sel = lo if a == 0 else ~lo
                qa = jnp.where(sel, qp, jnp.zeros_like(qp))
                doa = jnp.where(sel, dop, jnp.zeros_like(dop))
                p = _att_probs(qa, kp, b_ref[2 * hp + a], pen)
                dpv = lax.dot_general(doa, vp, (((1,), (1,)), ((), ())), preferred_element_type=f32)
                delta = jnp.sum(doa.astype(f32) * op.astype(f32), axis=-1, keepdims=True)
                ds = p * (dpv - delta)
                db_ref[2 * hp + a] += ds
                dsb = ds.astype(bf16)
                dqs.append(jnp.dot(dsb, kp, preferred_element_type=f32))
                dk = dk + lax.dot_general(dsb, qa, (((0,), (0,)), ((), ())), preferred_element_type=f32)
                dv = dv + lax.dot_general(p.astype(bf16), doa, (((0,), (0,)), ((), ())), preferred_element_type=f32)
            dq_ref[:, cols] = (jnp.where(lo, dqs[0], dqs[1]) * 0.125).astype(bf16)
            for w in range(3):
                dkp_ref[w, :, cols] = dk[w * TQ:(w + 1) * TQ].astype(bf16)
                dvp_ref[w, :, cols] = dv[w * TQ:(w + 1) * TQ].astype(bf16)

    tok = BS((TQ, BW), lambda i: (i, 2))
    part = BS((3, TQ, BW), lambda i: (0, i, 0))
    full = BS((8, TQ, WIN), lambda i: (0, 0, 0))
    return pl.pallas_call(
        body, out_shape=(SDS((S, NP), bf16), SDS((3, S, BW), bf16), SDS((3, S, BW), bf16), SDS((8, TQ, WIN), f32)),
        grid=(S // TQ,),
        in_specs=_att_specs() + [BS((8, TQ, WIN), lambda i: (0, 0, 0), pipeline_mode=pl.Buffered(1)), tok, tok, ANY],
        out_specs=(BS((TQ, BW), lambda i: (i, cq)), part, part, full),
        scratch_shapes=[pltpu.VMEM((WIN, BW), bf16), pltpu.VMEM((WIN, BW), bf16)],
        input_output_aliases={10: 0}, compiler_params=_cp("arbitrary"), name=name)(
            proj, proj, proj, proj, proj, proj, proj, bias, y, dy, dproj)


def att_shift_add(dkp, dvp, dproj, name):
    S = dkp.shape[1]
    nT = S // TQ
    creg = OUR_COLS["a_k"][0] // (2 * BW)

    def body(k2, k1, k0, v2, v1, v0, dp_in, dp_ref):
        del dp_in
        j = pl.program_id(0)
        m1 = (j + 1 < nT).astype(f32)
        m0 = (j + 2 < nT).astype(f32)
        dp_ref[:, 0:BW] = (k2[0].astype(f32) + m1 * k1[0].astype(f32) + m0 * k0[0].astype(f32)).astype(bf16)
        dp_ref[:, BW:2 * BW] = (v2[0].astype(f32) + m1 * v1[0].astype(f32) + m0 * v0[0].astype(f32)).astype(bf16)

    def spec(w):
        return BS((1, TQ, BW), functools.partial(lambda j, w: (w, jnp.minimum(j + 2 - w, nT - 1), 0), w=w))

    return pl.pallas_call(
        body, out_shape=SDS(dproj.shape, bf16), grid=(nT,),
        in_specs=[spec(2), spec(1), spec(0), spec(2), spec(1), spec(0), ANY],
        out_specs=BS((TQ, 2 * BW), lambda j: (j, creg)),
        input_output_aliases={6: 0}, compiler_params=_cp("parallel"), name=name)(dkp, dkp, dkp, dvp, dvp, dvp, dproj)


GQ, GV = 256, 512
TGC = 8


def _bd_mask():
    r = lax.broadcasted_iota(jnp.int32, (GQ, GV), 0) // 64
    c = lax.broadcasted_iota(jnp.int32, (GQ, GV), 1) // 128
    return (r == c).astype(f32)


def _tri(strict):
    r = lax.broadcasted_iota(jnp.int32, (CHUNK, CHUNK), 0)
    c = lax.broadcasted_iota(jnp.int32, (CHUNK, CHUNK), 1)
    return ((c < r) if strict else (c <= r)).astype(f32)


def _compact(s_bd):
    return jnp.concatenate([s_bd[h * 64:(h + 1) * 64, h * 128:(h + 1) * 128] for h in range(4)], axis=0)


def _expand(comp, mask):
    return jnp.tile(comp, (1, 4)) * mask


def _gla_gates(alr, wa_ref, ba_ref, tri_incl, ones_col):
    a = jnp.dot(alr, wa_ref[...], preferred_element_type=f32) + ba_ref[...]
    la = (jnp.minimum(a, 0.0) - jnp.log(1.0 + jnp.exp(-jnp.abs(a)))) * (1.0 / 16.0)
    cum = jnp.dot(tri_incl, la, precision=HI, preferred_element_type=f32)
    tot_row = cum[CHUNK - 1:CHUNK, :]
    tot_col = lax.dot_general(la, ones_col, (((0,), (0,)), ((), ())), precision=HI, preferred_element_type=f32)
    return a, cum, tot_row, jnp.tile(jnp.exp(tot_col), (1, 4))


def _head_norm(o):
    rns, ons = [], []
    for h in range(4):
        oh = o[:, h * 128:(h + 1) * 128]
        rn = lax.rsqrt(jnp.mean(oh * oh, axis=-1, keepdims=True) + EPS)
        rns.append(rn)
        ons.append(oh * rn)
    return rns, ons


def _gla_in_specs(T, imap):
    cq, ck = OUR_COLS["g_q"][0] // GQ, OUR_COLS["g_k"][0] // GQ
    cv, cr = OUR_COLS["g_v"][0] // GV, OUR_COLS["g_r"][0] // GV
    return [BS((T, GQ), lambda i: (imap(i), cq)), BS((T, GQ), lambda i: (imap(i), ck)), BS((T, GV), lambda i: (imap(i), cv)),
            BS((T, GV), lambda i: (imap(i), cr)), BS((T, RANKP), lambda i: (imap(i), 0))]


def gla_fwd(proj, pa, wa, ba, ng, y, name):
    S = proj.shape[0]
    T = min(TGC * CHUNK, S)
    nch = T // CHUNK

    def body(q_ref, k_ref, v_ref, r_ref, a_ref, wa_ref, ba_ref, ng_ref, y_in, y_ref, st_ref, s_scr):
        del y_in

        @pl.when(pl.program_id(0) == 0)
        def _():
            s_scr[...] = jnp.zeros_like(s_scr)

        mask = _bd_mask()
        tri = _tri(False)
        ones_col = jnp.ones((CHUNK, 128), f32)

        s_bd = s_scr[...]
        for ci in range(nch):
            rows = pl.ds(ci * CHUNK, CHUNK)
            _, cum, tot_row, dec4 = _gla_gates(a_ref[rows, :], wa_ref, ba_ref, tri, ones_col)
            kd = (k_ref[rows, :].astype(f32) * jnp.exp(tot_row - cum)).astype(bf16)
            upd = lax.dot_general(kd, v_ref[rows, :], (((0,), (0,)), ((), ())), preferred_element_type=f32) * mask
            s_bd = dec4 * s_bd + upd
            st_ref[pl.ds(ci * GQ, GQ), :] = _compact(s_bd)
            qs = (q_ref[rows, :].astype(f32) * 0.125).astype(bf16)
            o = jnp.dot(qs, s_bd.astype(bf16), preferred_element_type=f32)
            _, ons = _head_norm(o)
            rv = r_ref[rows, :].astype(f32)
            y_ref[rows, :] = (jnp.concatenate(ons, axis=1) * ng_ref[...] * (rv * jax.nn.sigmoid(rv))).astype(bf16)
        s_scr[...] = s_bd

    return pl.pallas_call(
        body, out_shape=(SDS(y.shape, bf16), SDS((S // CHUNK * GQ, 128), f32)), grid=(S // T,),
        in_specs=_gla_in_specs(T, lambda i: i) + [BS((RANKP, GQ), lambda i: (0, 0)), BS((1, GQ), lambda i: (0, 0)),
                                                  BS((1, GV), lambda i: (0, 0)), ANY],
        out_specs=(BS((T, GV), lambda i: (i, 1)), BS((nch * GQ, 128), lambda i: (i, 0))),
        scratch_shapes=[pltpu.VMEM((GQ, GV), f32)], input_output_aliases={8: 0}, compiler_params=_cp("arbitrary"),
        name=name)(proj, proj, proj, proj, pa, wa, ba, ng, y)


def gla_bwd(proj, pa, states, dy, wa, ba, ng, dproj, name):
    S = proj.shape[0]
    T = min(TGC * CHUNK, S)
    nch = T // CHUNK
    nT = S // T
    rev = lambda i: nT - 1 - i

    def body(q_ref, k_ref, v_ref, r_ref, a_ref, st_ref, sp_ref, dy_ref, wa_ref, ba_ref, ng_ref, dp_in,
             dp_ref, da_ref, dwa_ref, dba_ref, dng_ref, g_scr):
        del dp_in
        i = pl.program_id(0)

        @pl.when(i == 0)
        def _():
            g_scr[...] = jnp.zeros_like(g_scr)
            dwa_ref[...] = jnp.zeros_like(dwa_ref)
            dba_ref[...] = jnp.zeros_like(dba_ref)
            dng_ref[...] = jnp.zeros_like(dng_ref)

        mask = _bd_mask()
        tri = _tri(False)
        tri_strict = _tri(True)
        ones_col = jnp.ones((CHUNK, 128), f32)
        ones_row = jnp.ones((8, 128), f32)
        first_tile = (i == nT - 1).astype(f32)

        g_carry = g_scr[...]
        for ci in reversed(range(nch)):
            rows = pl.ds(ci * CHUNK, CHUNK)
            alr = a_ref[rows, :]
            a, cum, tot_row, dec4 = _gla_gates(alr, wa_ref, ba_ref, tri, ones_col)
            wdec = jnp.exp(tot_row - cum)
            kdf = k_ref[rows, :].astype(f32) * wdec
            kd = kdf.astype(bf16)
            s_c = _expand(st_ref[pl.ds(ci * GQ, GQ), :], mask)
            prev = st_ref[pl.ds((ci - 1) * GQ, GQ), :] if ci > 0 else sp_ref[...] * (1.0 - first_tile)
            qs = (q_ref[rows, :].astype(f32) * 0.125).astype(bf16)
            s_cb = s_c.astype(bf16)
            o = jnp.dot(qs, s_cb, preferred_element_type=f32)
            rns, ons = _head_norm(o)
            on = jnp.concatenate(ons, axis=1)
            rv = r_ref[rows, :].astype(f32)
            sg = jax.nn.sigmoid(rv)
            sr = rv * sg
            dyv = dy_ref[rows, :].astype(f32)
            ngv = ng_ref[...]
            dng_ref[...] += jnp.sum(dyv * on * sr, axis=0, keepdims=True)
            d_on = dyv * ngv * sr
            dr = dyv * on * ngv * (sg * (1.0 + rv * (1.0 - sg)))
            dos = []
            for h in range(4):
                cols = slice(h * 128, (h + 1) * 128)
                dh_ = d_on[:, cols]
                dos.append(rns[h] * (dh_ - ons[h] * jnp.mean(dh_ * ons[h], axis=-1, keepdims=True)))
            do = jnp.concatenate(dos, axis=1).astype(bf16)
            dq = lax.dot_general(do, s_cb, (((1,), (1,)), ((), ())), preferred_element_type=f32) * 0.125
            ds = lax.dot_general(qs, do, (((0,), (0,)), ((), ())), preferred_element_type=f32) * mask + g_carry
            ddec_row = lax.dot_general(ones_row, _compact(ds) * prev, (((1,), (1,)), ((), ())), precision=HI,
                                       preferred_element_type=f32)[0:1, :]
            dsb = ds.astype(bf16)
            dkd = lax.dot_general(v_ref[rows, :], dsb, (((1,), (1,)), ((), ())), preferred_element_type=f32)
            dv = jnp.dot(kd, dsb, preferred_element_type=f32)
            g_carry = dec4 * ds
            dk = dkd * wdec
            dwlog = dkd * kdf
            dla = ddec_row * jnp.exp(tot_row) + jnp.dot(tri_strict, dwlog, precision=HI, preferred_element_type=f32)
            da = dla * (1.0 - jax.nn.sigmoid(a)) * (1.0 / 16.0)
            dab = da.astype(bf16)
            da_ref[rows, :] = lax.dot_general(dab, wa_ref[...], (((1,), (1,)), ((), ())),
                                              preferred_element_type=f32).astype(bf16)
            dwa_ref[...] += lax.dot_general(alr, dab, (((0,), (0,)), ((), ())), preferred_element_type=f32)
            dba_ref[...] += jnp.sum(da, axis=0, keepdims=True)
            dp_ref[rows, 0:GQ] = dq.astype(bf16)
            dp_ref[rows, GQ:2 * GQ] = dk.astype(bf16)
            dp_ref[rows, 2 * GQ:2 * GQ + GV] = dv.astype(bf16)
            dp_ref[rows, 2 * GQ + GV:2 * GQ + 2 * GV] = dr.astype(bf16)
        g_scr[...] = g_carry

    REG = 2 * GQ + 2 * GV
    return pl.pallas_call(
        body,
        out_shape=(SDS((S, NP), bf16), SDS((S, RANKP), bf16), SDS((RANKP, GQ), f32), SDS((1, GQ), f32), SDS((1, GV), f32)),
        grid=(nT,),
        in_specs=_gla_in_specs(T, rev) + [
            BS((nch * GQ, 128), lambda i: (rev(i), 0)),
            BS((GQ, 128), lambda i: (jnp.maximum(rev(i) * nch - 1, 0), 0)),
            BS((T, GV), lambda i: (rev(i), 1)),
            BS((RANKP, GQ), lambda i: (0, 0)), BS((1, GQ), lambda i: (0, 0)), BS((1, GV), lambda i: (0, 0)), ANY],
        out_specs=(BS((T, REG), lambda i: (rev(i), 0)), BS((T, RANKP), lambda i: (rev(i), 0)),
                   BS((RANKP, GQ), lambda i: (0, 0)), BS((1, GQ), lambda i: (0, 0)), BS((1, GV), lambda i: (0, 0))),
        scratch_shapes=[pltpu.VMEM((GQ, GV), f32)],
        input_output_aliases={11: 0}, compiler_params=_cp("arbitrary"), name=name)(
            proj, proj, proj, proj, pa, states, states, dy, wa, ba, ng, dproj)


def _as2d(a):
    if a.ndim == 1:
        return a.reshape(1, a.shape[0])
    return a.reshape(-1, a.shape[-1])


def adamw(w, g, m, v, name):
    shape = w.shape
    w2, g2, m2, v2 = (_as2d(a) for a in (w, g, m, v))
    R, C = w2.shape
    tr = R
    for cand in (512, 256, 128, 64, 32, 16, 8):
        if R % cand == 0 and cand * C * 4 * 7 * 2 <= 40 * 1024 * 1024:
            tr = cand
            break

    def body(w_ref, g_ref, m_ref, v_ref, d_ref, mo_ref, vo_ref):
        gv = g_ref[...]
        mn = ADAM_B1 * m_ref[...] + (1.0 - ADAM_B1) * gv
        vn = ADAM_B2 * v_ref[...] + (1.0 - ADAM_B2) * (gv * gv)
        m_hat = mn / (1.0 - ADAM_B1 ** ADAM_STEP)
        v_hat = vn / (1.0 - ADAM_B2 ** ADAM_STEP)
        d_ref[...] = -ADAM_LR * (m_hat / (jnp.sqrt(v_hat) + ADAM_EPS) + ADAM_WD * w_ref[...])
        mo_ref[...] = mn
        vo_ref[...] = vn

    blk = BS((tr, C), lambda i: (i, 0))
    outs = pl.pallas_call(body, out_shape=tuple(SDS((R, C), f32) for _ in range(3)), grid=(R // tr,),
                          in_specs=[blk] * 4, out_specs=(blk,) * 3, compiler_params=_cp("parallel"), name=name)(w2, g2, m2, v2)
    return tuple(o.reshape(shape) for o in outs)


def _row_tile(rows, row_bytes, budget=4 * 1024 * 1024):
    best = None
    for t in range(16, rows + 1, 16):
        if rows % t == 0 and t * row_bytes <= budget:
            best = t
    return best or rows


def add_halves(gp, ra, c, name):
    shape = ra.shape
    cols = shape[-1]
    rows = int(np.prod(shape[:-1]))
    g3, r2 = gp.reshape(2, rows, cols), ra.reshape(rows, cols)
    tr = _row_tile(rows, cols * 2)

    grid_spec = pltpu.PrefetchScalarGridSpec(
        num_scalar_prefetch=1, grid=(rows // tr,),
        in_specs=[BS((1, tr, cols), lambda i, c_ref: (c_ref[0], i, 0)), BS((tr, cols), lambda i, c_ref: (i, 0))],
        out_specs=BS((tr, cols), lambda i, c_ref: (i, 0)))

    def body(c_ref, a_ref, b_ref, o_ref):
        del c_ref
        o_ref[...] = (a_ref[0].astype(f32) + b_ref[...].astype(f32)).astype(bf16)

    out = pl.pallas_call(body, out_shape=SDS((rows, cols), bf16), grid_spec=grid_spec, compiler_params=_cp("parallel"),
                         name=name)(jnp.reshape(c, (1,)).astype(jnp.int32), g3, r2)
    return out.reshape(shape)


def reduce_chips(rb, own, c, chip, name):
    shape = rb.shape[1:]
    cols = shape[-1]
    rows = int(np.prod(shape[:-1]))
    tr = _row_tile(rows, cols * 2 * 4)
    rb3, own3 = rb.reshape(4, rows, cols), own.reshape(4, rows, cols)

    def body(s_ref, own_ref, r1, r2, r3, o_ref):
        del s_ref
        o_ref[0] = ((own_ref[0].astype(f32) + r1[0].astype(f32)) + r2[0].astype(f32)) + r3[0].astype(f32)

    def slot(k):
        return BS((1, tr, cols), functools.partial(lambda i, s, k: ((s[1] + k) % 4, i, 0), k=k))

    grid_spec = pltpu.PrefetchScalarGridSpec(
        num_scalar_prefetch=1, grid=(rows // tr,), in_specs=[slot(0), slot(1), slot(2), slot(3)],
        out_specs=BS((1, tr, cols), lambda i, s: (s[0], i, 0)))
    out = pl.pallas_call(body, out_shape=SDS((2, rows, cols), f32), grid_spec=grid_spec, compiler_params=_cp("parallel"),
                         name=name)(jnp.stack([c, chip]).astype(jnp.int32), own3, rb3, rb3, rb3)
    return out.reshape((2,) + shape)


def sum_slots(x, name):
    N, shape = x.shape[0], x.shape[1:]
    cols = shape[-1]
    rows = int(np.prod(shape[:-1]))
    tr = _row_tile(rows, cols * x.dtype.itemsize * N)

    def body(x_ref, o_ref):
        acc = x_ref[0].astype(f32)
        for n in range(1, N):
            acc = acc + x_ref[n].astype(f32)
        o_ref[...] = acc

    out = pl.pallas_call(body, out_shape=SDS((rows, cols), f32), grid=(rows // tr,),
                         in_specs=[BS((N, tr, cols), lambda i: (0, i, 0))], out_specs=BS((tr, cols), lambda i: (i, 0)),
                         compiler_params=_cp("parallel"), name=name)(x.reshape(N, rows, cols))
    return out.reshape(shape)


def _me():
    return lax.axis_index("x"), lax.axis_index("y"), lax.axis_index("c")


def _rcopy(src, dst, send_sems, recv_sems, k, dev):
    return pltpu.make_async_remote_copy(src_ref=src, dst_ref=dst, send_sem=send_sems.at[k], recv_sem=recv_sems.at[k],
                                        device_id=dev, device_id_type=MESH)


def _comm_call(body, ins, out_shapes, n_remote, name, aliases=None):
    return pl.pallas_call(
        body, out_shape=tuple(out_shapes), in_specs=[ANY] * len(ins), out_specs=tuple(ANY for _ in out_shapes),
        scratch_shapes=[pltpu.SemaphoreType.DMA((n_remote,)), pltpu.SemaphoreType.DMA((n_remote,))],
        input_output_aliases=aliases or {}, name=name)(*ins)


def ag4(bufs, name):
    n = len(bufs)

    def body(*refs):
        xs, os = refs[:n], refs[n:2 * n]
        send_sems, recv_sems = refs[2 * n:]
        x, y, c = _me()
        j = 2 * x + y
        sib = (x, y, 1 - c)
        chips = [(1 - x, y), (x, 1 - y), (1 - x, 1 - y)]
        first = [_rcopy(xs[t].at[c], os[t].at[c, j], send_sems, recv_sems, 8 * t + k, (cx, cy, c))
                 for t in range(n) for k, (cx, cy) in enumerate(chips)]
        own = [_rcopy(xs[t].at[l], os[t].at[l, j], send_sems, recv_sems, 8 * t + 6 + l, sib) for t in range(n) for l in range(2)]
        for cp in first + own:
            cp.start()
        passed = []
        for k, (cx, cy) in enumerate(chips):
            for t in range(n):
                land = os[t].at[c, 2 * cx + cy]
                _rcopy(land, land, send_sems, recv_sems, 8 * t + k, (x, y, c)).wait_recv()
                fwd = _rcopy(land, land, send_sems, recv_sems, 8 * t + 3 + k, sib)
                fwd.start()
                passed.append(fwd)
        for t in range(n):
            for l in range(2):
                land = os[t].at[l, j]
                _rcopy(land, land, send_sems, recv_sems, 8 * t + 6 + l, (x, y, c)).wait_recv()
        for k, (cx, cy) in enumerate(chips):
            for t in range(n):
                land = os[t].at[1 - c, 2 * cx + cy]
                _rcopy(land, land, send_sems, recv_sems, 8 * t + 3 + k, (x, y, c)).wait_recv()
        for cp in first + own + passed:
            cp.wait_send()

    outs = [SDS((2, 4) + b.shape[1:], b.dtype) for b in bufs]
    return _comm_call(body, bufs, outs, 8 * n, name)


def sib_other_layer(gps, name):
    n = len(gps)

    def body(*refs):
        xs, os = refs[:n], refs[n:2 * n]
        send_sems, recv_sems = refs[2 * n:]
        x, y, c = _me()
        cps = [_rcopy(xs[t].at[1 - c], os[t], send_sems, recv_sems, t, (x, y, 1 - c)) for t in range(n)]
        for cp in cps:
            cp.start()
        for cp in cps:
            cp.wait()

    return _comm_call(body, gps, [SDS(g.shape[1:], g.dtype) for g in gps], n, name)


def a2a4(ps, name):
    n = len(ps)

    def body(*refs):
        xs, os = refs[:n], refs[n:2 * n]
        send_sems, recv_sems = refs[2 * n:]
        x, y, c = _me()
        j = 2 * x + y
        chips = [(1 - x, y), (x, 1 - y), (1 - x, 1 - y)]
        sends = [_rcopy(xs[t].at[2 * cx + cy], os[t].at[j], send_sems, recv_sems, 3 * t + k, (cx, cy, c))
                 for t in range(n) for k, (cx, cy) in enumerate(chips)]
        for cp in sends:
            cp.start()
        for t in range(n):
            for k, (cx, cy) in enumerate(chips):
                land = os[t].at[2 * cx + cy]
                _rcopy(land, land, send_sems, recv_sems, 3 * t + k, (x, y, c)).wait_recv()
        for cp in sends:
            cp.wait_send()

    return _comm_call(body, ps, [SDS(p.shape, p.dtype) for p in ps], 3 * n, name)


def ag2(bufs, name):
    n = len(bufs)

    def body(*refs):
        xs, os = refs[:n], refs[n:2 * n]
        send_sems, recv_sems = refs[2 * n:]
        x, y, c = _me()
        cps = [_rcopy(xs[t].at[c], os[t].at[c], send_sems, recv_sems, t, (x, y, 1 - c)) for t in range(n)]
        for cp in cps:
            cp.start()
        for t in range(n):
            land = os[t].at[1 - c]
            _rcopy(land, land, send_sems, recv_sems, t, (x, y, c)).wait_recv()
        for cp in cps:
            cp.wait_send()

    return _comm_call(body, bufs, [SDS(b.shape, b.dtype) for b in bufs], n, name, aliases={t: t for t in range(n)})


def ag8(blk, name):
    m_per, n = blk.shape

    def body(x_ref, out_ref, send_sems, recv_sems, local_sem):
        x, y, c = _me()
        me, sibling = (x, y, c), (x, y, 1 - c)
        chips = [(1 - x, y), (x, 1 - y), (1 - x, 1 - y)]

        def rows(px, py, pc):
            return out_ref.at[pl.ds((4 * px + 2 * py + pc) * m_per, m_per), :]

        def copy(k, block, to, src=None):
            return pltpu.make_async_remote_copy(
                src_ref=rows(*block) if src is None else src, dst_ref=rows(*block), send_sem=send_sems.at[k],
                recv_sem=recv_sems.at[k], device_id=to, device_id_type=MESH)

        mine = pltpu.make_async_copy(x_ref, rows(*me), local_sem)
        mine.start()
        first = [copy(0, me, sibling, src=x_ref)]
        first += [copy(1 + j, me, (*chip, c), src=x_ref) for j, chip in enumerate(chips)]
        for cp in first:
            cp.start()
        passed = [copy(4 + j, (*chip, c), sibling) for j, chip in enumerate(chips)]
        for j, chip in enumerate(chips):
            copy(1 + j, (*chip, c), me).wait_recv()
            passed[j].start()
        copy(0, sibling, me).wait_recv()
        for j, chip in enumerate(chips):
            copy(4 + j, (*chip, 1 - c), me).wait_recv()
        for cp in first + passed:
            cp.wait_send()
        mine.wait()

    return pl.pallas_call(
        body, out_shape=SDS((8 * m_per, n), blk.dtype), in_specs=[pl.BlockSpec(memory_space=pltpu.VMEM)],
        out_specs=pl.BlockSpec(memory_space=pltpu.VMEM),
        scratch_shapes=[pltpu.SemaphoreType.DMA((7,)), pltpu.SemaphoreType.DMA((7,)), pltpu.SemaphoreType.DMA],
        name=name)(blk)


def _split_chips(full, axis):
    n = full.shape[axis] // 4
    parts = full.reshape(full.shape[:axis] + (4, n) + full.shape[axis + 1:])
    return jnp.moveaxis(parts, axis, 0)


def _merge_chips(gathered, axis):
    parts = jnp.moveaxis(gathered, 0, axis)
    return parts.reshape(parts.shape[:axis] + (parts.shape[axis] * parts.shape[axis + 1],) + parts.shape[axis + 2:])


def _to_ref_cols(main, rank):
    pieces = []
    for n, width in REF_SPLITS:
        if n == "g_a":
            pieces.append(rank[..., :RANK])
        else:
            off = OUR_COLS[n][0]
            pieces.append(main[..., off:off + width])
    return jnp.concatenate(pieces, axis=-1)


def _from_ref_cols(w):
    offs, o = {}, 0
    for n, width in REF_SPLITS:
        offs[n] = (o, width)
        o += width
    main = jnp.concatenate([w[..., offs[n][0]:offs[n][0] + offs[n][1]] for n in sorted(OUR_COLS, key=lambda k: OUR_COLS[k][0])],
                           axis=-1)
    ro = offs["g_a"][0]
    rank = jnp.pad(w[..., ro:ro + RANK], [(0, 0)] * (w.ndim - 1) + [(0, RANKP - RANK)])
    return main, rank


def _layer_fwd(h, p_i, W, li):
    t = f"l{li}_"
    sv = {"h0": h}
    xn = rms_fwd(h, W["norm1_g"], t + "rms1")
    proj = mm_nn(xn, W["w_in_main"], name=t + "inproj")
    pa = mm_nn(xn, W["w_in_rank"], name=t + "inproj_rank")
    y = sg_fwd(proj, W["sg_ln_g"], W["sg_ln_b"], W["sg_wm"], W["sg_bsb"], t + "sg_fwd")
    y, states = gla_fwd(proj, pa, W["gla_wa"], W["gla_b_a"], W["gla_norm_g"], y, t + "gla_fwd")
    y = att_fwd(proj, W["att_bias"], y, t + "att_fwd")
    y = conv_fwd(proj, W["conv_dw_w"], W["conv_dw_b"], W["conv_ln_g"], W["conv_ln_b"], y, t + "conv_fwd")
    gate = mm_nn(xn, W["w_gate_all"], bias=W["b_gate_all"], act="sigmoid", name=t + "gate")
    z = mm_nn(y, W["w_branch"], name=t + "branch")
    m = gate_merge_fwd(gate, z, t + "merge")
    h1 = mm_nn(m, W["w_out"], res=h, out_dtype=f32, name=t + "outproj")
    hn = rms_fwd(h1, W["norm2_g"], t + "rms2")
    a = mm_nn(hn, W["w_ff1"], name=t + "ff1")
    h2 = mm_nn(a, W["w_ff2"], pre="relu2", res=h1, out_dtype=f32, name=t + "ff2")
    hg = rms_fwd(h2, W["norm3_g"], t + "rms3")
    pg = mm_nn(hg, W["w_ple_gate"], bias=W["b_ple_gate"], act="sigmoid", name=t + "ple_gate")
    h3, e = mm_nn(p_i, W["w_ple"], mul=pg, res=h2, out_dtype=f32, raw_out=True, name=t + "ple_out")
    sv.update(xn=xn, proj=proj, pa=pa, states=states, y=y, gate=gate, z=z, m=m, h1=h1, hn=hn, a=a, h2=h2, hg=hg, pg=pg, e=e)
    return h3, sv


def _layer_bwd(dh3, sv, p_i, W, li):
    t = f"l{li}_b_"
    G = {}
    dpg, de, G["b_ple_gate"] = ple_bwd_ew(dh3, sv["e"], sv["pg"], t + "ple_ew")
    G["w_ple_gate"] = mm_tn(sv["hg"], dpg, name=t + "dw_ple_gate")[0]
    G["w_ple"] = mm_tn(p_i, de, name=t + "dw_ple")[0]
    dhg = mm_nt(dpg, W["w_ple_gate"], name=t + "dhg")
    dh2, G["norm3_g"] = rms_bwd(dhg, sv["h2"], W["norm3_g"], dh3, t + "rms3")
    da = mm_nt(dh2, W["w_ff2"], post_a=sv["a"], out_dtype=bf16, name=t + "da")
    G["w_ff2"] = mm_tn(sv["a"], dh2, pre="relu2", name=t + "dw_ff2")[0]
    G["w_ff1"] = mm_tn(sv["hn"], da, name=t + "dw_ff1")[0]
    dhn = mm_nt(da, W["w_ff1"], name=t + "dhn")
    dh1, G["norm2_g"] = rms_bwd(dhn, sv["h1"], W["norm2_g"], dh2, t + "rms2")
    dm = mm_nt(dh1, W["w_out"], out_dtype=bf16, name=t + "dm")
    G["w_out"] = mm_tn(sv["m"], dh1, name=t + "dw_out")[0]
    dz, dgp, G["b_gate_all"] = gate_merge_bwd(dm, sv["gate"], sv["z"], t + "merge")
    G["w_branch"] = mm_tn(sv["y"], dz, G=4, name=t + "dw_branch")
    dy = mm_nt(dz, W["w_branch"], out_dtype=bf16, name=t + "dy")
    G["w_gate_all"] = mm_tn(sv["xn"], dgp, name=t + "dw_gate")[0]
    dxn = mm_nt(dgp, W["w_gate_all"], name=t + "dxn_gate")
    proj = sv["proj"]
    dproj, dwm, dbs, G["sg_ln_g"], G["sg_ln_b"] = sg_bwd(proj, dy, W["sg_ln_g"], W["sg_ln_b"], W["sg_wm"], W["sg_bsb"],
                                                          W["sg_maskf"], t + "sg")
    G["sg_w"], G["sg_b"] = dwm, dbs[:, :, 0]
    dproj, dpa, dwa, G["gla_b_a"], G["gla_norm_g"] = gla_bwd(proj, sv["pa"], sv["states"], dy, W["gla_wa"], W["gla_b_a"],
                                                             W["gla_norm_g"], dproj, t + "gla")
    G["gla_w_a2"] = dwa[:RANK]
    dproj, dkp, dvp, dbias = att_bwd(proj, sv["y"], dy, W["att_bias"], dproj, t + "att")
    dproj = att_shift_add(dkp, dvp, dproj, t + "att_kv")
    G["att_rel_bias"] = att_bias_grad(dbias, t + "att_bias")
    dz_c, G["conv_ln_g"], G["conv_ln_b"], G["conv_dw_b"] = conv_bwd_norm(proj, dy, W["conv_dw_w"], W["conv_dw_b"],
                                                                        W["conv_ln_g"], W["conv_ln_b"], t + "conv_norm")
    dproj, G["conv_dw_w"] = conv_bwd_taps(proj, dz_c, W["conv_dw_w"], dproj, t + "conv_taps")
    G["w_in_main"] = mm_tn(sv["xn"], dproj, name=t + "dw_in")[0]
    G["w_in_rank"] = mm_tn(sv["xn"], dpa, name=t + "dw_in_rank")[0]
    dxn = mm_nt(dpa, W["w_in_rank"], res=dxn, name=t + "dxn_rank")
    dxn = mm_nt(dproj, W["w_in_main"], res=dxn, name=t + "dxn_main")
    dh0, G["norm1_g"] = rms_bwd(dxn, sv["h0"], W["norm1_g"], dh1, t + "rms1")
    return dh0, G


def _prep_layer_weights(gathered, repl, li):
    W = {}
    row = lambda a: a.reshape(1, -1)
    full = {n: _merge_chips(gathered[n][li], SHARDED[n][1]) for n in BIG}
    small = {n: _merge_chips(gathered[n][li], SHARDED[n][1]) for n in SMALL}
    main, rank = _from_ref_cols(full["w_in"])
    W["w_in_main"], W["w_in_rank"] = main[None], rank[None]
    W["w_branch"] = full["w_branch"]
    W["w_gate_all"] = jnp.transpose(full["w_gate"], (1, 0, 2)).reshape(1, D, 4 * D)
    W["b_gate_all"] = small["b_gate"].reshape(1, 4 * D)
    for n in ("w_out", "w_ff1", "w_ff2", "w_ple_gate", "w_ple"):
        W[n] = full[n][None]
    for n in ("norm1_g", "norm2_g", "norm3_g", "b_ple_gate", "sg_ln_g", "sg_ln_b", "gla_b_a", "gla_norm_g", "conv_dw_b",
              "conv_ln_g", "conv_ln_b"):
        W[n] = row(repl[n][li])
    pos = np.arange(128)
    mask = (pos[None, :] // CHUNK) <= (pos[:, None] // CHUNK)
    W["sg_maskf"] = jnp.asarray(mask, f32)
    W["sg_wm"] = jnp.where(mask[None], repl["sg_w"][li], 0.0).astype(bf16)
    W["sg_bsb"] = jnp.broadcast_to(repl["sg_b"][li][:, :, None], (4, 128, 128))
    W["gla_wa"] = jnp.pad(small["gla_w_a2"], ((0, RANKP - RANK), (0, 0))).astype(bf16)
    W["att_bias"] = att_bias_build(small["att_rel_bias"], f"l{li}_att_bias")
    W["conv_dw_w"] = small["conv_dw_w"]
    return W


def _layer_grads_to_ref(G):
    out = {}
    out["w_in"] = _to_ref_cols(G["w_in_main"], G["w_in_rank"])
    out["w_gate"] = jnp.transpose(G["w_gate_all"].reshape(D, 4, D), (1, 0, 2))
    out["b_gate"] = G["b_gate_all"].reshape(4, D)
    for n in ("w_branch", "w_out", "w_ff1", "w_ff2", "w_ple_gate", "w_ple", "gla_w_a2", "att_rel_bias", "conv_dw_w", "sg_w",
              "sg_b"):
        out[n] = G[n]
    for n in ("norm1_g", "norm2_g", "norm3_g", "b_ple_gate", "sg_ln_g", "sg_ln_b", "gla_b_a", "gla_norm_g", "conv_dw_b",
              "conv_ln_g", "conv_ln_b"):
        out[n] = G[n].reshape(-1)
    return out


def kernel(x, p, norm1_g, w_in, sg_ln_g, sg_ln_b, sg_w, sg_b, gla_w_a2, gla_b_a, gla_norm_g, att_rel_bias, conv_dw_w, conv_dw_b, conv_ln_g, conv_ln_b, w_branch, w_gate, b_gate, w_out, norm2_g, w_ff1, w_ff2, norm3_g, w_ple_gate, b_ple_gate, w_ple, final_g, loss_target, m_norm1_g, m_w_in, m_sg_ln_g, m_sg_ln_b, m_sg_w, m_sg_b, m_gla_w_a2, m_gla_b_a, m_gla_norm_g, m_att_rel_bias, m_conv_dw_w, m_conv_dw_b, m_conv_ln_g, m_conv_ln_b, m_w_branch, m_w_gate, m_b_gate, m_w_out, m_norm2_g, m_w_ff1, m_w_ff2, m_norm3_g, m_w_ple_gate, m_b_ple_gate, m_w_ple, m_final_g, v_norm1_g, v_w_in, v_sg_ln_g, v_sg_ln_b, v_sg_w, v_sg_b, v_gla_w_a2, v_gla_b_a, v_gla_norm_g, v_att_rel_bias, v_conv_dw_w, v_conv_dw_b, v_conv_ln_g, v_conv_ln_b, v_w_branch, v_w_gate, v_b_gate, v_w_out, v_norm2_g, v_w_ff1, v_w_ff2, v_norm3_g, v_w_ple_gate, v_b_ple_gate, v_w_ple, v_final_g):
    args = dict(locals())
    weights = {n: args[n] for n in W_ORDER}
    moments_m = {n: args["m_" + n] for n in W_ORDER}
    moments_v = {n: args["v_" + n] for n in W_ORDER}
    c = lax.axis_index("c")
    sharded_names = BIG + SMALL

    shards = [weights[n].astype(bf16) for n in BIG] + [weights[n] for n in SMALL]
    gathered = dict(zip(sharded_names, ag4(shards, "ag_weights")))
    repl = {n: weights[n] for n in REPL}

    h = x[0]
    Ws, saved = [], []
    for li in range(DEPTH):
        W = _prep_layer_weights(gathered, repl, li)
        h, sv = _layer_fwd(h, p[li, 0], W, li)
        Ws.append(W)
        saved.append(sv)
    loss_part, dh, dfinal = loss_head(h, final_g.reshape(1, D), loss_target[0], "loss_head")
    loss = lax.psum(loss_part[0, 0], ("x", "y", "c"))

    layer_grads = [None] * DEPTH
    for li in reversed(range(DEPTH)):
        dh, G = _layer_bwd(dh, saved[li], p[li, 0], Ws[li], li)
        layer_grads[li] = _layer_grads_to_ref(G)
    grad_x = dh[None]

    gps = [jnp.stack([_split_chips(layer_grads[li][n], SHARDED[n][1]) for li in range(DEPTH)]).astype(bf16) for n in BIG]
    ras = sib_other_layer(gps, "rs_sibling_layer")
    psums = [add_halves(g, r, c, "rs_add_" + n) for n, g, r in zip(BIG, gps, ras)]
    rbs = a2a4(psums, "rs_all_to_all")
    chip = 2 * lax.axis_index("x") + lax.axis_index("y")
    reds = [reduce_chips(r, ps, c, chip, "rs_sum_" + n) for n, r, ps in zip(BIG, rbs, psums)]
    grads = dict(zip(BIG, ag2(reds, "rs_sibling_gather")))

    local = {n: jnp.stack([layer_grads[li][n] for li in range(DEPTH)]) for n in tuple(REPL)[:-1] + SMALL}
    local["final_g"] = dfinal.reshape(D)
    rnames = tuple(REPL) + SMALL
    rflat = jnp.concatenate([local[n].reshape(-1) for n in rnames])
    rflat = jnp.pad(rflat, (0, REPL_ROWS * PACK_W - rflat.shape[0])).reshape(REPL_ROWS, PACK_W)
    rall = ag8(rflat, "ar_gather").reshape(8, REPL_ROWS, PACK_W)
    rsum = sum_slots(rall, "ar_sum").reshape(-1)
    off = 0
    for n in rnames:
        shape = local[n].shape
        size = int(np.prod(shape))
        g = rsum[off:off + size].reshape(shape)
        off += size
        if n in SMALL:
            ax = SHARDED[n][1] + 1
            g = lax.dynamic_slice_in_dim(g, chip * (shape[ax] // 4), shape[ax] // 4, axis=ax)
        grads[n] = g

    deltas, new_m, new_v = {}, {}, {}
    for n in W_ORDER:
        deltas[n], new_m[n], new_v[n] = adamw(weights[n], grads[n], moments_m[n], moments_v[n], "adamw_" + n)
    return (loss, grad_x, *[grads[n] for n in W_ORDER], *[deltas[n] for n in W_ORDER], *[new_m[n] for n in W_ORDER],
            *[new_v[n] for n in W_ORDER])
```

```python
import functools

import jax
import jax.numpy as jnp
import numpy as np
from jax import lax
from jax.experimental import pallas as pl
from jax.experimental.pallas import tpu as pltpu

f32, bf16 = jnp.float32, jnp.bfloat16
HI = lax.Precision.HIGHEST
MESH = pl.DeviceIdType.MESH
SDS = jax.ShapeDtypeStruct
BS = pl.BlockSpec
ANY = pl.BlockSpec(memory_space=pl.ANY)

D = 1024
DEPTH = 2
CHUNK = 64
BW = 512
NP = 5120
RANK = 16
RANKP = 128
DFF = 4096
PLE = 256
CONV_K = 31
HALO = 32
TQ = 256
WIN = 768
REL_TABLE = 320
EPS = 1e-6
NEG_INF = -1e30
VMEM_LIMIT = 56 * 1024 * 1024

ADAM_LR, ADAM_B1, ADAM_B2, ADAM_EPS, ADAM_WD, ADAM_STEP = 0.001, 0.9, 0.999, 1e-08, 0.01, 10

OUR_COLS = dict(g_q=(0, 256), g_k=(256, 256), g_v=(512, 512), g_r=(1024, 512), a_q=(1536, 512), a_k=(2048, 512),
                a_v=(2560, 512), sg_u=(3072, 512), sg_v=(3584, 512), c_a=(4096, 512), c_g=(4608, 512))
REF_SPLITS = (("sg_u", 512), ("sg_v", 512), ("g_q", 256), ("g_k", 256), ("g_v", 512), ("g_r", 512), ("g_a", 16),
              ("a_q", 512), ("a_k", 512), ("a_v", 512), ("c_a", 512), ("c_g", 512))

SHARDED = dict(w_in=((1024, 5136), 1), w_branch=((4, 512, 1024), 2), w_gate=((4, 1024, 1024), 1), w_out=((1024, 1024), 0),
               w_ff1=((1024, 4096), 1), w_ff2=((4096, 1024), 0), w_ple_gate=((1024, 1024), 0), w_ple=((256, 1024), 1),
               gla_w_a2=((16, 256), 1), att_rel_bias=((8, 320), 1), conv_dw_w=((31, 512), 1), b_gate=((4, 1024), 1))
BIG = ("w_in", "w_branch", "w_gate", "w_out", "w_ff1", "w_ff2", "w_ple_gate", "w_ple")
SMALL = ("gla_w_a2", "att_rel_bias", "conv_dw_w", "b_gate")
REPL = dict(norm1_g=(2, 1024), sg_ln_g=(2, 512), sg_ln_b=(2, 512), sg_w=(2, 4, 128, 128), sg_b=(2, 4, 128), gla_b_a=(2, 256),
            gla_norm_g=(2, 512), conv_dw_b=(2, 512), conv_ln_g=(2, 512), conv_ln_b=(2, 512), norm2_g=(2, 1024),
            norm3_g=(2, 1024), b_ple_gate=(2, 1024), final_g=(1024,))
W_ORDER = ['norm1_g', 'w_in', 'sg_ln_g', 'sg_ln_b', 'sg_w', 'sg_b', 'gla_w_a2', 'gla_b_a', 'gla_norm_g', 'att_rel_bias',
           'conv_dw_w', 'conv_dw_b', 'conv_ln_g', 'conv_ln_b', 'w_branch', 'w_gate', 'b_gate', 'w_out', 'norm2_g', 'w_ff1',
           'w_ff2', 'norm3_g', 'w_ple_gate', 'b_ple_gate', 'w_ple', 'final_g']
PACK_W = 1024
REPL_ROWS = 200
TM = 1024
MM_VMEM_BUDGET = 44 * 1024 * 1024


def _tile(s):
    return 512 if s % 512 == 0 else s


def _token_tile(S, row_bytes, fixed_bytes):
    for t in (2048, 1024):
        if S % t == 0 and 2 * (t * row_bytes + fixed_bytes) <= MM_VMEM_BUDGET:
            return t
    return min(TM, S)


def _cp(*sem):
    return pltpu.CompilerParams(dimension_semantics=sem, vmem_limit_bytes=VMEM_LIMIT)


def rms_fwd(h, g, name):
    S, Dm = h.shape
    T = _tile(S)

    def body(h_ref, g_ref, o_ref):
        x = h_ref[...]
        r = lax.rsqrt(jnp.mean(x * x, axis=-1, keepdims=True) + EPS)
        o_ref[...] = (x * r * g_ref[...]).astype(bf16)

    return pl.pallas_call(
        body, out_shape=SDS((S, Dm), bf16), grid=(S // T,),
        in_specs=[BS((T, Dm), lambda i: (i, 0)), BS((1, Dm), lambda i: (0, 0))],
        out_specs=BS((T, Dm), lambda i: (i, 0)), compiler_params=_cp("parallel"), name=name)(h, g)


def mm_nn(x, w, *, name, bias=None, act=None, pre=None, mul=None, res=None, out_dtype=bf16, raw_out=False, gather=None):
    S = x.shape[0]
    G, K, N = w.shape
    tn = min(1024 if K <= 1024 else 512, N)
    nj = N // tn
    row_bytes = K * x.dtype.itemsize + tn * (jnp.dtype(out_dtype).itemsize + (2 if raw_out else 0)
                                             + sum(a.dtype.itemsize for a in (mul, res) if a is not None))
    T = _token_tile(S, row_bytes, K * tn * 2)
    extras = [a for a in (bias, mul, res) if a is not None]
    ng = len(gather) if gather else 0
    grid = (S // T, G, nj)

    def body(*refs):
        it = iter(refs)
        x_ref, w_ref = next(it), next(it)
        b_ref = next(it) if bias is not None else None
        m_ref = next(it) if mul is not None else None
        r_ref = next(it) if res is not None else None
        g_in = [next(it) for _ in range(ng)]
        o_ref = next(it)
        raw_ref = next(it) if raw_out else None
        if ng:
            g_out = [next(it) for _ in range(ng)]
            step = (pl.program_id(0) * G + pl.program_id(1)) * nj + pl.program_id(2)
            _ag4_over_grid(g_in, g_out, next(it), next(it), step, grid[0] * G * nj)
        xv = x_ref[...]
        if pre == "relu2":
            xf = jnp.maximum(xv.astype(f32), 0.0)
            xv = xf * xf
        acc = jnp.dot(xv.astype(bf16), w_ref[0], preferred_element_type=f32)
        if raw_out:
            raw_ref[...] = acc.astype(bf16)
        if b_ref is not None:
            acc = acc + b_ref[...]
        if act == "sigmoid":
            acc = jax.nn.sigmoid(acc)
        if m_ref is not None:
            acc = acc * m_ref[...].astype(f32)
        if r_ref is not None:
            acc = r_ref[...].astype(f32) + acc
        o_ref[...] = acc.astype(out_dtype)

    in_specs = [BS((T, K), lambda i, g, j: (i, g)), BS((1, K, tn), lambda i, g, j: (g, 0, j))]
    if bias is not None:
        in_specs.append(BS((1, tn), lambda i, g, j: (0, g * nj + j)))
    for a in (mul, res):
        if a is not None:
            in_specs.append(BS((T, tn), lambda i, g, j: (i, g * nj + j)))
    ospec = BS((T, tn), lambda i, g, j: (i, g * nj + j))
    out_shape = SDS((S, G * N), out_dtype)
    if raw_out:
        out_shape, ospec = (out_shape, SDS((S, G * N), bf16)), (ospec, ospec)
    if not ng:
        return pl.pallas_call(
            body, out_shape=out_shape, grid=grid, in_specs=in_specs, out_specs=ospec,
            compiler_params=_cp("parallel", "parallel", "parallel"), name=name)(x, w, *extras)
    out_shape = (out_shape if raw_out else (out_shape,)) + tuple(_ag4_out_shapes(gather))
    ospec = (ospec if raw_out else (ospec,)) + tuple(ANY for _ in gather)
    outs = pl.pallas_call(
        body, out_shape=out_shape, grid=grid, in_specs=in_specs + [ANY] * ng, out_specs=ospec,
        scratch_shapes=_ag4_sems(ng), compiler_params=_cp("arbitrary", "arbitrary", "arbitrary"), name=name)(
            x, w, *extras, *gather)
    nres = 2 if raw_out else 1
    return (outs[0] if nres == 1 else outs[:2]), list(outs[nres:])


def mm_nt(dy, w, *, name, res=None, post_a=None, out_dtype=f32):
    S = dy.shape[0]
    G, K, N = w.shape
    tk = min(1024 if N <= 1024 else 512, K)
    nk = K // tk
    row_bytes = N * dy.dtype.itemsize + tk * (jnp.dtype(out_dtype).itemsize
                                              + sum(a.dtype.itemsize for a in (res, post_a) if a is not None))
    T = _token_tile(S, row_bytes, tk * N * 2)
    extras = [a for a in (res, post_a) if a is not None]

    def body(*refs):
        it = iter(refs)
        d_ref, w_ref = next(it), next(it)
        r_ref = next(it) if res is not None else None
        a_ref = next(it) if post_a is not None else None
        o_ref = next(it)
        acc = lax.dot_general(d_ref[...].astype(bf16), w_ref[0], (((1,), (1,)), ((), ())), preferred_element_type=f32)
        if r_ref is not None:
            acc = acc + r_ref[...].astype(f32)
        if a_ref is not None:
            acc = acc * (2.0 * jnp.maximum(a_ref[...].astype(f32), 0.0))
        o_ref[...] = acc.astype(out_dtype)

    in_specs = [BS((T, N), lambda i, g, j: (i, g)), BS((1, tk, N), lambda i, g, j: (g, j, 0))]
    for a in extras:
        in_specs.append(BS((T, tk), lambda i, g, j: (i, g * nk + j)))
    return pl.pallas_call(
        body, out_shape=SDS((S, G * K), out_dtype), grid=(S // T, G, nk), in_specs=in_specs,
        out_specs=BS((T, tk), lambda i, g, j: (i, g * nk + j)),
        compiler_params=_cp("parallel", "parallel", "parallel"), name=name)(dy, w, *extras)


def mm_tn(x, dy, *, name, G=1, pre=None, ts=1024):
    S = x.shape[0]
    K, N = x.shape[1] // G, dy.shape[1] // G
    tk, tn = min(K, 1024), min(N, 1024)
    nk, nn = K // tk, N // tn
    ts = min(ts, S)

    def body(x_ref, d_ref, o_ref):
        @pl.when(pl.program_id(3) == 0)
        def _():
            o_ref[...] = jnp.zeros_like(o_ref)

        xv = x_ref[...]
        if pre == "relu2":
            xf = jnp.maximum(xv.astype(f32), 0.0)
            xv = xf * xf
        o_ref[0] += lax.dot_general(xv.astype(bf16), d_ref[...].astype(bf16), (((0,), (0,)), ((), ())),
                                    preferred_element_type=f32)

    return pl.pallas_call(
        body, out_shape=SDS((G, K, N), f32), grid=(G, nk, nn, S // ts),
        in_specs=[BS((ts, tk), lambda g, a, b, s: (s, g * nk + a)), BS((ts, tn), lambda g, a, b, s: (s, g * nn + b))],
        out_specs=BS((1, tk, tn), lambda g, a, b, s: (g, a, b)),
        compiler_params=_cp("parallel", "parallel", "parallel", "arbitrary"), name=name)(x, dy)


def rms_bwd(dxn, x, g, dres, name):
    S, Dm = x.shape
    T = _tile(S)

    def body(*refs):
        if dres is not None:
            d_ref, x_ref, g_ref, r_ref, dx_ref, dg_ref = refs
        else:
            d_ref, x_ref, g_ref, dx_ref, dg_ref = refs
        xv = x_ref[...]
        d = d_ref[...].astype(f32)
        r = lax.rsqrt(jnp.mean(xv * xv, axis=-1, keepdims=True) + EPS)
        u = d * g_ref[...]
        dx = r * u - xv * ((r * r * r) * (1.0 / Dm)) * jnp.sum(u * xv, axis=-1, keepdims=True)
        if dres is not None:
            dx = r_ref[...] + dx
        dx_ref[...] = dx

        @pl.when(pl.program_id(0) == 0)
        def _():
            dg_ref[...] = jnp.zeros_like(dg_ref)

        dg_ref[...] += jnp.sum(d * xv * r, axis=0, keepdims=True)

    tok = BS((T, Dm), lambda i: (i, 0))
    vec = BS((1, Dm), lambda i: (0, 0))
    args = (dxn, x, g) + ((dres,) if dres is not None else ())
    return pl.pallas_call(
        body, out_shape=(SDS((S, Dm), f32), SDS((1, Dm), f32)), grid=(S // T,),
        in_specs=[tok, tok, vec] + ([tok] if dres is not None else []), out_specs=(tok, vec),
        compiler_params=_cp("arbitrary"), name=name)(*args)


def loss_head(h, g, target, name):
    S, Dm = h.shape
    T = _tile(S)

    def body(h_ref, g_ref, t_ref, loss_ref, dh_ref, dg_ref):
        @pl.when(pl.program_id(0) == 0)
        def _():
            loss_ref[...] = jnp.zeros_like(loss_ref)
            dg_ref[...] = jnp.zeros_like(dg_ref)

        xv = h_ref[...]
        gv = g_ref[...]
        r = lax.rsqrt(jnp.mean(xv * xv, axis=-1, keepdims=True) + EPS)
        diff = xv * r * gv - t_ref[...]
        loss_ref[...] += 0.5 * jnp.sum(jnp.mean(diff * diff, axis=-1, keepdims=True))
        d = diff * (1.0 / Dm)
        u = d * gv
        dh_ref[...] = r * u - xv * ((r * r * r) * (1.0 / Dm)) * jnp.sum(u * xv, axis=-1, keepdims=True)
        dg_ref[...] += jnp.sum(d * xv * r, axis=0, keepdims=True)

    tok = BS((T, Dm), lambda i: (i, 0))
    vec = BS((1, Dm), lambda i: (0, 0))
    return pl.pallas_call(
        body, out_shape=(SDS((1, 128), f32), SDS((S, Dm), f32), SDS((1, Dm), f32)), grid=(S // T,),
        in_specs=[tok, vec, tok], out_specs=(BS((1, 128), lambda i: (0, 0)), tok, vec),
        compiler_params=_cp("arbitrary"), name=name)(h, g, target)


def gate_merge_fwd(gate, z, name):
    S = gate.shape[0]
    T = _tile(S)

    def body(g_ref, z_ref, o_ref):
        acc = jnp.zeros((T, D), f32)
        for n in range(4):
            acc = acc + g_ref[:, n * D:(n + 1) * D].astype(f32) * z_ref[:, n * D:(n + 1) * D].astype(f32)
        o_ref[...] = acc.astype(bf16)

    wide = BS((T, 4 * D), lambda i: (i, 0))
    return pl.pallas_call(body, out_shape=SDS((S, D), bf16), grid=(S // T,), in_specs=[wide, wide],
                          out_specs=BS((T, D), lambda i: (i, 0)), compiler_params=_cp("parallel"), name=name)(gate, z)


def gate_merge_bwd(dm, gate, z, name):
    S = gate.shape[0]
    T = _tile(S)

    def body(dm_ref, g_ref, z_ref, dz_ref, dg_ref, db_ref):
        @pl.when(pl.program_id(0) == 0)
        def _():
            db_ref[...] = jnp.zeros_like(db_ref)

        dmv = dm_ref[...].astype(f32)
        for n in range(4):
            cols = slice(n * D, (n + 1) * D)
            gv = g_ref[:, cols].astype(f32)
            dz_ref[:, cols] = (dmv * gv).astype(bf16)
            dgp = dmv * z_ref[:, cols].astype(f32) * gv * (1.0 - gv)
            dg_ref[:, cols] = dgp.astype(bf16)
            db_ref[:, cols] += jnp.sum(dgp, axis=0, keepdims=True)

    wide = BS((T, 4 * D), lambda i: (i, 0))
    return pl.pallas_call(
        body, out_shape=(SDS((S, 4 * D), bf16), SDS((S, 4 * D), bf16), SDS((1, 4 * D), f32)), grid=(S // T,),
        in_specs=[BS((T, D), lambda i: (i, 0)), wide, wide], out_specs=(wide, wide, BS((1, 4 * D), lambda i: (0, 0))),
        compiler_params=_cp("arbitrary"), name=name)(dm, gate, z)


def ple_bwd_ew(dh, e, pg, name):
    S = dh.shape[0]
    T = _tile(S)

    def body(dh_ref, e_ref, pg_ref, dp_ref, de_ref, db_ref):
        @pl.when(pl.program_id(0) == 0)
        def _():
            db_ref[...] = jnp.zeros_like(db_ref)

        d = dh_ref[...]
        g = pg_ref[...].astype(f32)
        dpre = d * e_ref[...].astype(f32) * g * (1.0 - g)
        dp_ref[...] = dpre.astype(bf16)
        de_ref[...] = (d * g).astype(bf16)
        db_ref[...] += jnp.sum(dpre, axis=0, keepdims=True)

    tok = BS((T, D), lambda i: (i, 0))
    return pl.pallas_call(
        body, out_shape=(SDS((S, D), bf16), SDS((S, D), bf16), SDS((1, D), f32)), grid=(S // T,),
        in_specs=[tok, tok, tok], out_specs=(tok, tok, BS((1, D), lambda i: (0, 0))),
        compiler_params=_cp("arbitrary"), name=name)(dh, e, pg)


_GK = 0.7978845608028654
_GC = 0.044715


def _gelu(x):
    return 0.5 * x * (1.0 + jnp.tanh(_GK * (x + _GC * (x * x * x))))


def _gelu_grad(x):
    x2 = x * x
    t = jnp.tanh(_GK * (x + _GC * (x * x2)))
    return 0.5 * (1.0 + t) + 0.5 * x * (1.0 - t * t) * (_GK * (1.0 + 3.0 * _GC * x2))


def _ln_stats(v):
    mu = jnp.mean(v, axis=-1, keepdims=True)
    vc = v - mu
    rs = lax.rsqrt(jnp.mean(vc * vc, axis=-1, keepdims=True) + EPS)
    return vc * rs, rs


def _ln_bwd(dvh, vh, rs):
    return rs * (dvh - jnp.mean(dvh, axis=-1, keepdims=True) - vh * jnp.mean(dvh * vh, axis=-1, keepdims=True))


def sg_fwd(proj, lg, lb, wm, bsb, name):
    S = proj.shape[0]
    T = _tile(S)
    cu, cv = OUR_COLS["sg_u"][0] // BW, OUR_COLS["sg_v"][0] // BW

    def body(u_ref, v_ref, lg_ref, lb_ref, wm_ref, bsb_ref, o_ref):
        for b in range(T // 128):
            rows = slice(b * 128, (b + 1) * 128)
            u = _gelu(u_ref[rows, :].astype(f32))
            vh, _ = _ln_stats(_gelu(v_ref[rows, :].astype(f32)))
            vb = (vh * lg_ref[...] + lb_ref[...]).astype(bf16)
            outs = []
            for g in range(4):
                cols = slice(g * 128, (g + 1) * 128)
                mixed = jnp.dot(wm_ref[g], vb[:, cols], preferred_element_type=f32) + bsb_ref[g]
                outs.append(u[:, cols] * mixed)
            o_ref[rows, :] = jnp.concatenate(outs, axis=1).astype(bf16)

    vec = BS((1, BW), lambda i: (0, 0))
    cube = BS((4, 128, 128), lambda i: (0, 0, 0))
    return pl.pallas_call(
        body, out_shape=SDS((S, 4 * BW), bf16), grid=(S // T,),
        in_specs=[BS((T, BW), lambda i: (i, cu)), BS((T, BW), lambda i: (i, cv)), vec, vec, cube, cube],
        out_specs=BS((T, BW), lambda i: (i, 0)), compiler_params=_cp("parallel"), name=name)(proj, proj, lg, lb, wm, bsb)


def sg_bwd(proj, dy, lg, lb, wm, bsb, maskf, name):
    S = proj.shape[0]
    T = _tile(S)
    cu, cv = OUR_COLS["sg_u"][0] // BW, OUR_COLS["sg_v"][0] // BW
    creg = OUR_COLS["sg_u"][0] // (2 * BW)

    def body(u_ref, v_ref, dy_ref, lg_ref, lb_ref, wm_ref, bsb_ref, mk_ref, dp_ref, dwm_ref, dbs_ref, dlg_ref, dlb_ref):
        @pl.when(pl.program_id(0) == 0)
        def _():
            dwm_ref[...] = jnp.zeros_like(dwm_ref)
            dbs_ref[...] = jnp.zeros_like(dbs_ref)
            dlg_ref[...] = jnp.zeros_like(dlg_ref)
            dlb_ref[...] = jnp.zeros_like(dlb_ref)

        for b in range(T // 128):
            rows = slice(b * 128, (b + 1) * 128)
            su = u_ref[rows, :].astype(f32)
            sv = v_ref[rows, :].astype(f32)
            dya = dy_ref[rows, :].astype(f32)
            u = _gelu(su)
            vh, rs = _ln_stats(_gelu(sv))
            vb = (vh * lg_ref[...] + lb_ref[...]).astype(bf16)
            dus, dvls = [], []
            for g in range(4):
                cols = slice(g * 128, (g + 1) * 128)
                mixed = jnp.dot(wm_ref[g], vb[:, cols], preferred_element_type=f32) + bsb_ref[g]
                dus.append(dya[:, cols] * mixed)
                dmg = dya[:, cols] * u[:, cols]
                dmb = dmg.astype(bf16)
                dbs_ref[g] += jnp.broadcast_to(jnp.sum(dmg, axis=1, keepdims=True), (128, 128))
                dwm_ref[g] += mk_ref[...] * lax.dot_general(dmb, vb[:, cols], (((1,), (1,)), ((), ())),
                                                            preferred_element_type=f32)
                dvls.append(lax.dot_general(wm_ref[g], dmb, (((0,), (0,)), ((), ())), preferred_element_type=f32))
            du = jnp.concatenate(dus, axis=1)
            dvln = jnp.concatenate(dvls, axis=1)
            dlg_ref[...] += jnp.sum(dvln * vh, axis=0, keepdims=True)
            dlb_ref[...] += jnp.sum(dvln, axis=0, keepdims=True)
            dv = _ln_bwd(dvln * lg_ref[...], vh, rs)
            dp_ref[rows, 0:BW] = (du * _gelu_grad(su)).astype(bf16)
            dp_ref[rows, BW:2 * BW] = (dv * _gelu_grad(sv)).astype(bf16)

    vec = BS((1, BW), lambda i: (0, 0))
    cube = BS((4, 128, 128), lambda i: (0, 0, 0))
    return pl.pallas_call(
        body,
        out_shape=(SDS((S, NP), bf16), SDS((4, 128, 128), f32), SDS((4, 128, 128), f32), SDS((1, BW), f32), SDS((1, BW), f32)),
        grid=(S // T,),
        in_specs=[BS((T, BW), lambda i: (i, cu)), BS((T, BW), lambda i: (i, cv)), BS((T, BW), lambda i: (i, 0)), vec, vec,
                  cube, cube, BS((128, 128), lambda i: (0, 0))],
        out_specs=(BS((T, 2 * BW), lambda i: (i, creg)), cube, cube, vec, vec),
        compiler_params=_cp("arbitrary"), name=name)(proj, proj, dy, lg, lb, wm, bsb, maskf)


_SUB = 64


def _conv_specs(S, T):
    ca, cg = OUR_COLS["c_a"][0] // BW, OUR_COLS["c_g"][0] // BW
    hb = T // HALO
    prev = lambda i: jnp.maximum(i * hb - 1, 0)
    return [BS((T, BW), lambda i: (i, ca)), BS((T, BW), lambda i: (i, cg)),
            BS((HALO, BW), lambda i: (prev(i), ca)), BS((HALO, BW), lambda i: (prev(i), cg))]


def _fill_shifts(sh):
    n = sh.shape[1] - 8
    for s in range(1, 8):
        sh[s, pl.ds(0, n), :] = sh[0, pl.ds(s, n), :]


def _shifted(sh, off, rows):
    s = off % 8
    return sh[s, pl.ds(off - s, rows), :]


def _conv_fill_ybuf(a_ref, g_ref, ap_ref, gp_ref, ysh):
    T = a_ref.shape[0]
    ysh[0, pl.ds(HALO, T), :] = a_ref[...].astype(f32) * jax.nn.sigmoid(g_ref[...].astype(f32))
    first = (pl.program_id(0) == 0).astype(f32)
    ysh[0, pl.ds(0, HALO), :] = (1.0 - first) * (ap_ref[...].astype(f32) * jax.nn.sigmoid(gp_ref[...].astype(f32)))
    _fill_shifts(ysh)


def _conv_taps(w_ref, ysh, r0):
    acc = jnp.zeros((_SUB, BW), f32)
    for k in range(CONV_K):
        acc = acc + w_ref[k:k + 1, :] * _shifted(ysh, r0 + HALO - (CONV_K - 1) + k, _SUB)
    return acc


def conv_fwd(proj, w, b, lg, lb, y, name):
    S = proj.shape[0]
    T = _tile(S)

    def body(a_ref, g_ref, ap_ref, gp_ref, w_ref, b_ref, lg_ref, lb_ref, y_in, o_ref, ybuf):
        del y_in
        _conv_fill_ybuf(a_ref, g_ref, ap_ref, gp_ref, ybuf)
        for sb in range(T // _SUB):
            z = _conv_taps(w_ref, ybuf, sb * _SUB) + b_ref[...]
            zh, _ = _ln_stats(z)
            zl = zh * lg_ref[...] + lb_ref[...]
            o_ref[pl.ds(sb * _SUB, _SUB), :] = (zl * jax.nn.sigmoid(zl)).astype(bf16)

    vec = BS((1, BW), lambda i: (0, 0))
    return pl.pallas_call(
        body, out_shape=SDS(y.shape, bf16), grid=(S // T,),
        in_specs=_conv_specs(S, T) + [BS((CONV_K, BW), lambda i: (0, 0)), vec, vec, vec, ANY],
        out_specs=BS((T, BW), lambda i: (i, 3)), scratch_shapes=[pltpu.VMEM((8, T + HALO, BW), f32)],
        input_output_aliases={8: 0}, compiler_params=_cp("parallel"), name=name)(proj, proj, proj, proj, w, b, lg, lb, y)


def conv_bwd_norm(proj, dy, w, b, lg, lb, name):
    S = proj.shape[0]
    T = _tile(S)

    def body(a_ref, g_ref, ap_ref, gp_ref, dy_ref, w_ref, b_ref, lg_ref, lb_ref, dz_ref, dlg_ref, dlb_ref, db_ref, ybuf):
        @pl.when(pl.program_id(0) == 0)
        def _():
            dlg_ref[...] = jnp.zeros_like(dlg_ref)
            dlb_ref[...] = jnp.zeros_like(dlb_ref)
            db_ref[...] = jnp.zeros_like(db_ref)

        _conv_fill_ybuf(a_ref, g_ref, ap_ref, gp_ref, ybuf)
        for sb in range(T // _SUB):
            rows = pl.ds(sb * _SUB, _SUB)
            z = _conv_taps(w_ref, ybuf, sb * _SUB) + b_ref[...]
            zh, rs = _ln_stats(z)
            zl = zh * lg_ref[...] + lb_ref[...]
            sg = jax.nn.sigmoid(zl)
            dzl = dy_ref[rows, :].astype(f32) * sg * (1.0 + zl * (1.0 - sg))
            dlg_ref[...] += jnp.sum(dzl * zh, axis=0, keepdims=True)
            dlb_ref[...] += jnp.sum(dzl, axis=0, keepdims=True)
            dz = _ln_bwd(dzl * lg_ref[...], zh, rs)
            db_ref[...] += jnp.sum(dz, axis=0, keepdims=True)
            dz_ref[rows, :] = dz

    vec = BS((1, BW), lambda i: (0, 0))
    tok = BS((T, BW), lambda i: (i, 0))
    return pl.pallas_call(
        body, out_shape=(SDS((S, BW), f32), SDS((1, BW), f32), SDS((1, BW), f32), SDS((1, BW), f32)), grid=(S // T,),
        in_specs=_conv_specs(S, T) + [BS((T, BW), lambda i: (i, 3)), BS((CONV_K, BW), lambda i: (0, 0)), vec, vec, vec],
        out_specs=(tok, vec, vec, vec), scratch_shapes=[pltpu.VMEM((8, T + HALO, BW), f32)],
        compiler_params=_cp("arbitrary"), name=name)(proj, proj, proj, proj, dy, w, b, lg, lb)


def conv_bwd_taps(proj, dz, w, dproj, name):
    S = proj.shape[0]
    T = _tile(S)
    nT = S // T
    hb = T // HALO
    creg = OUR_COLS["c_a"][0] // (2 * BW)

    def body(a_ref, g_ref, ap_ref, gp_ref, dz_ref, dzn_ref, w_ref, dp_in, dp_ref, dw_ref, ybuf, dzbuf, dwacc):
        del dp_in
        i = pl.program_id(0)

        @pl.when(i == 0)
        def _():
            dwacc[...] = jnp.zeros_like(dwacc)

        _conv_fill_ybuf(a_ref, g_ref, ap_ref, gp_ref, ybuf)
        dzbuf[0, pl.ds(0, T), :] = dz_ref[...]
        dzbuf[0, pl.ds(T, HALO), :] = (i < nT - 1).astype(f32) * dzn_ref[...]
        _fill_shifts(dzbuf)
        for sb in range(T // _SUB):
            r0 = sb * _SUB
            rows = pl.ds(r0, _SUB)
            dzs = dz_ref[rows, :]
            dyg = jnp.zeros((_SUB, BW), f32)
            for k in range(CONV_K):
                ysl = _shifted(ybuf, r0 + HALO - (CONV_K - 1) + k, _SUB)
                dwacc[pl.ds(k * 8, 8), :] += jnp.sum((dzs * ysl).reshape(_SUB // 8, 8, BW), axis=0)
                dyg = dyg + w_ref[k:k + 1, :] * _shifted(dzbuf, r0 + (CONV_K - 1) - k, _SUB)
            av = a_ref[rows, :].astype(f32)
            sg = jax.nn.sigmoid(g_ref[rows, :].astype(f32))
            dp_ref[rows, 0:BW] = (dyg * sg).astype(bf16)
            dp_ref[rows, BW:2 * BW] = (dyg * av * sg * (1.0 - sg)).astype(bf16)

        @pl.when(i == nT - 1)
        def _():
            for k in range(CONV_K):
                dw_ref[k:k + 1, :] = jnp.sum(dwacc[pl.ds(k * 8, 8), :], axis=0, keepdims=True)

    nxt = lambda i: jnp.minimum((i + 1) * hb, S // HALO - 1)
    return pl.pallas_call(
        body, out_shape=(SDS((S, NP), bf16), SDS((CONV_K, BW), f32)), grid=(nT,),
        in_specs=_conv_specs(S, T) + [BS((T, BW), lambda i: (i, 0)), BS((HALO, BW), lambda i: (nxt(i), 0)),
                                      BS((CONV_K, BW), lambda i: (0, 0)), ANY],
        out_specs=(BS((T, 2 * BW), lambda i: (i, creg)), BS((CONV_K, BW), lambda i: (0, 0))),
        scratch_shapes=[pltpu.VMEM((8, T + HALO, BW), f32), pltpu.VMEM((8, T + HALO, BW), f32),
                        pltpu.VMEM((CONV_K * 8, BW), f32)],
        input_output_aliases={7: 0}, compiler_params=_cp("arbitrary"), name=name)(proj, proj, proj, proj, dz, dz, w, dproj)


def _toeplitz_index():
    j = lax.broadcasted_iota(jnp.int32, (REL_TABLE, 1024), 1)
    t = lax.broadcasted_iota(jnp.int32, (REL_TABLE, 1024), 0)
    e = ((WIN - 1) - j) & 1023
    tidx = jnp.clip(e - (TQ - 1), -(CHUNK - 1), 256) + (CHUNK - 1)
    return (tidx == t).astype(f32)


def att_bias_build(table, name):
    H = table.shape[0]

    def body(t_ref, o_ref):
        u = jnp.dot(t_ref[...], _toeplitz_index(), precision=HI, preferred_element_type=f32)
        row = lax.broadcasted_iota(jnp.int32, (TQ, 1024), 0)
        r = lax.broadcasted_iota(jnp.int32, (TQ, WIN), 0)
        n = lax.broadcasted_iota(jnp.int32, (TQ, WIN), 1)
        dchunk = (r // CHUNK + 8) - n // CHUNK
        band = (dchunk >= 0) & (dchunk <= 8)
        for h in range(H):
            x = jnp.broadcast_to(u[h:h + 1, :], (TQ, 1024))
            for b in range(8):
                x = jnp.where(((row >> b) & 1) == 1, pltpu.roll(x, 1 << b, 1), x)
            o_ref[h] = jnp.where(band, x[:, :WIN], NEG_INF)

    return pl.pallas_call(body, out_shape=SDS((H, TQ, WIN), f32), compiler_params=pltpu.CompilerParams(vmem_limit_bytes=VMEM_LIMIT),
                          name=name)(table)


def att_bias_grad(dbias, name):
    H = dbias.shape[0]

    def body(d_ref, o_ref):
        row = lax.broadcasted_iota(jnp.int32, (TQ, 1024), 0)
        rows = []
        for h in range(H):
            x = jnp.concatenate([d_ref[h], jnp.zeros((TQ, 1024 - WIN), f32)], axis=1)
            for b in range(8):
                x = jnp.where(((row >> b) & 1) == 1, pltpu.roll(x, 1024 - (1 << b), 1), x)
            rows.append(jnp.sum(x, axis=0, keepdims=True))
        du = jnp.concatenate(rows, axis=0)
        o_ref[...] = lax.dot_general(du, _toeplitz_index(), (((1,), (1,)), ((), ())), precision=HI,
                                     preferred_element_type=f32)

    return pl.pallas_call(body, out_shape=SDS((H, REL_TABLE), f32), compiler_params=pltpu.CompilerParams(vmem_limit_bytes=VMEM_LIMIT),
                          name=name)(dbias)


def _att_specs():
    cq, ck, cv = (OUR_COLS[n][0] // BW for n in ("a_q", "a_k", "a_v"))
    specs = [BS((TQ, BW), lambda i: (i, cq))]
    for col in (ck, cv):
        for back in (2, 1, 0):
            specs.append(BS((TQ, BW), functools.partial(lambda i, back, col: (jnp.maximum(i - back, 0), col), back=back, col=col)))
    return specs


def _att_pen(i):
    n = lax.broadcasted_iota(jnp.int32, (1, WIN), 1)
    return jnp.where(n + (i - 2) * TQ >= 0, 0.0, NEG_INF).astype(f32)


def _att_probs(qa, kp, bias_h, pen):
    s = lax.dot_general(qa, kp, (((1,), (1,)), ((), ())), preferred_element_type=f32) + bias_h + pen
    e = jnp.exp(s - jnp.max(s, axis=-1, keepdims=True))
    return e * (1.0 / jnp.sum(e, axis=-1, keepdims=True))


def att_fwd(proj, bias, y, name, gather=None):
    S = proj.shape[0]
    ng = len(gather) if gather else 0

    def body(q_ref, k2, k1, k0, v2, v1, v0, b_ref, y_in, *rest):
        del y_in
        g_in, o_ref, g_out = rest[:ng], rest[ng], rest[ng + 1:2 * ng + 1]
        i = pl.program_id(0)
        if ng:
            kwin, vwin, send_sems, recv_sems = rest[2 * ng + 1:]
            _ag4_over_grid(g_in, g_out, send_sems, recv_sems, i, S // TQ)
        else:
            kwin, vwin = rest[1:]
        for w, (kr, vr) in enumerate(((k2, v2), (k1, v1), (k0, v0))):
            kwin[pl.ds(w * TQ, TQ), :] = kr[...]
            vwin[pl.ds(w * TQ, TQ), :] = vr[...]
        pen = _att_pen(i)
        lo = lax.broadcasted_iota(jnp.int32, (TQ, 128), 1) < 64
        for hp in range(4):
            cols = slice(hp * 128, (hp + 1) * 128)
            qp, kp, vp = q_ref[:, cols] * jnp.asarray(0.125, bf16), kwin[:, cols], vwin[:, cols]
            outs = []
            for a in range(2):
                qa = jnp.where(lo if a == 0 else ~lo, qp, jnp.zeros_like(qp))
                p = _att_probs(qa, kp, b_ref[2 * hp + a], pen)
                outs.append(jnp.dot(p.astype(bf16), vp, preferred_element_type=f32))
            o_ref[:, cols] = jnp.where(lo, outs[0], outs[1]).astype(bf16)

    outs = pl.pallas_call(
        body, out_shape=(SDS(y.shape, bf16),) + tuple(_ag4_out_shapes(gather or [])), grid=(S // TQ,),
        in_specs=_att_specs() + [BS((8, TQ, WIN), lambda i: (0, 0, 0), pipeline_mode=pl.Buffered(1)), ANY] + [ANY] * ng,
        out_specs=(BS((TQ, BW), lambda i: (i, 2)),) + tuple(ANY for _ in range(ng)),
        scratch_shapes=[pltpu.VMEM((WIN, BW), bf16), pltpu.VMEM((WIN, BW), bf16)] + (_ag4_sems(ng) if ng else []),
        input_output_aliases={8: 0}, compiler_params=_cp("arbitrary" if ng else "parallel"), name=name)(
            proj, proj, proj, proj, proj, proj, proj, bias, y, *(gather or []))
    return (outs[0], list(outs[1:])) if ng else outs[0]


def att_bwd(proj, y, dy, bias, dproj, name):
    S = proj.shape[0]
    cq = OUR_COLS["a_q"][0] // BW

    def body(q_ref, k2, k1, k0, v2, v1, v0, b_ref, o_ref, do_ref, dp_in, dq_ref, dkp_ref, dvp_ref, db_ref, kwin, vwin):
        del dp_in
        i = pl.program_id(0)

        @pl.when(i == 0)
        def _():
            db_ref[...] = jnp.zeros_like(db_ref)

        for w, (kr, vr) in enumerate(((k2, v2), (k1, v1), (k0, v0))):
            kwin[pl.ds(w * TQ, TQ), :] = kr[...]
            vwin[pl.ds(w * TQ, TQ), :] = vr[...]
        pen = _att_pen(i)
        lo = lax.broadcasted_iota(jnp.int32, (TQ, 128), 1) < 64
        for hp in range(4):
            cols = slice(hp * 128, (hp + 1) * 128)
            qp, kp, vp = q_ref[:, cols] * jnp.asarray(0.125, bf16), kwin[:, cols], vwin[:, cols]
            dop, op = do_ref[:, cols], o_ref[:, cols]
            dqs = []
            dk = jnp.zeros((WIN, 128), f32)
            dv = jnp.zeros((WIN, 128), f32)
            for a in range(2):
                sel = lo if a == 0 else ~lo
                qa = jnp.where(sel, qp, jnp.zeros_like(qp))
                doa = jnp.where(sel, dop, jnp.zeros_like(dop))
                p = _att_probs(qa, kp, b_ref[2 * hp + a], pen)
                dpv = lax.dot_general(doa, vp, (((1,), (1,)), ((), ())), preferred_element_type=f32)
                delta = jnp.sum(doa.astype(f32) * op.astype(f32), axis=-1, keepdims=True)
                ds = p * (dpv - delta)
                db_ref[2 * hp + a] += ds
                dsb = ds.astype(bf16)
                dqs.append(jnp.dot(dsb, kp, preferred_element_type=f32))
                dk = dk + lax.dot_general(dsb, qa, (((0,), (0,)), ((), ())), preferred_element_type=f32)
                dv = dv + lax.dot_general(p.astype(bf16), doa, (((0,), (0,)), ((), ())), preferred_element_type=f32)
            dq_ref[:, cols] = (jnp.where(lo, dqs[0], dqs[1]) * 0.125).astype(bf16)
            for w in range(3):
                dkp_ref[w, :, cols] = dk[w * TQ:(w + 1) * TQ].astype(bf16)
                dvp_ref[w, :, cols] = dv[w * TQ:(w + 1) * TQ].astype(bf16)

    tok = BS((TQ, BW), lambda i: (i, 2))
    part = BS((3, TQ, BW), lambda i: (0, i, 0))
    full = BS((8, TQ, WIN), lambda i: (0, 0, 0))
    return pl.pallas_call(
        body, out_shape=(SDS((S, NP), bf16), SDS((3, S, BW), bf16), SDS((3, S, BW), bf16), SDS((8, TQ, WIN), f32)),
        grid=(S // TQ,),
        in_specs=_att_specs() + [BS((8, TQ, WIN), lambda i: (0, 0, 0), pipeline_mode=pl.Buffered(1)), tok, tok, ANY],
        out_specs=(BS((TQ, BW), lambda i: (i, cq)), part, part, full),
        scratch_shapes=[pltpu.VMEM((WIN, BW), bf16), pltpu.VMEM((WIN, BW), bf16)],
        input_output_aliases={10: 0}, compiler_params=_cp("arbitrary"), name=name)(
            proj, proj, proj, proj, proj, proj, proj, bias, y, dy, dproj)


def att_shift_add(dkp, dvp, dproj, name):
    S = dkp.shape[1]
    nT = S // TQ
    creg = OUR_COLS["a_k"][0] // (2 * BW)

    def body(k2, k1, k0, v2, v1, v0, dp_in, dp_ref):
        del dp_in
        j = pl.program_id(0)
        m1 = (j + 1 < nT).astype(f32)
        m0 = (j + 2 < nT).astype(f32)
        dp_ref[:, 0:BW] = (k2[0].astype(f32) + m1 * k1[0].astype(f32) + m0 * k0[0].astype(f32)).astype(bf16)
        dp_ref[:, BW:2 * BW] = (v2[0].astype(f32) + m1 * v1[0].astype(f32) + m0 * v0[0].astype(f32)).astype(bf16)

    def spec(w):
        return BS((1, TQ, BW), functools.partial(lambda j, w: (w, jnp.minimum(j + 2 - w, nT - 1), 0), w=w))

    return pl.pallas_call(
        body, out_shape=SDS(dproj.shape, bf16), grid=(nT,),
        in_specs=[spec(2), spec(1), spec(0), spec(2), spec(1), spec(0), ANY],
        out_specs=BS((TQ, 2 * BW), lambda j: (j, creg)),
        input_output_aliases={6: 0}, compiler_params=_cp("parallel"), name=name)(dkp, dkp, dkp, dvp, dvp, dvp, dproj)


GQ, GV = 256, 512
TGC = 8


def _bd_mask():
    r = lax.broadcasted_iota(jnp.int32, (GQ, GV), 0) // 64
    c = lax.broadcasted_iota(jnp.int32, (GQ, GV), 1) // 128
    return (r == c).astype(f32)


def _tri(strict):
    r = lax.broadcasted_iota(jnp.int32, (CHUNK, CHUNK), 0)
    c = lax.broadcasted_iota(jnp.int32, (CHUNK, CHUNK), 1)
    return ((c < r) if strict else (c <= r)).astype(f32)


def _compact(s_bd):
    return jnp.concatenate([s_bd[h * 64:(h + 1) * 64, h * 128:(h + 1) * 128] for h in range(4)], axis=0)


def _expand(comp, mask):
    return jnp.tile(comp, (1, 4)) * mask


def _gla_gates(alr, wa_ref, ba_ref, tri_incl, ones_col):
    a = jnp.dot(alr, wa_ref[...], preferred_element_type=f32) + ba_ref[...]
    la = (jnp.minimum(a, 0.0) - jnp.log(1.0 + jnp.exp(-jnp.abs(a)))) * (1.0 / 16.0)
    cum = jnp.dot(tri_incl, la, precision=HI, preferred_element_type=f32)
    tot_row = cum[CHUNK - 1:CHUNK, :]
    tot_col = lax.dot_general(la, ones_col, (((0,), (0,)), ((), ())), precision=HI, preferred_element_type=f32)
    return a, cum, tot_row, jnp.tile(jnp.exp(tot_col), (1, 4))


def _head_norm(o):
    rns, ons = [], []
    for h in range(4):
        oh = o[:, h * 128:(h + 1) * 128]
        rn = lax.rsqrt(jnp.mean(oh * oh, axis=-1, keepdims=True) + EPS)
        rns.append(rn)
        ons.append(oh * rn)
    return rns, ons


def _gla_in_specs(T, imap):
    cq, ck = OUR_COLS["g_q"][0] // GQ, OUR_COLS["g_k"][0] // GQ
    cv, cr = OUR_COLS["g_v"][0] // GV, OUR_COLS["g_r"][0] // GV
    return [BS((T, GQ), lambda i: (imap(i), cq)), BS((T, GQ), lambda i: (imap(i), ck)), BS((T, GV), lambda i: (imap(i), cv)),
            BS((T, GV), lambda i: (imap(i), cr)), BS((T, RANKP), lambda i: (imap(i), 0))]


def gla_fwd(proj, pa, wa, ba, ng, y, name):
    S = proj.shape[0]
    T = min(TGC * CHUNK, S)
    nch = T // CHUNK

    def body(q_ref, k_ref, v_ref, r_ref, a_ref, wa_ref, ba_ref, ng_ref, y_in, y_ref, st_ref, s_scr):
        del y_in

        @pl.when(pl.program_id(0) == 0)
        def _():
            s_scr[...] = jnp.zeros_like(s_scr)

        mask = _bd_mask()
        tri = _tri(False)
        ones_col = jnp.ones((CHUNK, 128), f32)

        s_bd = s_scr[...]
        for ci in range(nch):
            rows = pl.ds(ci * CHUNK, CHUNK)
            _, cum, tot_row, dec4 = _gla_gates(a_ref[rows, :], wa_ref, ba_ref, tri, ones_col)
            kd = (k_ref[rows, :].astype(f32) * jnp.exp(tot_row - cum)).astype(bf16)
            upd = lax.dot_general(kd, v_ref[rows, :], (((0,), (0,)), ((), ())), preferred_element_type=f32) * mask
            s_bd = dec4 * s_bd + upd
            st_ref[pl.ds(ci * GQ, GQ), :] = _compact(s_bd)
            qs = (q_ref[rows, :].astype(f32) * 0.125).astype(bf16)
            o = jnp.dot(qs, s_bd.astype(bf16), preferred_element_type=f32)
            _, ons = _head_norm(o)
            rv = r_ref[rows, :].astype(f32)
            y_ref[rows, :] = (jnp.concatenate(ons, axis=1) * ng_ref[...] * (rv * jax.nn.sigmoid(rv))).astype(bf16)
        s_scr[...] = s_bd

    return pl.pallas_call(
        body, out_shape=(SDS(y.shape, bf16), SDS((S // CHUNK * GQ, 128), f32)), grid=(S // T,),
        in_specs=_gla_in_specs(T, lambda i: i) + [BS((RANKP, GQ), lambda i: (0, 0)), BS((1, GQ), lambda i: (0, 0)),
                                                  BS((1, GV), lambda i: (0, 0)), ANY],
        out_specs=(BS((T, GV), lambda i: (i, 1)), BS((nch * GQ, 128), lambda i: (i, 0))),
        scratch_shapes=[pltpu.VMEM((GQ, GV), f32)], input_output_aliases={8: 0}, compiler_params=_cp("arbitrary"),
        name=name)(proj, proj, proj, proj, pa, wa, ba, ng, y)


def gla_bwd(proj, pa, states, dy, wa, ba, ng, dproj, name):
    S = proj.shape[0]
    T = min(TGC * CHUNK, S)
    nch = T // CHUNK
    nT = S // T
    rev = lambda i: nT - 1 - i

    def body(q_ref, k_ref, v_ref, r_ref, a_ref, st_ref, sp_ref, dy_ref, wa_ref, ba_ref, ng_ref, dp_in,
             dp_ref, da_ref, dwa_ref, dba_ref, dng_ref, g_scr):
        del dp_in
        i = pl.program_id(0)

        @pl.when(i == 0)
        def _():
            g_scr[...] = jnp.zeros_like(g_scr)
            dwa_ref[...] = jnp.zeros_like(dwa_ref)
            dba_ref[...] = jnp.zeros_like(dba_ref)
            dng_ref[...] = jnp.zeros_like(dng_ref)

        mask = _bd_mask()
        tri = _tri(False)
        tri_strict = _tri(True)
        ones_col = jnp.ones((CHUNK, 128), f32)
        ones_row = jnp.ones((8, 128), f32)
        first_tile = (i == nT - 1).astype(f32)

        g_carry = g_scr[...]
        for ci in reversed(range(nch)):
            rows = pl.ds(ci * CHUNK, CHUNK)
            alr = a_ref[rows, :]
            a, cum, tot_row, dec4 = _gla_gates(alr, wa_ref, ba_ref, tri, ones_col)
            wdec = jnp.exp(tot_row - cum)
            kdf = k_ref[rows, :].astype(f32) * wdec
            kd = kdf.astype(bf16)
            s_c = _expand(st_ref[pl.ds(ci * GQ, GQ), :], mask)
            prev = st_ref[pl.ds((ci - 1) * GQ, GQ), :] if ci > 0 else sp_ref[...] * (1.0 - first_tile)
            qs = (q_ref[rows, :].astype(f32) * 0.125).astype(bf16)
            s_cb = s_c.astype(bf16)
            o = jnp.dot(qs, s_cb, preferred_element_type=f32)
            rns, ons = _head_norm(o)
            on = jnp.concatenate(ons, axis=1)
            rv = r_ref[rows, :].astype(f32)
            sg = jax.nn.sigmoid(rv)
            sr = rv * sg
            dyv = dy_ref[rows, :].astype(f32)
            ngv = ng_ref[...]
            dng_ref[...] += jnp.sum(dyv * on * sr, axis=0, keepdims=True)
            d_on = dyv * ngv * sr
            dr = dyv * on * ngv * (sg * (1.0 + rv * (1.0 - sg)))
            dos = []
            for h in range(4):
                cols = slice(h * 128, (h + 1) * 128)
                dh_ = d_on[:, cols]
                dos.append(rns[h] * (dh_ - ons[h] * jnp.mean(dh_ * ons[h], axis=-1, keepdims=True)))
            do = jnp.concatenate(dos, axis=1).astype(bf16)
            dq = lax.dot_general(do, s_cb, (((1,), (1,)), ((), ())), preferred_element_type=f32) * 0.125
            ds = lax.dot_general(qs, do, (((0,), (0,)), ((), ())), preferred_element_type=f32) * mask + g_carry
            ddec_row = lax.dot_general(ones_row, _compact(ds) * prev, (((1,), (1,)), ((), ())), precision=HI,
                                       preferred_element_type=f32)[0:1, :]
            dsb = ds.astype(bf16)
            dkd = lax.dot_general(v_ref[rows, :], dsb, (((1,), (1,)), ((), ())), preferred_element_type=f32)
            dv = jnp.dot(kd, dsb, preferred_element_type=f32)
            g_carry = dec4 * ds
            dk = dkd * wdec
            dwlog = dkd * kdf
            dla = ddec_row * jnp.exp(tot_row) + jnp.dot(tri_strict, dwlog, precision=HI, preferred_element_type=f32)
            da = dla * (1.0 - jax.nn.sigmoid(a)) * (1.0 / 16.0)
            dab = da.astype(bf16)
            da_ref[rows, :] = lax.dot_general(dab, wa_ref[...], (((1,), (1,)), ((), ())),
                                              preferred_element_type=f32).astype(bf16)
            dwa_ref[...] += lax.dot_general(alr, dab, (((0,), (0,)), ((), ())), preferred_element_type=f32)
            dba_ref[...] += jnp.sum(da, axis=0, keepdims=True)
            dp_ref[rows, 0:GQ] = dq.astype(bf16)
            dp_ref[rows, GQ:2 * GQ] = dk.astype(bf16)
            dp_ref[rows, 2 * GQ:2 * GQ + GV] = dv.astype(bf16)
            dp_ref[rows, 2 * GQ + GV:2 * GQ + 2 * GV] = dr.astype(bf16)
        g_scr[...] = g_carry

    REG = 2 * GQ + 2 * GV
    return pl.pallas_call(
        body,
        out_shape=(SDS((S, NP), bf16), SDS((S, RANKP), bf16), SDS((RANKP, GQ), f32), SDS((1, GQ), f32), SDS((1, GV), f32)),
        grid=(nT,),
        in_specs=_gla_in_specs(T, rev) + [
            BS((nch * GQ, 128), lambda i: (rev(i), 0)),
            BS((GQ, 128), lambda i: (jnp.maximum(rev(i) * nch - 1, 0), 0)),
            BS((T, GV), lambda i: (rev(i), 1)),
            BS((RANKP, GQ), lambda i: (0, 0)), BS((1, GQ), lambda i: (0, 0)), BS((1, GV), lambda i: (0, 0)), ANY],
        out_specs=(BS((T, REG), lambda i: (rev(i), 0)), BS((T, RANKP), lambda i: (rev(i), 0)),
                   BS((RANKP, GQ), lambda i: (0, 0)), BS((1, GQ), lambda i: (0, 0)), BS((1, GV), lambda i: (0, 0))),
        scratch_shapes=[pltpu.VMEM((GQ, GV), f32)],
        input_output_aliases={11: 0}, compiler_params=_cp("arbitrary"), name=name)(
            proj, proj, proj, proj, pa, states, states, dy, wa, ba, ng, dproj)


def _as2d(a):
    if a.ndim == 1:
        return a.reshape(1, a.shape[0])
    return a.reshape(-1, a.shape[-1])


def adamw(w, g, m, v, name):
    shape = w.shape
    w2, g2, m2, v2 = (_as2d(a) for a in (w, g, m, v))
    R, C = w2.shape
    tr = R
    for cand in (512, 256, 128, 64, 32, 16, 8):
        if R % cand == 0 and cand * C * 4 * 7 * 2 <= 40 * 1024 * 1024:
            tr = cand
            break

    def body(w_ref, g_ref, m_ref, v_ref, d_ref, mo_ref, vo_ref):
        gv = g_ref[...]
        mn = ADAM_B1 * m_ref[...] + (1.0 - ADAM_B1) * gv
        vn = ADAM_B2 * v_ref[...] + (1.0 - ADAM_B2) * (gv * gv)
        m_hat = mn / (1.0 - ADAM_B1 ** ADAM_STEP)
        v_hat = vn / (1.0 - ADAM_B2 ** ADAM_STEP)
        d_ref[...] = -ADAM_LR * (m_hat / (jnp.sqrt(v_hat) + ADAM_EPS) + ADAM_WD * w_ref[...])
        mo_ref[...] = mn
        vo_ref[...] = vn

    blk = BS((tr, C), lambda i: (i, 0))
    outs = pl.pallas_call(body, out_shape=tuple(SDS((R, C), f32) for _ in range(3)), grid=(R // tr,),
                          in_specs=[blk] * 4, out_specs=(blk,) * 3, compiler_params=_cp("parallel"), name=name)(w2, g2, m2, v2)
    return tuple(o.reshape(shape) for o in outs)


def _row_tile(rows, row_bytes, budget=4 * 1024 * 1024):
    best = None
    for t in range(16, rows + 1, 16):
        if rows % t == 0 and t * row_bytes <= budget:
            best = t
    return best or rows


def add_halves(gp, ra, c, name):
    shape = ra.shape
    cols = shape[-1]
    rows = int(np.prod(shape[:-1]))
    g3, r2 = gp.reshape(2, rows, cols), ra.reshape(rows, cols)
    tr = _row_tile(rows, cols * 2)

    grid_spec = pltpu.PrefetchScalarGridSpec(
        num_scalar_prefetch=1, grid=(rows // tr,),
        in_specs=[BS((1, tr, cols), lambda i, c_ref: (c_ref[0], i, 0)), BS((tr, cols), lambda i, c_ref: (i, 0))],
        out_specs=BS((tr, cols), lambda i, c_ref: (i, 0)))

    def body(c_ref, a_ref, b_ref, o_ref):
        del c_ref
        o_ref[...] = (a_ref[0].astype(f32) + b_ref[...].astype(f32)).astype(bf16)

    out = pl.pallas_call(body, out_shape=SDS((rows, cols), bf16), grid_spec=grid_spec, compiler_params=_cp("parallel"),
                         name=name)(jnp.reshape(c, (1,)).astype(jnp.int32), g3, r2)
    return out.reshape(shape)


def reduce_chips(rb, own, c, chip, name):
    shape = rb.shape[1:]
    cols = shape[-1]
    rows = int(np.prod(shape[:-1]))
    tr = _row_tile(rows, cols * 2 * 4)
    rb3, own3 = rb.reshape(4, rows, cols), own.reshape(4, rows, cols)

    def body(s_ref, own_ref, r1, r2, r3, o_ref):
        del s_ref
        o_ref[0] = ((own_ref[0].astype(f32) + r1[0].astype(f32)) + r2[0].astype(f32)) + r3[0].astype(f32)

    def slot(k):
        return BS((1, tr, cols), functools.partial(lambda i, s, k: ((s[1] + k) % 4, i, 0), k=k))

    grid_spec = pltpu.PrefetchScalarGridSpec(
        num_scalar_prefetch=1, grid=(rows // tr,), in_specs=[slot(0), slot(1), slot(2), slot(3)],
        out_specs=BS((1, tr, cols), lambda i, s: (s[0], i, 0)))
    out = pl.pallas_call(body, out_shape=SDS((2, rows, cols), f32), grid_spec=grid_spec, compiler_params=_cp("parallel"),
                         name=name)(jnp.stack([c, chip]).astype(jnp.int32), own3, rb3, rb3, rb3)
    return out.reshape((2,) + shape)


def sum_slots(x, name):
    N, shape = x.shape[0], x.shape[1:]
    cols = shape[-1]
    rows = int(np.prod(shape[:-1]))
    tr = _row_tile(rows, cols * x.dtype.itemsize * N)

    def body(x_ref, o_ref):
        acc = x_ref[0].astype(f32)
        for n in range(1, N):
            acc = acc + x_ref[n].astype(f32)
        o_ref[...] = acc

    out = pl.pallas_call(body, out_shape=SDS((rows, cols), f32), grid=(rows // tr,),
                         in_specs=[BS((N, tr, cols), lambda i: (0, i, 0))], out_specs=BS((tr, cols), lambda i: (i, 0)),
                         compiler_params=_cp("parallel"), name=name)(x.reshape(N, rows, cols))
    return out.reshape(shape)


def _me():
    return lax.axis_index("x"), lax.axis_index("y"), lax.axis_index("c")


def _rcopy(src, dst, send_sems, recv_sems, k, dev):
    return pltpu.make_async_remote_copy(src_ref=src, dst_ref=dst, send_sem=send_sems.at[k], recv_sem=recv_sems.at[k],
                                        device_id=dev, device_id_type=MESH)


def _comm_call(body, ins, out_shapes, n_remote, name, aliases=None):
    return pl.pallas_call(
        body, out_shape=tuple(out_shapes), in_specs=[ANY] * len(ins), out_specs=tuple(ANY for _ in out_shapes),
        scratch_shapes=[pltpu.SemaphoreType.DMA((n_remote,)), pltpu.SemaphoreType.DMA((n_remote,))],
        input_output_aliases=aliases or {}, name=name)(*ins)


def ag4(bufs, name):
    n = len(bufs)

    def body(*refs):
        start, forward, finish = _ag4_phases(refs[:n], refs[n:2 * n], *refs[2 * n:])
        start()
        forward()
        finish()

    return _comm_call(body, bufs, _ag4_out_shapes(bufs), 8 * n, name)


def _ag4_out_shapes(bufs):
    return [SDS((2, 4) + b.shape[1:], b.dtype) for b in bufs]


def _ag4_sems(n):
    return [pltpu.SemaphoreType.DMA((8 * n,)), pltpu.SemaphoreType.DMA((8 * n,))]


def _ag4_phases(xs, os, send_sems, recv_sems):
    n = len(xs)

    def place():
        x, y, c = _me()
        return x, y, c, 2 * x + y, (x, y, 1 - c), [(1 - x, y), (x, 1 - y), (1 - x, 1 - y)]

    def sends():
        x, y, c, j, sib, chips = place()
        first = [_rcopy(xs[t].at[c], os[t].at[c, j], send_sems, recv_sems, 8 * t + k, (cx, cy, c))
                 for t in range(n) for k, (cx, cy) in enumerate(chips)]
        own = [_rcopy(xs[t].at[l], os[t].at[l, j], send_sems, recv_sems, 8 * t + 6 + l, sib) for t in range(n) for l in range(2)]
        return first + own

    def forwards():
        x, y, c, j, sib, chips = place()
        return [(_rcopy(os[t].at[c, 2 * cx + cy], os[t].at[c, 2 * cx + cy], send_sems, recv_sems, 8 * t + k, (x, y, c)),
                 _rcopy(os[t].at[c, 2 * cx + cy], os[t].at[c, 2 * cx + cy], send_sems, recv_sems, 8 * t + 3 + k, sib))
                for k, (cx, cy) in enumerate(chips) for t in range(n)]

    def start():
        for cp in sends():
            cp.start()

    def forward():
        for landed, fwd in forwards():
            landed.wait_recv()
            fwd.start()

    def finish():
        x, y, c, j, sib, chips = place()
        for t in range(n):
            for l in range(2):
                land = os[t].at[l, j]
                _rcopy(land, land, send_sems, recv_sems, 8 * t + 6 + l, (x, y, c)).wait_recv()
        for k, (cx, cy) in enumerate(chips):
            for t in range(n):
                land = os[t].at[1 - c, 2 * cx + cy]
                _rcopy(land, land, send_sems, recv_sems, 8 * t + 3 + k, (x, y, c)).wait_recv()
        for cp in sends() + [fwd for _, fwd in forwards()]:
            cp.wait_send()

    return start, forward, finish


def _ag4_over_grid(xs, os, send_sems, recv_sems, step, nsteps):
    start, forward, finish = _ag4_phases(xs, os, send_sems, recv_sems)
    pl.when(step == 0)(start)
    pl.when(step == nsteps // 2)(forward)
    pl.when(step == nsteps - 1)(finish)


def sib_other_layer(gps, name):
    n = len(gps)

    def body(*refs):
        xs, os = refs[:n], refs[n:2 * n]
        send_sems, recv_sems = refs[2 * n:]
        x, y, c = _me()
        cps = [_rcopy(xs[t].at[1 - c], os[t], send_sems, recv_sems, t, (x, y, 1 - c)) for t in range(n)]
        for cp in cps:
            cp.start()
        for cp in cps:
            cp.wait()

    return _comm_call(body, gps, [SDS(g.shape[1:], g.dtype) for g in gps], n, name)


def a2a4(ps, name):
    n = len(ps)

    def body(*refs):
        xs, os = refs[:n], refs[n:2 * n]
        send_sems, recv_sems = refs[2 * n:]
        x, y, c = _me()
        j = 2 * x + y
        chips = [(1 - x, y), (x, 1 - y), (1 - x, 1 - y)]
        sends = [_rcopy(xs[t].at[2 * cx + cy], os[t].at[j], send_sems, recv_sems, 3 * t + k, (cx, cy, c))
                 for t in range(n) for k, (cx, cy) in enumerate(chips)]
        for cp in sends:
            cp.start()
        for t in range(n):
            for k, (cx, cy) in enumerate(chips):
                land = os[t].at[2 * cx + cy]
                _rcopy(land, land, send_sems, recv_sems, 3 * t + k, (x, y, c)).wait_recv()
        for cp in sends:
            cp.wait_send()

    return _comm_call(body, ps, [SDS(p.shape, p.dtype) for p in ps], 3 * n, name)


def ag2(bufs, name):
    n = len(bufs)

    def body(*refs):
        xs, os = refs[:n], refs[n:2 * n]
        send_sems, recv_sems = refs[2 * n:]
        x, y, c = _me()
        cps = [_rcopy(xs[t].at[c], os[t].at[c], send_sems, recv_sems, t, (x, y, 1 - c)) for t in range(n)]
        for cp in cps:
            cp.start()
        for t in range(n):
            land = os[t].at[1 - c]
            _rcopy(land, land, send_sems, recv_sems, t, (x, y, c)).wait_recv()
        for cp in cps:
            cp.wait_send()

    return _comm_call(body, bufs, [SDS(b.shape, b.dtype) for b in bufs], n, name, aliases={t: t for t in range(n)})


def ag8(blk, name):
    m_per, n = blk.shape

    def body(x_ref, out_ref, send_sems, recv_sems, local_sem):
        x, y, c = _me()
        me, sibling = (x, y, c), (x, y, 1 - c)
        chips = [(1 - x, y), (x, 1 - y), (1 - x, 1 - y)]

        def rows(px, py, pc):
            return out_ref.at[pl.ds((4 * px + 2 * py + pc) * m_per, m_per), :]

        def copy(k, block, to, src=None):
            return pltpu.make_async_remote_copy(
                src_ref=rows(*block) if src is None else src, dst_ref=rows(*block), send_sem=send_sems.at[k],
                recv_sem=recv_sems.at[k], device_id=to, device_id_type=MESH)

        mine = pltpu.make_async_copy(x_ref, rows(*me), local_sem)
        mine.start()
        first = [copy(0, me, sibling, src=x_ref)]
        first += [copy(1 + j, me, (*chip, c), src=x_ref) for j, chip in enumerate(chips)]
        for cp in first:
            cp.start()
        passed = [copy(4 + j, (*chip, c), sibling) for j, chip in enumerate(chips)]
        for j, chip in enumerate(chips):
            copy(1 + j, (*chip, c), me).wait_recv()
            passed[j].start()
        copy(0, sibling, me).wait_recv()
        for j, chip in enumerate(chips):
            copy(4 + j, (*chip, 1 - c), me).wait_recv()
        for cp in first + passed:
            cp.wait_send()
        mine.wait()

    return pl.pallas_call(
        body, out_shape=SDS((8 * m_per, n), blk.dtype), in_specs=[pl.BlockSpec(memory_space=pltpu.VMEM)],
        out_specs=pl.BlockSpec(memory_space=pltpu.VMEM),
        scratch_shapes=[pltpu.SemaphoreType.DMA((7,)), pltpu.SemaphoreType.DMA((7,)), pltpu.SemaphoreType.DMA],
        name=name)(blk)


def _split_chips(full, axis):
    n = full.shape[axis] // 4
    parts = full.reshape(full.shape[:axis] + (4, n) + full.shape[axis + 1:])
    return jnp.moveaxis(parts, axis, 0)


def _merge_chips(gathered, axis):
    parts = jnp.moveaxis(gathered, 0, axis)
    return parts.reshape(parts.shape[:axis] + (parts.shape[axis] * parts.shape[axis + 1],) + parts.shape[axis + 2:])


def _to_ref_cols(main, rank):
    pieces = []
    for n, width in REF_SPLITS:
        if n == "g_a":
            pieces.append(rank[..., :RANK])
        else:
            off = OUR_COLS[n][0]
            pieces.append(main[..., off:off + width])
    return jnp.concatenate(pieces, axis=-1)


def _from_ref_cols(w):
    offs, o = {}, 0
    for n, width in REF_SPLITS:
        offs[n] = (o, width)
        o += width
    main = jnp.concatenate([w[..., offs[n][0]:offs[n][0] + offs[n][1]] for n in sorted(OUR_COLS, key=lambda k: OUR_COLS[k][0])],
                           axis=-1)
    ro = offs["g_a"][0]
    rank = jnp.pad(w[..., ro:ro + RANK], [(0, 0)] * (w.ndim - 1) + [(0, RANKP - RANK)])
    return main, rank


def _layer_fwd(h, p_i, W, li, late=None):
    t = f"l{li}_"
    sv = {"h0": h}

    def arrived(names, gathered, Ws):
        for l, Wl in enumerate(Ws):
            Wl.update(_prep_layer_weights(dict(zip(names, gathered)), None, l))

    xn = rms_fwd(h, W["norm1_g"], t + "rms1")
    proj = mm_nn(xn, W["w_in_main"], name=t + "inproj")
    pa = mm_nn(xn, W["w_in_rank"], name=t + "inproj_rank")
    y = sg_fwd(proj, W["sg_ln_g"], W["sg_ln_b"], W["sg_wm"], W["sg_bsb"], t + "sg_fwd")
    y, states = gla_fwd(proj, pa, W["gla_wa"], W["gla_b_a"], W["gla_norm_g"], y, t + "gla_fwd")
    if late is None:
        y = att_fwd(proj, W["att_bias"], y, t + "att_fwd")
    else:
        (names, shards, Ws) = late[0]
        y, gathered = att_fwd(proj, W["att_bias"], y, t + "att_fwd", gather=shards)
        arrived(names, gathered, Ws)
    y = conv_fwd(proj, W["conv_dw_w"], W["conv_dw_b"], W["conv_ln_g"], W["conv_ln_b"], y, t + "conv_fwd")
    if late is None:
        gate = mm_nn(xn, W["w_gate_all"], bias=W["b_gate_all"], act="sigmoid", name=t + "gate")
    else:
        (names, shards, Ws) = late[1]
        gate, gathered = mm_nn(xn, W["w_gate_all"], bias=W["b_gate_all"], act="sigmoid", name=t + "gate", gather=shards)
        arrived(names, gathered, Ws)
    z = mm_nn(y, W["w_branch"], name=t + "branch")
    m = gate_merge_fwd(gate, z, t + "merge")
    h1 = mm_nn(m, W["w_out"], res=h, out_dtype=f32, name=t + "outproj")
    hn = rms_fwd(h1, W["norm2_g"], t + "rms2")
    a = mm_nn(hn, W["w_ff1"], name=t + "ff1")
    h2 = mm_nn(a, W["w_ff2"], pre="relu2", res=h1, out_dtype=f32, name=t + "ff2")
    hg = rms_fwd(h2, W["norm3_g"], t + "rms3")
    pg = mm_nn(hg, W["w_ple_gate"], bias=W["b_ple_gate"], act="sigmoid", name=t + "ple_gate")
    h3, e = mm_nn(p_i, W["w_ple"], mul=pg, res=h2, out_dtype=f32, raw_out=True, name=t + "ple_out")
    sv.update(xn=xn, proj=proj, pa=pa, states=states, y=y, gate=gate, z=z, m=m, h1=h1, hn=hn, a=a, h2=h2, hg=hg, pg=pg, e=e)
    return h3, sv


def _layer_bwd(dh3, sv, p_i, W, li):
    t = f"l{li}_b_"
    G = {}
    dpg, de, G["b_ple_gate"] = ple_bwd_ew(dh3, sv["e"], sv["pg"], t + "ple_ew")
    G["w_ple_gate"] = mm_tn(sv["hg"], dpg, name=t + "dw_ple_gate")[0]
    G["w_ple"] = mm_tn(p_i, de, name=t + "dw_ple")[0]
    dhg = mm_nt(dpg, W["w_ple_gate"], name=t + "dhg")
    dh2, G["norm3_g"] = rms_bwd(dhg, sv["h2"], W["norm3_g"], dh3, t + "rms3")
    da = mm_nt(dh2, W["w_ff2"], post_a=sv["a"], out_dtype=bf16, name=t + "da")
    G["w_ff2"] = mm_tn(sv["a"], dh2, pre="relu2", name=t + "dw_ff2")[0]
    G["w_ff1"] = mm_tn(sv["hn"], da, name=t + "dw_ff1")[0]
    dhn = mm_nt(da, W["w_ff1"], name=t + "dhn")
    dh1, G["norm2_g"] = rms_bwd(dhn, sv["h1"], W["norm2_g"], dh2, t + "rms2")
    dm = mm_nt(dh1, W["w_out"], out_dtype=bf16, name=t + "dm")
    G["w_out"] = mm_tn(sv["m"], dh1, name=t + "dw_out")[0]
    dz, dgp, G["b_gate_all"] = gate_merge_bwd(dm, sv["gate"], sv["z"], t + "merge")
    G["w_branch"] = mm_tn(sv["y"], dz, G=4, name=t + "dw_branch")
    dy = mm_nt(dz, W["w_branch"], out_dtype=bf16, name=t + "dy")
    G["w_gate_all"] = mm_tn(sv["xn"], dgp, name=t + "dw_gate")[0]
    dxn = mm_nt(dgp, W["w_gate_all"], name=t + "dxn_gate")
    proj = sv["proj"]
    dproj, dwm, dbs, G["sg_ln_g"], G["sg_ln_b"] = sg_bwd(proj, dy, W["sg_ln_g"], W["sg_ln_b"], W["sg_wm"], W["sg_bsb"],
                                                          W["sg_maskf"], t + "sg")
    G["sg_w"], G["sg_b"] = dwm, dbs[:, :, 0]
    dproj, dpa, dwa, G["gla_b_a"], G["gla_norm_g"] = gla_bwd(proj, sv["pa"], sv["states"], dy, W["gla_wa"], W["gla_b_a"],
                                                             W["gla_norm_g"], dproj, t + "gla")
    G["gla_w_a2"] = dwa[:RANK]
    dproj, dkp, dvp, dbias = att_bwd(proj, sv["y"], dy, W["att_bias"], dproj, t + "att")
    dproj = att_shift_add(dkp, dvp, dproj, t + "att_kv")
    G["att_rel_bias"] = att_bias_grad(dbias, t + "att_bias")
    dz_c, G["conv_ln_g"], G["conv_ln_b"], G["conv_dw_b"] = conv_bwd_norm(proj, dy, W["conv_dw_w"], W["conv_dw_b"],
                                                                        W["conv_ln_g"], W["conv_ln_b"], t + "conv_norm")
    dproj, G["conv_dw_w"] = conv_bwd_taps(proj, dz_c, W["conv_dw_w"], dproj, t + "conv_taps")
    G["w_in_main"] = mm_tn(sv["xn"], dproj, name=t + "dw_in")[0]
    G["w_in_rank"] = mm_tn(sv["xn"], dpa, name=t + "dw_in_rank")[0]
    dxn = mm_nt(dpa, W["w_in_rank"], res=dxn, name=t + "dxn_rank")
    dxn = mm_nt(dproj, W["w_in_main"], res=dxn, name=t + "dxn_main")
    dh0, G["norm1_g"] = rms_bwd(dxn, sv["h0"], W["norm1_g"], dh1, t + "rms1")
    return dh0, G


def _prep_layer_weights(gathered, repl, li):
    W = {}
    full = {n: _merge_chips(g[li], SHARDED[n][1]) for n, g in gathered.items()}
    if "w_in" in full:
        main, rank = _from_ref_cols(full["w_in"])
        W["w_in_main"], W["w_in_rank"] = main[None], rank[None]
    if "w_branch" in full:
        W["w_branch"] = full["w_branch"]
    if "w_gate" in full:
        W["w_gate_all"] = jnp.transpose(full["w_gate"], (1, 0, 2)).reshape(1, D, 4 * D)
    if "b_gate" in full:
        W["b_gate_all"] = full["b_gate"].reshape(1, 4 * D)
    for n in ("w_out", "w_ff1", "w_ff2", "w_ple_gate", "w_ple"):
        if n in full:
            W[n] = full[n][None]
    if "gla_w_a2" in full:
        W["gla_wa"] = jnp.pad(full["gla_w_a2"], ((0, RANKP - RANK), (0, 0))).astype(bf16)
    if "att_rel_bias" in full:
        W["att_bias"] = att_bias_build(full["att_rel_bias"], f"l{li}_att_bias")
    if "conv_dw_w" in full:
        W["conv_dw_w"] = full["conv_dw_w"]
    if repl is not None:
        for n in ("norm1_g", "norm2_g", "norm3_g", "b_ple_gate", "sg_ln_g", "sg_ln_b", "gla_b_a", "gla_norm_g", "conv_dw_b",
                  "conv_ln_g", "conv_ln_b"):
            W[n] = repl[n][li].reshape(1, -1)
        pos = np.arange(128)
        mask = (pos[None, :] // CHUNK) <= (pos[:, None] // CHUNK)
        W["sg_maskf"] = jnp.asarray(mask, f32)
        W["sg_wm"] = jnp.where(mask[None], repl["sg_w"][li], 0.0).astype(bf16)
        W["sg_bsb"] = jnp.broadcast_to(repl["sg_b"][li][:, :, None], (4, 128, 128))
    return W


def _layer_grads_to_ref(G):
    out = {}
    out["w_in"] = _to_ref_cols(G["w_in_main"], G["w_in_rank"])
    out["w_gate"] = jnp.transpose(G["w_gate_all"].reshape(D, 4, D), (1, 0, 2))
    out["b_gate"] = G["b_gate_all"].reshape(4, D)
    for n in ("w_branch", "w_out", "w_ff1", "w_ff2", "w_ple_gate", "w_ple", "gla_w_a2", "att_rel_bias", "conv_dw_w", "sg_w",
              "sg_b"):
        out[n] = G[n]
    for n in ("norm1_g", "norm2_g", "norm3_g", "b_ple_gate", "sg_ln_g", "sg_ln_b", "gla_b_a", "gla_norm_g", "conv_dw_b",
              "conv_ln_g", "conv_ln_b"):
        out[n] = G[n].reshape(-1)
    return out


def kernel(x, p, norm1_g, w_in, sg_ln_g, sg_ln_b, sg_w, sg_b, gla_w_a2, gla_b_a, gla_norm_g, att_rel_bias, conv_dw_w, conv_dw_b, conv_ln_g, conv_ln_b, w_branch, w_gate, b_gate, w_out, norm2_g, w_ff1, w_ff2, norm3_g, w_ple_gate, b_ple_gate, w_ple, final_g, loss_target, m_norm1_g, m_w_in, m_sg_ln_g, m_sg_ln_b, m_sg_w, m_sg_b, m_gla_w_a2, m_gla_b_a, m_gla_norm_g, m_att_rel_bias, m_conv_dw_w, m_conv_dw_b, m_conv_ln_g, m_conv_ln_b, m_w_branch, m_w_gate, m_b_gate, m_w_out, m_norm2_g, m_w_ff1, m_w_ff2, m_norm3_g, m_w_ple_gate, m_b_ple_gate, m_w_ple, m_final_g, v_norm1_g, v_w_in, v_sg_ln_g, v_sg_ln_b, v_sg_w, v_sg_b, v_gla_w_a2, v_gla_b_a, v_gla_norm_g, v_att_rel_bias, v_conv_dw_w, v_conv_dw_b, v_conv_ln_g, v_conv_ln_b, v_w_branch, v_w_gate, v_b_gate, v_w_out, v_norm2_g, v_w_ff1, v_w_ff2, v_norm3_g, v_w_ple_gate, v_b_ple_gate, v_w_ple, v_final_g):
    args = dict(locals())
    weights = {n: args[n] for n in W_ORDER}
    moments_m = {n: args["m_" + n] for n in W_ORDER}
    moments_v = {n: args["v_" + n] for n in W_ORDER}
    c = lax.axis_index("c")
    sharded_names = BIG + SMALL

    early = ("w_in",) + SMALL
    shards = {n: (weights[n].astype(bf16) if n in BIG else weights[n]) for n in sharded_names}
    gathered = dict(zip(early, ag4([shards[n] for n in early], "ag_weights_early")))
    repl = {n: weights[n] for n in REPL}
    Ws = [_prep_layer_weights(gathered, repl, li) for li in range(DEPTH)]
    in_att, in_gate = ("w_gate", "w_branch", "w_out"), ("w_ff1", "w_ff2", "w_ple_gate", "w_ple")
    late = [(names, [shards[n] for n in names], Ws) for names in (in_att, in_gate)]

    h = x[0]
    saved = []
    for li in range(DEPTH):
        h, sv = _layer_fwd(h, p[li, 0], Ws[li], li, late if li == 0 else None)
        saved.append(sv)
    loss_part, dh, dfinal = loss_head(h, final_g.reshape(1, D), loss_target[0], "loss_head")
    loss = lax.psum(loss_part[0, 0], ("x", "y", "c"))

    layer_grads = [None] * DEPTH
    for li in reversed(range(DEPTH)):
        dh, G = _layer_bwd(dh, saved[li], p[li, 0], Ws[li], li)
        layer_grads[li] = _layer_grads_to_ref(G)
    grad_x = dh[None]

    gps = [jnp.stack([_split_chips(layer_grads[li][n], SHARDED[n][1]) for li in range(DEPTH)]).astype(bf16) for n in BIG]
    ras = sib_other_layer(gps, "rs_sibling_layer")
    psums = [add_halves(g, r, c, "rs_add_" + n) for n, g, r in zip(BIG, gps, ras)]
    rbs = a2a4(psums, "rs_all_to_all")
    chip = 2 * lax.axis_index("x") + lax.axis_index("y")
    reds = [reduce_chips(r, ps, c, chip, "rs_sum_" + n) for n, r, ps in zip(BIG, rbs, psums)]
    grads = dict(zip(BIG, ag2(reds, "rs_sibling_gather")))

    local = {n: jnp.stack([layer_grads[li][n] for li in range(DEPTH)]) for n in tuple(REPL)[:-1] + SMALL}
    local["final_g"] = dfinal.reshape(D)
    rnames = tuple(REPL) + SMALL
    rflat = jnp.concatenate([local[n].reshape(-1) for n in rnames])
    rflat = jnp.pad(rflat, (0, REPL_ROWS * PACK_W - rflat.shape[0])).reshape(REPL_ROWS, PACK_W)
    rall = ag8(rflat, "ar_gather").reshape(8, REPL_ROWS, PACK_W)
    rsum = sum_slots(rall, "ar_sum").reshape(-1)
    off = 0
    for n in rnames:
        shape = local[n].shape
        size = int(np.prod(shape))
        g = rsum[off:off + size].reshape(shape)
        off += size
        if n in SMALL:
            ax = SHARDED[n][1] + 1
            g = lax.dynamic_slice_in_dim(g, chip * (shape[ax] // 4), shape[ax] // 4, axis=ax)
        grads[n] = g

    deltas, new_m, new_v = {}, {}, {}
    for n in W_ORDER:
        deltas[n], new_m[n], new_v[n] = adamw(weights[n], grads[n], moments_m[n], moments_v[n], "adamw_" + n)
    return (loss, grad_x, *[grads[n] for n in W_ORDER], *[deltas[n] for n in W_ORDER], *[new_m[n] for n in W_ORDER],
            *[new_v[n] for n in W_ORDER])
```

```python
import functools

import jax
import jax.numpy as jnp
import numpy as np
from jax import lax
from jax.experimental import pallas as pl
from jax.experimental.pallas import tpu as pltpu

f32, bf16 = jnp.float32, jnp.bfloat16
HI = lax.Precision.HIGHEST
MESH = pl.DeviceIdType.MESH
SDS = jax.ShapeDtypeStruct
BS = pl.BlockSpec
ANY = pl.BlockSpec(memory_space=pl.ANY)

D = 1024
DEPTH = 2
CHUNK = 64
BW = 512
NP = 5120
RANK = 16
RANKP = 128
DFF = 4096
PLE = 256
CONV_K = 31
HALO = 32
TQ = 256
WIN = 768
REL_TABLE = 320
EPS = 1e-6
NEG_INF = -1e30
VMEM_LIMIT = 56 * 1024 * 1024

ADAM_LR, ADAM_B1, ADAM_B2, ADAM_EPS, ADAM_WD, ADAM_STEP = 0.001, 0.9, 0.999, 1e-08, 0.01, 10

OUR_COLS = dict(g_q=(0, 256), g_k=(256, 256), g_v=(512, 512), g_r=(1024, 512), a_q=(1536, 512), a_k=(2048, 512),
                a_v=(2560, 512), sg_u=(3072, 512), sg_v=(3584, 512), c_a=(4096, 512), c_g=(4608, 512))
REF_SPLITS = (("sg_u", 512), ("sg_v", 512), ("g_q", 256), ("g_k", 256), ("g_v", 512), ("g_r", 512), ("g_a", 16),
              ("a_q", 512), ("a_k", 512), ("a_v", 512), ("c_a", 512), ("c_g", 512))

SHARDED = dict(w_in=((1024, 5136), 1), w_branch=((4, 512, 1024), 2), w_gate=((4, 1024, 1024), 1), w_out=((1024, 1024), 0),
               w_ff1=((1024, 4096), 1), w_ff2=((4096, 1024), 0), w_ple_gate=((1024, 1024), 0), w_ple=((256, 1024), 1),
               gla_w_a2=((16, 256), 1), att_rel_bias=((8, 320), 1), conv_dw_w=((31, 512), 1), b_gate=((4, 1024), 1))
BIG = ("w_in", "w_branch", "w_gate", "w_out", "w_ff1", "w_ff2", "w_ple_gate", "w_ple")
SMALL = ("gla_w_a2", "att_rel_bias", "conv_dw_w", "b_gate")
REPL = dict(norm1_g=(2, 1024), sg_ln_g=(2, 512), sg_ln_b=(2, 512), sg_w=(2, 4, 128, 128), sg_b=(2, 4, 128), gla_b_a=(2, 256),
            gla_norm_g=(2, 512), conv_dw_b=(2, 512), conv_ln_g=(2, 512), conv_ln_b=(2, 512), norm2_g=(2, 1024),
            norm3_g=(2, 1024), b_ple_gate=(2, 1024), final_g=(1024,))
W_ORDER = ['norm1_g', 'w_in', 'sg_ln_g', 'sg_ln_b', 'sg_w', 'sg_b', 'gla_w_a2', 'gla_b_a', 'gla_norm_g', 'att_rel_bias',
           'conv_dw_w', 'conv_dw_b', 'conv_ln_g', 'conv_ln_b', 'w_branch', 'w_gate', 'b_gate', 'w_out', 'norm2_g', 'w_ff1',
           'w_ff2', 'norm3_g', 'w_ple_gate', 'b_ple_gate', 'w_ple', 'final_g']
PACK_W = 1024
REPL_ROWS = 200
TM = 1024
MM_VMEM_BUDGET = 44 * 1024 * 1024


def _tile(s):
    return 512 if s % 512 == 0 else s


def _token_tile(S, row_bytes, fixed_bytes):
    for t in (2048, 1024):
        if S % t == 0 and 2 * (t * row_bytes + fixed_bytes) <= MM_VMEM_BUDGET:
            return t
    return min(TM, S)


def _cp(*sem):
    return pltpu.CompilerParams(dimension_semantics=sem, vmem_limit_bytes=VMEM_LIMIT)


def rms_fwd(h, g, name):
    S, Dm = h.shape
    T = _tile(S)

    def body(h_ref, g_ref, o_ref):
        x = h_ref[...]
        r = lax.rsqrt(jnp.mean(x * x, axis=-1, keepdims=True) + EPS)
        o_ref[...] = (x * r * g_ref[...]).astype(bf16)

    return pl.pallas_call(
        body, out_shape=SDS((S, Dm), bf16), grid=(S // T,),
        in_specs=[BS((T, Dm), lambda i: (i, 0)), BS((1, Dm), lambda i: (0, 0))],
        out_specs=BS((T, Dm), lambda i: (i, 0)), compiler_params=_cp("parallel"), name=name)(h, g)


def mm_nn(x, w, *, name, bias=None, act=None, pre=None, mul=None, res=None, out_dtype=bf16, raw_out=False, gather=None):
    S = x.shape[0]
    G, K, N = w.shape
    tn = min(1024 if K <= 1024 else 512, N)
    nj = N // tn
    row_bytes = K * x.dtype.itemsize + tn * (jnp.dtype(out_dtype).itemsize + (2 if raw_out else 0)
                                             + sum(a.dtype.itemsize for a in (mul, res) if a is not None))
    T = _token_tile(S, row_bytes, K * tn * 2)
    extras = [a for a in (bias, mul, res) if a is not None]
    ng = len(gather) if gather else 0
    grid = (S // T, G, nj)

    def body(*refs):
        it = iter(refs)
        x_ref, w_ref = next(it), next(it)
        b_ref = next(it) if bias is not None else None
        m_ref = next(it) if mul is not None else None
        r_ref = next(it) if res is not None else None
        g_in = [next(it) for _ in range(ng)]
        o_ref = next(it)
        raw_ref = next(it) if raw_out else None
        if ng:
            g_out = [next(it) for _ in range(ng)]
            step = (pl.program_id(0) * G + pl.program_id(1)) * nj + pl.program_id(2)
            _ag4_over_grid(g_in, g_out, next(it), next(it), step, grid[0] * G * nj, 0.875)
        xv = x_ref[...]
        if pre == "relu2":
            xf = jnp.maximum(xv.astype(f32), 0.0)
            xv = xf * xf
        acc = jnp.dot(xv.astype(bf16), w_ref[0], preferred_element_type=f32)
        if raw_out:
            raw_ref[...] = acc.astype(bf16)
        if b_ref is not None:
            acc = acc + b_ref[...]
        if act == "sigmoid":
            acc = jax.nn.sigmoid(acc)
        if m_ref is not None:
            acc = acc * m_ref[...].astype(f32)
        if r_ref is not None:
            acc = r_ref[...].astype(f32) + acc
        o_ref[...] = acc.astype(out_dtype)

    in_specs = [BS((T, K), lambda i, g, j: (i, g)), BS((1, K, tn), lambda i, g, j: (g, 0, j))]
    if bias is not None:
        in_specs.append(BS((1, tn), lambda i, g, j: (0, g * nj + j)))
    for a in (mul, res):
        if a is not None:
            in_specs.append(BS((T, tn), lambda i, g, j: (i, g * nj + j)))
    ospec = BS((T, tn), lambda i, g, j: (i, g * nj + j))
    out_shape = SDS((S, G * N), out_dtype)
    if raw_out:
        out_shape, ospec = (out_shape, SDS((S, G * N), bf16)), (ospec, ospec)
    if not ng:
        return pl.pallas_call(
            body, out_shape=out_shape, grid=grid, in_specs=in_specs, out_specs=ospec,
            compiler_params=_cp("parallel", "parallel", "parallel"), name=name)(x, w, *extras)
    out_shape = (out_shape if raw_out else (out_shape,)) + tuple(_ag4_out_shapes(gather))
    ospec = (ospec if raw_out else (ospec,)) + tuple(ANY for _ in gather)
    outs = pl.pallas_call(
        body, out_shape=out_shape, grid=grid, in_specs=in_specs + [ANY] * ng, out_specs=ospec,
        scratch_shapes=_ag4_sems(ng), compiler_params=_cp("arbitrary", "arbitrary", "arbitrary"), name=name)(
            x, w, *extras, *gather)
    nres = 2 if raw_out else 1
    return (outs[0] if nres == 1 else outs[:2]), list(outs[nres:])


def mm_nt(dy, w, *, name, res=None, post_a=None, out_dtype=f32):
    S = dy.shape[0]
    G, K, N = w.shape
    tk = min(1024 if N <= 1024 else 512, K)
    nk = K // tk
    row_bytes = N * dy.dtype.itemsize + tk * (jnp.dtype(out_dtype).itemsize
                                              + sum(a.dtype.itemsize for a in (res, post_a) if a is not None))
    T = _token_tile(S, row_bytes, tk * N * 2)
    extras = [a for a in (res, post_a) if a is not None]

    def body(*refs):
        it = iter(refs)
        d_ref, w_ref = next(it), next(it)
        r_ref = next(it) if res is not None else None
        a_ref = next(it) if post_a is not None else None
        o_ref = next(it)
        acc = lax.dot_general(d_ref[...].astype(bf16), w_ref[0], (((1,), (1,)), ((), ())), preferred_element_type=f32)
        if r_ref is not None:
            acc = acc + r_ref[...].astype(f32)
        if a_ref is not None:
            acc = acc * (2.0 * jnp.maximum(a_ref[...].astype(f32), 0.0))
        o_ref[...] = acc.astype(out_dtype)

    in_specs = [BS((T, N), lambda i, g, j: (i, g)), BS((1, tk, N), lambda i, g, j: (g, j, 0))]
    for a in extras:
        in_specs.append(BS((T, tk), lambda i, g, j: (i, g * nk + j)))
    return pl.pallas_call(
        body, out_shape=SDS((S, G * K), out_dtype), grid=(S // T, G, nk), in_specs=in_specs,
        out_specs=BS((T, tk), lambda i, g, j: (i, g * nk + j)),
        compiler_params=_cp("parallel", "parallel", "parallel"), name=name)(dy, w, *extras)


def mm_tn(x, dy, *, name, G=1, pre=None, ts=1024):
    S = x.shape[0]
    K, N = x.shape[1] // G, dy.shape[1] // G
    tk, tn = min(K, 1024), min(N, 1024)
    nk, nn = K // tk, N // tn
    ts = min(ts, S)

    def body(x_ref, d_ref, o_ref):
        @pl.when(pl.program_id(3) == 0)
        def _():
            o_ref[...] = jnp.zeros_like(o_ref)

        xv = x_ref[...]
        if pre == "relu2":
            xf = jnp.maximum(xv.astype(f32), 0.0)
            xv = xf * xf
        o_ref[0] += lax.dot_general(xv.astype(bf16), d_ref[...].astype(bf16), (((0,), (0,)), ((), ())),
                                    preferred_element_type=f32)

    return pl.pallas_call(
        body, out_shape=SDS((G, K, N), f32), grid=(G, nk, nn, S // ts),
        in_specs=[BS((ts, tk), lambda g, a, b, s: (s, g * nk + a)), BS((ts, tn), lambda g, a, b, s: (s, g * nn + b))],
        out_specs=BS((1, tk, tn), lambda g, a, b, s: (g, a, b)),
        compiler_params=_cp("parallel", "parallel", "parallel", "arbitrary"), name=name)(x, dy)


def rms_bwd(dxn, x, g, dres, name):
    S, Dm = x.shape
    T = _tile(S)

    def body(*refs):
        if dres is not None:
            d_ref, x_ref, g_ref, r_ref, dx_ref, dg_ref = refs
        else:
            d_ref, x_ref, g_ref, dx_ref, dg_ref = refs
        xv = x_ref[...]
        d = d_ref[...].astype(f32)
        r = lax.rsqrt(jnp.mean(xv * xv, axis=-1, keepdims=True) + EPS)
        u = d * g_ref[...]
        dx = r * u - xv * ((r * r * r) * (1.0 / Dm)) * jnp.sum(u * xv, axis=-1, keepdims=True)
        if dres is not None:
            dx = r_ref[...] + dx
        dx_ref[...] = dx

        @pl.when(pl.program_id(0) == 0)
        def _():
            dg_ref[...] = jnp.zeros_like(dg_ref)

        dg_ref[...] += jnp.sum(d * xv * r, axis=0, keepdims=True)

    tok = BS((T, Dm), lambda i: (i, 0))
    vec = BS((1, Dm), lambda i: (0, 0))
    args = (dxn, x, g) + ((dres,) if dres is not None else ())
    return pl.pallas_call(
        body, out_shape=(SDS((S, Dm), f32), SDS((1, Dm), f32)), grid=(S // T,),
        in_specs=[tok, tok, vec] + ([tok] if dres is not None else []), out_specs=(tok, vec),
        compiler_params=_cp("arbitrary"), name=name)(*args)


def loss_head(h, g, target, name):
    S, Dm = h.shape
    T = _tile(S)

    def body(h_ref, g_ref, t_ref, loss_ref, dh_ref, dg_ref):
        @pl.when(pl.program_id(0) == 0)
        def _():
            loss_ref[...] = jnp.zeros_like(loss_ref)
            dg_ref[...] = jnp.zeros_like(dg_ref)

        xv = h_ref[...]
        gv = g_ref[...]
        r = lax.rsqrt(jnp.mean(xv * xv, axis=-1, keepdims=True) + EPS)
        diff = xv * r * gv - t_ref[...]
        loss_ref[...] += 0.5 * jnp.sum(jnp.mean(diff * diff, axis=-1, keepdims=True))
        d = diff * (1.0 / Dm)
        u = d * gv
        dh_ref[...] = r * u - xv * ((r * r * r) * (1.0 / Dm)) * jnp.sum(u * xv, axis=-1, keepdims=True)
        dg_ref[...] += jnp.sum(d * xv * r, axis=0, keepdims=True)

    tok = BS((T, Dm), lambda i: (i, 0))
    vec = BS((1, Dm), lambda i: (0, 0))
    return pl.pallas_call(
        body, out_shape=(SDS((1, 128), f32), SDS((S, Dm), f32), SDS((1, Dm), f32)), grid=(S // T,),
        in_specs=[tok, vec, tok], out_specs=(BS((1, 128), lambda i: (0, 0)), tok, vec),
        compiler_params=_cp("arbitrary"), name=name)(h, g, target)


def gate_merge_fwd(gate, z, name):
    S = gate.shape[0]
    T = _tile(S)

    def body(g_ref, z_ref, o_ref):
        acc = jnp.zeros((T, D), f32)
        for n in range(4):
            acc = acc + g_ref[:, n * D:(n + 1) * D].astype(f32) * z_ref[:, n * D:(n + 1) * D].astype(f32)
        o_ref[...] = acc.astype(bf16)

    wide = BS((T, 4 * D), lambda i: (i, 0))
    return pl.pallas_call(body, out_shape=SDS((S, D), bf16), grid=(S // T,), in_specs=[wide, wide],
                          out_specs=BS((T, D), lambda i: (i, 0)), compiler_params=_cp("parallel"), name=name)(gate, z)


def gate_merge_bwd(dm, gate, z, name):
    S = gate.shape[0]
    T = _tile(S)

    def body(dm_ref, g_ref, z_ref, dz_ref, dg_ref, db_ref):
        @pl.when(pl.program_id(0) == 0)
        def _():
            db_ref[...] = jnp.zeros_like(db_ref)

        dmv = dm_ref[...].astype(f32)
        for n in range(4):
            cols = slice(n * D, (n + 1) * D)
            gv = g_ref[:, cols].astype(f32)
            dz_ref[:, cols] = (dmv * gv).astype(bf16)
            dgp = dmv * z_ref[:, cols].astype(f32) * gv * (1.0 - gv)
            dg_ref[:, cols] = dgp.astype(bf16)
            db_ref[:, cols] += jnp.sum(dgp, axis=0, keepdims=True)

    wide = BS((T, 4 * D), lambda i: (i, 0))
    return pl.pallas_call(
        body, out_shape=(SDS((S, 4 * D), bf16), SDS((S, 4 * D), bf16), SDS((1, 4 * D), f32)), grid=(S // T,),
        in_specs=[BS((T, D), lambda i: (i, 0)), wide, wide], out_specs=(wide, wide, BS((1, 4 * D), lambda i: (0, 0))),
        compiler_params=_cp("arbitrary"), name=name)(dm, gate, z)


def ple_bwd_ew(dh, e, pg, name):
    S = dh.shape[0]
    T = _tile(S)

    def body(dh_ref, e_ref, pg_ref, dp_ref, de_ref, db_ref):
        @pl.when(pl.program_id(0) == 0)
        def _():
            db_ref[...] = jnp.zeros_like(db_ref)

        d = dh_ref[...]
        g = pg_ref[...].astype(f32)
        dpre = d * e_ref[...].astype(f32) * g * (1.0 - g)
        dp_ref[...] = dpre.astype(bf16)
        de_ref[...] = (d * g).astype(bf16)
        db_ref[...] += jnp.sum(dpre, axis=0, keepdims=True)

    tok = BS((T, D), lambda i: (i, 0))
    return pl.pallas_call(
        body, out_shape=(SDS((S, D), bf16), SDS((S, D), bf16), SDS((1, D), f32)), grid=(S // T,),
        in_specs=[tok, tok, tok], out_specs=(tok, tok, BS((1, D), lambda i: (0, 0))),
        compiler_params=_cp("arbitrary"), name=name)(dh, e, pg)


_GK = 0.7978845608028654
_GC = 0.044715


def _gelu(x):
    return 0.5 * x * (1.0 + jnp.tanh(_GK * (x + _GC * (x * x * x))))


def _gelu_grad(x):
    x2 = x * x
    t = jnp.tanh(_GK * (x + _GC * (x * x2)))
    return 0.5 * (1.0 + t) + 0.5 * x * (1.0 - t * t) * (_GK * (1.0 + 3.0 * _GC * x2))


def _ln_stats(v):
    mu = jnp.mean(v, axis=-1, keepdims=True)
    vc = v - mu
    rs = lax.rsqrt(jnp.mean(vc * vc, axis=-1, keepdims=True) + EPS)
    return vc * rs, rs


def _ln_bwd(dvh, vh, rs):
    return rs * (dvh - jnp.mean(dvh, axis=-1, keepdims=True) - vh * jnp.mean(dvh * vh, axis=-1, keepdims=True))


def sg_fwd(proj, lg, lb, wm, bsb, name):
    S = proj.shape[0]
    T = _tile(S)
    cu, cv = OUR_COLS["sg_u"][0] // BW, OUR_COLS["sg_v"][0] // BW

    def body(u_ref, v_ref, lg_ref, lb_ref, wm_ref, bsb_ref, o_ref):
        for b in range(T // 128):
            rows = slice(b * 128, (b + 1) * 128)
            u = _gelu(u_ref[rows, :].astype(f32))
            vh, _ = _ln_stats(_gelu(v_ref[rows, :].astype(f32)))
            vb = (vh * lg_ref[...] + lb_ref[...]).astype(bf16)
            outs = []
            for g in range(4):
                cols = slice(g * 128, (g + 1) * 128)
                mixed = jnp.dot(wm_ref[g], vb[:, cols], preferred_element_type=f32) + bsb_ref[g]
                outs.append(u[:, cols] * mixed)
            o_ref[rows, :] = jnp.concatenate(outs, axis=1).astype(bf16)

    vec = BS((1, BW), lambda i: (0, 0))
    cube = BS((4, 128, 128), lambda i: (0, 0, 0))
    return pl.pallas_call(
        body, out_shape=SDS((S, 4 * BW), bf16), grid=(S // T,),
        in_specs=[BS((T, BW), lambda i: (i, cu)), BS((T, BW), lambda i: (i, cv)), vec, vec, cube, cube],
        out_specs=BS((T, BW), lambda i: (i, 0)), compiler_params=_cp("parallel"), name=name)(proj, proj, lg, lb, wm, bsb)


def sg_bwd(proj, dy, lg, lb, wm, bsb, maskf, name):
    S = proj.shape[0]
    T = _tile(S)
    cu, cv = OUR_COLS["sg_u"][0] // BW, OUR_COLS["sg_v"][0] // BW
    creg = OUR_COLS["sg_u"][0] // (2 * BW)

    def body(u_ref, v_ref, dy_ref, lg_ref, lb_ref, wm_ref, bsb_ref, mk_ref, dp_ref, dwm_ref, dbs_ref, dlg_ref, dlb_ref):
        @pl.when(pl.program_id(0) == 0)
        def _():
            dwm_ref[...] = jnp.zeros_like(dwm_ref)
            dbs_ref[...] = jnp.zeros_like(dbs_ref)
            dlg_ref[...] = jnp.zeros_like(dlg_ref)
            dlb_ref[...] = jnp.zeros_like(dlb_ref)

        for b in range(T // 128):
            rows = slice(b * 128, (b + 1) * 128)
            su = u_ref[rows, :].astype(f32)
            sv = v_ref[rows, :].astype(f32)
            dya = dy_ref[rows, :].astype(f32)
            u = _gelu(su)
            vh, rs = _ln_stats(_gelu(sv))
            vb = (vh * lg_ref[...] + lb_ref[...]).astype(bf16)
            dus, dvls = [], []
            for g in range(4):
                cols = slice(g * 128, (g + 1) * 128)
                mixed = jnp.dot(wm_ref[g], vb[:, cols], preferred_element_type=f32) + bsb_ref[g]
                dus.append(dya[:, cols] * mixed)
                dmg = dya[:, cols] * u[:, cols]
                dmb = dmg.astype(bf16)
                dbs_ref[g] += jnp.broadcast_to(jnp.sum(dmg, axis=1, keepdims=True), (128, 128))
                dwm_ref[g] += mk_ref[...] * lax.dot_general(dmb, vb[:, cols], (((1,), (1,)), ((), ())),
                                                            preferred_element_type=f32)
                dvls.append(lax.dot_general(wm_ref[g], dmb, (((0,), (0,)), ((), ())), preferred_element_type=f32))
            du = jnp.concatenate(dus, axis=1)
            dvln = jnp.concatenate(dvls, axis=1)
            dlg_ref[...] += jnp.sum(dvln * vh, axis=0, keepdims=True)
            dlb_ref[...] += jnp.sum(dvln, axis=0, keepdims=True)
            dv = _ln_bwd(dvln * lg_ref[...], vh, rs)
            dp_ref[rows, 0:BW] = (du * _gelu_grad(su)).astype(bf16)
            dp_ref[rows, BW:2 * BW] = (dv * _gelu_grad(sv)).astype(bf16)

    vec = BS((1, BW), lambda i: (0, 0))
    cube = BS((4, 128, 128), lambda i: (0, 0, 0))
    return pl.pallas_call(
        body,
        out_shape=(SDS((S, NP), bf16), SDS((4, 128, 128), f32), SDS((4, 128, 128), f32), SDS((1, BW), f32), SDS((1, BW), f32)),
        grid=(S // T,),
        in_specs=[BS((T, BW), lambda i: (i, cu)), BS((T, BW), lambda i: (i, cv)), BS((T, BW), lambda i: (i, 0)), vec, vec,
                  cube, cube, BS((128, 128), lambda i: (0, 0))],
        out_specs=(BS((T, 2 * BW), lambda i: (i, creg)), cube, cube, vec, vec),
        compiler_params=_cp("arbitrary"), name=name)(proj, proj, dy, lg, lb, wm, bsb, maskf)


_SUB = 64


def _conv_specs(S, T):
    ca, cg = OUR_COLS["c_a"][0] // BW, OUR_COLS["c_g"][0] // BW
    hb = T // HALO
    prev = lambda i: jnp.maximum(i * hb - 1, 0)
    return [BS((T, BW), lambda i: (i, ca)), BS((T, BW), lambda i: (i, cg)),
            BS((HALO, BW), lambda i: (prev(i), ca)), BS((HALO, BW), lambda i: (prev(i), cg))]


def _fill_shifts(sh):
    n = sh.shape[1] - 8
    for s in range(1, 8):
        sh[s, pl.ds(0, n), :] = sh[0, pl.ds(s, n), :]


def _shifted(sh, off, rows):
    s = off % 8
    return sh[s, pl.ds(off - s, rows), :]


def _conv_fill_ybuf(a_ref, g_ref, ap_ref, gp_ref, ysh):
    T = a_ref.shape[0]
    ysh[0, pl.ds(HALO, T), :] = a_ref[...].astype(f32) * jax.nn.sigmoid(g_ref[...].astype(f32))
    first = (pl.program_id(0) == 0).astype(f32)
    ysh[0, pl.ds(0, HALO), :] = (1.0 - first) * (ap_ref[...].astype(f32) * jax.nn.sigmoid(gp_ref[...].astype(f32)))
    _fill_shifts(ysh)


def _conv_taps(w_ref, ysh, r0):
    acc = jnp.zeros((_SUB, BW), f32)
    for k in range(CONV_K):
        acc = acc + w_ref[k:k + 1, :] * _shifted(ysh, r0 + HALO - (CONV_K - 1) + k, _SUB)
    return acc


def conv_fwd(proj, w, b, lg, lb, y, name):
    S = proj.shape[0]
    T = _tile(S)

    def body(a_ref, g_ref, ap_ref, gp_ref, w_ref, b_ref, lg_ref, lb_ref, y_in, o_ref, ybuf):
        del y_in
        _conv_fill_ybuf(a_ref, g_ref, ap_ref, gp_ref, ybuf)
        for sb in range(T // _SUB):
            z = _conv_taps(w_ref, ybuf, sb * _SUB) + b_ref[...]
            zh, _ = _ln_stats(z)
            zl = zh * lg_ref[...] + lb_ref[...]
            o_ref[pl.ds(sb * _SUB, _SUB), :] = (zl * jax.nn.sigmoid(zl)).astype(bf16)

    vec = BS((1, BW), lambda i: (0, 0))
    return pl.pallas_call(
        body, out_shape=SDS(y.shape, bf16), grid=(S // T,),
        in_specs=_conv_specs(S, T) + [BS((CONV_K, BW), lambda i: (0, 0)), vec, vec, vec, ANY],
        out_specs=BS((T, BW), lambda i: (i, 3)), scratch_shapes=[pltpu.VMEM((8, T + HALO, BW), f32)],
        input_output_aliases={8: 0}, compiler_params=_cp("parallel"), name=name)(proj, proj, proj, proj, w, b, lg, lb, y)


def conv_bwd_norm(proj, dy, w, b, lg, lb, name):
    S = proj.shape[0]
    T = _tile(S)

    def body(a_ref, g_ref, ap_ref, gp_ref, dy_ref, w_ref, b_ref, lg_ref, lb_ref, dz_ref, dlg_ref, dlb_ref, db_ref, ybuf):
        @pl.when(pl.program_id(0) == 0)
        def _():
            dlg_ref[...] = jnp.zeros_like(dlg_ref)
            dlb_ref[...] = jnp.zeros_like(dlb_ref)
            db_ref[...] = jnp.zeros_like(db_ref)

        _conv_fill_ybuf(a_ref, g_ref, ap_ref, gp_ref, ybuf)
        for sb in range(T // _SUB):
            rows = pl.ds(sb * _SUB, _SUB)
            z = _conv_taps(w_ref, ybuf, sb * _SUB) + b_ref[...]
            zh, rs = _ln_stats(z)
            zl = zh * lg_ref[...] + lb_ref[...]
            sg = jax.nn.sigmoid(zl)
            dzl = dy_ref[rows, :].astype(f32) * sg * (1.0 + zl * (1.0 - sg))
            dlg_ref[...] += jnp.sum(dzl * zh, axis=0, keepdims=True)
            dlb_ref[...] += jnp.sum(dzl, axis=0, keepdims=True)
            dz = _ln_bwd(dzl * lg_ref[...], zh, rs)
            db_ref[...] += jnp.sum(dz, axis=0, keepdims=True)
            dz_ref[rows, :] = dz

    vec = BS((1, BW), lambda i: (0, 0))
    tok = BS((T, BW), lambda i: (i, 0))
    return pl.pallas_call(
        body, out_shape=(SDS((S, BW), f32), SDS((1, BW), f32), SDS((1, BW), f32), SDS((1, BW), f32)), grid=(S // T,),
        in_specs=_conv_specs(S, T) + [BS((T, BW), lambda i: (i, 3)), BS((CONV_K, BW), lambda i: (0, 0)), vec, vec, vec],
        out_specs=(tok, vec, vec, vec), scratch_shapes=[pltpu.VMEM((8, T + HALO, BW), f32)],
        compiler_params=_cp("arbitrary"), name=name)(proj, proj, proj, proj, dy, w, b, lg, lb)


def conv_bwd_taps(proj, dz, w, dproj, name):
    S = proj.shape[0]
    T = _tile(S)
    nT = S // T
    hb = T // HALO
    creg = OUR_COLS["c_a"][0] // (2 * BW)

    def body(a_ref, g_ref, ap_ref, gp_ref, dz_ref, dzn_ref, w_ref, dp_in, dp_ref, dw_ref, ybuf, dzbuf, dwacc):
        del dp_in
        i = pl.program_id(0)

        @pl.when(i == 0)
        def _():
            dwacc[...] = jnp.zeros_like(dwacc)

        _conv_fill_ybuf(a_ref, g_ref, ap_ref, gp_ref, ybuf)
        dzbuf[0, pl.ds(0, T), :] = dz_ref[...]
        dzbuf[0, pl.ds(T, HALO), :] = (i < nT - 1).astype(f32) * dzn_ref[...]
        _fill_shifts(dzbuf)
        for sb in range(T // _SUB):
            r0 = sb * _SUB
            rows = pl.ds(r0, _SUB)
            dzs = dz_ref[rows, :]
            dyg = jnp.zeros((_SUB, BW), f32)
            for k in range(CONV_K):
                ysl = _shifted(ybuf, r0 + HALO - (CONV_K - 1) + k, _SUB)
                dwacc[pl.ds(k * 8, 8), :] += jnp.sum((dzs * ysl).reshape(_SUB // 8, 8, BW), axis=0)
                dyg = dyg + w_ref[k:k + 1, :] * _shifted(dzbuf, r0 + (CONV_K - 1) - k, _SUB)
            av = a_ref[rows, :].astype(f32)
            sg = jax.nn.sigmoid(g_ref[rows, :].astype(f32))
            dp_ref[rows, 0:BW] = (dyg * sg).astype(bf16)
            dp_ref[rows, BW:2 * BW] = (dyg * av * sg * (1.0 - sg)).astype(bf16)

        @pl.when(i == nT - 1)
        def _():
            for k in range(CONV_K):
                dw_ref[k:k + 1, :] = jnp.sum(dwacc[pl.ds(k * 8, 8), :], axis=0, keepdims=True)

    nxt = lambda i: jnp.minimum((i + 1) * hb, S // HALO - 1)
    return pl.pallas_call(
        body, out_shape=(SDS((S, NP), bf16), SDS((CONV_K, BW), f32)), grid=(nT,),
        in_specs=_conv_specs(S, T) + [BS((T, BW), lambda i: (i, 0)), BS((HALO, BW), lambda i: (nxt(i), 0)),
                                      BS((CONV_K, BW), lambda i: (0, 0)), ANY],
        out_specs=(BS((T, 2 * BW), lambda i: (i, creg)), BS((CONV_K, BW), lambda i: (0, 0))),
        scratch_shapes=[pltpu.VMEM((8, T + HALO, BW), f32), pltpu.VMEM((8, T + HALO, BW), f32),
                        pltpu.VMEM((CONV_K * 8, BW), f32)],
        input_output_aliases={7: 0}, compiler_params=_cp("arbitrary"), name=name)(proj, proj, proj, proj, dz, dz, w, dproj)


def _toeplitz_index():
    j = lax.broadcasted_iota(jnp.int32, (REL_TABLE, 1024), 1)
    t = lax.broadcasted_iota(jnp.int32, (REL_TABLE, 1024), 0)
    e = ((WIN - 1) - j) & 1023
    tidx = jnp.clip(e - (TQ - 1), -(CHUNK - 1), 256) + (CHUNK - 1)
    return (tidx == t).astype(f32)


def att_bias_build(table, name):
    H = table.shape[0]

    def body(t_ref, o_ref):
        u = jnp.dot(t_ref[...], _toeplitz_index(), precision=HI, preferred_element_type=f32)
        row = lax.broadcasted_iota(jnp.int32, (TQ, 1024), 0)
        r = lax.broadcasted_iota(jnp.int32, (TQ, WIN), 0)
        n = lax.broadcasted_iota(jnp.int32, (TQ, WIN), 1)
        dchunk = (r // CHUNK + 8) - n // CHUNK
        band = (dchunk >= 0) & (dchunk <= 8)
        for h in range(H):
            x = jnp.broadcast_to(u[h:h + 1, :], (TQ, 1024))
            for b in range(8):
                x = jnp.where(((row >> b) & 1) == 1, pltpu.roll(x, 1 << b, 1), x)
            o_ref[h] = jnp.where(band, x[:, :WIN], NEG_INF)

    return pl.pallas_call(body, out_shape=SDS((H, TQ, WIN), f32), compiler_params=pltpu.CompilerParams(vmem_limit_bytes=VMEM_LIMIT),
                          name=name)(table)


def att_bias_grad(dbias, name):
    H = dbias.shape[0]

    def body(d_ref, o_ref):
        row = lax.broadcasted_iota(jnp.int32, (TQ, 1024), 0)
        rows = []
        for h in range(H):
            x = jnp.concatenate([d_ref[h], jnp.zeros((TQ, 1024 - WIN), f32)], axis=1)
            for b in range(8):
                x = jnp.where(((row >> b) & 1) == 1, pltpu.roll(x, 1024 - (1 << b), 1), x)
            rows.append(jnp.sum(x, axis=0, keepdims=True))
        du = jnp.concatenate(rows, axis=0)
        o_ref[...] = lax.dot_general(du, _toeplitz_index(), (((1,), (1,)), ((), ())), precision=HI,
                                     preferred_element_type=f32)

    return pl.pallas_call(body, out_shape=SDS((H, REL_TABLE), f32), compiler_params=pltpu.CompilerParams(vmem_limit_bytes=VMEM_LIMIT),
                          name=name)(dbias)


def _att_specs():
    cq, ck, cv = (OUR_COLS[n][0] // BW for n in ("a_q", "a_k", "a_v"))
    specs = [BS((TQ, BW), lambda i: (i, cq))]
    for col in (ck, cv):
        for back in (2, 1, 0):
            specs.append(BS((TQ, BW), functools.partial(lambda i, back, col: (jnp.maximum(i - back, 0), col), back=back, col=col)))
    return specs


def _att_pen(i):
    n = lax.broadcasted_iota(jnp.int32, (1, WIN), 1)
    return jnp.where(n + (i - 2) * TQ >= 0, 0.0, NEG_INF).astype(f32)


def _att_probs(qa, kp, bias_h, pen):
    s = lax.dot_general(qa, kp, (((1,), (1,)), ((), ())), preferred_element_type=f32) + bias_h + pen
    e = jnp.exp(s - jnp.max(s, axis=-1, keepdims=True))
    return e * (1.0 / jnp.sum(e, axis=-1, keepdims=True))


def att_fwd(proj, bias, y, name, gather=None):
    S = proj.shape[0]
    ng = len(gather) if gather else 0

    def body(q_ref, k2, k1, k0, v2, v1, v0, b_ref, y_in, *rest):
        del y_in
        g_in, o_ref, g_out = rest[:ng], rest[ng], rest[ng + 1:2 * ng + 1]
        i = pl.program_id(0)
        if ng:
            kwin, vwin, send_sems, recv_sems = rest[2 * ng + 1:]
            _ag4_over_grid(g_in, g_out, send_sems, recv_sems, i, S // TQ, 0.75)
        else:
            kwin, vwin = rest[1:]
        for w, (kr, vr) in enumerate(((k2, v2), (k1, v1), (k0, v0))):
            kwin[pl.ds(w * TQ, TQ), :] = kr[...]
            vwin[pl.ds(w * TQ, TQ), :] = vr[...]
        pen = _att_pen(i)
        lo = lax.broadcasted_iota(jnp.int32, (TQ, 128), 1) < 64
        for hp in range(4):
            cols = slice(hp * 128, (hp + 1) * 128)
            qp, kp, vp = q_ref[:, cols] * jnp.asarray(0.125, bf16), kwin[:, cols], vwin[:, cols]
            outs = []
            for a in range(2):
                qa = jnp.where(lo if a == 0 else ~lo, qp, jnp.zeros_like(qp))
                p = _att_probs(qa, kp, b_ref[2 * hp + a], pen)
                outs.append(jnp.dot(p.astype(bf16), vp, preferred_element_type=f32))
            o_ref[:, cols] = jnp.where(lo, outs[0], outs[1]).astype(bf16)

    outs = pl.pallas_call(
        body, out_shape=(SDS(y.shape, bf16),) + tuple(_ag4_out_shapes(gather or [])), grid=(S // TQ,),
        in_specs=_att_specs() + [BS((8, TQ, WIN), lambda i: (0, 0, 0), pipeline_mode=pl.Buffered(1)), ANY] + [ANY] * ng,
        out_specs=(BS((TQ, BW), lambda i: (i, 2)),) + tuple(ANY for _ in range(ng)),
        scratch_shapes=[pltpu.VMEM((WIN, BW), bf16), pltpu.VMEM((WIN, BW), bf16)] + (_ag4_sems(ng) if ng else []),
        input_output_aliases={8: 0}, compiler_params=_cp("arbitrary" if ng else "parallel"), name=name)(
            proj, proj, proj, proj, proj, proj, proj, bias, y, *(gather or []))
    return (outs[0], list(outs[1:])) if ng else outs[0]


def att_bwd(proj, y, dy, bias, dproj, name):
    S = proj.shape[0]
    cq = OUR_COLS["a_q"][0] // BW

    def body(q_ref, k2, k1, k0, v2, v1, v0, b_ref, o_ref, do_ref, dp_in, dq_ref, dkp_ref, dvp_ref, db_ref, kwin, vwin):
        del dp_in
        i = pl.program_id(0)

        @pl.when(i == 0)
        def _():
            db_ref[...] = jnp.zeros_like(db_ref)

        for w, (kr, vr) in enumerate(((k2, v2), (k1, v1), (k0, v0))):
            kwin[pl.ds(w * TQ, TQ), :] = kr[...]
            vwin[pl.ds(w * TQ, TQ), :] = vr[...]
        pen = _att_pen(i)
        lo = lax.broadcasted_iota(jnp.int32, (TQ, 128), 1) < 64
        for hp in range(4):
            cols = slice(hp * 128, (hp + 1) * 128)
            qp, kp, vp = q_ref[:, cols] * jnp.asarray(0.125, bf16), kwin[:, cols], vwin[:, cols]
            dop, op = do_ref[:, cols], o_ref[:, cols]
            dqs = []
            dk = jnp.zeros((WIN, 128), f32)
            dv = jnp.zeros((WIN, 128), f32)
            for a in range(2):
                sel = lo if a == 0 else ~lo
                qa = jnp.where(sel, qp, jnp.zeros_like(qp))
                doa = jnp.where(sel, dop, jnp.zeros_like(dop))
                p = _att_probs(qa, kp, b_ref[2 * hp + a], pen)
                dpv = lax.dot_general(doa, vp, (((1,), (1,)), ((), ())), preferred_element_type=f32)
                delta = jnp.sum(doa.astype(f32) * op.astype(f32), axis=-1, keepdims=True)
                ds = p * (dpv - delta)
                db_ref[2 * hp + a] += ds
                dsb = ds.astype(bf16)
                dqs.append(jnp.dot(dsb, kp, preferred_element_type=f32))
                dk = dk + lax.dot_general(dsb, qa, (((0,), (0,)), ((), ())), preferred_element_type=f32)
                dv = dv + lax.dot_general(p.astype(bf16), doa, (((0,), (0,)), ((), ())), preferred_element_type=f32)
            dq_ref[:, cols] = (jnp.where(lo, dqs[0], dqs[1]) * 0.125).astype(bf16)
            for w in range(3):
                dkp_ref[w, :, cols] = dk[w * TQ:(w + 1) * TQ].astype(bf16)
                dvp_ref[w, :, cols] = dv[w * TQ:(w + 1) * TQ].astype(bf16)

    tok = BS((TQ, BW), lambda i: (i, 2))
    part = BS((3, TQ, BW), lambda i: (0, i, 0))
    full = BS((8, TQ, WIN), lambda i: (0, 0, 0))
    return pl.pallas_call(
        body, out_shape=(SDS((S, NP), bf16), SDS((3, S, BW), bf16), SDS((3, S, BW), bf16), SDS((8, TQ, WIN), f32)),
        grid=(S // TQ,),
        in_specs=_att_specs() + [BS((8, TQ, WIN), lambda i: (0, 0, 0), pipeline_mode=pl.Buffered(1)), tok, tok, ANY],
        out_specs=(BS((TQ, BW), lambda i: (i, cq)), part, part, full),
        scratch_shapes=[pltpu.VMEM((WIN, BW), bf16), pltpu.VMEM((WIN, BW), bf16)],
        input_output_aliases={10: 0}, compiler_params=_cp("arbitrary"), name=name)(
            proj, proj, proj, proj, proj, proj, proj, bias, y, dy, dproj)


def att_shift_add(dkp, dvp, dproj, name):
    S = dkp.shape[1]
    nT = S // TQ
    creg = OUR_COLS["a_k"][0] // (2 * BW)

    def body(k2, k1, k0, v2, v1, v0, dp_in, dp_ref):
        del dp_in
        j = pl.program_id(0)
        m1 = (j + 1 < nT).astype(f32)
        m0 = (j + 2 < nT).astype(f32)
        dp_ref[:, 0:BW] = (k2[0].astype(f32) + m1 * k1[0].astype(f32) + m0 * k0[0].astype(f32)).astype(bf16)
        dp_ref[:, BW:2 * BW] = (v2[0].astype(f32) + m1 * v1[0].astype(f32) + m0 * v0[0].astype(f32)).astype(bf16)

    def spec(w):
        return BS((1, TQ, BW), functools.partial(lambda j, w: (w, jnp.minimum(j + 2 - w, nT - 1), 0), w=w))

    return pl.pallas_call(
        body, out_shape=SDS(dproj.shape, bf16), grid=(nT,),
        in_specs=[spec(2), spec(1), spec(0), spec(2), spec(1), spec(0), ANY],
        out_specs=BS((TQ, 2 * BW), lambda j: (j, creg)),
        input_output_aliases={6: 0}, compiler_params=_cp("parallel"), name=name)(dkp, dkp, dkp, dvp, dvp, dvp, dproj)


GQ, GV = 256, 512
TGC = 8


def _bd_mask():
    r = lax.broadcasted_iota(jnp.int32, (GQ, GV), 0) // 64
    c = lax.broadcasted_iota(jnp.int32, (GQ, GV), 1) // 128
    return (r == c).astype(f32)


def _tri(strict):
    r = lax.broadcasted_iota(jnp.int32, (CHUNK, CHUNK), 0)
    c = lax.broadcasted_iota(jnp.int32, (CHUNK, CHUNK), 1)
    return ((c < r) if strict else (c <= r)).astype(f32)


def _compact(s_bd):
    return jnp.concatenate([s_bd[h * 64:(h + 1) * 64, h * 128:(h + 1) * 128] for h in range(4)], axis=0)


def _expand(comp, mask):
    return jnp.tile(comp, (1, 4)) * mask


def _gla_gates(alr, wa_ref, ba_ref, tri_incl, ones_col):
    a = jnp.dot(alr, wa_ref[...], preferred_element_type=f32) + ba_ref[...]
    la = (jnp.minimum(a, 0.0) - jnp.log(1.0 + jnp.exp(-jnp.abs(a)))) * (1.0 / 16.0)
    cum = jnp.dot(tri_incl, la, precision=HI, preferred_element_type=f32)
    tot_row = cum[CHUNK - 1:CHUNK, :]
    tot_col = lax.dot_general(la, ones_col, (((0,), (0,)), ((), ())), precision=HI, preferred_element_type=f32)
    return a, cum, tot_row, jnp.tile(jnp.exp(tot_col), (1, 4))


def _head_norm(o):
    rns, ons = [], []
    for h in range(4):
        oh = o[:, h * 128:(h + 1) * 128]
        rn = lax.rsqrt(jnp.mean(oh * oh, axis=-1, keepdims=True) + EPS)
        rns.append(rn)
        ons.append(oh * rn)
    return rns, ons


def _gla_in_specs(T, imap):
    cq, ck = OUR_COLS["g_q"][0] // GQ, OUR_COLS["g_k"][0] // GQ
    cv, cr = OUR_COLS["g_v"][0] // GV, OUR_COLS["g_r"][0] // GV
    return [BS((T, GQ), lambda i: (imap(i), cq)), BS((T, GQ), lambda i: (imap(i), ck)), BS((T, GV), lambda i: (imap(i), cv)),
            BS((T, GV), lambda i: (imap(i), cr)), BS((T, RANKP), lambda i: (imap(i), 0))]


def gla_fwd(proj, pa, wa, ba, ng, y, name):
    S = proj.shape[0]
    T = min(TGC * CHUNK, S)
    nch = T // CHUNK

    def body(q_ref, k_ref, v_ref, r_ref, a_ref, wa_ref, ba_ref, ng_ref, y_in, y_ref, st_ref, s_scr):
        del y_in

        @pl.when(pl.program_id(0) == 0)
        def _():
            s_scr[...] = jnp.zeros_like(s_scr)

        mask = _bd_mask()
        tri = _tri(False)
        ones_col = jnp.ones((CHUNK, 128), f32)

        s_bd = s_scr[...]
        for ci in range(nch):
            rows = pl.ds(ci * CHUNK, CHUNK)
            _, cum, tot_row, dec4 = _gla_gates(a_ref[rows, :], wa_ref, ba_ref, tri, ones_col)
            kd = (k_ref[rows, :].astype(f32) * jnp.exp(tot_row - cum)).astype(bf16)
            upd = lax.dot_general(kd, v_ref[rows, :], (((0,), (0,)), ((), ())), preferred_element_type=f32) * mask
            s_bd = dec4 * s_bd + upd
            st_ref[pl.ds(ci * GQ, GQ), :] = _compact(s_bd)
            qs = (q_ref[rows, :].astype(f32) * 0.125).astype(bf16)
            o = jnp.dot(qs, s_bd.astype(bf16), preferred_element_type=f32)
            _, ons = _head_norm(o)
            rv = r_ref[rows, :].astype(f32)
            y_ref[rows, :] = (jnp.concatenate(ons, axis=1) * ng_ref[...] * (rv * jax.nn.sigmoid(rv))).astype(bf16)
        s_scr[...] = s_bd

    return pl.pallas_call(
        body, out_shape=(SDS(y.shape, bf16), SDS((S // CHUNK * GQ, 128), f32)), grid=(S // T,),
        in_specs=_gla_in_specs(T, lambda i: i) + [BS((RANKP, GQ), lambda i: (0, 0)), BS((1, GQ), lambda i: (0, 0)),
                                                  BS((1, GV), lambda i: (0, 0)), ANY],
        out_specs=(BS((T, GV), lambda i: (i, 1)), BS((nch * GQ, 128), lambda i: (i, 0))),
        scratch_shapes=[pltpu.VMEM((GQ, GV), f32)], input_output_aliases={8: 0}, compiler_params=_cp("arbitrary"),
        name=name)(proj, proj, proj, proj, pa, wa, ba, ng, y)


def gla_bwd(proj, pa, states, dy, wa, ba, ng, dproj, name):
    S = proj.shape[0]
    T = min(TGC * CHUNK, S)
    nch = T // CHUNK
    nT = S // T
    rev = lambda i: nT - 1 - i

    def body(q_ref, k_ref, v_ref, r_ref, a_ref, st_ref, sp_ref, dy_ref, wa_ref, ba_ref, ng_ref, dp_in,
             dp_ref, da_ref, dwa_ref, dba_ref, dng_ref, g_scr):
        del dp_in
        i = pl.program_id(0)

        @pl.when(i == 0)
        def _():
            g_scr[...] = jnp.zeros_like(g_scr)
            dwa_ref[...] = jnp.zeros_like(dwa_ref)
            dba_ref[...] = jnp.zeros_like(dba_ref)
            dng_ref[...] = jnp.zeros_like(dng_ref)

        mask = _bd_mask()
        tri = _tri(False)
        tri_strict = _tri(True)
        ones_col = jnp.ones((CHUNK, 128), f32)
        ones_row = jnp.ones((8, 128), f32)
        first_tile = (i == nT - 1).astype(f32)

        g_carry = g_scr[...]
        for ci in reversed(range(nch)):
            rows = pl.ds(ci * CHUNK, CHUNK)
            alr = a_ref[rows, :]
            a, cum, tot_row, dec4 = _gla_gates(alr, wa_ref, ba_ref, tri, ones_col)
            wdec = jnp.exp(tot_row - cum)
            kdf = k_ref[rows, :].astype(f32) * wdec
            kd = kdf.astype(bf16)
            s_c = _expand(st_ref[pl.ds(ci * GQ, GQ), :], mask)
            prev = st_ref[pl.ds((ci - 1) * GQ, GQ), :] if ci > 0 else sp_ref[...] * (1.0 - first_tile)
            qs = (q_ref[rows, :].astype(f32) * 0.125).astype(bf16)
            s_cb = s_c.astype(bf16)
            o = jnp.dot(qs, s_cb, preferred_element_type=f32)
            rns, ons = _head_norm(o)
            on = jnp.concatenate(ons, axis=1)
            rv = r_ref[rows, :].astype(f32)
            sg = jax.nn.sigmoid(rv)
            sr = rv * sg
            dyv = dy_ref[rows, :].astype(f32)
            ngv = ng_ref[...]
            dng_ref[...] += jnp.sum(dyv * on * sr, axis=0, keepdims=True)
            d_on = dyv * ngv * sr
            dr = dyv * on * ngv * (sg * (1.0 + rv * (1.0 - sg)))
            dos = []
            for h in range(4):
                cols = slice(h * 128, (h + 1) * 128)
                dh_ = d_on[:, cols]
                dos.append(rns[h] * (dh_ - ons[h] * jnp.mean(dh_ * ons[h], axis=-1, keepdims=True)))
            do = jnp.concatenate(dos, axis=1).astype(bf16)
            dq = lax.dot_general(do, s_cb, (((1,), (1,)), ((), ())), preferred_element_type=f32) * 0.125
            ds = lax.dot_general(qs, do, (((0,), (0,)), ((), ())), preferred_element_type=f32) * mask + g_carry
            ddec_row = lax.dot_general(ones_row, _compact(ds) * prev, (((1,), (1,)), ((), ())), precision=HI,
                                       preferred_element_type=f32)[0:1, :]
            dsb = ds.astype(bf16)
            dkd = lax.dot_general(v_ref[rows, :], dsb, (((1,), (1,)), ((), ())), preferred_element_type=f32)
            dv = jnp.dot(kd, dsb, preferred_element_type=f32)
            g_carry = dec4 * ds
            dk = dkd * wdec
            dwlog = dkd * kdf
            dla = ddec_row * jnp.exp(tot_row) + jnp.dot(tri_strict, dwlog, precision=HI, preferred_element_type=f32)
            da = dla * (1.0 - jax.nn.sigmoid(a)) * (1.0 / 16.0)
            dab = da.astype(bf16)
            da_ref[rows, :] = lax.dot_general(dab, wa_ref[...], (((1,), (1,)), ((), ())),
                                              preferred_element_type=f32).astype(bf16)
            dwa_ref[...] += lax.dot_general(alr, dab, (((0,), (0,)), ((), ())), preferred_element_type=f32)
            dba_ref[...] += jnp.sum(da, axis=0, keepdims=True)
            dp_ref[rows, 0:GQ] = dq.astype(bf16)
            dp_ref[rows, GQ:2 * GQ] = dk.astype(bf16)
            dp_ref[rows, 2 * GQ:2 * GQ + GV] = dv.astype(bf16)
            dp_ref[rows, 2 * GQ + GV:2 * GQ + 2 * GV] = dr.astype(bf16)
        g_scr[...] = g_carry

    REG = 2 * GQ + 2 * GV
    return pl.pallas_call(
        body,
        out_shape=(SDS((S, NP), bf16), SDS((S, RANKP), bf16), SDS((RANKP, GQ), f32), SDS((1, GQ), f32), SDS((1, GV), f32)),
        grid=(nT,),
        in_specs=_gla_in_specs(T, rev) + [
            BS((nch * GQ, 128), lambda i: (rev(i), 0)),
            BS((GQ, 128), lambda i: (jnp.maximum(rev(i) * nch - 1, 0), 0)),
            BS((T, GV), lambda i: (rev(i), 1)),
            BS((RANKP, GQ), lambda i: (0, 0)), BS((1, GQ), lambda i: (0, 0)), BS((1, GV), lambda i: (0, 0)), ANY],
        out_specs=(BS((T, REG), lambda i: (rev(i), 0)), BS((T, RANKP), lambda i: (rev(i), 0)),
                   BS((RANKP, GQ), lambda i: (0, 0)), BS((1, GQ), lambda i: (0, 0)), BS((1, GV), lambda i: (0, 0))),
        scratch_shapes=[pltpu.VMEM((GQ, GV), f32)],
        input_output_aliases={11: 0}, compiler_params=_cp("arbitrary"), name=name)(
            proj, proj, proj, proj, pa, states, states, dy, wa, ba, ng, dproj)


def _as2d(a):
    if a.ndim == 1:
        return a.reshape(1, a.shape[0])
    return a.reshape(-1, a.shape[-1])


def adamw(w, g, m, v, name):
    shape = w.shape
    w2, g2, m2, v2 = (_as2d(a) for a in (w, g, m, v))
    R, C = w2.shape
    tr = R
    for cand in (512, 256, 128, 64, 32, 16, 8):
        if R % cand == 0 and cand * C * 4 * 7 * 2 <= 40 * 1024 * 1024:
            tr = cand
            break

    def body(w_ref, g_ref, m_ref, v_ref, d_ref, mo_ref, vo_ref):
        gv = g_ref[...]
        mn = ADAM_B1 * m_ref[...] + (1.0 - ADAM_B1) * gv
        vn = ADAM_B2 * v_ref[...] + (1.0 - ADAM_B2) * (gv * gv)
        m_hat = mn / (1.0 - ADAM_B1 ** ADAM_STEP)
        v_hat = vn / (1.0 - ADAM_B2 ** ADAM_STEP)
        d_ref[...] = -ADAM_LR * (m_hat / (jnp.sqrt(v_hat) + ADAM_EPS) + ADAM_WD * w_ref[...])
        mo_ref[...] = mn
        vo_ref[...] = vn

    blk = BS((tr, C), lambda i: (i, 0))
    outs = pl.pallas_call(body, out_shape=tuple(SDS((R, C), f32) for _ in range(3)), grid=(R // tr,),
                          in_specs=[blk] * 4, out_specs=(blk,) * 3, compiler_params=_cp("parallel"), name=name)(w2, g2, m2, v2)
    return tuple(o.reshape(shape) for o in outs)


def _row_tile(rows, row_bytes, budget=4 * 1024 * 1024):
    best = None
    for t in range(16, rows + 1, 16):
        if rows % t == 0 and t * row_bytes <= budget:
            best = t
    return best or rows


def add_halves(g0, g1, ra, c, name):
    shape = ra.shape
    cols = shape[-1]
    rows = int(np.prod(shape[:-1]))
    tr = _row_tile(rows, cols * 2)
    blk = lambda: BS((tr, cols), lambda i, c_ref: (i, 0))
    grid_spec = pltpu.PrefetchScalarGridSpec(num_scalar_prefetch=1, grid=(rows // tr,), in_specs=[blk(), blk(), blk()],
                                             out_specs=blk())

    def body(c_ref, a0_ref, a1_ref, b_ref, o_ref):
        mine = jnp.where(c_ref[0] == 0, a0_ref[...], a1_ref[...])
        o_ref[...] = (mine.astype(f32) + b_ref[...].astype(f32)).astype(bf16)

    out = pl.pallas_call(body, out_shape=SDS((rows, cols), bf16), grid_spec=grid_spec, compiler_params=_cp("parallel"),
                         name=name)(jnp.reshape(c, (1,)).astype(jnp.int32), g0.reshape(rows, cols), g1.reshape(rows, cols),
                                    ra.reshape(rows, cols))
    return out.reshape(shape)


def reduce_chips(rb, own, c, chip, name):
    shape = rb.shape[1:]
    cols = shape[-1]
    rows = int(np.prod(shape[:-1]))
    tr = _row_tile(rows, cols * 2 * 4)
    rb3, own3 = rb.reshape(4, rows, cols), own.reshape(4, rows, cols)

    def body(s_ref, own_ref, r1, r2, r3, o_ref):
        del s_ref
        o_ref[0] = ((own_ref[0].astype(f32) + r1[0].astype(f32)) + r2[0].astype(f32)) + r3[0].astype(f32)

    def slot(k):
        return BS((1, tr, cols), functools.partial(lambda i, s, k: ((s[1] + k) % 4, i, 0), k=k))

    grid_spec = pltpu.PrefetchScalarGridSpec(
        num_scalar_prefetch=1, grid=(rows // tr,), in_specs=[slot(0), slot(1), slot(2), slot(3)],
        out_specs=BS((1, tr, cols), lambda i, s: (s[0], i, 0)))
    out = pl.pallas_call(body, out_shape=SDS((2, rows, cols), f32), grid_spec=grid_spec, compiler_params=_cp("parallel"),
                         name=name)(jnp.stack([c, chip]).astype(jnp.int32), own3, rb3, rb3, rb3)
    return out.reshape((2,) + shape)


def sum_slots(x, name):
    N, shape = x.shape[0], x.shape[1:]
    cols = shape[-1]
    rows = int(np.prod(shape[:-1]))
    tr = _row_tile(rows, cols * x.dtype.itemsize * N)

    def body(x_ref, o_ref):
        acc = x_ref[0].astype(f32)
        for n in range(1, N):
            acc = acc + x_ref[n].astype(f32)
        o_ref[...] = acc

    out = pl.pallas_call(body, out_shape=SDS((rows, cols), f32), grid=(rows // tr,),
                         in_specs=[BS((N, tr, cols), lambda i: (0, i, 0))], out_specs=BS((tr, cols), lambda i: (i, 0)),
                         compiler_params=_cp("parallel"), name=name)(x.reshape(N, rows, cols))
    return out.reshape(shape)


def _me():
    return lax.axis_index("x"), lax.axis_index("y"), lax.axis_index("c")


def _rcopy(src, dst, send_sems, recv_sems, k, dev):
    return pltpu.make_async_remote_copy(src_ref=src, dst_ref=dst, send_sem=send_sems.at[k], recv_sem=recv_sems.at[k],
                                        device_id=dev, device_id_type=MESH)


def _comm_call(body, ins, out_shapes, n_remote, name, aliases=None):
    return pl.pallas_call(
        body, out_shape=tuple(out_shapes), in_specs=[ANY] * len(ins), out_specs=tuple(ANY for _ in out_shapes),
        scratch_shapes=[pltpu.SemaphoreType.DMA((n_remote,)), pltpu.SemaphoreType.DMA((n_remote,))],
        input_output_aliases=aliases or {}, name=name)(*ins)


def ag4(bufs, name):
    n = len(bufs)

    def body(*refs):
        start, forward, finish = _ag4_phases(refs[:n], refs[n:2 * n], *refs[2 * n:])
        start()
        forward()
        finish()

    return _comm_call(body, bufs, _ag4_out_shapes(bufs), 8 * n, name)


def _ag4_out_shapes(bufs):
    return [SDS((2, 4) + b.shape[1:], b.dtype) for b in bufs]


def _ag4_sems(n):
    return [pltpu.SemaphoreType.DMA((8 * n,)), pltpu.SemaphoreType.DMA((8 * n,))]


def _ag4_phases(xs, os, send_sems, recv_sems):
    n = len(xs)

    def place():
        x, y, c = _me()
        return x, y, c, 2 * x + y, (x, y, 1 - c), [(1 - x, y), (x, 1 - y), (1 - x, 1 - y)]

    def sends():
        x, y, c, j, sib, chips = place()
        first = [_rcopy(xs[t].at[c], os[t].at[c, j], send_sems, recv_sems, 8 * t + k, (cx, cy, c))
                 for t in range(n) for k, (cx, cy) in enumerate(chips)]
        own = [_rcopy(xs[t].at[l], os[t].at[l, j], send_sems, recv_sems, 8 * t + 6 + l, sib) for t in range(n) for l in range(2)]
        return first + own

    def forwards():
        x, y, c, j, sib, chips = place()
        return [(_rcopy(os[t].at[c, 2 * cx + cy], os[t].at[c, 2 * cx + cy], send_sems, recv_sems, 8 * t + k, (x, y, c)),
                 _rcopy(os[t].at[c, 2 * cx + cy], os[t].at[c, 2 * cx + cy], send_sems, recv_sems, 8 * t + 3 + k, sib))
                for k, (cx, cy) in enumerate(chips) for t in range(n)]

    def start():
        for cp in sends():
            cp.start()

    def forward():
        for landed, fwd in forwards():
            landed.wait_recv()
            fwd.start()

    def finish():
        x, y, c, j, sib, chips = place()
        for t in range(n):
            for l in range(2):
                land = os[t].at[l, j]
                _rcopy(land, land, send_sems, recv_sems, 8 * t + 6 + l, (x, y, c)).wait_recv()
        for k, (cx, cy) in enumerate(chips):
            for t in range(n):
                land = os[t].at[1 - c, 2 * cx + cy]
                _rcopy(land, land, send_sems, recv_sems, 8 * t + 3 + k, (x, y, c)).wait_recv()
        for cp in sends() + [fwd for _, fwd in forwards()]:
            cp.wait_send()

    return start, forward, finish


def _ag4_over_grid(xs, os, send_sems, recv_sems, step, nsteps, forward_frac):
    start, forward, finish = _ag4_phases(xs, os, send_sems, recv_sems)
    pl.when(step == 0)(start)
    pl.when(step == min(nsteps - 1, int(nsteps * forward_frac)))(forward)
    pl.when(step == nsteps - 1)(finish)


def sib_other_layer(g0s, g1s, name):
    n = len(g0s)

    def body(*refs):
        layers, os = (refs[:n], refs[n:2 * n]), refs[2 * n:3 * n]
        send_sems, recv_sems = refs[3 * n:]
        x, y, c = _me()
        for mine in range(2):
            @pl.when(c == mine)
            def _():
                cps = [_rcopy(layers[1 - mine][t], os[t], send_sems, recv_sems, t, (x, y, 1 - c)) for t in range(n)]
                for cp in cps:
                    cp.start()
                for cp in cps:
                    cp.wait()

    return _comm_call(body, list(g0s) + list(g1s), [SDS(g.shape, g.dtype) for g in g0s], n, name)


def a2a4(ps, name):
    n = len(ps)

    def body(*refs):
        xs, os = refs[:n], refs[n:2 * n]
        send_sems, recv_sems = refs[2 * n:]
        x, y, c = _me()
        j = 2 * x + y
        chips = [(1 - x, y), (x, 1 - y), (1 - x, 1 - y)]
        sends = [_rcopy(xs[t].at[2 * cx + cy], os[t].at[j], send_sems, recv_sems, 3 * t + k, (cx, cy, c))
                 for t in range(n) for k, (cx, cy) in enumerate(chips)]
        for cp in sends:
            cp.start()
        for t in range(n):
            for k, (cx, cy) in enumerate(chips):
                land = os[t].at[2 * cx + cy]
                _rcopy(land, land, send_sems, recv_sems, 3 * t + k, (x, y, c)).wait_recv()
        for cp in sends:
            cp.wait_send()

    return _comm_call(body, ps, [SDS(p.shape, p.dtype) for p in ps], 3 * n, name)


def ag2(bufs, name):
    n = len(bufs)

    def body(*refs):
        xs, os = refs[:n], refs[n:2 * n]
        send_sems, recv_sems = refs[2 * n:]
        x, y, c = _me()
        cps = [_rcopy(xs[t].at[c], os[t].at[c], send_sems, recv_sems, t, (x, y, 1 - c)) for t in range(n)]
        for cp in cps:
            cp.start()
        for t in range(n):
            land = os[t].at[1 - c]
            _rcopy(land, land, send_sems, recv_sems, t, (x, y, c)).wait_recv()
        for cp in cps:
            cp.wait_send()

    return _comm_call(body, bufs, [SDS(b.shape, b.dtype) for b in bufs], n, name, aliases={t: t for t in range(n)})


def ag8(blk, name):
    m_per, n = blk.shape

    def body(x_ref, out_ref, send_sems, recv_sems, local_sem):
        x, y, c = _me()
        me, sibling = (x, y, c), (x, y, 1 - c)
        chips = [(1 - x, y), (x, 1 - y), (1 - x, 1 - y)]

        def rows(px, py, pc):
            return out_ref.at[pl.ds((4 * px + 2 * py + pc) * m_per, m_per), :]

        def copy(k, block, to, src=None):
            return pltpu.make_async_remote_copy(
                src_ref=rows(*block) if src is None else src, dst_ref=rows(*block), send_sem=send_sems.at[k],
                recv_sem=recv_sems.at[k], device_id=to, device_id_type=MESH)

        mine = pltpu.make_async_copy(x_ref, rows(*me), local_sem)
        mine.start()
        first = [copy(0, me, sibling, src=x_ref)]
        first += [copy(1 + j, me, (*chip, c), src=x_ref) for j, chip in enumerate(chips)]
        for cp in first:
            cp.start()
        passed = [copy(4 + j, (*chip, c), sibling) for j, chip in enumerate(chips)]
        for j, chip in enumerate(chips):
            copy(1 + j, (*chip, c), me).wait_recv()
            passed[j].start()
        copy(0, sibling, me).wait_recv()
        for j, chip in enumerate(chips):
            copy(4 + j, (*chip, 1 - c), me).wait_recv()
        for cp in first + passed:
            cp.wait_send()
        mine.wait()

    return pl.pallas_call(
        body, out_shape=SDS((8 * m_per, n), blk.dtype), in_specs=[pl.BlockSpec(memory_space=pltpu.VMEM)],
        out_specs=pl.BlockSpec(memory_space=pltpu.VMEM),
        scratch_shapes=[pltpu.SemaphoreType.DMA((7,)), pltpu.SemaphoreType.DMA((7,)), pltpu.SemaphoreType.DMA],
        name=name)(blk)


def _split_chips(full, axis):
    n = full.shape[axis] // 4
    parts = full.reshape(full.shape[:axis] + (4, n) + full.shape[axis + 1:])
    return jnp.moveaxis(parts, axis, 0)


def _merge_chips(gathered, axis):
    parts = jnp.moveaxis(gathered, 0, axis)
    return parts.reshape(parts.shape[:axis] + (parts.shape[axis] * parts.shape[axis + 1],) + parts.shape[axis + 2:])


def _to_ref_cols(main, rank):
    pieces = []
    for n, width in REF_SPLITS:
        if n == "g_a":
            pieces.append(rank[..., :RANK])
        else:
            off = OUR_COLS[n][0]
            pieces.append(main[..., off:off + width])
    return jnp.concatenate(pieces, axis=-1)


def _from_ref_cols(w):
    offs, o = {}, 0
    for n, width in REF_SPLITS:
        offs[n] = (o, width)
        o += width
    main = jnp.concatenate([w[..., offs[n][0]:offs[n][0] + offs[n][1]] for n in sorted(OUR_COLS, key=lambda k: OUR_COLS[k][0])],
                           axis=-1)
    ro = offs["g_a"][0]
    rank = jnp.pad(w[..., ro:ro + RANK], [(0, 0)] * (w.ndim - 1) + [(0, RANKP - RANK)])
    return main, rank


def _layer_fwd(h, p_i, W, li, late=None):
    t = f"l{li}_"
    sv = {"h0": h}

    def arrived(names, gathered, Ws):
        for l, Wl in enumerate(Ws):
            Wl.update(_prep_layer_weights(dict(zip(names, gathered)), None, l))

    xn = rms_fwd(h, W["norm1_g"], t + "rms1")
    if late is None:
        proj = mm_nn(xn, W["w_in_main"], name=t + "inproj")
    else:
        (names, shards, Ws) = late[1]
        proj, gathered = mm_nn(xn, W["w_in_main"], name=t + "inproj", gather=shards)
        arrived(names, gathered, Ws)
    pa = mm_nn(xn, W["w_in_rank"], name=t + "inproj_rank")
    y = sg_fwd(proj, W["sg_ln_g"], W["sg_ln_b"], W["sg_wm"], W["sg_bsb"], t + "sg_fwd")
    y, states = gla_fwd(proj, pa, W["gla_wa"], W["gla_b_a"], W["gla_norm_g"], y, t + "gla_fwd")
    if late is None:
        y = att_fwd(proj, W["att_bias"], y, t + "att_fwd")
    else:
        (names, shards, Ws) = late[0]
        y, gathered = att_fwd(proj, W["att_bias"], y, t + "att_fwd", gather=shards)
        arrived(names, gathered, Ws)
    y = conv_fwd(proj, W["conv_dw_w"], W["conv_dw_b"], W["conv_ln_g"], W["conv_ln_b"], y, t + "conv_fwd")
    gate = mm_nn(xn, W["w_gate_all"], bias=W["b_gate_all"], act="sigmoid", name=t + "gate")
    z = mm_nn(y, W["w_branch"], name=t + "branch")
    m = gate_merge_fwd(gate, z, t + "merge")
    h1 = mm_nn(m, W["w_out"], res=h, out_dtype=f32, name=t + "outproj")
    hn = rms_fwd(h1, W["norm2_g"], t + "rms2")
    a = mm_nn(hn, W["w_ff1"], name=t + "ff1")
    h2 = mm_nn(a, W["w_ff2"], pre="relu2", res=h1, out_dtype=f32, name=t + "ff2")
    hg = rms_fwd(h2, W["norm3_g"], t + "rms3")
    pg = mm_nn(hg, W["w_ple_gate"], bias=W["b_ple_gate"], act="sigmoid", name=t + "ple_gate")
    h3, e = mm_nn(p_i, W["w_ple"], mul=pg, res=h2, out_dtype=f32, raw_out=True, name=t + "ple_out")
    sv.update(xn=xn, proj=proj, pa=pa, states=states, y=y, gate=gate, z=z, m=m, h1=h1, hn=hn, a=a, h2=h2, hg=hg, pg=pg, e=e)
    return h3, sv


def _layer_bwd(dh3, sv, p_i, W, li):
    t = f"l{li}_b_"
    G = {}
    dpg, de, G["b_ple_gate"] = ple_bwd_ew(dh3, sv["e"], sv["pg"], t + "ple_ew")
    G["w_ple_gate"] = mm_tn(sv["hg"], dpg, name=t + "dw_ple_gate")[0]
    G["w_ple"] = mm_tn(p_i, de, name=t + "dw_ple")[0]
    dhg = mm_nt(dpg, W["w_ple_gate"], name=t + "dhg")
    dh2, G["norm3_g"] = rms_bwd(dhg, sv["h2"], W["norm3_g"], dh3, t + "rms3")
    da = mm_nt(dh2, W["w_ff2"], post_a=sv["a"], out_dtype=bf16, name=t + "da")
    G["w_ff2"] = mm_tn(sv["a"], dh2, pre="relu2", name=t + "dw_ff2")[0]
    G["w_ff1"] = mm_tn(sv["hn"], da, name=t + "dw_ff1")[0]
    dhn = mm_nt(da, W["w_ff1"], name=t + "dhn")
    dh1, G["norm2_g"] = rms_bwd(dhn, sv["h1"], W["norm2_g"], dh2, t + "rms2")
    dm = mm_nt(dh1, W["w_out"], out_dtype=bf16, name=t + "dm")
    G["w_out"] = mm_tn(sv["m"], dh1, name=t + "dw_out")[0]
    dz, dgp, G["b_gate_all"] = gate_merge_bwd(dm, sv["gate"], sv["z"], t + "merge")
    G["w_branch"] = mm_tn(sv["y"], dz, G=4, name=t + "dw_branch")
    dy = mm_nt(dz, W["w_branch"], out_dtype=bf16, name=t + "dy")
    G["w_gate_all"] = mm_tn(sv["xn"], dgp, name=t + "dw_gate")[0]
    dxn = mm_nt(dgp, W["w_gate_all"], name=t + "dxn_gate")
    proj = sv["proj"]
    dproj, dwm, dbs, G["sg_ln_g"], G["sg_ln_b"] = sg_bwd(proj, dy, W["sg_ln_g"], W["sg_ln_b"], W["sg_wm"], W["sg_bsb"],
                                                          W["sg_maskf"], t + "sg")
    G["sg_w"], G["sg_b"] = dwm, dbs[:, :, 0]
    dproj, dpa, dwa, G["gla_b_a"], G["gla_norm_g"] = gla_bwd(proj, sv["pa"], sv["states"], dy, W["gla_wa"], W["gla_b_a"],
                                                             W["gla_norm_g"], dproj, t + "gla")
    G["gla_w_a2"] = dwa[:RANK]
    dproj, dkp, dvp, dbias = att_bwd(proj, sv["y"], dy, W["att_bias"], dproj, t + "att")
    dproj = att_shift_add(dkp, dvp, dproj, t + "att_kv")
    G["att_rel_bias"] = att_bias_grad(dbias, t + "att_bias")
    dz_c, G["conv_ln_g"], G["conv_ln_b"], G["conv_dw_b"] = conv_bwd_norm(proj, dy, W["conv_dw_w"], W["conv_dw_b"],
                                                                        W["conv_ln_g"], W["conv_ln_b"], t + "conv_norm")
    dproj, G["conv_dw_w"] = conv_bwd_taps(proj, dz_c, W["conv_dw_w"], dproj, t + "conv_taps")
    G["w_in_main"] = mm_tn(sv["xn"], dproj, name=t + "dw_in")[0]
    G["w_in_rank"] = mm_tn(sv["xn"], dpa, name=t + "dw_in_rank")[0]
    dxn = mm_nt(dpa, W["w_in_rank"], res=dxn, name=t + "dxn_rank")
    dxn = mm_nt(dproj, W["w_in_main"], res=dxn, name=t + "dxn_main")
    dh0, G["norm1_g"] = rms_bwd(dxn, sv["h0"], W["norm1_g"], dh1, t + "rms1")
    return dh0, G


def _prep_layer_weights(gathered, repl, li):
    W = {}
    full = {n: _merge_chips(g[li], SHARDED[n][1]) for n, g in gathered.items()}
    if "w_in" in full:
        main, rank = _from_ref_cols(full["w_in"])
        W["w_in_main"], W["w_in_rank"] = main[None], rank[None]
    if "w_branch" in full:
        W["w_branch"] = full["w_branch"]
    if "w_gate" in full:
        W["w_gate_all"] = jnp.transpose(full["w_gate"], (1, 0, 2)).reshape(1, D, 4 * D)
    if "b_gate" in full:
        W["b_gate_all"] = full["b_gate"].reshape(1, 4 * D)
    for n in ("w_out", "w_ff1", "w_ff2", "w_ple_gate", "w_ple"):
        if n in full:
            W[n] = full[n][None]
    if "gla_w_a2" in full:
        W["gla_wa"] = jnp.pad(full["gla_w_a2"], ((0, RANKP - RANK), (0, 0))).astype(bf16)
    if "att_rel_bias" in full:
        W["att_bias"] = att_bias_build(full["att_rel_bias"], f"l{li}_att_bias")
    if "conv_dw_w" in full:
        W["conv_dw_w"] = full["conv_dw_w"]
    if repl is not None:
        for n in ("norm1_g", "norm2_g", "norm3_g", "b_ple_gate", "sg_ln_g", "sg_ln_b", "gla_b_a", "gla_norm_g", "conv_dw_b",
                  "conv_ln_g", "conv_ln_b"):
            W[n] = repl[n][li].reshape(1, -1)
        pos = np.arange(128)
        mask = (pos[None, :] // CHUNK) <= (pos[:, None] // CHUNK)
        W["sg_maskf"] = jnp.asarray(mask, f32)
        W["sg_wm"] = jnp.where(mask[None], repl["sg_w"][li], 0.0).astype(bf16)
        W["sg_bsb"] = jnp.broadcast_to(repl["sg_b"][li][:, :, None], (4, 128, 128))
    return W


def _layer_grads_to_ref(G):
    out = {}
    out["w_in"] = _to_ref_cols(G["w_in_main"], G["w_in_rank"])
    out["w_gate"] = jnp.transpose(G["w_gate_all"].reshape(D, 4, D), (1, 0, 2))
    out["b_gate"] = G["b_gate_all"].reshape(4, D)
    for n in ("w_branch", "w_out", "w_ff1", "w_ff2", "w_ple_gate", "w_ple", "gla_w_a2", "att_rel_bias", "conv_dw_w", "sg_w",
              "sg_b"):
        out[n] = G[n]
    for n in ("norm1_g", "norm2_g", "norm3_g", "b_ple_gate", "sg_ln_g", "sg_ln_b", "gla_b_a", "gla_norm_g", "conv_dw_b",
              "conv_ln_g", "conv_ln_b"):
        out[n] = G[n].reshape(-1)
    return out


def kernel(x, p, norm1_g, w_in, sg_ln_g, sg_ln_b, sg_w, sg_b, gla_w_a2, gla_b_a, gla_norm_g, att_rel_bias, conv_dw_w, conv_dw_b, conv_ln_g, conv_ln_b, w_branch, w_gate, b_gate, w_out, norm2_g, w_ff1, w_ff2, norm3_g, w_ple_gate, b_ple_gate, w_ple, final_g, loss_target, m_norm1_g, m_w_in, m_sg_ln_g, m_sg_ln_b, m_sg_w, m_sg_b, m_gla_w_a2, m_gla_b_a, m_gla_norm_g, m_att_rel_bias, m_conv_dw_w, m_conv_dw_b, m_conv_ln_g, m_conv_ln_b, m_w_branch, m_w_gate, m_b_gate, m_w_out, m_norm2_g, m_w_ff1, m_w_ff2, m_norm3_g, m_w_ple_gate, m_b_ple_gate, m_w_ple, m_final_g, v_norm1_g, v_w_in, v_sg_ln_g, v_sg_ln_b, v_sg_w, v_sg_b, v_gla_w_a2, v_gla_b_a, v_gla_norm_g, v_att_rel_bias, v_conv_dw_w, v_conv_dw_b, v_conv_ln_g, v_conv_ln_b, v_w_branch, v_w_gate, v_b_gate, v_w_out, v_norm2_g, v_w_ff1, v_w_ff2, v_norm3_g, v_w_ple_gate, v_b_ple_gate, v_w_ple, v_final_g):
    args = dict(locals())
    weights = {n: args[n] for n in W_ORDER}
    moments_m = {n: args["m_" + n] for n in W_ORDER}
    moments_v = {n: args["v_" + n] for n in W_ORDER}
    c = lax.axis_index("c")
    sharded_names = BIG + SMALL

    early = ("w_in",) + SMALL
    shards = {n: (weights[n].astype(bf16) if n in BIG else weights[n]) for n in sharded_names}
    gathered = dict(zip(early, ag4([shards[n] for n in early], "ag_weights_early")))
    repl = {n: weights[n] for n in REPL}
    Ws = [_prep_layer_weights(gathered, repl, li) for li in range(DEPTH)]
    in_att, in_proj = ("w_ff1", "w_ff2", "w_ple_gate", "w_ple"), ("w_gate", "w_branch", "w_out")
    late = [(names, [shards[n] for n in names], Ws) for names in (in_att, in_proj)]

    h = x[0]
    saved = []
    for li in range(DEPTH):
        h, sv = _layer_fwd(h, p[li, 0], Ws[li], li, late if li == 0 else None)
        saved.append(sv)
    loss_part, dh, dfinal = loss_head(h, final_g.reshape(1, D), loss_target[0], "loss_head")
    loss = lax.psum(loss_part[0, 0], ("x", "y", "c"))

    layer_grads = [None] * DEPTH
    for li in reversed(range(DEPTH)):
        dh, G = _layer_bwd(dh, saved[li], p[li, 0], Ws[li], li)
        layer_grads[li] = _layer_grads_to_ref(G)
    grad_x = dh[None]

    g0s, g1s = ([_split_chips(layer_grads[li][n], SHARDED[n][1]).astype(bf16) for n in BIG] for li in range(DEPTH))
    ras = sib_other_layer(g0s, g1s, "rs_sibling_layer")
    psums = [add_halves(a0, a1, r, c, "rs_add_" + n) for n, a0, a1, r in zip(BIG, g0s, g1s, ras)]
    rbs = a2a4(psums, "rs_all_to_all")
    chip = 2 * lax.axis_index("x") + lax.axis_index("y")
    reds = [reduce_chips(r, ps, c, chip, "rs_sum_" + n) for n, r, ps in zip(BIG, rbs, psums)]
    grads = dict(zip(BIG, ag2(reds, "rs_sibling_gather")))

    local = {n: jnp.stack([layer_grads[li][n] for li in range(DEPTH)]) for n in tuple(REPL)[:-1] + SMALL}
    local["final_g"] = dfinal.reshape(D)
    rnames = tuple(REPL) + SMALL
    rflat = jnp.concatenate([local[n].reshape(-1) for n in rnames])
    rflat = jnp.pad(rflat, (0, REPL_ROWS * PACK_W - rflat.shape[0])).reshape(REPL_ROWS, PACK_W)
    rall = ag8(rflat, "ar_gather").reshape(8, REPL_ROWS, PACK_W)
    rsum = sum_slots(rall, "ar_sum").reshape(-1)
    off = 0
    for n in rnames:
        shape = local[n].shape
        size = int(np.prod(shape))
        g = rsum[off:off + size].reshape(shape)
        off += size
        if n in SMALL:
            ax = SHARDED[n][1] + 1
            g = lax.dynamic_slice_in_dim(g, chip * (shape[ax] // 4), shape[ax] // 4, axis=ax)
        grads[n] = g

    deltas, new_m, new_v = {}, {}, {}
    for n in W_ORDER:
        deltas[n], new_m[n], new_v[n] = adamw(weights[n], grads[n], moments_m[n], moments_v[n], "adamw_" + n)
    return (loss, grad_x, *[grads[n] for n in W_ORDER], *[deltas[n] for n in W_ORDER], *[new_m[n] for n in W_ORDER],
            *[new_v[n] for n in W_ORDER])
```

```python
import functools

import jax
import jax.numpy as jnp
import numpy as np
from jax import lax
from jax.experimental import pallas as pl
from jax.experimental.pallas import tpu as pltpu

f32, bf16 = jnp.float32, jnp.bfloat16
HI = lax.Precision.HIGHEST
MESH = pl.DeviceIdType.MESH
SDS = jax.ShapeDtypeStruct
BS = pl.BlockSpec
ANY = pl.BlockSpec(memory_space=pl.ANY)

D = 1024
DEPTH = 2
CHUNK = 64
BW = 512
NP = 5120
RANK = 16
RANKP = 128
DFF = 4096
PLE = 256
CONV_K = 31
HALO = 32
TQ = 256
WIN = 768
REL_TABLE = 320
EPS = 1e-6
NEG_INF = -1e30
VMEM_LIMIT = 56 * 1024 * 1024

ADAM_LR, ADAM_B1, ADAM_B2, ADAM_EPS, ADAM_WD, ADAM_STEP = 0.001, 0.9, 0.999, 1e-08, 0.01, 10

OUR_COLS = dict(g_q=(0, 256), g_k=(256, 256), g_v=(512, 512), g_r=(1024, 512), a_q=(1536, 512), a_k=(2048, 512),
                a_v=(2560, 512), sg_u=(3072, 512), sg_v=(3584, 512), c_a=(4096, 512), c_g=(4608, 512))
REF_SPLITS = (("sg_u", 512), ("sg_v", 512), ("g_q", 256), ("g_k", 256), ("g_v", 512), ("g_r", 512), ("g_a", 16),
              ("a_q", 512), ("a_k", 512), ("a_v", 512), ("c_a", 512), ("c_g", 512))

SHARDED = dict(w_in=((1024, 5136), 1), w_branch=((4, 512, 1024), 2), w_gate=((4, 1024, 1024), 1), w_out=((1024, 1024), 0),
               w_ff1=((1024, 4096), 1), w_ff2=((4096, 1024), 0), w_ple_gate=((1024, 1024), 0), w_ple=((256, 1024), 1),
               gla_w_a2=((16, 256), 1), att_rel_bias=((8, 320), 1), conv_dw_w=((31, 512), 1), b_gate=((4, 1024), 1))
BIG = ("w_in", "w_branch", "w_gate", "w_out", "w_ff1", "w_ff2", "w_ple_gate", "w_ple")
SMALL = ("gla_w_a2", "att_rel_bias", "conv_dw_w", "b_gate")
REPL = dict(norm1_g=(2, 1024), sg_ln_g=(2, 512), sg_ln_b=(2, 512), sg_w=(2, 4, 128, 128), sg_b=(2, 4, 128), gla_b_a=(2, 256),
            gla_norm_g=(2, 512), conv_dw_b=(2, 512), conv_ln_g=(2, 512), conv_ln_b=(2, 512), norm2_g=(2, 1024),
            norm3_g=(2, 1024), b_ple_gate=(2, 1024), final_g=(1024,))
W_ORDER = ['norm1_g', 'w_in', 'sg_ln_g', 'sg_ln_b', 'sg_w', 'sg_b', 'gla_w_a2', 'gla_b_a', 'gla_norm_g', 'att_rel_bias',
           'conv_dw_w', 'conv_dw_b', 'conv_ln_g', 'conv_ln_b', 'w_branch', 'w_gate', 'b_gate', 'w_out', 'norm2_g', 'w_ff1',
           'w_ff2', 'norm3_g', 'w_ple_gate', 'b_ple_gate', 'w_ple', 'final_g']
PACK_W = 1024
REPL_ROWS = 200
AG_FORWARD_AT_END = 1.0
AG_FORWARD_INPROJ = 0.875
AG_FORWARD_ATTENTION = 0.75
TM = 1024
MM_VMEM_BUDGET = 44 * 1024 * 1024


def _tile(s):
    return 512 if s % 512 == 0 else s


def _token_tile(S, row_bytes, fixed_bytes):
    for t in (2048, 1024):
        if S % t == 0 and 2 * (t * row_bytes + fixed_bytes) <= MM_VMEM_BUDGET:
            return t
    return min(TM, S)


def _cp(*sem):
    return pltpu.CompilerParams(dimension_semantics=sem, vmem_limit_bytes=VMEM_LIMIT)


def rms_fwd(h, g, name, gather=None):
    S, Dm = h.shape
    T = _tile(S)
    ng = len(gather) if gather else 0

    def body(h_ref, g_ref, *rest):
        o_ref = rest[ng]
        if ng:
            _ag4_over_grid(rest[:ng], rest[ng + 1:2 * ng + 1], *rest[2 * ng + 1:], pl.program_id(0), S // T,
                           AG_FORWARD_AT_END)
        x = h_ref[...]
        r = lax.rsqrt(jnp.mean(x * x, axis=-1, keepdims=True) + EPS)
        o_ref[...] = (x * r * g_ref[...]).astype(bf16)

    outs = pl.pallas_call(
        body, out_shape=(SDS((S, Dm), bf16),) + tuple(_ag4_out_shapes(gather or [])), grid=(S // T,),
        in_specs=[BS((T, Dm), lambda i: (i, 0)), BS((1, Dm), lambda i: (0, 0))] + [ANY] * ng,
        out_specs=(BS((T, Dm), lambda i: (i, 0)),) + tuple(ANY for _ in range(ng)),
        scratch_shapes=_ag4_sems(ng) if ng else [], compiler_params=_cp("arbitrary" if ng else "parallel"),
        name=name)(h, g, *(gather or []))
    return (outs[0], list(outs[1:])) if ng else outs[0]


def mm_nn(x, w, *, name, bias=None, act=None, pre=None, mul=None, res=None, out_dtype=bf16, raw_out=False, gather=None):
    S = x.shape[0]
    G, K, N = w.shape
    tn = min(1024 if K <= 1024 else 512, N)
    nj = N // tn
    row_bytes = K * x.dtype.itemsize + tn * (jnp.dtype(out_dtype).itemsize + (2 if raw_out else 0)
                                             + sum(a.dtype.itemsize for a in (mul, res) if a is not None))
    T = _token_tile(S, row_bytes, K * tn * 2)
    extras = [a for a in (bias, mul, res) if a is not None]
    ng = len(gather) if gather else 0
    grid = (S // T, G, nj)

    def body(*refs):
        it = iter(refs)
        x_ref, w_ref = next(it), next(it)
        b_ref = next(it) if bias is not None else None
        m_ref = next(it) if mul is not None else None
        r_ref = next(it) if res is not None else None
        g_in = [next(it) for _ in range(ng)]
        o_ref = next(it)
        raw_ref = next(it) if raw_out else None
        if ng:
            g_out = [next(it) for _ in range(ng)]
            step = (pl.program_id(0) * G + pl.program_id(1)) * nj + pl.program_id(2)
            _ag4_over_grid(g_in, g_out, next(it), next(it), step, grid[0] * G * nj, AG_FORWARD_INPROJ)
        xv = x_ref[...]
        if pre == "relu2":
            xf = jnp.maximum(xv.astype(f32), 0.0)
            xv = xf * xf
        acc = jnp.dot(xv.astype(bf16), w_ref[0], preferred_element_type=f32)
        if raw_out:
            raw_ref[...] = acc.astype(bf16)
        if b_ref is not None:
            acc = acc + b_ref[...]
        if act == "sigmoid":
            acc = jax.nn.sigmoid(acc)
        if m_ref is not None:
            acc = acc * m_ref[...].astype(f32)
        if r_ref is not None:
            acc = r_ref[...].astype(f32) + acc
        o_ref[...] = acc.astype(out_dtype)

    in_specs = [BS((T, K), lambda i, g, j: (i, g)), BS((1, K, tn), lambda i, g, j: (g, 0, j))]
    if bias is not None:
        in_specs.append(BS((1, tn), lambda i, g, j: (0, g * nj + j)))
    for a in (mul, res):
        if a is not None:
            in_specs.append(BS((T, tn), lambda i, g, j: (i, g * nj + j)))
    ospec = BS((T, tn), lambda i, g, j: (i, g * nj + j))
    out_shape = SDS((S, G * N), out_dtype)
    if raw_out:
        out_shape, ospec = (out_shape, SDS((S, G * N), bf16)), (ospec, ospec)
    if not ng:
        return pl.pallas_call(
            body, out_shape=out_shape, grid=grid, in_specs=in_specs, out_specs=ospec,
            compiler_params=_cp("parallel", "parallel", "parallel"), name=name)(x, w, *extras)
    out_shape = (out_shape if raw_out else (out_shape,)) + tuple(_ag4_out_shapes(gather))
    ospec = (ospec if raw_out else (ospec,)) + tuple(ANY for _ in gather)
    outs = pl.pallas_call(
        body, out_shape=out_shape, grid=grid, in_specs=in_specs + [ANY] * ng, out_specs=ospec,
        scratch_shapes=_ag4_sems(ng), compiler_params=_cp("arbitrary", "arbitrary", "arbitrary"), name=name)(
            x, w, *extras, *gather)
    nres = 2 if raw_out else 1
    return (outs[0] if nres == 1 else outs[:2]), list(outs[nres:])


def mm_nt(dy, w, *, name, res=None, post_a=None, out_dtype=f32):
    S = dy.shape[0]
    G, K, N = w.shape
    tk = min(1024 if N <= 1024 else 512, K)
    nk = K // tk
    row_bytes = N * dy.dtype.itemsize + tk * (jnp.dtype(out_dtype).itemsize
                                              + sum(a.dtype.itemsize for a in (res, post_a) if a is not None))
    T = _token_tile(S, row_bytes, tk * N * 2)
    extras = [a for a in (res, post_a) if a is not None]

    def body(*refs):
        it = iter(refs)
        d_ref, w_ref = next(it), next(it)
        r_ref = next(it) if res is not None else None
        a_ref = next(it) if post_a is not None else None
        o_ref = next(it)
        acc = lax.dot_general(d_ref[...].astype(bf16), w_ref[0], (((1,), (1,)), ((), ())), preferred_element_type=f32)
        if r_ref is not None:
            acc = acc + r_ref[...].astype(f32)
        if a_ref is not None:
            acc = acc * (2.0 * jnp.maximum(a_ref[...].astype(f32), 0.0))
        o_ref[...] = acc.astype(out_dtype)

    in_specs = [BS((T, N), lambda i, g, j: (i, g)), BS((1, tk, N), lambda i, g, j: (g, j, 0))]
    for a in extras:
        in_specs.append(BS((T, tk), lambda i, g, j: (i, g * nk + j)))
    return pl.pallas_call(
        body, out_shape=SDS((S, G * K), out_dtype), grid=(S // T, G, nk), in_specs=in_specs,
        out_specs=BS((T, tk), lambda i, g, j: (i, g * nk + j)),
        compiler_params=_cp("parallel", "parallel", "parallel"), name=name)(dy, w, *extras)


def mm_tn(x, dy, *, name, G=1, pre=None, ts=1024):
    S = x.shape[0]
    K, N = x.shape[1] // G, dy.shape[1] // G
    tk, tn = min(K, 1024), min(N, 1024)
    nk, nn = K // tk, N // tn
    ts = min(ts, S)

    def body(x_ref, d_ref, o_ref):
        @pl.when(pl.program_id(3) == 0)
        def _():
            o_ref[...] = jnp.zeros_like(o_ref)

        xv = x_ref[...]
        if pre == "relu2":
            xf = jnp.maximum(xv.astype(f32), 0.0)
            xv = xf * xf
        o_ref[0] += lax.dot_general(xv.astype(bf16), d_ref[...].astype(bf16), (((0,), (0,)), ((), ())),
                                    preferred_element_type=f32)

    return pl.pallas_call(
        body, out_shape=SDS((G, K, N), f32), grid=(G, nk, nn, S // ts),
        in_specs=[BS((ts, tk), lambda g, a, b, s: (s, g * nk + a)), BS((ts, tn), lambda g, a, b, s: (s, g * nn + b))],
        out_specs=BS((1, tk, tn), lambda g, a, b, s: (g, a, b)),
        compiler_params=_cp("parallel", "parallel", "parallel", "arbitrary"), name=name)(x, dy)


def rms_bwd(dxn, x, g, dres, name):
    S, Dm = x.shape
    T = _tile(S)

    def body(*refs):
        if dres is not None:
            d_ref, x_ref, g_ref, r_ref, dx_ref, dg_ref = refs
        else:
            d_ref, x_ref, g_ref, dx_ref, dg_ref = refs
        xv = x_ref[...]
        d = d_ref[...].astype(f32)
        r = lax.rsqrt(jnp.mean(xv * xv, axis=-1, keepdims=True) + EPS)
        u = d * g_ref[...]
        dx = r * u - xv * ((r * r * r) * (1.0 / Dm)) * jnp.sum(u * xv, axis=-1, keepdims=True)
        if dres is not None:
            dx = r_ref[...] + dx
        dx_ref[...] = dx

        @pl.when(pl.program_id(0) == 0)
        def _():
            dg_ref[...] = jnp.zeros_like(dg_ref)

        dg_ref[...] += jnp.sum(d * xv * r, axis=0, keepdims=True)

    tok = BS((T, Dm), lambda i: (i, 0))
    vec = BS((1, Dm), lambda i: (0, 0))
    args = (dxn, x, g) + ((dres,) if dres is not None else ())
    return pl.pallas_call(
        body, out_shape=(SDS((S, Dm), f32), SDS((1, Dm), f32)), grid=(S // T,),
        in_specs=[tok, tok, vec] + ([tok] if dres is not None else []), out_specs=(tok, vec),
        compiler_params=_cp("arbitrary"), name=name)(*args)


def loss_head(h, g, target, name):
    S, Dm = h.shape
    T = _tile(S)

    def body(h_ref, g_ref, t_ref, loss_ref, dh_ref, dg_ref):
        @pl.when(pl.program_id(0) == 0)
        def _():
            loss_ref[...] = jnp.zeros_like(loss_ref)
            dg_ref[...] = jnp.zeros_like(dg_ref)

        xv = h_ref[...]
        gv = g_ref[...]
        r = lax.rsqrt(jnp.mean(xv * xv, axis=-1, keepdims=True) + EPS)
        diff = xv * r * gv - t_ref[...]
        loss_ref[...] += 0.5 * jnp.sum(jnp.mean(diff * diff, axis=-1, keepdims=True))
        d = diff * (1.0 / Dm)
        u = d * gv
        dh_ref[...] = r * u - xv * ((r * r * r) * (1.0 / Dm)) * jnp.sum(u * xv, axis=-1, keepdims=True)
        dg_ref[...] += jnp.sum(d * xv * r, axis=0, keepdims=True)

    tok = BS((T, Dm), lambda i: (i, 0))
    vec = BS((1, Dm), lambda i: (0, 0))
    return pl.pallas_call(
        body, out_shape=(SDS((1, 128), f32), SDS((S, Dm), f32), SDS((1, Dm), f32)), grid=(S // T,),
        in_specs=[tok, vec, tok], out_specs=(BS((1, 128), lambda i: (0, 0)), tok, vec),
        compiler_params=_cp("arbitrary"), name=name)(h, g, target)


def gate_merge_fwd(gate, z, name):
    S = gate.shape[0]
    T = _tile(S)

    def body(g_ref, z_ref, o_ref):
        acc = jnp.zeros((T, D), f32)
        for n in range(4):
            acc = acc + g_ref[:, n * D:(n + 1) * D].astype(f32) * z_ref[:, n * D:(n + 1) * D].astype(f32)
        o_ref[...] = acc.astype(bf16)

    wide = BS((T, 4 * D), lambda i: (i, 0))
    return pl.pallas_call(body, out_shape=SDS((S, D), bf16), grid=(S // T,), in_specs=[wide, wide],
                          out_specs=BS((T, D), lambda i: (i, 0)), compiler_params=_cp("parallel"), name=name)(gate, z)


def gate_merge_bwd(dm, gate, z, name):
    S = gate.shape[0]
    T = _tile(S)

    def body(dm_ref, g_ref, z_ref, dz_ref, dg_ref, db_ref):
        @pl.when(pl.program_id(0) == 0)
        def _():
            db_ref[...] = jnp.zeros_like(db_ref)

        dmv = dm_ref[...].astype(f32)
        for n in range(4):
            cols = slice(n * D, (n + 1) * D)
            gv = g_ref[:, cols].astype(f32)
            dz_ref[:, cols] = (dmv * gv).astype(bf16)
            dgp = dmv * z_ref[:, cols].astype(f32) * gv * (1.0 - gv)
            dg_ref[:, cols] = dgp.astype(bf16)
            db_ref[:, cols] += jnp.sum(dgp, axis=0, keepdims=True)

    wide = BS((T, 4 * D), lambda i: (i, 0))
    return pl.pallas_call(
        body, out_shape=(SDS((S, 4 * D), bf16), SDS((S, 4 * D), bf16), SDS((1, 4 * D), f32)), grid=(S // T,),
        in_specs=[BS((T, D), lambda i: (i, 0)), wide, wide], out_specs=(wide, wide, BS((1, 4 * D), lambda i: (0, 0))),
        compiler_params=_cp("arbitrary"), name=name)(dm, gate, z)


def ple_bwd_ew(dh, e, pg, name):
    S = dh.shape[0]
    T = _tile(S)

    def body(dh_ref, e_ref, pg_ref, dp_ref, de_ref, db_ref):
        @pl.when(pl.program_id(0) == 0)
        def _():
            db_ref[...] = jnp.zeros_like(db_ref)

        d = dh_ref[...]
        g = pg_ref[...].astype(f32)
        dpre = d * e_ref[...].astype(f32) * g * (1.0 - g)
        dp_ref[...] = dpre.astype(bf16)
        de_ref[...] = (d * g).astype(bf16)
        db_ref[...] += jnp.sum(dpre, axis=0, keepdims=True)

    tok = BS((T, D), lambda i: (i, 0))
    return pl.pallas_call(
        body, out_shape=(SDS((S, D), bf16), SDS((S, D), bf16), SDS((1, D), f32)), grid=(S // T,),
        in_specs=[tok, tok, tok], out_specs=(tok, tok, BS((1, D), lambda i: (0, 0))),
        compiler_params=_cp("arbitrary"), name=name)(dh, e, pg)


_GK = 0.7978845608028654
_GC = 0.044715


def _gelu(x):
    return 0.5 * x * (1.0 + jnp.tanh(_GK * (x + _GC * (x * x * x))))


def _gelu_grad(x):
    x2 = x * x
    t = jnp.tanh(_GK * (x + _GC * (x * x2)))
    return 0.5 * (1.0 + t) + 0.5 * x * (1.0 - t * t) * (_GK * (1.0 + 3.0 * _GC * x2))


def _ln_stats(v):
    mu = jnp.mean(v, axis=-1, keepdims=True)
    vc = v - mu
    rs = lax.rsqrt(jnp.mean(vc * vc, axis=-1, keepdims=True) + EPS)
    return vc * rs, rs


def _ln_bwd(dvh, vh, rs):
    return rs * (dvh - jnp.mean(dvh, axis=-1, keepdims=True) - vh * jnp.mean(dvh * vh, axis=-1, keepdims=True))


def sg_fwd(proj, lg, lb, wm, bsb, name):
    S = proj.shape[0]
    T = _tile(S)
    cu, cv = OUR_COLS["sg_u"][0] // BW, OUR_COLS["sg_v"][0] // BW

    def body(u_ref, v_ref, lg_ref, lb_ref, wm_ref, bsb_ref, o_ref):
        for b in range(T // 128):
            rows = slice(b * 128, (b + 1) * 128)
            u = _gelu(u_ref[rows, :].astype(f32))
            vh, _ = _ln_stats(_gelu(v_ref[rows, :].astype(f32)))
            vb = (vh * lg_ref[...] + lb_ref[...]).astype(bf16)
            outs = []
            for g in range(4):
                cols = slice(g * 128, (g + 1) * 128)
                mixed = jnp.dot(wm_ref[g], vb[:, cols], preferred_element_type=f32) + bsb_ref[g]
                outs.append(u[:, cols] * mixed)
            o_ref[rows, :] = jnp.concatenate(outs, axis=1).astype(bf16)

    vec = BS((1, BW), lambda i: (0, 0))
    cube = BS((4, 128, 128), lambda i: (0, 0, 0))
    return pl.pallas_call(
        body, out_shape=SDS((S, 4 * BW), bf16), grid=(S // T,),
        in_specs=[BS((T, BW), lambda i: (i, cu)), BS((T, BW), lambda i: (i, cv)), vec, vec, cube, cube],
        out_specs=BS((T, BW), lambda i: (i, 0)), compiler_params=_cp("parallel"), name=name)(proj, proj, lg, lb, wm, bsb)


def sg_bwd(proj, dy, lg, lb, wm, bsb, maskf, name):
    S = proj.shape[0]
    T = _tile(S)
    cu, cv = OUR_COLS["sg_u"][0] // BW, OUR_COLS["sg_v"][0] // BW
    creg = OUR_COLS["sg_u"][0] // (2 * BW)

    def body(u_ref, v_ref, dy_ref, lg_ref, lb_ref, wm_ref, bsb_ref, mk_ref, dp_ref, dwm_ref, dbs_ref, dlg_ref, dlb_ref):
        @pl.when(pl.program_id(0) == 0)
        def _():
            dwm_ref[...] = jnp.zeros_like(dwm_ref)
            dbs_ref[...] = jnp.zeros_like(dbs_ref)
            dlg_ref[...] = jnp.zeros_like(dlg_ref)
            dlb_ref[...] = jnp.zeros_like(dlb_ref)

        for b in range(T // 128):
            rows = slice(b * 128, (b + 1) * 128)
            su = u_ref[rows, :].astype(f32)
            sv = v_ref[rows, :].astype(f32)
            dya = dy_ref[rows, :].astype(f32)
            u = _gelu(su)
            vh, rs = _ln_stats(_gelu(sv))
            vb = (vh * lg_ref[...] + lb_ref[...]).astype(bf16)
            dus, dvls = [], []
            for g in range(4):
                cols = slice(g * 128, (g + 1) * 128)
                mixed = jnp.dot(wm_ref[g], vb[:, cols], preferred_element_type=f32) + bsb_ref[g]
                dus.append(dya[:, cols] * mixed)
                dmg = dya[:, cols] * u[:, cols]
                dmb = dmg.astype(bf16)
                dbs_ref[g] += jnp.broadcast_to(jnp.sum(dmg, axis=1, keepdims=True), (128, 128))
                dwm_ref[g] += mk_ref[...] * lax.dot_general(dmb, vb[:, cols], (((1,), (1,)), ((), ())),
                                                            preferred_element_type=f32)
                dvls.append(lax.dot_general(wm_ref[g], dmb, (((0,), (0,)), ((), ())), preferred_element_type=f32))
            du = jnp.concatenate(dus, axis=1)
            dvln = jnp.concatenate(dvls, axis=1)
            dlg_ref[...] += jnp.sum(dvln * vh, axis=0, keepdims=True)
            dlb_ref[...] += jnp.sum(dvln, axis=0, keepdims=True)
            dv = _ln_bwd(dvln * lg_ref[...], vh, rs)
            dp_ref[rows, 0:BW] = (du * _gelu_grad(su)).astype(bf16)
            dp_ref[rows, BW:2 * BW] = (dv * _gelu_grad(sv)).astype(bf16)

    vec = BS((1, BW), lambda i: (0, 0))
    cube = BS((4, 128, 128), lambda i: (0, 0, 0))
    return pl.pallas_call(
        body,
        out_shape=(SDS((S, NP), bf16), SDS((4, 128, 128), f32), SDS((4, 128, 128), f32), SDS((1, BW), f32), SDS((1, BW), f32)),
        grid=(S // T,),
        in_specs=[BS((T, BW), lambda i: (i, cu)), BS((T, BW), lambda i: (i, cv)), BS((T, BW), lambda i: (i, 0)), vec, vec,
                  cube, cube, BS((128, 128), lambda i: (0, 0))],
        out_specs=(BS((T, 2 * BW), lambda i: (i, creg)), cube, cube, vec, vec),
        compiler_params=_cp("arbitrary"), name=name)(proj, proj, dy, lg, lb, wm, bsb, maskf)


_SUB = 64


def _conv_specs(S, T):
    ca, cg = OUR_COLS["c_a"][0] // BW, OUR_COLS["c_g"][0] // BW
    hb = T // HALO
    prev = lambda i: jnp.maximum(i * hb - 1, 0)
    return [BS((T, BW), lambda i: (i, ca)), BS((T, BW), lambda i: (i, cg)),
            BS((HALO, BW), lambda i: (prev(i), ca)), BS((HALO, BW), lambda i: (prev(i), cg))]


def _fill_shifts(sh):
    n = sh.shape[1] - 8
    for s in range(1, 8):
        sh[s, pl.ds(0, n), :] = sh[0, pl.ds(s, n), :]


def _shifted(sh, off, rows):
    s = off % 8
    return sh[s, pl.ds(off - s, rows), :]


def _conv_fill_ybuf(a_ref, g_ref, ap_ref, gp_ref, ysh):
    T = a_ref.shape[0]
    ysh[0, pl.ds(HALO, T), :] = a_ref[...].astype(f32) * jax.nn.sigmoid(g_ref[...].astype(f32))
    first = (pl.program_id(0) == 0).astype(f32)
    ysh[0, pl.ds(0, HALO), :] = (1.0 - first) * (ap_ref[...].astype(f32) * jax.nn.sigmoid(gp_ref[...].astype(f32)))
    _fill_shifts(ysh)


def _conv_taps(w_ref, ysh, r0):
    acc = jnp.zeros((_SUB, BW), f32)
    for k in range(CONV_K):
        acc = acc + w_ref[k:k + 1, :] * _shifted(ysh, r0 + HALO - (CONV_K - 1) + k, _SUB)
    return acc


def conv_fwd(proj, w, b, lg, lb, y, name):
    S = proj.shape[0]
    T = _tile(S)

    def body(a_ref, g_ref, ap_ref, gp_ref, w_ref, b_ref, lg_ref, lb_ref, y_in, o_ref, ybuf):
        del y_in
        _conv_fill_ybuf(a_ref, g_ref, ap_ref, gp_ref, ybuf)
        for sb in range(T // _SUB):
            z = _conv_taps(w_ref, ybuf, sb * _SUB) + b_ref[...]
            zh, _ = _ln_stats(z)
            zl = zh * lg_ref[...] + lb_ref[...]
            o_ref[pl.ds(sb * _SUB, _SUB), :] = (zl * jax.nn.sigmoid(zl)).astype(bf16)

    vec = BS((1, BW), lambda i: (0, 0))
    return pl.pallas_call(
        body, out_shape=SDS(y.shape, bf16), grid=(S // T,),
        in_specs=_conv_specs(S, T) + [BS((CONV_K, BW), lambda i: (0, 0)), vec, vec, vec, ANY],
        out_specs=BS((T, BW), lambda i: (i, 3)), scratch_shapes=[pltpu.VMEM((8, T + HALO, BW), f32)],
        input_output_aliases={8: 0}, compiler_params=_cp("parallel"), name=name)(proj, proj, proj, proj, w, b, lg, lb, y)


def conv_bwd_norm(proj, dy, w, b, lg, lb, name):
    S = proj.shape[0]
    T = _tile(S)

    def body(a_ref, g_ref, ap_ref, gp_ref, dy_ref, w_ref, b_ref, lg_ref, lb_ref, dz_ref, dlg_ref, dlb_ref, db_ref, ybuf):
        @pl.when(pl.program_id(0) == 0)
        def _():
            dlg_ref[...] = jnp.zeros_like(dlg_ref)
            dlb_ref[...] = jnp.zeros_like(dlb_ref)
            db_ref[...] = jnp.zeros_like(db_ref)

        _conv_fill_ybuf(a_ref, g_ref, ap_ref, gp_ref, ybuf)
        for sb in range(T // _SUB):
            rows = pl.ds(sb * _SUB, _SUB)
            z = _conv_taps(w_ref, ybuf, sb * _SUB) + b_ref[...]
            zh, rs = _ln_stats(z)
            zl = zh * lg_ref[...] + lb_ref[...]
            sg = jax.nn.sigmoid(zl)
            dzl = dy_ref[rows, :].astype(f32) * sg * (1.0 + zl * (1.0 - sg))
            dlg_ref[...] += jnp.sum(dzl * zh, axis=0, keepdims=True)
            dlb_ref[...] += jnp.sum(dzl, axis=0, keepdims=True)
            dz = _ln_bwd(dzl * lg_ref[...], zh, rs)
            db_ref[...] += jnp.sum(dz, axis=0, keepdims=True)
            dz_ref[rows, :] = dz

    vec = BS((1, BW), lambda i: (0, 0))
    tok = BS((T, BW), lambda i: (i, 0))
    return pl.pallas_call(
        body, out_shape=(SDS((S, BW), f32), SDS((1, BW), f32), SDS((1, BW), f32), SDS((1, BW), f32)), grid=(S // T,),
        in_specs=_conv_specs(S, T) + [BS((T, BW), lambda i: (i, 3)), BS((CONV_K, BW), lambda i: (0, 0)), vec, vec, vec],
        out_specs=(tok, vec, vec, vec), scratch_shapes=[pltpu.VMEM((8, T + HALO, BW), f32)],
        compiler_params=_cp("arbitrary"), name=name)(proj, proj, proj, proj, dy, w, b, lg, lb)


def conv_bwd_taps(proj, dz, w, dproj, name):
    S = proj.shape[0]
    T = _tile(S)
    nT = S // T
    hb = T // HALO
    creg = OUR_COLS["c_a"][0] // (2 * BW)

    def body(a_ref, g_ref, ap_ref, gp_ref, dz_ref, dzn_ref, w_ref, dp_in, dp_ref, dw_ref, ybuf, dzbuf, dwacc):
        del dp_in
        i = pl.program_id(0)

        @pl.when(i == 0)
        def _():
            dwacc[...] = jnp.zeros_like(dwacc)

        _conv_fill_ybuf(a_ref, g_ref, ap_ref, gp_ref, ybuf)
        dzbuf[0, pl.ds(0, T), :] = dz_ref[...]
        dzbuf[0, pl.ds(T, HALO), :] = (i < nT - 1).astype(f32) * dzn_ref[...]
        _fill_shifts(dzbuf)
        for sb in range(T // _SUB):
            r0 = sb * _SUB
            rows = pl.ds(r0, _SUB)
            dzs = dz_ref[rows, :]
            dyg = jnp.zeros((_SUB, BW), f32)
            for k in range(CONV_K):
                ysl = _shifted(ybuf, r0 + HALO - (CONV_K - 1) + k, _SUB)
                dwacc[pl.ds(k * 8, 8), :] += jnp.sum((dzs * ysl).reshape(_SUB // 8, 8, BW), axis=0)
                dyg = dyg + w_ref[k:k + 1, :] * _shifted(dzbuf, r0 + (CONV_K - 1) - k, _SUB)
            av = a_ref[rows, :].astype(f32)
            sg = jax.nn.sigmoid(g_ref[rows, :].astype(f32))
            dp_ref[rows, 0:BW] = (dyg * sg).astype(bf16)
            dp_ref[rows, BW:2 * BW] = (dyg * av * sg * (1.0 - sg)).astype(bf16)

        @pl.when(i == nT - 1)
        def _():
            for k in range(CONV_K):
                dw_ref[k:k + 1, :] = jnp.sum(dwacc[pl.ds(k * 8, 8), :], axis=0, keepdims=True)

    nxt = lambda i: jnp.minimum((i + 1) * hb, S // HALO - 1)
    return pl.pallas_call(
        body, out_shape=(SDS((S, NP), bf16), SDS((CONV_K, BW), f32)), grid=(nT,),
        in_specs=_conv_specs(S, T) + [BS((T, BW), lambda i: (i, 0)), BS((HALO, BW), lambda i: (nxt(i), 0)),
                                      BS((CONV_K, BW), lambda i: (0, 0)), ANY],
        out_specs=(BS((T, 2 * BW), lambda i: (i, creg)), BS((CONV_K, BW), lambda i: (0, 0))),
        scratch_shapes=[pltpu.VMEM((8, T + HALO, BW), f32), pltpu.VMEM((8, T + HALO, BW), f32),
                        pltpu.VMEM((CONV_K * 8, BW), f32)],
        input_output_aliases={7: 0}, compiler_params=_cp("arbitrary"), name=name)(proj, proj, proj, proj, dz, dz, w, dproj)


def _toeplitz_index():
    j = lax.broadcasted_iota(jnp.int32, (REL_TABLE, 1024), 1)
    t = lax.broadcasted_iota(jnp.int32, (REL_TABLE, 1024), 0)
    e = ((WIN - 1) - j) & 1023
    tidx = jnp.clip(e - (TQ - 1), -(CHUNK - 1), 256) + (CHUNK - 1)
    return (tidx == t).astype(f32)


def att_bias_build(table, name):
    H = table.shape[0]

    def body(t_ref, o_ref):
        u = jnp.dot(t_ref[...], _toeplitz_index(), precision=HI, preferred_element_type=f32)
        row = lax.broadcasted_iota(jnp.int32, (TQ, 1024), 0)
        r = lax.broadcasted_iota(jnp.int32, (TQ, WIN), 0)
        n = lax.broadcasted_iota(jnp.int32, (TQ, WIN), 1)
        dchunk = (r // CHUNK + 8) - n // CHUNK
        band = (dchunk >= 0) & (dchunk <= 8)
        for h in range(H):
            x = jnp.broadcast_to(u[h:h + 1, :], (TQ, 1024))
            for b in range(8):
                x = jnp.where(((row >> b) & 1) == 1, pltpu.roll(x, 1 << b, 1), x)
            o_ref[h] = jnp.where(band, x[:, :WIN], NEG_INF)

    return pl.pallas_call(body, out_shape=SDS((H, TQ, WIN), f32), compiler_params=pltpu.CompilerParams(vmem_limit_bytes=VMEM_LIMIT),
                          name=name)(table)


def att_bias_grad(dbias, name):
    H = dbias.shape[0]

    def body(d_ref, o_ref):
        row = lax.broadcasted_iota(jnp.int32, (TQ, 1024), 0)
        rows = []
        for h in range(H):
            x = jnp.concatenate([d_ref[h], jnp.zeros((TQ, 1024 - WIN), f32)], axis=1)
            for b in range(8):
                x = jnp.where(((row >> b) & 1) == 1, pltpu.roll(x, 1024 - (1 << b), 1), x)
            rows.append(jnp.sum(x, axis=0, keepdims=True))
        du = jnp.concatenate(rows, axis=0)
        o_ref[...] = lax.dot_general(du, _toeplitz_index(), (((1,), (1,)), ((), ())), precision=HI,
                                     preferred_element_type=f32)

    return pl.pallas_call(body, out_shape=SDS((H, REL_TABLE), f32), compiler_params=pltpu.CompilerParams(vmem_limit_bytes=VMEM_LIMIT),
                          name=name)(dbias)


def _att_specs():
    cq, ck, cv = (OUR_COLS[n][0] // BW for n in ("a_q", "a_k", "a_v"))
    specs = [BS((TQ, BW), lambda i: (i, cq))]
    for col in (ck, cv):
        for back in (2, 1, 0):
            specs.append(BS((TQ, BW), functools.partial(lambda i, back, col: (jnp.maximum(i - back, 0), col), back=back, col=col)))
    return specs


def _att_pen(i):
    n = lax.broadcasted_iota(jnp.int32, (1, WIN), 1)
    return jnp.where(n + (i - 2) * TQ >= 0, 0.0, NEG_INF).astype(f32)


def _att_probs(qa, kp, bias_h, pen):
    s = lax.dot_general(qa, kp, (((1,), (1,)), ((), ())), preferred_element_type=f32) + bias_h + pen
    e = jnp.exp(s - jnp.max(s, axis=-1, keepdims=True))
    return e * (1.0 / jnp.sum(e, axis=-1, keepdims=True))


def att_fwd(proj, bias, y, name, gather=None):
    S = proj.shape[0]
    ng = len(gather) if gather else 0

    def body(q_ref, k2, k1, k0, v2, v1, v0, b_ref, y_in, *rest):
        del y_in
        g_in, o_ref, g_out = rest[:ng], rest[ng], rest[ng + 1:2 * ng + 1]
        i = pl.program_id(0)
        if ng:
            kwin, vwin, send_sems, recv_sems = rest[2 * ng + 1:]
            _ag4_over_grid(g_in, g_out, send_sems, recv_sems, i, S // TQ, AG_FORWARD_ATTENTION)
        else:
            kwin, vwin = rest[1:]
        for w, (kr, vr) in enumerate(((k2, v2), (k1, v1), (k0, v0))):
            kwin[pl.ds(w * TQ, TQ), :] = kr[...]
            vwin[pl.ds(w * TQ, TQ), :] = vr[...]
        pen = _att_pen(i)
        lo = lax.broadcasted_iota(jnp.int32, (TQ, 128), 1) < 64
        for hp in range(4):
            cols = slice(hp * 128, (hp + 1) * 128)
            qp, kp, vp = q_ref[:, cols] * jnp.asarray(0.125, bf16), kwin[:, cols], vwin[:, cols]
            outs = []
            for a in range(2):
                qa = jnp.where(lo if a == 0 else ~lo, qp, jnp.zeros_like(qp))
                p = _att_probs(qa, kp, b_ref[2 * hp + a], pen)
                outs.append(jnp.dot(p.astype(bf16), vp, preferred_element_type=f32))
            o_ref[:, cols] = jnp.where(lo, outs[0], outs[1]).astype(bf16)

    outs = pl.pallas_call(
        body, out_shape=(SDS(y.shape, bf16),) + tuple(_ag4_out_shapes(gather or [])), grid=(S // TQ,),
        in_specs=_att_specs() + [BS((8, TQ, WIN), lambda i: (0, 0, 0), pipeline_mode=pl.Buffered(1)), ANY] + [ANY] * ng,
        out_specs=(BS((TQ, BW), lambda i: (i, 2)),) + tuple(ANY for _ in range(ng)),
        scratch_shapes=[pltpu.VMEM((WIN, BW), bf16), pltpu.VMEM((WIN, BW), bf16)] + (_ag4_sems(ng) if ng else []),
        input_output_aliases={8: 0}, compiler_params=_cp("arbitrary" if ng else "parallel"), name=name)(
            proj, proj, proj, proj, proj, proj, proj, bias, y, *(gather or []))
    return (outs[0], list(outs[1:])) if ng else outs[0]


def att_bwd(proj, y, dy, bias, dproj, name):
    S = proj.shape[0]
    cq = OUR_COLS["a_q"][0] // BW

    def body(q_ref, k2, k1, k0, v2, v1, v0, b_ref, o_ref, do_ref, dp_in, dq_ref, dkp_ref, dvp_ref, db_ref, kwin, vwin):
        del dp_in
        i = pl.program_id(0)

        @pl.when(i == 0)
        def _():
            db_ref[...] = jnp.zeros_like(db_ref)

        for w, (kr, vr) in enumerate(((k2, v2), (k1, v1), (k0, v0))):
            kwin[pl.ds(w * TQ, TQ), :] = kr[...]
            vwin[pl.ds(w * TQ, TQ), :] = vr[...]
        pen = _att_pen(i)
        lo = lax.broadcasted_iota(jnp.int32, (TQ, 128), 1) < 64
        for hp in range(4):
            cols = slice(hp * 128, (hp + 1) * 128)
            qp, kp, vp = q_ref[:, cols] * jnp.asarray(0.125, bf16), kwin[:, cols], vwin[:, cols]
            dop, op = do_ref[:, cols], o_ref[:, cols]
            dqs = []
            dk = jnp.zeros((WIN, 128), f32)
            dv = jnp.zeros((WIN, 128), f32)
            for a in range(2):
                sel = lo if a == 0 else ~lo
                qa = jnp.where(sel, qp, jnp.zeros_like(qp))
                doa = jnp.where(sel, dop, jnp.zeros_like(dop))
                p = _att_probs(qa, kp, b_ref[2 * hp + a], pen)
                dpv = lax.dot_general(doa, vp, (((1,), (1,)), ((), ())), preferred_element_type=f32)
                delta = jnp.sum(doa.astype(f32) * op.astype(f32), axis=-1, keepdims=True)
                ds = p * (dpv - delta)
                db_ref[2 * hp + a] += ds
                dsb = ds.astype(bf16)
                dqs.append(jnp.dot(dsb, kp, preferred_element_type=f32))
                dk = dk + lax.dot_general(dsb, qa, (((0,), (0,)), ((), ())), preferred_element_type=f32)
                dv = dv + lax.dot_general(p.astype(bf16), doa, (((0,), (0,)), ((), ())), preferred_element_type=f32)
            dq_ref[:, cols] = (jnp.where(lo, dqs[0], dqs[1]) * 0.125).astype(bf16)
            for w in range(3):
                dkp_ref[w, :, cols] = dk[w * TQ:(w + 1) * TQ].astype(bf16)
                dvp_ref[w, :, cols] = dv[w * TQ:(w + 1) * TQ].astype(bf16)

    tok = BS((TQ, BW), lambda i: (i, 2))
    part = BS((3, TQ, BW), lambda i: (0, i, 0))
    full = BS((8, TQ, WIN), lambda i: (0, 0, 0))
    return pl.pallas_call(
        body, out_shape=(SDS((S, NP), bf16), SDS((3, S, BW), bf16), SDS((3, S, BW), bf16), SDS((8, TQ, WIN), f32)),
        grid=(S // TQ,),
        in_specs=_att_specs() + [BS((8, TQ, WIN), lambda i: (0, 0, 0), pipeline_mode=pl.Buffered(1)), tok, tok, ANY],
        out_specs=(BS((TQ, BW), lambda i: (i, cq)), part, part, full),
        scratch_shapes=[pltpu.VMEM((WIN, BW), bf16), pltpu.VMEM((WIN, BW), bf16)],
        input_output_aliases={10: 0}, compiler_params=_cp("arbitrary"), name=name)(
            proj, proj, proj, proj, proj, proj, proj, bias, y, dy, dproj)


def att_shift_add(dkp, dvp, dproj, name):
    S = dkp.shape[1]
    nT = S // TQ
    creg = OUR_COLS["a_k"][0] // (2 * BW)

    def body(k2, k1, k0, v2, v1, v0, dp_in, dp_ref):
        del dp_in
        j = pl.program_id(0)
        m1 = (j + 1 < nT).astype(f32)
        m0 = (j + 2 < nT).astype(f32)
        dp_ref[:, 0:BW] = (k2[0].astype(f32) + m1 * k1[0].astype(f32) + m0 * k0[0].astype(f32)).astype(bf16)
        dp_ref[:, BW:2 * BW] = (v2[0].astype(f32) + m1 * v1[0].astype(f32) + m0 * v0[0].astype(f32)).astype(bf16)

    def spec(w):
        return BS((1, TQ, BW), functools.partial(lambda j, w: (w, jnp.minimum(j + 2 - w, nT - 1), 0), w=w))

    return pl.pallas_call(
        body, out_shape=SDS(dproj.shape, bf16), grid=(nT,),
        in_specs=[spec(2), spec(1), spec(0), spec(2), spec(1), spec(0), ANY],
        out_specs=BS((TQ, 2 * BW), lambda j: (j, creg)),
        input_output_aliases={6: 0}, compiler_params=_cp("parallel"), name=name)(dkp, dkp, dkp, dvp, dvp, dvp, dproj)


GQ, GV = 256, 512
TGC = 8


def _bd_mask():
    r = lax.broadcasted_iota(jnp.int32, (GQ, GV), 0) // 64
    c = lax.broadcasted_iota(jnp.int32, (GQ, GV), 1) // 128
    return (r == c).astype(f32)


def _tri(strict):
    r = lax.broadcasted_iota(jnp.int32, (CHUNK, CHUNK), 0)
    c = lax.broadcasted_iota(jnp.int32, (CHUNK, CHUNK), 1)
    return ((c < r) if strict else (c <= r)).astype(f32)


def _compact(s_bd):
    return jnp.concatenate([s_bd[h * 64:(h + 1) * 64, h * 128:(h + 1) * 128] for h in range(4)], axis=0)


def _expand(comp, mask):
    return jnp.tile(comp, (1, 4)) * mask


def _gla_gates(alr, wa_ref, ba_ref, tri_incl, ones_col):
    a = jnp.dot(alr, wa_ref[...], preferred_element_type=f32) + ba_ref[...]
    la = (jnp.minimum(a, 0.0) - jnp.log(1.0 + jnp.exp(-jnp.abs(a)))) * (1.0 / 16.0)
    cum = jnp.dot(tri_incl, la, precision=HI, preferred_element_type=f32)
    tot_row = cum[CHUNK - 1:CHUNK, :]
    tot_col = lax.dot_general(la, ones_col, (((0,), (0,)), ((), ())), precision=HI, preferred_element_type=f32)
    return a, cum, tot_row, jnp.tile(jnp.exp(tot_col), (1, 4))


def _head_norm(o):
    rns, ons = [], []
    for h in range(4):
        oh = o[:, h * 128:(h + 1) * 128]
        rn = lax.rsqrt(jnp.mean(oh * oh, axis=-1, keepdims=True) + EPS)
        rns.append(rn)
        ons.append(oh * rn)
    return rns, ons


def _gla_in_specs(T, imap):
    cq, ck = OUR_COLS["g_q"][0] // GQ, OUR_COLS["g_k"][0] // GQ
    cv, cr = OUR_COLS["g_v"][0] // GV, OUR_COLS["g_r"][0] // GV
    return [BS((T, GQ), lambda i: (imap(i), cq)), BS((T, GQ), lambda i: (imap(i), ck)), BS((T, GV), lambda i: (imap(i), cv)),
            BS((T, GV), lambda i: (imap(i), cr)), BS((T, RANKP), lambda i: (imap(i), 0))]


def gla_fwd(proj, pa, wa, ba, ng, y, name):
    S = proj.shape[0]
    T = min(TGC * CHUNK, S)
    nch = T // CHUNK

    def body(q_ref, k_ref, v_ref, r_ref, a_ref, wa_ref, ba_ref, ng_ref, y_in, y_ref, st_ref, s_scr):
        del y_in

        @pl.when(pl.program_id(0) == 0)
        def _():
            s_scr[...] = jnp.zeros_like(s_scr)

        mask = _bd_mask()
        tri = _tri(False)
        ones_col = jnp.ones((CHUNK, 128), f32)

        s_bd = s_scr[...]
        for ci in range(nch):
            rows = pl.ds(ci * CHUNK, CHUNK)
            _, cum, tot_row, dec4 = _gla_gates(a_ref[rows, :], wa_ref, ba_ref, tri, ones_col)
            kd = (k_ref[rows, :].astype(f32) * jnp.exp(tot_row - cum)).astype(bf16)
            upd = lax.dot_general(kd, v_ref[rows, :], (((0,), (0,)), ((), ())), preferred_element_type=f32) * mask
            s_bd = dec4 * s_bd + upd
            st_ref[pl.ds(ci * GQ, GQ), :] = _compact(s_bd)
            qs = (q_ref[rows, :].astype(f32) * 0.125).astype(bf16)
            o = jnp.dot(qs, s_bd.astype(bf16), preferred_element_type=f32)
            _, ons = _head_norm(o)
            rv = r_ref[rows, :].astype(f32)
            y_ref[rows, :] = (jnp.concatenate(ons, axis=1) * ng_ref[...] * (rv * jax.nn.sigmoid(rv))).astype(bf16)
        s_scr[...] = s_bd

    return pl.pallas_call(
        body, out_shape=(SDS(y.shape, bf16), SDS((S // CHUNK * GQ, 128), f32)), grid=(S // T,),
        in_specs=_gla_in_specs(T, lambda i: i) + [BS((RANKP, GQ), lambda i: (0, 0)), BS((1, GQ), lambda i: (0, 0)),
                                                  BS((1, GV), lambda i: (0, 0)), ANY],
        out_specs=(BS((T, GV), lambda i: (i, 1)), BS((nch * GQ, 128), lambda i: (i, 0))),
        scratch_shapes=[pltpu.VMEM((GQ, GV), f32)], input_output_aliases={8: 0}, compiler_params=_cp("arbitrary"),
        name=name)(proj, proj, proj, proj, pa, wa, ba, ng, y)


def gla_bwd(proj, pa, states, dy, wa, ba, ng, dproj, name):
    S = proj.shape[0]
    T = min(TGC * CHUNK, S)
    nch = T // CHUNK
    nT = S // T
    rev = lambda i: nT - 1 - i

    def body(q_ref, k_ref, v_ref, r_ref, a_ref, st_ref, sp_ref, dy_ref, wa_ref, ba_ref, ng_ref, dp_in,
             dp_ref, da_ref, dwa_ref, dba_ref, dng_ref, g_scr):
        del dp_in
        i = pl.program_id(0)

        @pl.when(i == 0)
        def _():
            g_scr[...] = jnp.zeros_like(g_scr)
            dwa_ref[...] = jnp.zeros_like(dwa_ref)
            dba_ref[...] = jnp.zeros_like(dba_ref)
            dng_ref[...] = jnp.zeros_like(dng_ref)

        mask = _bd_mask()
        tri = _tri(False)
        tri_strict = _tri(True)
        ones_col = jnp.ones((CHUNK, 128), f32)
        ones_row = jnp.ones((8, 128), f32)
        first_tile = (i == nT - 1).astype(f32)

        g_carry = g_scr[...]
        for ci in reversed(range(nch)):
            rows = pl.ds(ci * CHUNK, CHUNK)
            alr = a_ref[rows, :]
            a, cum, tot_row, dec4 = _gla_gates(alr, wa_ref, ba_ref, tri, ones_col)
            wdec = jnp.exp(tot_row - cum)
            kdf = k_ref[rows, :].astype(f32) * wdec
            kd = kdf.astype(bf16)
            s_c = _expand(st_ref[pl.ds(ci * GQ, GQ), :], mask)
            prev = st_ref[pl.ds((ci - 1) * GQ, GQ), :] if ci > 0 else sp_ref[...] * (1.0 - first_tile)
            qs = (q_ref[rows, :].astype(f32) * 0.125).astype(bf16)
            s_cb = s_c.astype(bf16)
            o = jnp.dot(qs, s_cb, preferred_element_type=f32)
            rns, ons = _head_norm(o)
            on = jnp.concatenate(ons, axis=1)
            rv = r_ref[rows, :].astype(f32)
            sg = jax.nn.sigmoid(rv)
            sr = rv * sg
            dyv = dy_ref[rows, :].astype(f32)
            ngv = ng_ref[...]
            dng_ref[...] += jnp.sum(dyv * on * sr, axis=0, keepdims=True)
            d_on = dyv * ngv * sr
            dr = dyv * on * ngv * (sg * (1.0 + rv * (1.0 - sg)))
            dos = []
            for h in range(4):
                cols = slice(h * 128, (h + 1) * 128)
                dh_ = d_on[:, cols]
                dos.append(rns[h] * (dh_ - ons[h] * jnp.mean(dh_ * ons[h], axis=-1, keepdims=True)))
            do = jnp.concatenate(dos, axis=1).astype(bf16)
            dq = lax.dot_general(do, s_cb, (((1,), (1,)), ((), ())), preferred_element_type=f32) * 0.125
            ds = lax.dot_general(qs, do, (((0,), (0,)), ((), ())), preferred_element_type=f32) * mask + g_carry
            ddec_row = lax.dot_general(ones_row, _compact(ds) * prev, (((1,), (1,)), ((), ())), precision=HI,
                                       preferred_element_type=f32)[0:1, :]
            dsb = ds.astype(bf16)
            dkd = lax.dot_general(v_ref[rows, :], dsb, (((1,), (1,)), ((), ())), preferred_element_type=f32)
            dv = jnp.dot(kd, dsb, preferred_element_type=f32)
            g_carry = dec4 * ds
            dk = dkd * wdec
            dwlog = dkd * kdf
            dla = ddec_row * jnp.exp(tot_row) + jnp.dot(tri_strict, dwlog, precision=HI, preferred_element_type=f32)
            da = dla * (1.0 - jax.nn.sigmoid(a)) * (1.0 / 16.0)
            dab = da.astype(bf16)
            da_ref[rows, :] = lax.dot_general(dab, wa_ref[...], (((1,), (1,)), ((), ())),
                                              preferred_element_type=f32).astype(bf16)
            dwa_ref[...] += lax.dot_general(alr, dab, (((0,), (0,)), ((), ())), preferred_element_type=f32)
            dba_ref[...] += jnp.sum(da, axis=0, keepdims=True)
            dp_ref[rows, 0:GQ] = dq.astype(bf16)
            dp_ref[rows, GQ:2 * GQ] = dk.astype(bf16)
            dp_ref[rows, 2 * GQ:2 * GQ + GV] = dv.astype(bf16)
            dp_ref[rows, 2 * GQ + GV:2 * GQ + 2 * GV] = dr.astype(bf16)
        g_scr[...] = g_carry

    REG = 2 * GQ + 2 * GV
    return pl.pallas_call(
        body,
        out_shape=(SDS((S, NP), bf16), SDS((S, RANKP), bf16), SDS((RANKP, GQ), f32), SDS((1, GQ), f32), SDS((1, GV), f32)),
        grid=(nT,),
        in_specs=_gla_in_specs(T, rev) + [
            BS((nch * GQ, 128), lambda i: (rev(i), 0)),
            BS((GQ, 128), lambda i: (jnp.maximum(rev(i) * nch - 1, 0), 0)),
            BS((T, GV), lambda i: (rev(i), 1)),
            BS((RANKP, GQ), lambda i: (0, 0)), BS((1, GQ), lambda i: (0, 0)), BS((1, GV), lambda i: (0, 0)), ANY],
        out_specs=(BS((T, REG), lambda i: (rev(i), 0)), BS((T, RANKP), lambda i: (rev(i), 0)),
                   BS((RANKP, GQ), lambda i: (0, 0)), BS((1, GQ), lambda i: (0, 0)), BS((1, GV), lambda i: (0, 0))),
        scratch_shapes=[pltpu.VMEM((GQ, GV), f32)],
        input_output_aliases={11: 0}, compiler_params=_cp("arbitrary"), name=name)(
            proj, proj, proj, proj, pa, states, states, dy, wa, ba, ng, dproj)


def _as2d(a):
    if a.ndim == 1:
        return a.reshape(1, a.shape[0])
    return a.reshape(-1, a.shape[-1])


def adamw(w, g, m, v, name):
    shape = w.shape
    w2, g2, m2, v2 = (_as2d(a) for a in (w, g, m, v))
    R, C = w2.shape
    tr = R
    for cand in (512, 256, 128, 64, 32, 16, 8):
        if R % cand == 0 and cand * C * 4 * 7 * 2 <= 40 * 1024 * 1024:
            tr = cand
            break

    def body(w_ref, g_ref, m_ref, v_ref, d_ref, mo_ref, vo_ref):
        gv = g_ref[...]
        mn = ADAM_B1 * m_ref[...] + (1.0 - ADAM_B1) * gv
        vn = ADAM_B2 * v_ref[...] + (1.0 - ADAM_B2) * (gv * gv)
        m_hat = mn / (1.0 - ADAM_B1 ** ADAM_STEP)
        v_hat = vn / (1.0 - ADAM_B2 ** ADAM_STEP)
        d_ref[...] = -ADAM_LR * (m_hat / (jnp.sqrt(v_hat) + ADAM_EPS) + ADAM_WD * w_ref[...])
        mo_ref[...] = mn
        vo_ref[...] = vn

    blk = BS((tr, C), lambda i: (i, 0))
    outs = pl.pallas_call(body, out_shape=tuple(SDS((R, C), f32) for _ in range(3)), grid=(R // tr,),
                          in_specs=[blk] * 4, out_specs=(blk,) * 3, compiler_params=_cp("parallel"), name=name)(w2, g2, m2, v2)
    return tuple(o.reshape(shape) for o in outs)


def _row_tile(rows, row_bytes, budget=4 * 1024 * 1024):
    best = None
    for t in range(16, rows + 1, 16):
        if rows % t == 0 and t * row_bytes <= budget:
            best = t
    return best or rows


def add_halves(g0, g1, ra, c, name):
    shape = ra.shape
    cols = shape[-1]
    rows = int(np.prod(shape[:-1]))
    tr = _row_tile(rows, cols * 2)
    blk = lambda: BS((tr, cols), lambda i, c_ref: (i, 0))
    grid_spec = pltpu.PrefetchScalarGridSpec(num_scalar_prefetch=1, grid=(rows // tr,), in_specs=[blk(), blk(), blk()],
                                             out_specs=blk())

    def body(c_ref, a0_ref, a1_ref, b_ref, o_ref):
        mine = jnp.where(c_ref[0] == 0, a0_ref[...], a1_ref[...])
        o_ref[...] = (mine.astype(f32) + b_ref[...].astype(f32)).astype(bf16)

    out = pl.pallas_call(body, out_shape=SDS((rows, cols), bf16), grid_spec=grid_spec, compiler_params=_cp("parallel"),
                         name=name)(jnp.reshape(c, (1,)).astype(jnp.int32), g0.reshape(rows, cols), g1.reshape(rows, cols),
                                    ra.reshape(rows, cols))
    return out.reshape(shape)


def reduce_chips(rb, own, c, chip, name):
    shape = rb.shape[1:]
    cols = shape[-1]
    rows = int(np.prod(shape[:-1]))
    tr = _row_tile(rows, cols * 2 * 4)
    rb3, own3 = rb.reshape(4, rows, cols), own.reshape(4, rows, cols)

    def body(s_ref, own_ref, r1, r2, r3, o_ref):
        del s_ref
        o_ref[0] = ((own_ref[0].astype(f32) + r1[0].astype(f32)) + r2[0].astype(f32)) + r3[0].astype(f32)

    def slot(k):
        return BS((1, tr, cols), functools.partial(lambda i, s, k: ((s[1] + k) % 4, i, 0), k=k))

    grid_spec = pltpu.PrefetchScalarGridSpec(
        num_scalar_prefetch=1, grid=(rows // tr,), in_specs=[slot(0), slot(1), slot(2), slot(3)],
        out_specs=BS((1, tr, cols), lambda i, s: (s[0], i, 0)))
    out = pl.pallas_call(body, out_shape=SDS((2, rows, cols), f32), grid_spec=grid_spec, compiler_params=_cp("parallel"),
                         name=name)(jnp.stack([c, chip]).astype(jnp.int32), own3, rb3, rb3, rb3)
    return out.reshape((2,) + shape)


def sum_slots(x, name):
    N, shape = x.shape[0], x.shape[1:]
    cols = shape[-1]
    rows = int(np.prod(shape[:-1]))
    tr = _row_tile(rows, cols * x.dtype.itemsize * N)

    def body(x_ref, o_ref):
        acc = x_ref[0].astype(f32)
        for n in range(1, N):
            acc = acc + x_ref[n].astype(f32)
        o_ref[...] = acc

    out = pl.pallas_call(body, out_shape=SDS((rows, cols), f32), grid=(rows // tr,),
                         in_specs=[BS((N, tr, cols), lambda i: (0, i, 0))], out_specs=BS((tr, cols), lambda i: (i, 0)),
                         compiler_params=_cp("parallel"), name=name)(x.reshape(N, rows, cols))
    return out.reshape(shape)


def _me():
    return lax.axis_index("x"), lax.axis_index("y"), lax.axis_index("c")


def _rcopy(src, dst, send_sems, recv_sems, k, dev):
    return pltpu.make_async_remote_copy(src_ref=src, dst_ref=dst, send_sem=send_sems.at[k], recv_sem=recv_sems.at[k],
                                        device_id=dev, device_id_type=MESH)


def _comm_call(body, ins, out_shapes, n_remote, name, aliases=None):
    return pl.pallas_call(
        body, out_shape=tuple(out_shapes), in_specs=[ANY] * len(ins), out_specs=tuple(ANY for _ in out_shapes),
        scratch_shapes=[pltpu.SemaphoreType.DMA((n_remote,)), pltpu.SemaphoreType.DMA((n_remote,))],
        input_output_aliases=aliases or {}, name=name)(*ins)


def _ag4_out_shapes(bufs):
    return [SDS((2, 4) + b.shape[1:], b.dtype) for b in bufs]


def _ag4_sems(n):
    return [pltpu.SemaphoreType.DMA((8 * n,)), pltpu.SemaphoreType.DMA((8 * n,))]


def _ag4_phases(xs, os, send_sems, recv_sems):
    n = len(xs)

    def place():
        x, y, c = _me()
        return x, y, c, 2 * x + y, (x, y, 1 - c), [(1 - x, y), (x, 1 - y), (1 - x, 1 - y)]

    def sends():
        x, y, c, j, sib, chips = place()
        first = [_rcopy(xs[t].at[c], os[t].at[c, j], send_sems, recv_sems, 8 * t + k, (cx, cy, c))
                 for t in range(n) for k, (cx, cy) in enumerate(chips)]
        own = [_rcopy(xs[t].at[l], os[t].at[l, j], send_sems, recv_sems, 8 * t + 6 + l, sib) for t in range(n) for l in range(2)]
        return first + own

    def forwards():
        x, y, c, j, sib, chips = place()
        return [(_rcopy(os[t].at[c, 2 * cx + cy], os[t].at[c, 2 * cx + cy], send_sems, recv_sems, 8 * t + k, (x, y, c)),
                 _rcopy(os[t].at[c, 2 * cx + cy], os[t].at[c, 2 * cx + cy], send_sems, recv_sems, 8 * t + 3 + k, sib))
                for k, (cx, cy) in enumerate(chips) for t in range(n)]

    def start():
        for cp in sends():
            cp.start()

    def forward():
        for landed, fwd in forwards():
            landed.wait_recv()
            fwd.start()

    def finish():
        x, y, c, j, sib, chips = place()
        for t in range(n):
            for l in range(2):
                land = os[t].at[l, j]
                _rcopy(land, land, send_sems, recv_sems, 8 * t + 6 + l, (x, y, c)).wait_recv()
        for k, (cx, cy) in enumerate(chips):
            for t in range(n):
                land = os[t].at[1 - c, 2 * cx + cy]
                _rcopy(land, land, send_sems, recv_sems, 8 * t + 3 + k, (x, y, c)).wait_recv()
        for cp in sends() + [fwd for _, fwd in forwards()]:
            cp.wait_send()

    return start, forward, finish


def _ag4_over_grid(xs, os, send_sems, recv_sems, step, nsteps, forward_frac):
    start, forward, finish = _ag4_phases(xs, os, send_sems, recv_sems)
    pl.when(step == 0)(start)
    pl.when(step == min(nsteps - 1, int(nsteps * forward_frac)))(forward)
    pl.when(step == nsteps - 1)(finish)


def sib_other_layer(g0s, g1s, name):
    n = len(g0s)

    def body(*refs):
        layers, os = (refs[:n], refs[n:2 * n]), refs[2 * n:3 * n]
        send_sems, recv_sems = refs[3 * n:]
        x, y, c = _me()
        for mine in range(2):
            @pl.when(c == mine)
            def _():
                cps = [_rcopy(layers[1 - mine][t], os[t], send_sems, recv_sems, t, (x, y, 1 - c)) for t in range(n)]
                for cp in cps:
                    cp.start()
                for cp in cps:
                    cp.wait()

    return _comm_call(body, list(g0s) + list(g1s), [SDS(g.shape, g.dtype) for g in g0s], n, name)


def a2a4(ps, name):
    n = len(ps)

    def body(*refs):
        xs, os = refs[:n], refs[n:2 * n]
        send_sems, recv_sems = refs[2 * n:]
        x, y, c = _me()
        j = 2 * x + y
        chips = [(1 - x, y), (x, 1 - y), (1 - x, 1 - y)]
        sends = [_rcopy(xs[t].at[2 * cx + cy], os[t].at[j], send_sems, recv_sems, 3 * t + k, (cx, cy, c))
                 for t in range(n) for k, (cx, cy) in enumerate(chips)]
        for cp in sends:
            cp.start()
        for t in range(n):
            for k, (cx, cy) in enumerate(chips):
                land = os[t].at[2 * cx + cy]
                _rcopy(land, land, send_sems, recv_sems, 3 * t + k, (x, y, c)).wait_recv()
        for cp in sends:
            cp.wait_send()

    return _comm_call(body, ps, [SDS(p.shape, p.dtype) for p in ps], 3 * n, name)


def ag2(bufs, name):
    n = len(bufs)

    def body(*refs):
        xs, os = refs[:n], refs[n:2 * n]
        send_sems, recv_sems = refs[2 * n:]
        x, y, c = _me()
        cps = [_rcopy(xs[t].at[c], os[t].at[c], send_sems, recv_sems, t, (x, y, 1 - c)) for t in range(n)]
        for cp in cps:
            cp.start()
        for t in range(n):
            land = os[t].at[1 - c]
            _rcopy(land, land, send_sems, recv_sems, t, (x, y, c)).wait_recv()
        for cp in cps:
            cp.wait_send()

    return _comm_call(body, bufs, [SDS(b.shape, b.dtype) for b in bufs], n, name, aliases={t: t for t in range(n)})


def ag8(blk, name):
    m_per, n = blk.shape

    def body(x_ref, out_ref, send_sems, recv_sems, local_sem):
        x, y, c = _me()
        me, sibling = (x, y, c), (x, y, 1 - c)
        chips = [(1 - x, y), (x, 1 - y), (1 - x, 1 - y)]

        def rows(px, py, pc):
            return out_ref.at[pl.ds((4 * px + 2 * py + pc) * m_per, m_per), :]

        def copy(k, block, to, src=None):
            return pltpu.make_async_remote_copy(
                src_ref=rows(*block) if src is None else src, dst_ref=rows(*block), send_sem=send_sems.at[k],
                recv_sem=recv_sems.at[k], device_id=to, device_id_type=MESH)

        mine = pltpu.make_async_copy(x_ref, rows(*me), local_sem)
        mine.start()
        first = [copy(0, me, sibling, src=x_ref)]
        first += [copy(1 + j, me, (*chip, c), src=x_ref) for j, chip in enumerate(chips)]
        for cp in first:
            cp.start()
        passed = [copy(4 + j, (*chip, c), sibling) for j, chip in enumerate(chips)]
        for j, chip in enumerate(chips):
            copy(1 + j, (*chip, c), me).wait_recv()
            passed[j].start()
        copy(0, sibling, me).wait_recv()
        for j, chip in enumerate(chips):
            copy(4 + j, (*chip, 1 - c), me).wait_recv()
        for cp in first + passed:
            cp.wait_send()
        mine.wait()

    return pl.pallas_call(
        body, out_shape=SDS((8 * m_per, n), blk.dtype), in_specs=[pl.BlockSpec(memory_space=pltpu.VMEM)],
        out_specs=pl.BlockSpec(memory_space=pltpu.VMEM),
        scratch_shapes=[pltpu.SemaphoreType.DMA((7,)), pltpu.SemaphoreType.DMA((7,)), pltpu.SemaphoreType.DMA],
        name=name)(blk)


def _split_chips(full, axis):
    n = full.shape[axis] // 4
    parts = full.reshape(full.shape[:axis] + (4, n) + full.shape[axis + 1:])
    return jnp.moveaxis(parts, axis, 0)


def _merge_chips(gathered, axis):
    parts = jnp.moveaxis(gathered, 0, axis)
    return parts.reshape(parts.shape[:axis] + (parts.shape[axis] * parts.shape[axis + 1],) + parts.shape[axis + 2:])


def _to_ref_cols(main, rank):
    pieces = []
    for n, width in REF_SPLITS:
        if n == "g_a":
            pieces.append(rank[..., :RANK])
        else:
            off = OUR_COLS[n][0]
            pieces.append(main[..., off:off + width])
    return jnp.concatenate(pieces, axis=-1)


def _from_ref_cols(w):
    offs, o = {}, 0
    for n, width in REF_SPLITS:
        offs[n] = (o, width)
        o += width
    main = jnp.concatenate([w[..., offs[n][0]:offs[n][0] + offs[n][1]] for n in sorted(OUR_COLS, key=lambda k: OUR_COLS[k][0])],
                           axis=-1)
    ro = offs["g_a"][0]
    rank = jnp.pad(w[..., ro:ro + RANK], [(0, 0)] * (w.ndim - 1) + [(0, RANKP - RANK)])
    return main, rank


def _layer_fwd(h, p_i, W, li, late=None, xn=None):
    t = f"l{li}_"
    sv = {"h0": h}

    def arrived(names, gathered, Ws):
        for l, Wl in enumerate(Ws):
            Wl.update(_prep_layer_weights(dict(zip(names, gathered)), None, l))

    if xn is None:
        xn = rms_fwd(h, W["norm1_g"], t + "rms1")
    if late is None:
        proj = mm_nn(xn, W["w_in_main"], name=t + "inproj")
    else:
        (names, shards, Ws) = late[1]
        proj, gathered = mm_nn(xn, W["w_in_main"], name=t + "inproj", gather=shards)
        arrived(names, gathered, Ws)
    pa = mm_nn(xn, W["w_in_rank"], name=t + "inproj_rank")
    y = sg_fwd(proj, W["sg_ln_g"], W["sg_ln_b"], W["sg_wm"], W["sg_bsb"], t + "sg_fwd")
    y, states = gla_fwd(proj, pa, W["gla_wa"], W["gla_b_a"], W["gla_norm_g"], y, t + "gla_fwd")
    if late is None:
        y = att_fwd(proj, W["att_bias"], y, t + "att_fwd")
    else:
        (names, shards, Ws) = late[0]
        y, gathered = att_fwd(proj, W["att_bias"], y, t + "att_fwd", gather=shards)
        arrived(names, gathered, Ws)
    y = conv_fwd(proj, W["conv_dw_w"], W["conv_dw_b"], W["conv_ln_g"], W["conv_ln_b"], y, t + "conv_fwd")
    gate = mm_nn(xn, W["w_gate_all"], bias=W["b_gate_all"], act="sigmoid", name=t + "gate")
    z = mm_nn(y, W["w_branch"], name=t + "branch")
    m = gate_merge_fwd(gate, z, t + "merge")
    h1 = mm_nn(m, W["w_out"], res=h, out_dtype=f32, name=t + "outproj")
    hn = rms_fwd(h1, W["norm2_g"], t + "rms2")
    a = mm_nn(hn, W["w_ff1"], name=t + "ff1")
    h2 = mm_nn(a, W["w_ff2"], pre="relu2", res=h1, out_dtype=f32, name=t + "ff2")
    hg = rms_fwd(h2, W["norm3_g"], t + "rms3")
    pg = mm_nn(hg, W["w_ple_gate"], bias=W["b_ple_gate"], act="sigmoid", name=t + "ple_gate")
    h3, e = mm_nn(p_i, W["w_ple"], mul=pg, res=h2, out_dtype=f32, raw_out=True, name=t + "ple_out")
    sv.update(xn=xn, proj=proj, pa=pa, states=states, y=y, gate=gate, z=z, m=m, h1=h1, hn=hn, a=a, h2=h2, hg=hg, pg=pg, e=e)
    return h3, sv


def _layer_bwd(dh3, sv, p_i, W, li):
    t = f"l{li}_b_"
    G = {}
    dpg, de, G["b_ple_gate"] = ple_bwd_ew(dh3, sv["e"], sv["pg"], t + "ple_ew")
    G["w_ple_gate"] = mm_tn(sv["hg"], dpg, name=t + "dw_ple_gate")[0]
    G["w_ple"] = mm_tn(p_i, de, name=t + "dw_ple")[0]
    dhg = mm_nt(dpg, W["w_ple_gate"], name=t + "dhg")
    dh2, G["norm3_g"] = rms_bwd(dhg, sv["h2"], W["norm3_g"], dh3, t + "rms3")
    da = mm_nt(dh2, W["w_ff2"], post_a=sv["a"], out_dtype=bf16, name=t + "da")
    G["w_ff2"] = mm_tn(sv["a"], dh2, pre="relu2", name=t + "dw_ff2")[0]
    G["w_ff1"] = mm_tn(sv["hn"], da, name=t + "dw_ff1")[0]
    dhn = mm_nt(da, W["w_ff1"], name=t + "dhn")
    dh1, G["norm2_g"] = rms_bwd(dhn, sv["h1"], W["norm2_g"], dh2, t + "rms2")
    dm = mm_nt(dh1, W["w_out"], out_dtype=bf16, name=t + "dm")
    G["w_out"] = mm_tn(sv["m"], dh1, name=t + "dw_out")[0]
    dz, dgp, G["b_gate_all"] = gate_merge_bwd(dm, sv["gate"], sv["z"], t + "merge")
    G["w_branch"] = mm_tn(sv["y"], dz, G=4, name=t + "dw_branch")
    dy = mm_nt(dz, W["w_branch"], out_dtype=bf16, name=t + "dy")
    G["w_gate_all"] = mm_tn(sv["xn"], dgp, name=t + "dw_gate")[0]
    dxn = mm_nt(dgp, W["w_gate_all"], name=t + "dxn_gate")
    proj = sv["proj"]
    dproj, dwm, dbs, G["sg_ln_g"], G["sg_ln_b"] = sg_bwd(proj, dy, W["sg_ln_g"], W["sg_ln_b"], W["sg_wm"], W["sg_bsb"],
                                                          W["sg_maskf"], t + "sg")
    G["sg_w"], G["sg_b"] = dwm, dbs[:, :, 0]
    dproj, dpa, dwa, G["gla_b_a"], G["gla_norm_g"] = gla_bwd(proj, sv["pa"], sv["states"], dy, W["gla_wa"], W["gla_b_a"],
                                                             W["gla_norm_g"], dproj, t + "gla")
    G["gla_w_a2"] = dwa[:RANK]
    dproj, dkp, dvp, dbias = att_bwd(proj, sv["y"], dy, W["att_bias"], dproj, t + "att")
    dproj = att_shift_add(dkp, dvp, dproj, t + "att_kv")
    G["att_rel_bias"] = att_bias_grad(dbias, t + "att_bias")
    dz_c, G["conv_ln_g"], G["conv_ln_b"], G["conv_dw_b"] = conv_bwd_norm(proj, dy, W["conv_dw_w"], W["conv_dw_b"],
                                                                        W["conv_ln_g"], W["conv_ln_b"], t + "conv_norm")
    dproj, G["conv_dw_w"] = conv_bwd_taps(proj, dz_c, W["conv_dw_w"], dproj, t + "conv_taps")
    G["w_in_main"] = mm_tn(sv["xn"], dproj, name=t + "dw_in")[0]
    G["w_in_rank"] = mm_tn(sv["xn"], dpa, name=t + "dw_in_rank")[0]
    dxn = mm_nt(dpa, W["w_in_rank"], res=dxn, name=t + "dxn_rank")
    dxn = mm_nt(dproj, W["w_in_main"], res=dxn, name=t + "dxn_main")
    dh0, G["norm1_g"] = rms_bwd(dxn, sv["h0"], W["norm1_g"], dh1, t + "rms1")
    return dh0, G


def _prep_layer_weights(gathered, repl, li):
    W = {}
    full = {n: _merge_chips(g[li], SHARDED[n][1]) for n, g in gathered.items()}
    if "w_in" in full:
        main, rank = _from_ref_cols(full["w_in"])
        W["w_in_main"], W["w_in_rank"] = main[None], rank[None]
    if "w_branch" in full:
        W["w_branch"] = full["w_branch"]
    if "w_gate" in full:
        W["w_gate_all"] = jnp.transpose(full["w_gate"], (1, 0, 2)).reshape(1, D, 4 * D)
    if "b_gate" in full:
        W["b_gate_all"] = full["b_gate"].reshape(1, 4 * D)
    for n in ("w_out", "w_ff1", "w_ff2", "w_ple_gate", "w_ple"):
        if n in full:
            W[n] = full[n][None]
    if "gla_w_a2" in full:
        W["gla_wa"] = jnp.pad(full["gla_w_a2"], ((0, RANKP - RANK), (0, 0))).astype(bf16)
    if "att_rel_bias" in full:
        W["att_bias"] = att_bias_build(full["att_rel_bias"], f"l{li}_att_bias")
    if "conv_dw_w" in full:
        W["conv_dw_w"] = full["conv_dw_w"]
    if repl is not None:
        for n in ("norm1_g", "norm2_g", "norm3_g", "b_ple_gate", "sg_ln_g", "sg_ln_b", "gla_b_a", "gla_norm_g", "conv_dw_b",
                  "conv_ln_g", "conv_ln_b"):
            W[n] = repl[n][li].reshape(1, -1)
        pos = np.arange(128)
        mask = (pos[None, :] // CHUNK) <= (pos[:, None] // CHUNK)
        W["sg_maskf"] = jnp.asarray(mask, f32)
        W["sg_wm"] = jnp.where(mask[None], repl["sg_w"][li], 0.0).astype(bf16)
        W["sg_bsb"] = jnp.broadcast_to(repl["sg_b"][li][:, :, None], (4, 128, 128))
    return W


def _layer_grads_to_ref(G):
    out = {}
    out["w_in"] = _to_ref_cols(G["w_in_main"], G["w_in_rank"])
    out["w_gate"] = jnp.transpose(G["w_gate_all"].reshape(D, 4, D), (1, 0, 2))
    out["b_gate"] = G["b_gate_all"].reshape(4, D)
    for n in ("w_branch", "w_out", "w_ff1", "w_ff2", "w_ple_gate", "w_ple", "gla_w_a2", "att_rel_bias", "conv_dw_w", "sg_w",
              "sg_b"):
        out[n] = G[n]
    for n in ("norm1_g", "norm2_g", "norm3_g", "b_ple_gate", "sg_ln_g", "sg_ln_b", "gla_b_a", "gla_norm_g", "conv_dw_b",
              "conv_ln_g", "conv_ln_b"):
        out[n] = G[n].reshape(-1)
    return out


def kernel(x, p, norm1_g, w_in, sg_ln_g, sg_ln_b, sg_w, sg_b, gla_w_a2, gla_b_a, gla_norm_g, att_rel_bias, conv_dw_w, conv_dw_b, conv_ln_g, conv_ln_b, w_branch, w_gate, b_gate, w_out, norm2_g, w_ff1, w_ff2, norm3_g, w_ple_gate, b_ple_gate, w_ple, final_g, loss_target, m_norm1_g, m_w_in, m_sg_ln_g, m_sg_ln_b, m_sg_w, m_sg_b, m_gla_w_a2, m_gla_b_a, m_gla_norm_g, m_att_rel_bias, m_conv_dw_w, m_conv_dw_b, m_conv_ln_g, m_conv_ln_b, m_w_branch, m_w_gate, m_b_gate, m_w_out, m_norm2_g, m_w_ff1, m_w_ff2, m_norm3_g, m_w_ple_gate, m_b_ple_gate, m_w_ple, m_final_g, v_norm1_g, v_w_in, v_sg_ln_g, v_sg_ln_b, v_sg_w, v_sg_b, v_gla_w_a2, v_gla_b_a, v_gla_norm_g, v_att_rel_bias, v_conv_dw_w, v_conv_dw_b, v_conv_ln_g, v_conv_ln_b, v_w_branch, v_w_gate, v_b_gate, v_w_out, v_norm2_g, v_w_ff1, v_w_ff2, v_norm3_g, v_w_ple_gate, v_b_ple_gate, v_w_ple, v_final_g):
    args = dict(locals())
    weights = {n: args[n] for n in W_ORDER}
    moments_m = {n: args["m_" + n] for n in W_ORDER}
    moments_v = {n: args["v_" + n] for n in W_ORDER}
    c = lax.axis_index("c")
    sharded_names = BIG + SMALL

    early = ("w_in",) + SMALL
    shards = {n: (weights[n].astype(bf16) if n in BIG else weights[n]) for n in sharded_names}
    repl = {n: weights[n] for n in REPL}
    h = x[0]
    xn0, gathered = rms_fwd(h, norm1_g[0].reshape(1, D), "l0_rms1", gather=[shards[n] for n in early])
    Ws = [_prep_layer_weights(dict(zip(early, gathered)), repl, li) for li in range(DEPTH)]
    in_att, in_proj = ("w_ff1", "w_ff2", "w_ple_gate", "w_ple"), ("w_gate", "w_branch", "w_out")
    late = [(names, [shards[n] for n in names], Ws) for names in (in_att, in_proj)]

    saved = []
    for li in range(DEPTH):
        h, sv = _layer_fwd(h, p[li, 0], Ws[li], li, late if li == 0 else None, xn0 if li == 0 else None)
        saved.append(sv)
    loss_part, dh, dfinal = loss_head(h, final_g.reshape(1, D), loss_target[0], "loss_head")
    loss = lax.psum(loss_part[0, 0], ("x", "y", "c"))

    layer_grads = [None] * DEPTH
    for li in reversed(range(DEPTH)):
        dh, G = _layer_bwd(dh, saved[li], p[li, 0], Ws[li], li)
        layer_grads[li] = _layer_grads_to_ref(G)
    grad_x = dh[None]

    g0s, g1s = ([_split_chips(layer_grads[li][n], SHARDED[n][1]).astype(bf16) for n in BIG] for li in range(DEPTH))
    ras = sib_other_layer(g0s, g1s, "rs_sibling_layer")
    psums = [add_halves(a0, a1, r, c, "rs_add_" + n) for n, a0, a1, r in zip(BIG, g0s, g1s, ras)]
    rbs = a2a4(psums, "rs_all_to_all")
    chip = 2 * lax.axis_index("x") + lax.axis_index("y")
    reds = [reduce_chips(r, ps, c, chip, "rs_sum_" + n) for n, r, ps in zip(BIG, rbs, psums)]
    grads = dict(zip(BIG, ag2(reds, "rs_sibling_gather")))

    local = {n: jnp.stack([layer_grads[li][n] for li in range(DEPTH)]) for n in tuple(REPL)[:-1] + SMALL}
    local["final_g"] = dfinal.reshape(D)
    rnames = tuple(REPL) + SMALL
    rflat = jnp.concatenate([local[n].reshape(-1) for n in rnames])
    rflat = jnp.pad(rflat, (0, REPL_ROWS * PACK_W - rflat.shape[0])).reshape(REPL_ROWS, PACK_W)
    rall = ag8(rflat, "ar_gather").reshape(8, REPL_ROWS, PACK_W)
    rsum = sum_slots(rall, "ar_sum").reshape(-1)
    off = 0
    for n in rnames:
        shape = local[n].shape
        size = int(np.prod(shape))
        g = rsum[off:off + size].reshape(shape)
        off += size
        if n in SMALL:
            ax = SHARDED[n][1] + 1
            g = lax.dynamic_slice_in_dim(g, chip * (shape[ax] // 4), shape[ax] // 4, axis=ax)
        grads[n] = g

    deltas, new_m, new_v = {}, {}, {}
    for n in W_ORDER:
        deltas[n], new_m[n], new_v[n] = adamw(weights[n], grads[n], moments_m[n], moments_v[n], "adamw_" + n)
    return (loss, grad_x, *[grads[n] for n in W_ORDER], *[deltas[n] for n in W_ORDER], *[new_m[n] for n in W_ORDER],
            *[new_v[n] for n in W_ORDER])
```

```python
import functools

import jax
import jax.numpy as jnp
import numpy as np
from jax import lax
from jax.experimental import pallas as pl
from jax.experimental.pallas import tpu as pltpu

f32, bf16 = jnp.float32, jnp.bfloat16
HI = lax.Precision.HIGHEST
MESH = pl.DeviceIdType.MESH
SDS = jax.ShapeDtypeStruct
BS = pl.BlockSpec
ANY = pl.BlockSpec(memory_space=pl.ANY)

D = 1024
DEPTH = 2
CHUNK = 64
BW = 512
NP = 5120
RANK = 16
RANKP = 128
DFF = 4096
PLE = 256
CONV_K = 31
HALO = 32
TQ = 256
WIN = 768
REL_TABLE = 320
EPS = 1e-6
NEG_INF = -1e30
VMEM_LIMIT = 56 * 1024 * 1024

ADAM_LR, ADAM_B1, ADAM_B2, ADAM_EPS, ADAM_WD, ADAM_STEP = 0.001, 0.9, 0.999, 1e-08, 0.01, 10

OUR_COLS = dict(g_q=(0, 256), g_k=(256, 256), g_v=(512, 512), g_r=(1024, 512), a_q=(1536, 512), a_k=(2048, 512),
                a_v=(2560, 512), sg_u=(3072, 512), sg_v=(3584, 512), c_a=(4096, 512), c_g=(4608, 512))
REF_SPLITS = (("sg_u", 512), ("sg_v", 512), ("g_q", 256), ("g_k", 256), ("g_v", 512), ("g_r", 512), ("g_a", 16),
              ("a_q", 512), ("a_k", 512), ("a_v", 512), ("c_a", 512), ("c_g", 512))

SHARDED = dict(w_in=((1024, 5136), 1), w_branch=((4, 512, 1024), 2), w_gate=((4, 1024, 1024), 1), w_out=((1024, 1024), 0),
               w_ff1=((1024, 4096), 1), w_ff2=((4096, 1024), 0), w_ple_gate=((1024, 1024), 0), w_ple=((256, 1024), 1),
               gla_w_a2=((16, 256), 1), att_rel_bias=((8, 320), 1), conv_dw_w=((31, 512), 1), b_gate=((4, 1024), 1))
BIG = ("w_in", "w_branch", "w_gate", "w_out", "w_ff1", "w_ff2", "w_ple_gate", "w_ple")
SMALL = ("gla_w_a2", "att_rel_bias", "conv_dw_w", "b_gate")
REPL = dict(norm1_g=(2, 1024), sg_ln_g=(2, 512), sg_ln_b=(2, 512), sg_w=(2, 4, 128, 128), sg_b=(2, 4, 128), gla_b_a=(2, 256),
            gla_norm_g=(2, 512), conv_dw_b=(2, 512), conv_ln_g=(2, 512), conv_ln_b=(2, 512), norm2_g=(2, 1024),
            norm3_g=(2, 1024), b_ple_gate=(2, 1024), final_g=(1024,))
W_ORDER = ['norm1_g', 'w_in', 'sg_ln_g', 'sg_ln_b', 'sg_w', 'sg_b', 'gla_w_a2', 'gla_b_a', 'gla_norm_g', 'att_rel_bias',
           'conv_dw_w', 'conv_dw_b', 'conv_ln_g', 'conv_ln_b', 'w_branch', 'w_gate', 'b_gate', 'w_out', 'norm2_g', 'w_ff1',
           'w_ff2', 'norm3_g', 'w_ple_gate', 'b_ple_gate', 'w_ple', 'final_g']
PACK_W = 1024
REPL_ROWS = 200
AG_FORWARD_AT_END = 1.0
AG_FORWARD_INPROJ = 0.875
AG_FORWARD_ATTENTION = 0.75
TM = 1024
MM_VMEM_BUDGET = 44 * 1024 * 1024


def _tile(s):
    return 512 if s % 512 == 0 else s


def _token_tile(S, row_bytes, fixed_bytes):
    for t in (2048, 1024):
        if S % t == 0 and 2 * (t * row_bytes + fixed_bytes) <= MM_VMEM_BUDGET:
            return t
    return min(TM, S)


def _cp(*sem):
    return pltpu.CompilerParams(dimension_semantics=sem, vmem_limit_bytes=VMEM_LIMIT)


def rms_fwd(h, g, name, gather=None):
    S, Dm = h.shape
    T = _tile(S)
    ng = len(gather) if gather else 0

    def body(h_ref, g_ref, *rest):
        o_ref = rest[ng]
        if ng:
            _ag4_over_grid(rest[:ng], rest[ng + 1:2 * ng + 1], *rest[2 * ng + 1:], pl.program_id(0), S // T,
                           AG_FORWARD_AT_END)
        x = h_ref[...]
        r = lax.rsqrt(jnp.mean(x * x, axis=-1, keepdims=True) + EPS)
        o_ref[...] = (x * r * g_ref[...]).astype(bf16)

    outs = pl.pallas_call(
        body, out_shape=(SDS((S, Dm), bf16),) + tuple(_ag4_out_shapes(gather or [])), grid=(S // T,),
        in_specs=[BS((T, Dm), lambda i: (i, 0)), BS((1, Dm), lambda i: (0, 0))] + [ANY] * ng,
        out_specs=(BS((T, Dm), lambda i: (i, 0)),) + tuple(ANY for _ in range(ng)),
        scratch_shapes=_ag4_sems(ng) if ng else [], compiler_params=_cp("arbitrary" if ng else "parallel"),
        name=name)(h, g, *(gather or []))
    return (outs[0], list(outs[1:])) if ng else outs[0]


def mm_nn(x, w, *, name, bias=None, act=None, pre=None, mul=None, res=None, out_dtype=bf16, raw_out=False, gather=None):
    S = x.shape[0]
    G, K, N = w.shape
    tn = min(1024 if K <= 1024 else 512, N)
    nj = N // tn
    row_bytes = K * x.dtype.itemsize + tn * (jnp.dtype(out_dtype).itemsize + (2 if raw_out else 0)
                                             + sum(a.dtype.itemsize for a in (mul, res) if a is not None))
    T = _token_tile(S, row_bytes, K * tn * 2)
    extras = [a for a in (bias, mul, res) if a is not None]
    ng = len(gather) if gather else 0
    grid = (S // T, G, nj)

    def body(*refs):
        it = iter(refs)
        x_ref, w_ref = next(it), next(it)
        b_ref = next(it) if bias is not None else None
        m_ref = next(it) if mul is not None else None
        r_ref = next(it) if res is not None else None
        g_in = [next(it) for _ in range(ng)]
        o_ref = next(it)
        raw_ref = next(it) if raw_out else None
        if ng:
            g_out = [next(it) for _ in range(ng)]
            step = (pl.program_id(0) * G + pl.program_id(1)) * nj + pl.program_id(2)
            _ag4_over_grid(g_in, g_out, next(it), next(it), step, grid[0] * G * nj, AG_FORWARD_INPROJ)
        xv = x_ref[...]
        if pre == "relu2":
            xf = jnp.maximum(xv.astype(f32), 0.0)
            xv = xf * xf
        acc = jnp.dot(xv.astype(bf16), w_ref[0], preferred_element_type=f32)
        if raw_out:
            raw_ref[...] = acc.astype(bf16)
        if b_ref is not None:
            acc = acc + b_ref[...]
        if act == "sigmoid":
            acc = jax.nn.sigmoid(acc)
        if m_ref is not None:
            acc = acc * m_ref[...].astype(f32)
        if r_ref is not None:
            acc = r_ref[...].astype(f32) + acc
        o_ref[...] = acc.astype(out_dtype)

    in_specs = [BS((T, K), lambda i, g, j: (i, g)), BS((1, K, tn), lambda i, g, j: (g, 0, j))]
    if bias is not None:
        in_specs.append(BS((1, tn), lambda i, g, j: (0, g * nj + j)))
    for a in (mul, res):
        if a is not None:
            in_specs.append(BS((T, tn), lambda i, g, j: (i, g * nj + j)))
    ospec = BS((T, tn), lambda i, g, j: (i, g * nj + j))
    out_shape = SDS((S, G * N), out_dtype)
    if raw_out:
        out_shape, ospec = (out_shape, SDS((S, G * N), bf16)), (ospec, ospec)
    if not ng:
        return pl.pallas_call(
            body, out_shape=out_shape, grid=grid, in_specs=in_specs, out_specs=ospec,
            compiler_params=_cp("parallel", "parallel", "parallel"), name=name)(x, w, *extras)
    out_shape = (out_shape if raw_out else (out_shape,)) + tuple(_ag4_out_shapes(gather))
    ospec = (ospec if raw_out else (ospec,)) + tuple(ANY for _ in gather)
    outs = pl.pallas_call(
        body, out_shape=out_shape, grid=grid, in_specs=in_specs + [ANY] * ng, out_specs=ospec,
        scratch_shapes=_ag4_sems(ng), compiler_params=_cp("arbitrary", "arbitrary", "arbitrary"), name=name)(
            x, w, *extras, *gather)
    nres = 2 if raw_out else 1
    return (outs[0] if nres == 1 else outs[:2]), list(outs[nres:])


def mm_nt(dy, w, *, name, res=None, post_a=None, out_dtype=f32):
    S = dy.shape[0]
    G, K, N = w.shape
    tk = min(1024 if N <= 1024 else 512, K)
    nk = K // tk
    row_bytes = N * dy.dtype.itemsize + tk * (jnp.dtype(out_dtype).itemsize
                                              + sum(a.dtype.itemsize for a in (res, post_a) if a is not None))
    T = _token_tile(S, row_bytes, tk * N * 2)
    extras = [a for a in (res, post_a) if a is not None]

    def body(*refs):
        it = iter(refs)
        d_ref, w_ref = next(it), next(it)
        r_ref = next(it) if res is not None else None
        a_ref = next(it) if post_a is not None else None
        o_ref = next(it)
        acc = lax.dot_general(d_ref[...].astype(bf16), w_ref[0], (((1,), (1,)), ((), ())), preferred_element_type=f32)
        if r_ref is not None:
            acc = acc + r_ref[...].astype(f32)
        if a_ref is not None:
            acc = acc * (2.0 * jnp.maximum(a_ref[...].astype(f32), 0.0))
        o_ref[...] = acc.astype(out_dtype)

    in_specs = [BS((T, N), lambda i, g, j: (i, g)), BS((1, tk, N), lambda i, g, j: (g, j, 0))]
    for a in extras:
        in_specs.append(BS((T, tk), lambda i, g, j: (i, g * nk + j)))
    return pl.pallas_call(
        body, out_shape=SDS((S, G * K), out_dtype), grid=(S // T, G, nk), in_specs=in_specs,
        out_specs=BS((T, tk), lambda i, g, j: (i, g * nk + j)),
        compiler_params=_cp("parallel", "parallel", "parallel"), name=name)(dy, w, *extras)


def mm_tn(x, dy, *, name, G=1, pre=None, ts=1024):
    S = x.shape[0]
    K, N = x.shape[1] // G, dy.shape[1] // G
    tk, tn = min(K, 1024), min(N, 1024)
    nk, nn = K // tk, N // tn
    ts = min(ts, S)

    def body(x_ref, d_ref, o_ref, acc):
        @pl.when(pl.program_id(3) == 0)
        def _():
            acc[...] = jnp.zeros_like(acc)

        xv = x_ref[...]
        if pre == "relu2":
            xf = jnp.maximum(xv.astype(f32), 0.0)
            xv = xf * xf
        acc[...] += lax.dot_general(xv.astype(bf16), d_ref[...].astype(bf16), (((0,), (0,)), ((), ())),
                                    preferred_element_type=f32)

        @pl.when(pl.program_id(3) == S // ts - 1)
        def _():
            o_ref[0] = acc[...].astype(bf16)

    return pl.pallas_call(
        body, out_shape=SDS((G, K, N), bf16), grid=(G, nk, nn, S // ts),
        in_specs=[BS((ts, tk), lambda g, a, b, s: (s, g * nk + a)), BS((ts, tn), lambda g, a, b, s: (s, g * nn + b))],
        out_specs=BS((1, tk, tn), lambda g, a, b, s: (g, a, b)), scratch_shapes=[pltpu.VMEM((tk, tn), f32)],
        compiler_params=_cp("parallel", "parallel", "parallel", "arbitrary"), name=name)(x, dy)


def rms_bwd(dxn, x, g, dres, name):
    S, Dm = x.shape
    T = _tile(S)

    def body(*refs):
        if dres is not None:
            d_ref, x_ref, g_ref, r_ref, dx_ref, dg_ref = refs
        else:
            d_ref, x_ref, g_ref, dx_ref, dg_ref = refs
        xv = x_ref[...]
        d = d_ref[...].astype(f32)
        r = lax.rsqrt(jnp.mean(xv * xv, axis=-1, keepdims=True) + EPS)
        u = d * g_ref[...]
        dx = r * u - xv * ((r * r * r) * (1.0 / Dm)) * jnp.sum(u * xv, axis=-1, keepdims=True)
        if dres is not None:
            dx = r_ref[...] + dx
        dx_ref[...] = dx

        @pl.when(pl.program_id(0) == 0)
        def _():
            dg_ref[...] = jnp.zeros_like(dg_ref)

        dg_ref[...] += jnp.sum(d * xv * r, axis=0, keepdims=True)

    tok = BS((T, Dm), lambda i: (i, 0))
    vec = BS((1, Dm), lambda i: (0, 0))
    args = (dxn, x, g) + ((dres,) if dres is not None else ())
    return pl.pallas_call(
        body, out_shape=(SDS((S, Dm), f32), SDS((1, Dm), f32)), grid=(S // T,),
        in_specs=[tok, tok, vec] + ([tok] if dres is not None else []), out_specs=(tok, vec),
        compiler_params=_cp("arbitrary"), name=name)(*args)


def loss_head(h, g, target, name):
    S, Dm = h.shape
    T = _tile(S)

    def body(h_ref, g_ref, t_ref, loss_ref, dh_ref, dg_ref):
        @pl.when(pl.program_id(0) == 0)
        def _():
            loss_ref[...] = jnp.zeros_like(loss_ref)
            dg_ref[...] = jnp.zeros_like(dg_ref)

        xv = h_ref[...]
        gv = g_ref[...]
        r = lax.rsqrt(jnp.mean(xv * xv, axis=-1, keepdims=True) + EPS)
        diff = xv * r * gv - t_ref[...]
        loss_ref[...] += 0.5 * jnp.sum(jnp.mean(diff * diff, axis=-1, keepdims=True))
        d = diff * (1.0 / Dm)
        u = d * gv
        dh_ref[...] = r * u - xv * ((r * r * r) * (1.0 / Dm)) * jnp.sum(u * xv, axis=-1, keepdims=True)
        dg_ref[...] += jnp.sum(d * xv * r, axis=0, keepdims=True)

    tok = BS((T, Dm), lambda i: (i, 0))
    vec = BS((1, Dm), lambda i: (0, 0))
    return pl.pallas_call(
        body, out_shape=(SDS((1, 128), f32), SDS((S, Dm), f32), SDS((1, Dm), f32)), grid=(S // T,),
        in_specs=[tok, vec, tok], out_specs=(BS((1, 128), lambda i: (0, 0)), tok, vec),
        compiler_params=_cp("arbitrary"), name=name)(h, g, target)


def gate_merge_fwd(gate, z, name):
    S = gate.shape[0]
    T = _tile(S)

    def body(g_ref, z_ref, o_ref):
        acc = jnp.zeros((T, D), f32)
        for n in range(4):
            acc = acc + g_ref[:, n * D:(n + 1) * D].astype(f32) * z_ref[:, n * D:(n + 1) * D].astype(f32)
        o_ref[...] = acc.astype(bf16)

    wide = BS((T, 4 * D), lambda i: (i, 0))
    return pl.pallas_call(body, out_shape=SDS((S, D), bf16), grid=(S // T,), in_specs=[wide, wide],
                          out_specs=BS((T, D), lambda i: (i, 0)), compiler_params=_cp("parallel"), name=name)(gate, z)


def gate_merge_bwd(dm, gate, z, name):
    S = gate.shape[0]
    T = _tile(S)

    def body(dm_ref, g_ref, z_ref, dz_ref, dg_ref, db_ref):
        @pl.when(pl.program_id(0) == 0)
        def _():
            db_ref[...] = jnp.zeros_like(db_ref)

        dmv = dm_ref[...].astype(f32)
        for n in range(4):
            cols = slice(n * D, (n + 1) * D)
            gv = g_ref[:, cols].astype(f32)
            dz_ref[:, cols] = (dmv * gv).astype(bf16)
            dgp = dmv * z_ref[:, cols].astype(f32) * gv * (1.0 - gv)
            dg_ref[:, cols] = dgp.astype(bf16)
            db_ref[:, cols] += jnp.sum(dgp, axis=0, keepdims=True)

    wide = BS((T, 4 * D), lambda i: (i, 0))
    return pl.pallas_call(
        body, out_shape=(SDS((S, 4 * D), bf16), SDS((S, 4 * D), bf16), SDS((1, 4 * D), f32)), grid=(S // T,),
        in_specs=[BS((T, D), lambda i: (i, 0)), wide, wide], out_specs=(wide, wide, BS((1, 4 * D), lambda i: (0, 0))),
        compiler_params=_cp("arbitrary"), name=name)(dm, gate, z)


def ple_bwd_ew(dh, e, pg, name):
    S = dh.shape[0]
    T = _tile(S)

    def body(dh_ref, e_ref, pg_ref, dp_ref, de_ref, db_ref):
        @pl.when(pl.program_id(0) == 0)
        def _():
            db_ref[...] = jnp.zeros_like(db_ref)

        d = dh_ref[...]
        g = pg_ref[...].astype(f32)
        dpre = d * e_ref[...].astype(f32) * g * (1.0 - g)
        dp_ref[...] = dpre.astype(bf16)
        de_ref[...] = (d * g).astype(bf16)
        db_ref[...] += jnp.sum(dpre, axis=0, keepdims=True)

    tok = BS((T, D), lambda i: (i, 0))
    return pl.pallas_call(
        body, out_shape=(SDS((S, D), bf16), SDS((S, D), bf16), SDS((1, D), f32)), grid=(S // T,),
        in_specs=[tok, tok, tok], out_specs=(tok, tok, BS((1, D), lambda i: (0, 0))),
        compiler_params=_cp("arbitrary"), name=name)(dh, e, pg)


_GK = 0.7978845608028654
_GC = 0.044715


def _gelu(x):
    return 0.5 * x * (1.0 + jnp.tanh(_GK * (x + _GC * (x * x * x))))


def _gelu_grad(x):
    x2 = x * x
    t = jnp.tanh(_GK * (x + _GC * (x * x2)))
    return 0.5 * (1.0 + t) + 0.5 * x * (1.0 - t * t) * (_GK * (1.0 + 3.0 * _GC * x2))


def _ln_stats(v):
    mu = jnp.mean(v, axis=-1, keepdims=True)
    vc = v - mu
    rs = lax.rsqrt(jnp.mean(vc * vc, axis=-1, keepdims=True) + EPS)
    return vc * rs, rs


def _ln_bwd(dvh, vh, rs):
    return rs * (dvh - jnp.mean(dvh, axis=-1, keepdims=True) - vh * jnp.mean(dvh * vh, axis=-1, keepdims=True))


def sg_fwd(proj, lg, lb, wm, bsb, name):
    S = proj.shape[0]
    T = _tile(S)
    cu, cv = OUR_COLS["sg_u"][0] // BW, OUR_COLS["sg_v"][0] // BW

    def body(u_ref, v_ref, lg_ref, lb_ref, wm_ref, bsb_ref, o_ref):
        for b in range(T // 128):
            rows = slice(b * 128, (b + 1) * 128)
            u = _gelu(u_ref[rows, :].astype(f32))
            vh, _ = _ln_stats(_gelu(v_ref[rows, :].astype(f32)))
            vb = (vh * lg_ref[...] + lb_ref[...]).astype(bf16)
            outs = []
            for g in range(4):
                cols = slice(g * 128, (g + 1) * 128)
                mixed = jnp.dot(wm_ref[g], vb[:, cols], preferred_element_type=f32) + bsb_ref[g]
                outs.append(u[:, cols] * mixed)
            o_ref[rows, :] = jnp.concatenate(outs, axis=1).astype(bf16)

    vec = BS((1, BW), lambda i: (0, 0))
    cube = BS((4, 128, 128), lambda i: (0, 0, 0))
    return pl.pallas_call(
        body, out_shape=SDS((S, 4 * BW), bf16), grid=(S // T,),
        in_specs=[BS((T, BW), lambda i: (i, cu)), BS((T, BW), lambda i: (i, cv)), vec, vec, cube, cube],
        out_specs=BS((T, BW), lambda i: (i, 0)), compiler_params=_cp("parallel"), name=name)(proj, proj, lg, lb, wm, bsb)


def sg_bwd(proj, dy, lg, lb, wm, bsb, maskf, name):
    S = proj.shape[0]
    T = _tile(S)
    cu, cv = OUR_COLS["sg_u"][0] // BW, OUR_COLS["sg_v"][0] // BW
    creg = OUR_COLS["sg_u"][0] // (2 * BW)

    def body(u_ref, v_ref, dy_ref, lg_ref, lb_ref, wm_ref, bsb_ref, mk_ref, dp_ref, dwm_ref, dbs_ref, dlg_ref, dlb_ref):
        @pl.when(pl.program_id(0) == 0)
        def _():
            dwm_ref[...] = jnp.zeros_like(dwm_ref)
            dbs_ref[...] = jnp.zeros_like(dbs_ref)
            dlg_ref[...] = jnp.zeros_like(dlg_ref)
            dlb_ref[...] = jnp.zeros_like(dlb_ref)

        for b in range(T // 128):
            rows = slice(b * 128, (b + 1) * 128)
            su = u_ref[rows, :].astype(f32)
            sv = v_ref[rows, :].astype(f32)
            dya = dy_ref[rows, :].astype(f32)
            u = _gelu(su)
            vh, rs = _ln_stats(_gelu(sv))
            vb = (vh * lg_ref[...] + lb_ref[...]).astype(bf16)
            dus, dvls = [], []
            for g in range(4):
                cols = slice(g * 128, (g + 1) * 128)
                mixed = jnp.dot(wm_ref[g], vb[:, cols], preferred_element_type=f32) + bsb_ref[g]
                dus.append(dya[:, cols] * mixed)
                dmg = dya[:, cols] * u[:, cols]
                dmb = dmg.astype(bf16)
                dbs_ref[g] += jnp.broadcast_to(jnp.sum(dmg, axis=1, keepdims=True), (128, 128))
                dwm_ref[g] += mk_ref[...] * lax.dot_general(dmb, vb[:, cols], (((1,), (1,)), ((), ())),
                                                            preferred_element_type=f32)
                dvls.append(lax.dot_general(wm_ref[g], dmb, (((0,), (0,)), ((), ())), preferred_element_type=f32))
            du = jnp.concatenate(dus, axis=1)
            dvln = jnp.concatenate(dvls, axis=1)
            dlg_ref[...] += jnp.sum(dvln * vh, axis=0, keepdims=True)
            dlb_ref[...] += jnp.sum(dvln, axis=0, keepdims=True)
            dv = _ln_bwd(dvln * lg_ref[...], vh, rs)
            dp_ref[rows, 0:BW] = (du * _gelu_grad(su)).astype(bf16)
            dp_ref[rows, BW:2 * BW] = (dv * _gelu_grad(sv)).astype(bf16)

    vec = BS((1, BW), lambda i: (0, 0))
    cube = BS((4, 128, 128), lambda i: (0, 0, 0))
    return pl.pallas_call(
        body,
        out_shape=(SDS((S, NP), bf16), SDS((4, 128, 128), f32), SDS((4, 128, 128), f32), SDS((1, BW), f32), SDS((1, BW), f32)),
        grid=(S // T,),
        in_specs=[BS((T, BW), lambda i: (i, cu)), BS((T, BW), lambda i: (i, cv)), BS((T, BW), lambda i: (i, 0)), vec, vec,
                  cube, cube, BS((128, 128), lambda i: (0, 0))],
        out_specs=(BS((T, 2 * BW), lambda i: (i, creg)), cube, cube, vec, vec),
        compiler_params=_cp("arbitrary"), name=name)(proj, proj, dy, lg, lb, wm, bsb, maskf)


_SUB = 64


def _conv_specs(S, T):
    ca, cg = OUR_COLS["c_a"][0] // BW, OUR_COLS["c_g"][0] // BW
    hb = T // HALO
    prev = lambda i: jnp.maximum(i * hb - 1, 0)
    return [BS((T, BW), lambda i: (i, ca)), BS((T, BW), lambda i: (i, cg)),
            BS((HALO, BW), lambda i: (prev(i), ca)), BS((HALO, BW), lambda i: (prev(i), cg))]


def _fill_shifts(sh):
    n = sh.shape[1] - 8
    for s in range(1, 8):
        sh[s, pl.ds(0, n), :] = sh[0, pl.ds(s, n), :]


def _shifted(sh, off, rows):
    s = off % 8
    return sh[s, pl.ds(off - s, rows), :]


def _conv_fill_ybuf(a_ref, g_ref, ap_ref, gp_ref, ysh):
    T = a_ref.shape[0]
    ysh[0, pl.ds(HALO, T), :] = a_ref[...].astype(f32) * jax.nn.sigmoid(g_ref[...].astype(f32))
    first = (pl.program_id(0) == 0).astype(f32)
    ysh[0, pl.ds(0, HALO), :] = (1.0 - first) * (ap_ref[...].astype(f32) * jax.nn.sigmoid(gp_ref[...].astype(f32)))
    _fill_shifts(ysh)


def _conv_taps(w_ref, ysh, r0):
    acc = jnp.zeros((_SUB, BW), f32)
    for k in range(CONV_K):
        acc = acc + w_ref[k:k + 1, :] * _shifted(ysh, r0 + HALO - (CONV_K - 1) + k, _SUB)
    return acc


def conv_fwd(proj, w, b, lg, lb, y, name):
    S = proj.shape[0]
    T = _tile(S)

    def body(a_ref, g_ref, ap_ref, gp_ref, w_ref, b_ref, lg_ref, lb_ref, y_in, o_ref, ybuf):
        del y_in
        _conv_fill_ybuf(a_ref, g_ref, ap_ref, gp_ref, ybuf)
        for sb in range(T // _SUB):
            z = _conv_taps(w_ref, ybuf, sb * _SUB) + b_ref[...]
            zh, _ = _ln_stats(z)
            zl = zh * lg_ref[...] + lb_ref[...]
            o_ref[pl.ds(sb * _SUB, _SUB), :] = (zl * jax.nn.sigmoid(zl)).astype(bf16)

    vec = BS((1, BW), lambda i: (0, 0))
    return pl.pallas_call(
        body, out_shape=SDS(y.shape, bf16), grid=(S // T,),
        in_specs=_conv_specs(S, T) + [BS((CONV_K, BW), lambda i: (0, 0)), vec, vec, vec, ANY],
        out_specs=BS((T, BW), lambda i: (i, 3)), scratch_shapes=[pltpu.VMEM((8, T + HALO, BW), f32)],
        input_output_aliases={8: 0}, compiler_params=_cp("parallel"), name=name)(proj, proj, proj, proj, w, b, lg, lb, y)


def conv_bwd_norm(proj, dy, w, b, lg, lb, name):
    S = proj.shape[0]
    T = _tile(S)

    def body(a_ref, g_ref, ap_ref, gp_ref, dy_ref, w_ref, b_ref, lg_ref, lb_ref, dz_ref, dlg_ref, dlb_ref, db_ref, ybuf):
        @pl.when(pl.program_id(0) == 0)
        def _():
            dlg_ref[...] = jnp.zeros_like(dlg_ref)
            dlb_ref[...] = jnp.zeros_like(dlb_ref)
            db_ref[...] = jnp.zeros_like(db_ref)

        _conv_fill_ybuf(a_ref, g_ref, ap_ref, gp_ref, ybuf)
        for sb in range(T // _SUB):
            rows = pl.ds(sb * _SUB, _SUB)
            z = _conv_taps(w_ref, ybuf, sb * _SUB) + b_ref[...]
            zh, rs = _ln_stats(z)
            zl = zh * lg_ref[...] + lb_ref[...]
            sg = jax.nn.sigmoid(zl)
            dzl = dy_ref[rows, :].astype(f32) * sg * (1.0 + zl * (1.0 - sg))
            dlg_ref[...] += jnp.sum(dzl * zh, axis=0, keepdims=True)
            dlb_ref[...] += jnp.sum(dzl, axis=0, keepdims=True)
            dz = _ln_bwd(dzl * lg_ref[...], zh, rs)
            db_ref[...] += jnp.sum(dz, axis=0, keepdims=True)
            dz_ref[rows, :] = dz

    vec = BS((1, BW), lambda i: (0, 0))
    tok = BS((T, BW), lambda i: (i, 0))
    return pl.pallas_call(
        body, out_shape=(SDS((S, BW), f32), SDS((1, BW), f32), SDS((1, BW), f32), SDS((1, BW), f32)), grid=(S // T,),
        in_specs=_conv_specs(S, T) + [BS((T, BW), lambda i: (i, 3)), BS((CONV_K, BW), lambda i: (0, 0)), vec, vec, vec],
        out_specs=(tok, vec, vec, vec), scratch_shapes=[pltpu.VMEM((8, T + HALO, BW), f32)],
        compiler_params=_cp("arbitrary"), name=name)(proj, proj, proj, proj, dy, w, b, lg, lb)


def conv_bwd_taps(proj, dz, w, dproj, name):
    S = proj.shape[0]
    T = _tile(S)
    nT = S // T
    hb = T // HALO
    creg = OUR_COLS["c_a"][0] // (2 * BW)

    def body(a_ref, g_ref, ap_ref, gp_ref, dz_ref, dzn_ref, w_ref, dp_in, dp_ref, dw_ref, ybuf, dzbuf, dwacc):
        del dp_in
        i = pl.program_id(0)

        @pl.when(i == 0)
        def _():
            dwacc[...] = jnp.zeros_like(dwacc)

        _conv_fill_ybuf(a_ref, g_ref, ap_ref, gp_ref, ybuf)
        dzbuf[0, pl.ds(0, T), :] = dz_ref[...]
        dzbuf[0, pl.ds(T, HALO), :] = (i < nT - 1).astype(f32) * dzn_ref[...]
        _fill_shifts(dzbuf)
        for sb in range(T // _SUB):
            r0 = sb * _SUB
            rows = pl.ds(r0, _SUB)
            dzs = dz_ref[rows, :]
            dyg = jnp.zeros((_SUB, BW), f32)
            for k in range(CONV_K):
                ysl = _shifted(ybuf, r0 + HALO - (CONV_K - 1) + k, _SUB)
                dwacc[pl.ds(k * 8, 8), :] += jnp.sum((dzs * ysl).reshape(_SUB // 8, 8, BW), axis=0)
                dyg = dyg + w_ref[k:k + 1, :] * _shifted(dzbuf, r0 + (CONV_K - 1) - k, _SUB)
            av = a_ref[rows, :].astype(f32)
            sg = jax.nn.sigmoid(g_ref[rows, :].astype(f32))
            dp_ref[rows, 0:BW] = (dyg * sg).astype(bf16)
            dp_ref[rows, BW:2 * BW] = (dyg * av * sg * (1.0 - sg)).astype(bf16)

        @pl.when(i == nT - 1)
        def _():
            for k in range(CONV_K):
                dw_ref[k:k + 1, :] = jnp.sum(dwacc[pl.ds(k * 8, 8), :], axis=0, keepdims=True)

    nxt = lambda i: jnp.minimum((i + 1) * hb, S // HALO - 1)
    return pl.pallas_call(
        body, out_shape=(SDS((S, NP), bf16), SDS((CONV_K, BW), f32)), grid=(nT,),
        in_specs=_conv_specs(S, T) + [BS((T, BW), lambda i: (i, 0)), BS((HALO, BW), lambda i: (nxt(i), 0)),
                                      BS((CONV_K, BW), lambda i: (0, 0)), ANY],
        out_specs=(BS((T, 2 * BW), lambda i: (i, creg)), BS((CONV_K, BW), lambda i: (0, 0))),
        scratch_shapes=[pltpu.VMEM((8, T + HALO, BW), f32), pltpu.VMEM((8, T + HALO, BW), f32),
                        pltpu.VMEM((CONV_K * 8, BW), f32)],
        input_output_aliases={7: 0}, compiler_params=_cp("arbitrary"), name=name)(proj, proj, proj, proj, dz, dz, w, dproj)


def _toeplitz_index():
    j = lax.broadcasted_iota(jnp.int32, (REL_TABLE, 1024), 1)
    t = lax.broadcasted_iota(jnp.int32, (REL_TABLE, 1024), 0)
    e = ((WIN - 1) - j) & 1023
    tidx = jnp.clip(e - (TQ - 1), -(CHUNK - 1), 256) + (CHUNK - 1)
    return (tidx == t).astype(f32)


def att_bias_build(table, name):
    H = table.shape[0]

    def body(t_ref, o_ref):
        u = jnp.dot(t_ref[...], _toeplitz_index(), precision=HI, preferred_element_type=f32)
        row = lax.broadcasted_iota(jnp.int32, (TQ, 1024), 0)
        r = lax.broadcasted_iota(jnp.int32, (TQ, WIN), 0)
        n = lax.broadcasted_iota(jnp.int32, (TQ, WIN), 1)
        dchunk = (r // CHUNK + 8) - n // CHUNK
        band = (dchunk >= 0) & (dchunk <= 8)
        for h in range(H):
            x = jnp.broadcast_to(u[h:h + 1, :], (TQ, 1024))
            for b in range(8):
                x = jnp.where(((row >> b) & 1) == 1, pltpu.roll(x, 1 << b, 1), x)
            o_ref[h] = jnp.where(band, x[:, :WIN], NEG_INF)

    return pl.pallas_call(body, out_shape=SDS((H, TQ, WIN), f32), compiler_params=pltpu.CompilerParams(vmem_limit_bytes=VMEM_LIMIT),
                          name=name)(table)


def att_bias_grad(dbias, name):
    H = dbias.shape[0]

    def body(d_ref, o_ref):
        row = lax.broadcasted_iota(jnp.int32, (TQ, 1024), 0)
        rows = []
        for h in range(H):
            x = jnp.concatenate([d_ref[h], jnp.zeros((TQ, 1024 - WIN), f32)], axis=1)
            for b in range(8):
                x = jnp.where(((row >> b) & 1) == 1, pltpu.roll(x, 1024 - (1 << b), 1), x)
            rows.append(jnp.sum(x, axis=0, keepdims=True))
        du = jnp.concatenate(rows, axis=0)
        o_ref[...] = lax.dot_general(du, _toeplitz_index(), (((1,), (1,)), ((), ())), precision=HI,
                                     preferred_element_type=f32)

    return pl.pallas_call(body, out_shape=SDS((H, REL_TABLE), f32), compiler_params=pltpu.CompilerParams(vmem_limit_bytes=VMEM_LIMIT),
                          name=name)(dbias)


def _att_specs():
    cq, ck, cv = (OUR_COLS[n][0] // BW for n in ("a_q", "a_k", "a_v"))
    specs = [BS((TQ, BW), lambda i: (i, cq))]
    for col in (ck, cv):
        for back in (2, 1, 0):
            specs.append(BS((TQ, BW), functools.partial(lambda i, back, col: (jnp.maximum(i - back, 0), col), back=back, col=col)))
    return specs


def _att_pen(i):
    n = lax.broadcasted_iota(jnp.int32, (1, WIN), 1)
    return jnp.where(n + (i - 2) * TQ >= 0, 0.0, NEG_INF).astype(f32)


def _att_probs(qa, kp, bias_h, pen):
    s = lax.dot_general(qa, kp, (((1,), (1,)), ((), ())), preferred_element_type=f32) + bias_h + pen
    e = jnp.exp(s - jnp.max(s, axis=-1, keepdims=True))
    return e * (1.0 / jnp.sum(e, axis=-1, keepdims=True))


def att_fwd(proj, bias, y, name, gather=None):
    S = proj.shape[0]
    ng = len(gather) if gather else 0

    def body(q_ref, k2, k1, k0, v2, v1, v0, b_ref, y_in, *rest):
        del y_in
        g_in, o_ref, g_out = rest[:ng], rest[ng], rest[ng + 1:2 * ng + 1]
        i = pl.program_id(0)
        if ng:
            kwin, vwin, send_sems, recv_sems = rest[2 * ng + 1:]
            _ag4_over_grid(g_in, g_out, send_sems, recv_sems, i, S // TQ, AG_FORWARD_ATTENTION)
        else:
            kwin, vwin = rest[1:]
        for w, (kr, vr) in enumerate(((k2, v2), (k1, v1), (k0, v0))):
            kwin[pl.ds(w * TQ, TQ), :] = kr[...]
            vwin[pl.ds(w * TQ, TQ), :] = vr[...]
        pen = _att_pen(i)
        lo = lax.broadcasted_iota(jnp.int32, (TQ, 128), 1) < 64
        for hp in range(4):
            cols = slice(hp * 128, (hp + 1) * 128)
            qp, kp, vp = q_ref[:, cols] * jnp.asarray(0.125, bf16), kwin[:, cols], vwin[:, cols]
            outs = []
            for a in range(2):
                qa = jnp.where(lo if a == 0 else ~lo, qp, jnp.zeros_like(qp))
                p = _att_probs(qa, kp, b_ref[2 * hp + a], pen)
                outs.append(jnp.dot(p.astype(bf16), vp, preferred_element_type=f32))
            o_ref[:, cols] = jnp.where(lo, outs[0], outs[1]).astype(bf16)

    outs = pl.pallas_call(
        body, out_shape=(SDS(y.shape, bf16),) + tuple(_ag4_out_shapes(gather or [])), grid=(S // TQ,),
        in_specs=_att_specs() + [BS((8, TQ, WIN), lambda i: (0, 0, 0), pipeline_mode=pl.Buffered(1)), ANY] + [ANY] * ng,
        out_specs=(BS((TQ, BW), lambda i: (i, 2)),) + tuple(ANY for _ in range(ng)),
        scratch_shapes=[pltpu.VMEM((WIN, BW), bf16), pltpu.VMEM((WIN, BW), bf16)] + (_ag4_sems(ng) if ng else []),
        input_output_aliases={8: 0}, compiler_params=_cp("arbitrary" if ng else "parallel"), name=name)(
            proj, proj, proj, proj, proj, proj, proj, bias, y, *(gather or []))
    return (outs[0], list(outs[1:])) if ng else outs[0]


def att_bwd(proj, y, dy, bias, dproj, name):
    S = proj.shape[0]
    cq = OUR_COLS["a_q"][0] // BW

    def body(q_ref, k2, k1, k0, v2, v1, v0, b_ref, o_ref, do_ref, dp_in, dq_ref, dkp_ref, dvp_ref, db_ref, kwin, vwin):
        del dp_in
        i = pl.program_id(0)

        @pl.when(i == 0)
        def _():
            db_ref[...] = jnp.zeros_like(db_ref)

        for w, (kr, vr) in enumerate(((k2, v2), (k1, v1), (k0, v0))):
            kwin[pl.ds(w * TQ, TQ), :] = kr[...]
            vwin[pl.ds(w * TQ, TQ), :] = vr[...]
        pen = _att_pen(i)
        lo = lax.broadcasted_iota(jnp.int32, (TQ, 128), 1) < 64
        for hp in range(4):
            cols = slice(hp * 128, (hp + 1) * 128)
            qp, kp, vp = q_ref[:, cols] * jnp.asarray(0.125, bf16), kwin[:, cols], vwin[:, cols]
            dop, op = do_ref[:, cols], o_ref[:, cols]
            dqs = []
            dk = jnp.zeros((WIN, 128), f32)
            dv = jnp.zeros((WIN, 128), f32)
            for a in range(2):
                sel = lo if a == 0 else ~lo
                qa = jnp.where(sel, qp, jnp.zeros_like(qp))
                doa = jnp.where(sel, dop, jnp.zeros_like(dop))
                p = _att_probs(qa, kp, b_ref[2 * hp + a], pen)
                dpv = lax.dot_general(doa, vp, (((1,), (1,)), ((), ())), preferred_element_type=f32)
                delta = jnp.sum(doa.astype(f32) * op.astype(f32), axis=-1, keepdims=True)
                ds = p * (dpv - delta)
                db_ref[2 * hp + a] += ds
                dsb = ds.astype(bf16)
                dqs.append(jnp.dot(dsb, kp, preferred_element_type=f32))
                dk = dk + lax.dot_general(dsb, qa, (((0,), (0,)), ((), ())), preferred_element_type=f32)
                dv = dv + lax.dot_general(p.astype(bf16), doa, (((0,), (0,)), ((), ())), preferred_element_type=f32)
            dq_ref[:, cols] = (jnp.where(lo, dqs[0], dqs[1]) * 0.125).astype(bf16)
            for w in range(3):
                dkp_ref[w, :, cols] = dk[w * TQ:(w + 1) * TQ].astype(bf16)
                dvp_ref[w, :, cols] = dv[w * TQ:(w + 1) * TQ].astype(bf16)

    tok = BS((TQ, BW), lambda i: (i, 2))
    part = BS((3, TQ, BW), lambda i: (0, i, 0))
    full = BS((8, TQ, WIN), lambda i: (0, 0, 0))
    return pl.pallas_call(
        body, out_shape=(SDS((S, NP), bf16), SDS((3, S, BW), bf16), SDS((3, S, BW), bf16), SDS((8, TQ, WIN), f32)),
        grid=(S // TQ,),
        in_specs=_att_specs() + [BS((8, TQ, WIN), lambda i: (0, 0, 0), pipeline_mode=pl.Buffered(1)), tok, tok, ANY],
        out_specs=(BS((TQ, BW), lambda i: (i, cq)), part, part, full),
        scratch_shapes=[pltpu.VMEM((WIN, BW), bf16), pltpu.VMEM((WIN, BW), bf16)],
        input_output_aliases={10: 0}, compiler_params=_cp("arbitrary"), name=name)(
            proj, proj, proj, proj, proj, proj, proj, bias, y, dy, dproj)


def att_shift_add(dkp, dvp, dproj, name):
    S = dkp.shape[1]
    nT = S // TQ
    creg = OUR_COLS["a_k"][0] // (2 * BW)

    def body(k2, k1, k0, v2, v1, v0, dp_in, dp_ref):
        del dp_in
        j = pl.program_id(0)
        m1 = (j + 1 < nT).astype(f32)
        m0 = (j + 2 < nT).astype(f32)
        dp_ref[:, 0:BW] = (k2[0].astype(f32) + m1 * k1[0].astype(f32) + m0 * k0[0].astype(f32)).astype(bf16)
        dp_ref[:, BW:2 * BW] = (v2[0].astype(f32) + m1 * v1[0].astype(f32) + m0 * v0[0].astype(f32)).astype(bf16)

    def spec(w):
        return BS((1, TQ, BW), functools.partial(lambda j, w: (w, jnp.minimum(j + 2 - w, nT - 1), 0), w=w))

    return pl.pallas_call(
        body, out_shape=SDS(dproj.shape, bf16), grid=(nT,),
        in_specs=[spec(2), spec(1), spec(0), spec(2), spec(1), spec(0), ANY],
        out_specs=BS((TQ, 2 * BW), lambda j: (j, creg)),
        input_output_aliases={6: 0}, compiler_params=_cp("parallel"), name=name)(dkp, dkp, dkp, dvp, dvp, dvp, dproj)


GQ, GV = 256, 512
TGC = 8


def _bd_mask():
    r = lax.broadcasted_iota(jnp.int32, (GQ, GV), 0) // 64
    c = lax.broadcasted_iota(jnp.int32, (GQ, GV), 1) // 128
    return (r == c).astype(f32)


def _tri(strict):
    r = lax.broadcasted_iota(jnp.int32, (CHUNK, CHUNK), 0)
    c = lax.broadcasted_iota(jnp.int32, (CHUNK, CHUNK), 1)
    return ((c < r) if strict else (c <= r)).astype(f32)


def _compact(s_bd):
    return jnp.concatenate([s_bd[h * 64:(h + 1) * 64, h * 128:(h + 1) * 128] for h in range(4)], axis=0)


def _expand(comp, mask):
    return jnp.tile(comp, (1, 4)) * mask


def _gla_gates(alr, wa_ref, ba_ref, tri_incl, ones_col):
    a = jnp.dot(alr, wa_ref[...], preferred_element_type=f32) + ba_ref[...]
    la = (jnp.minimum(a, 0.0) - jnp.log(1.0 + jnp.exp(-jnp.abs(a)))) * (1.0 / 16.0)
    cum = jnp.dot(tri_incl, la, precision=HI, preferred_element_type=f32)
    tot_row = cum[CHUNK - 1:CHUNK, :]
    tot_col = lax.dot_general(la, ones_col, (((0,), (0,)), ((), ())), precision=HI, preferred_element_type=f32)
    return a, cum, tot_row, jnp.tile(jnp.exp(tot_col), (1, 4))


def _head_norm(o):
    rns, ons = [], []
    for h in range(4):
        oh = o[:, h * 128:(h + 1) * 128]
        rn = lax.rsqrt(jnp.mean(oh * oh, axis=-1, keepdims=True) + EPS)
        rns.append(rn)
        ons.append(oh * rn)
    return rns, ons


def _gla_in_specs(T, imap):
    cq, ck = OUR_COLS["g_q"][0] // GQ, OUR_COLS["g_k"][0] // GQ
    cv, cr = OUR_COLS["g_v"][0] // GV, OUR_COLS["g_r"][0] // GV
    return [BS((T, GQ), lambda i: (imap(i), cq)), BS((T, GQ), lambda i: (imap(i), ck)), BS((T, GV), lambda i: (imap(i), cv)),
            BS((T, GV), lambda i: (imap(i), cr)), BS((T, RANKP), lambda i: (imap(i), 0))]


def gla_fwd(proj, pa, wa, ba, ng, y, name):
    S = proj.shape[0]
    T = min(TGC * CHUNK, S)
    nch = T // CHUNK

    def body(q_ref, k_ref, v_ref, r_ref, a_ref, wa_ref, ba_ref, ng_ref, y_in, y_ref, st_ref, s_scr):
        del y_in

        @pl.when(pl.program_id(0) == 0)
        def _():
            s_scr[...] = jnp.zeros_like(s_scr)

        mask = _bd_mask()
        tri = _tri(False)
        ones_col = jnp.ones((CHUNK, 128), f32)

        s_bd = s_scr[...]
        for ci in range(nch):
            rows = pl.ds(ci * CHUNK, CHUNK)
            _, cum, tot_row, dec4 = _gla_gates(a_ref[rows, :], wa_ref, ba_ref, tri, ones_col)
            kd = (k_ref[rows, :].astype(f32) * jnp.exp(tot_row - cum)).astype(bf16)
            upd = lax.dot_general(kd, v_ref[rows, :], (((0,), (0,)), ((), ())), preferred_element_type=f32) * mask
            s_bd = dec4 * s_bd + upd
            st_ref[pl.ds(ci * GQ, GQ), :] = _compact(s_bd)
            qs = (q_ref[rows, :].astype(f32) * 0.125).astype(bf16)
            o = jnp.dot(qs, s_bd.astype(bf16), preferred_element_type=f32)
            _, ons = _head_norm(o)
            rv = r_ref[rows, :].astype(f32)
            y_ref[rows, :] = (jnp.concatenate(ons, axis=1) * ng_ref[...] * (rv * jax.nn.sigmoid(rv))).astype(bf16)
        s_scr[...] = s_bd

    return pl.pallas_call(
        body, out_shape=(SDS(y.shape, bf16), SDS((S // CHUNK * GQ, 128), f32)), grid=(S // T,),
        in_specs=_gla_in_specs(T, lambda i: i) + [BS((RANKP, GQ), lambda i: (0, 0)), BS((1, GQ), lambda i: (0, 0)),
                                                  BS((1, GV), lambda i: (0, 0)), ANY],
        out_specs=(BS((T, GV), lambda i: (i, 1)), BS((nch * GQ, 128), lambda i: (i, 0))),
        scratch_shapes=[pltpu.VMEM((GQ, GV), f32)], input_output_aliases={8: 0}, compiler_params=_cp("arbitrary"),
        name=name)(proj, proj, proj, proj, pa, wa, ba, ng, y)


def gla_bwd(proj, pa, states, dy, wa, ba, ng, dproj, name):
    S = proj.shape[0]
    T = min(TGC * CHUNK, S)
    nch = T // CHUNK
    nT = S // T
    rev = lambda i: nT - 1 - i

    def body(q_ref, k_ref, v_ref, r_ref, a_ref, st_ref, sp_ref, dy_ref, wa_ref, ba_ref, ng_ref, dp_in,
             dp_ref, da_ref, dwa_ref, dba_ref, dng_ref, g_scr):
        del dp_in
        i = pl.program_id(0)

        @pl.when(i == 0)
        def _():
            g_scr[...] = jnp.zeros_like(g_scr)
            dwa_ref[...] = jnp.zeros_like(dwa_ref)
            dba_ref[...] = jnp.zeros_like(dba_ref)
            dng_ref[...] = jnp.zeros_like(dng_ref)

        mask = _bd_mask()
        tri = _tri(False)
        tri_strict = _tri(True)
        ones_col = jnp.ones((CHUNK, 128), f32)
        ones_row = jnp.ones((8, 128), f32)
        first_tile = (i == nT - 1).astype(f32)

        g_carry = g_scr[...]
        for ci in reversed(range(nch)):
            rows = pl.ds(ci * CHUNK, CHUNK)
            alr = a_ref[rows, :]
            a, cum, tot_row, dec4 = _gla_gates(alr, wa_ref, ba_ref, tri, ones_col)
            wdec = jnp.exp(tot_row - cum)
            kdf = k_ref[rows, :].astype(f32) * wdec
            kd = kdf.astype(bf16)
            s_c = _expand(st_ref[pl.ds(ci * GQ, GQ), :], mask)
            prev = st_ref[pl.ds((ci - 1) * GQ, GQ), :] if ci > 0 else sp_ref[...] * (1.0 - first_tile)
            qs = (q_ref[rows, :].astype(f32) * 0.125).astype(bf16)
            s_cb = s_c.astype(bf16)
            o = jnp.dot(qs, s_cb, preferred_element_type=f32)
            rns, ons = _head_norm(o)
            on = jnp.concatenate(ons, axis=1)
            rv = r_ref[rows, :].astype(f32)
            sg = jax.nn.sigmoid(rv)
            sr = rv * sg
            dyv = dy_ref[rows, :].astype(f32)
            ngv = ng_ref[...]
            dng_ref[...] += jnp.sum(dyv * on * sr, axis=0, keepdims=True)
            d_on = dyv * ngv * sr
            dr = dyv * on * ngv * (sg * (1.0 + rv * (1.0 - sg)))
            dos = []
            for h in range(4):
                cols = slice(h * 128, (h + 1) * 128)
                dh_ = d_on[:, cols]
                dos.append(rns[h] * (dh_ - ons[h] * jnp.mean(dh_ * ons[h], axis=-1, keepdims=True)))
            do = jnp.concatenate(dos, axis=1).astype(bf16)
            dq = lax.dot_general(do, s_cb, (((1,), (1,)), ((), ())), preferred_element_type=f32) * 0.125
            ds = lax.dot_general(qs, do, (((0,), (0,)), ((), ())), preferred_element_type=f32) * mask + g_carry
            ddec_row = lax.dot_general(ones_row, _compact(ds) * prev, (((1,), (1,)), ((), ())), precision=HI,
                                       preferred_element_type=f32)[0:1, :]
            dsb = ds.astype(bf16)
            dkd = lax.dot_general(v_ref[rows, :], dsb, (((1,), (1,)), ((), ())), preferred_element_type=f32)
            dv = jnp.dot(kd, dsb, preferred_element_type=f32)
            g_carry = dec4 * ds
            dk = dkd * wdec
            dwlog = dkd * kdf
            dla = ddec_row * jnp.exp(tot_row) + jnp.dot(tri_strict, dwlog, precision=HI, preferred_element_type=f32)
            da = dla * (1.0 - jax.nn.sigmoid(a)) * (1.0 / 16.0)
            dab = da.astype(bf16)
            da_ref[rows, :] = lax.dot_general(dab, wa_ref[...], (((1,), (1,)), ((), ())),
                                              preferred_element_type=f32).astype(bf16)
            dwa_ref[...] += lax.dot_general(alr, dab, (((0,), (0,)), ((), ())), preferred_element_type=f32)
            dba_ref[...] += jnp.sum(da, axis=0, keepdims=True)
            dp_ref[rows, 0:GQ] = dq.astype(bf16)
            dp_ref[rows, GQ:2 * GQ] = dk.astype(bf16)
            dp_ref[rows, 2 * GQ:2 * GQ + GV] = dv.astype(bf16)
            dp_ref[rows, 2 * GQ + GV:2 * GQ + 2 * GV] = dr.astype(bf16)
        g_scr[...] = g_carry

    REG = 2 * GQ + 2 * GV
    return pl.pallas_call(
        body,
        out_shape=(SDS((S, NP), bf16), SDS((S, RANKP), bf16), SDS((RANKP, GQ), f32), SDS((1, GQ), f32), SDS((1, GV), f32)),
        grid=(nT,),
        in_specs=_gla_in_specs(T, rev) + [
            BS((nch * GQ, 128), lambda i: (rev(i), 0)),
            BS((GQ, 128), lambda i: (jnp.maximum(rev(i) * nch - 1, 0), 0)),
            BS((T, GV), lambda i: (rev(i), 1)),
            BS((RANKP, GQ), lambda i: (0, 0)), BS((1, GQ), lambda i: (0, 0)), BS((1, GV), lambda i: (0, 0)), ANY],
        out_specs=(BS((T, REG), lambda i: (rev(i), 0)), BS((T, RANKP), lambda i: (rev(i), 0)),
                   BS((RANKP, GQ), lambda i: (0, 0)), BS((1, GQ), lambda i: (0, 0)), BS((1, GV), lambda i: (0, 0))),
        scratch_shapes=[pltpu.VMEM((GQ, GV), f32)],
        input_output_aliases={11: 0}, compiler_params=_cp("arbitrary"), name=name)(
            proj, proj, proj, proj, pa, states, states, dy, wa, ba, ng, dproj)


def _as2d(a):
    if a.ndim == 1:
        return a.reshape(1, a.shape[0])
    return a.reshape(-1, a.shape[-1])


def adamw(w, g, m, v, name):
    shape = w.shape
    w2, g2, m2, v2 = (_as2d(a) for a in (w, g, m, v))
    R, C = w2.shape
    tr = R
    for cand in (512, 256, 128, 64, 32, 16, 8):
        if R % cand == 0 and cand * C * 4 * 7 * 2 <= 40 * 1024 * 1024:
            tr = cand
            break

    def body(w_ref, g_ref, m_ref, v_ref, d_ref, mo_ref, vo_ref):
        gv = g_ref[...]
        mn = ADAM_B1 * m_ref[...] + (1.0 - ADAM_B1) * gv
        vn = ADAM_B2 * v_ref[...] + (1.0 - ADAM_B2) * (gv * gv)
        m_hat = mn / (1.0 - ADAM_B1 ** ADAM_STEP)
        v_hat = vn / (1.0 - ADAM_B2 ** ADAM_STEP)
        d_ref[...] = -ADAM_LR * (m_hat / (jnp.sqrt(v_hat) + ADAM_EPS) + ADAM_WD * w_ref[...])
        mo_ref[...] = mn
        vo_ref[...] = vn

    blk = BS((tr, C), lambda i: (i, 0))
    outs = pl.pallas_call(body, out_shape=tuple(SDS((R, C), f32) for _ in range(3)), grid=(R // tr,),
                          in_specs=[blk] * 4, out_specs=(blk,) * 3, compiler_params=_cp("parallel"), name=name)(w2, g2, m2, v2)
    return tuple(o.reshape(shape) for o in outs)


def _row_tile(rows, row_bytes, budget=4 * 1024 * 1024):
    best = None
    for t in range(16, rows + 1, 16):
        if rows % t == 0 and t * row_bytes <= budget:
            best = t
    return best or rows


def add_halves(g0, g1, ra, c, name):
    shape = ra.shape
    cols = shape[-1]
    rows = int(np.prod(shape[:-1]))
    tr = _row_tile(rows, cols * 2)
    blk = lambda: BS((tr, cols), lambda i, c_ref: (i, 0))
    grid_spec = pltpu.PrefetchScalarGridSpec(num_scalar_prefetch=1, grid=(rows // tr,), in_specs=[blk(), blk(), blk()],
                                             out_specs=blk())

    def body(c_ref, a0_ref, a1_ref, b_ref, o_ref):
        mine = jnp.where(c_ref[0] == 0, a0_ref[...], a1_ref[...])
        o_ref[...] = (mine.astype(f32) + b_ref[...].astype(f32)).astype(bf16)

    out = pl.pallas_call(body, out_shape=SDS((rows, cols), bf16), grid_spec=grid_spec, compiler_params=_cp("parallel"),
                         name=name)(jnp.reshape(c, (1,)).astype(jnp.int32), g0.reshape(rows, cols), g1.reshape(rows, cols),
                                    ra.reshape(rows, cols))
    return out.reshape(shape)


def reduce_chips(rb, own, c, chip, name):
    shape = rb.shape[1:]
    cols = shape[-1]
    rows = int(np.prod(shape[:-1]))
    tr = _row_tile(rows, cols * 2 * 4)
    rb3, own3 = rb.reshape(4, rows, cols), own.reshape(4, rows, cols)

    def body(s_ref, own_ref, r1, r2, r3, o_ref):
        del s_ref
        o_ref[0] = ((own_ref[0].astype(f32) + r1[0].astype(f32)) + r2[0].astype(f32)) + r3[0].astype(f32)

    def slot(k):
        return BS((1, tr, cols), functools.partial(lambda i, s, k: ((s[1] + k) % 4, i, 0), k=k))

    grid_spec = pltpu.PrefetchScalarGridSpec(
        num_scalar_prefetch=1, grid=(rows // tr,), in_specs=[slot(0), slot(1), slot(2), slot(3)],
        out_specs=BS((1, tr, cols), lambda i, s: (s[0], i, 0)))
    out = pl.pallas_call(body, out_shape=SDS((2, rows, cols), f32), grid_spec=grid_spec, compiler_params=_cp("parallel"),
                         name=name)(jnp.stack([c, chip]).astype(jnp.int32), own3, rb3, rb3, rb3)
    return out.reshape((2,) + shape)


def sum_slots(x, name):
    N, shape = x.shape[0], x.shape[1:]
    cols = shape[-1]
    rows = int(np.prod(shape[:-1]))
    tr = _row_tile(rows, cols * x.dtype.itemsize * N)

    def body(x_ref, o_ref):
        acc = x_ref[0].astype(f32)
        for n in range(1, N):
            acc = acc + x_ref[n].astype(f32)
        o_ref[...] = acc

    out = pl.pallas_call(body, out_shape=SDS((rows, cols), f32), grid=(rows // tr,),
                         in_specs=[BS((N, tr, cols), lambda i: (0, i, 0))], out_specs=BS((tr, cols), lambda i: (i, 0)),
                         compiler_params=_cp("parallel"), name=name)(x.reshape(N, rows, cols))
    return out.reshape(shape)


def _me():
    return lax.axis_index("x"), lax.axis_index("y"), lax.axis_index("c")


def _rcopy(src, dst, send_sems, recv_sems, k, dev):
    return pltpu.make_async_remote_copy(src_ref=src, dst_ref=dst, send_sem=send_sems.at[k], recv_sem=recv_sems.at[k],
                                        device_id=dev, device_id_type=MESH)


def _comm_call(body, ins, out_shapes, n_remote, name, aliases=None):
    return pl.pallas_call(
        body, out_shape=tuple(out_shapes), in_specs=[ANY] * len(ins), out_specs=tuple(ANY for _ in out_shapes),
        scratch_shapes=[pltpu.SemaphoreType.DMA((n_remote,)), pltpu.SemaphoreType.DMA((n_remote,))],
        input_output_aliases=aliases or {}, name=name)(*ins)


def _ag4_out_shapes(bufs):
    return [SDS((2, 4) + b.shape[1:], b.dtype) for b in bufs]


def _ag4_sems(n):
    return [pltpu.SemaphoreType.DMA((8 * n,)), pltpu.SemaphoreType.DMA((8 * n,))]


def _ag4_phases(xs, os, send_sems, recv_sems):
    n = len(xs)

    def place():
        x, y, c = _me()
        return x, y, c, 2 * x + y, (x, y, 1 - c), [(1 - x, y), (x, 1 - y), (1 - x, 1 - y)]

    def sends():
        x, y, c, j, sib, chips = place()
        first = [_rcopy(xs[t].at[c], os[t].at[c, j], send_sems, recv_sems, 8 * t + k, (cx, cy, c))
                 for t in range(n) for k, (cx, cy) in enumerate(chips)]
        own = [_rcopy(xs[t].at[l], os[t].at[l, j], send_sems, recv_sems, 8 * t + 6 + l, sib) for t in range(n) for l in range(2)]
        return first + own

    def forwards():
        x, y, c, j, sib, chips = place()
        return [(_rcopy(os[t].at[c, 2 * cx + cy], os[t].at[c, 2 * cx + cy], send_sems, recv_sems, 8 * t + k, (x, y, c)),
                 _rcopy(os[t].at[c, 2 * cx + cy], os[t].at[c, 2 * cx + cy], send_sems, recv_sems, 8 * t + 3 + k, sib))
                for k, (cx, cy) in enumerate(chips) for t in range(n)]

    def start():
        for cp in sends():
            cp.start()

    def forward():
        for landed, fwd in forwards():
            landed.wait_recv()
            fwd.start()

    def finish():
        x, y, c, j, sib, chips = place()
        for t in range(n):
            for l in range(2):
                land = os[t].at[l, j]
                _rcopy(land, land, send_sems, recv_sems, 8 * t + 6 + l, (x, y, c)).wait_recv()
        for k, (cx, cy) in enumerate(chips):
            for t in range(n):
                land = os[t].at[1 - c, 2 * cx + cy]
                _rcopy(land, land, send_sems, recv_sems, 8 * t + 3 + k, (x, y, c)).wait_recv()
        for cp in sends() + [fwd for _, fwd in forwards()]:
            cp.wait_send()

    return start, forward, finish


def _ag4_over_grid(xs, os, send_sems, recv_sems, step, nsteps, forward_frac):
    start, forward, finish = _ag4_phases(xs, os, send_sems, recv_sems)
    pl.when(step == 0)(start)
    pl.when(step == min(nsteps - 1, int(nsteps * forward_frac)))(forward)
    pl.when(step == nsteps - 1)(finish)


def sib_other_layer(g0s, g1s, name):
    n = len(g0s)

    def body(*refs):
        layers, os = (refs[:n], refs[n:2 * n]), refs[2 * n:3 * n]
        send_sems, recv_sems = refs[3 * n:]
        x, y, c = _me()
        for mine in range(2):
            @pl.when(c == mine)
            def _():
                cps = [_rcopy(layers[1 - mine][t], os[t], send_sems, recv_sems, t, (x, y, 1 - c)) for t in range(n)]
                for cp in cps:
                    cp.start()
                for cp in cps:
                    cp.wait()

    return _comm_call(body, list(g0s) + list(g1s), [SDS(g.shape, g.dtype) for g in g0s], n, name)


def a2a4(ps, name):
    n = len(ps)

    def body(*refs):
        xs, os = refs[:n], refs[n:2 * n]
        send_sems, recv_sems = refs[2 * n:]
        x, y, c = _me()
        j = 2 * x + y
        chips = [(1 - x, y), (x, 1 - y), (1 - x, 1 - y)]
        sends = [_rcopy(xs[t].at[2 * cx + cy], os[t].at[j], send_sems, recv_sems, 3 * t + k, (cx, cy, c))
                 for t in range(n) for k, (cx, cy) in enumerate(chips)]
        for cp in sends:
            cp.start()
        for t in range(n):
            for k, (cx, cy) in enumerate(chips):
                land = os[t].at[2 * cx + cy]
                _rcopy(land, land, send_sems, recv_sems, 3 * t + k, (x, y, c)).wait_recv()
        for cp in sends:
            cp.wait_send()

    return _comm_call(body, ps, [SDS(p.shape, p.dtype) for p in ps], 3 * n, name)


def ag2(bufs, name):
    n = len(bufs)

    def body(*refs):
        xs, os = refs[:n], refs[n:2 * n]
        send_sems, recv_sems = refs[2 * n:]
        x, y, c = _me()
        cps = [_rcopy(xs[t].at[c], os[t].at[c], send_sems, recv_sems, t, (x, y, 1 - c)) for t in range(n)]
        for cp in cps:
            cp.start()
        for t in range(n):
            land = os[t].at[1 - c]
            _rcopy(land, land, send_sems, recv_sems, t, (x, y, c)).wait_recv()
        for cp in cps:
            cp.wait_send()

    return _comm_call(body, bufs, [SDS(b.shape, b.dtype) for b in bufs], n, name, aliases={t: t for t in range(n)})


def ag8(blk, name):
    m_per, n = blk.shape

    def body(x_ref, out_ref, send_sems, recv_sems, local_sem):
        x, y, c = _me()
        me, sibling = (x, y, c), (x, y, 1 - c)
        chips = [(1 - x, y), (x, 1 - y), (1 - x, 1 - y)]

        def rows(px, py, pc):
            return out_ref.at[pl.ds((4 * px + 2 * py + pc) * m_per, m_per), :]

        def copy(k, block, to, src=None):
            return pltpu.make_async_remote_copy(
                src_ref=rows(*block) if src is None else src, dst_ref=rows(*block), send_sem=send_sems.at[k],
                recv_sem=recv_sems.at[k], device_id=to, device_id_type=MESH)

        mine = pltpu.make_async_copy(x_ref, rows(*me), local_sem)
        mine.start()
        first = [copy(0, me, sibling, src=x_ref)]
        first += [copy(1 + j, me, (*chip, c), src=x_ref) for j, chip in enumerate(chips)]
        for cp in first:
            cp.start()
        passed = [copy(4 + j, (*chip, c), sibling) for j, chip in enumerate(chips)]
        for j, chip in enumerate(chips):
            copy(1 + j, (*chip, c), me).wait_recv()
            passed[j].start()
        copy(0, sibling, me).wait_recv()
        for j, chip in enumerate(chips):
            copy(4 + j, (*chip, 1 - c), me).wait_recv()
        for cp in first + passed:
            cp.wait_send()
        mine.wait()

    return pl.pallas_call(
        body, out_shape=SDS((8 * m_per, n), blk.dtype), in_specs=[pl.BlockSpec(memory_space=pltpu.VMEM)],
        out_specs=pl.BlockSpec(memory_space=pltpu.VMEM),
        scratch_shapes=[pltpu.SemaphoreType.DMA((7,)), pltpu.SemaphoreType.DMA((7,)), pltpu.SemaphoreType.DMA],
        name=name)(blk)


def _split_chips(full, axis):
    n = full.shape[axis] // 4
    parts = full.reshape(full.shape[:axis] + (4, n) + full.shape[axis + 1:])
    return jnp.moveaxis(parts, axis, 0)


def _merge_chips(gathered, axis):
    parts = jnp.moveaxis(gathered, 0, axis)
    return parts.reshape(parts.shape[:axis] + (parts.shape[axis] * parts.shape[axis + 1],) + parts.shape[axis + 2:])


def _to_ref_cols(main, rank):
    pieces = []
    for n, width in REF_SPLITS:
        if n == "g_a":
            pieces.append(rank[..., :RANK])
        else:
            off = OUR_COLS[n][0]
            pieces.append(main[..., off:off + width])
    return jnp.concatenate(pieces, axis=-1)


def _from_ref_cols(w):
    offs, o = {}, 0
    for n, width in REF_SPLITS:
        offs[n] = (o, width)
        o += width
    main = jnp.concatenate([w[..., offs[n][0]:offs[n][0] + offs[n][1]] for n in sorted(OUR_COLS, key=lambda k: OUR_COLS[k][0])],
                           axis=-1)
    ro = offs["g_a"][0]
    rank = jnp.pad(w[..., ro:ro + RANK], [(0, 0)] * (w.ndim - 1) + [(0, RANKP - RANK)])
    return main, rank


def _layer_fwd(h, p_i, W, li, late=None, xn=None):
    t = f"l{li}_"
    sv = {"h0": h}

    def arrived(names, gathered, Ws):
        for l, Wl in enumerate(Ws):
            Wl.update(_prep_layer_weights(dict(zip(names, gathered)), None, l))

    if xn is None:
        xn = rms_fwd(h, W["norm1_g"], t + "rms1")
    if late is None:
        proj = mm_nn(xn, W["w_in_main"], name=t + "inproj")
    else:
        (names, shards, Ws) = late[1]
        proj, gathered = mm_nn(xn, W["w_in_main"], name=t + "inproj", gather=shards)
        arrived(names, gathered, Ws)
    pa = mm_nn(xn, W["w_in_rank"], name=t + "inproj_rank")
    y = sg_fwd(proj, W["sg_ln_g"], W["sg_ln_b"], W["sg_wm"], W["sg_bsb"], t + "sg_fwd")
    y, states = gla_fwd(proj, pa, W["gla_wa"], W["gla_b_a"], W["gla_norm_g"], y, t + "gla_fwd")
    if late is None:
        y = att_fwd(proj, W["att_bias"], y, t + "att_fwd")
    else:
        (names, shards, Ws) = late[0]
        y, gathered = att_fwd(proj, W["att_bias"], y, t + "att_fwd", gather=shards)
        arrived(names, gathered, Ws)
    y = conv_fwd(proj, W["conv_dw_w"], W["conv_dw_b"], W["conv_ln_g"], W["conv_ln_b"], y, t + "conv_fwd")
    gate = mm_nn(xn, W["w_gate_all"], bias=W["b_gate_all"], act="sigmoid", name=t + "gate")
    z = mm_nn(y, W["w_branch"], name=t + "branch")
    m = gate_merge_fwd(gate, z, t + "merge")
    h1 = mm_nn(m, W["w_out"], res=h, out_dtype=f32, name=t + "outproj")
    hn = rms_fwd(h1, W["norm2_g"], t + "rms2")
    a = mm_nn(hn, W["w_ff1"], name=t + "ff1")
    h2 = mm_nn(a, W["w_ff2"], pre="relu2", res=h1, out_dtype=f32, name=t + "ff2")
    hg = rms_fwd(h2, W["norm3_g"], t + "rms3")
    pg = mm_nn(hg, W["w_ple_gate"], bias=W["b_ple_gate"], act="sigmoid", name=t + "ple_gate")
    h3, e = mm_nn(p_i, W["w_ple"], mul=pg, res=h2, out_dtype=f32, raw_out=True, name=t + "ple_out")
    sv.update(xn=xn, proj=proj, pa=pa, states=states, y=y, gate=gate, z=z, m=m, h1=h1, hn=hn, a=a, h2=h2, hg=hg, pg=pg, e=e)
    return h3, sv


def _layer_bwd(dh3, sv, p_i, W, li):
    t = f"l{li}_b_"
    G = {}
    dpg, de, G["b_ple_gate"] = ple_bwd_ew(dh3, sv["e"], sv["pg"], t + "ple_ew")
    G["w_ple_gate"] = mm_tn(sv["hg"], dpg, name=t + "dw_ple_gate")[0]
    G["w_ple"] = mm_tn(p_i, de, name=t + "dw_ple")[0]
    dhg = mm_nt(dpg, W["w_ple_gate"], name=t + "dhg")
    dh2, G["norm3_g"] = rms_bwd(dhg, sv["h2"], W["norm3_g"], dh3, t + "rms3")
    da = mm_nt(dh2, W["w_ff2"], post_a=sv["a"], out_dtype=bf16, name=t + "da")
    G["w_ff2"] = mm_tn(sv["a"], dh2, pre="relu2", name=t + "dw_ff2")[0]
    G["w_ff1"] = mm_tn(sv["hn"], da, name=t + "dw_ff1")[0]
    dhn = mm_nt(da, W["w_ff1"], name=t + "dhn")
    dh1, G["norm2_g"] = rms_bwd(dhn, sv["h1"], W["norm2_g"], dh2, t + "rms2")
    dm = mm_nt(dh1, W["w_out"], out_dtype=bf16, name=t + "dm")
    G["w_out"] = mm_tn(sv["m"], dh1, name=t + "dw_out")[0]
    dz, dgp, G["b_gate_all"] = gate_merge_bwd(dm, sv["gate"], sv["z"], t + "merge")
    G["w_branch"] = mm_tn(sv["y"], dz, G=4, name=t + "dw_branch")
    dy = mm_nt(dz, W["w_branch"], out_dtype=bf16, name=t + "dy")
    G["w_gate_all"] = mm_tn(sv["xn"], dgp, name=t + "dw_gate")[0]
    dxn = mm_nt(dgp, W["w_gate_all"], name=t + "dxn_gate")
    proj = sv["proj"]
    dproj, dwm, dbs, G["sg_ln_g"], G["sg_ln_b"] = sg_bwd(proj, dy, W["sg_ln_g"], W["sg_ln_b"], W["sg_wm"], W["sg_bsb"],
                                                          W["sg_maskf"], t + "sg")
    G["sg_w"], G["sg_b"] = dwm, dbs[:, :, 0]
    dproj, dpa, dwa, G["gla_b_a"], G["gla_norm_g"] = gla_bwd(proj, sv["pa"], sv["states"], dy, W["gla_wa"], W["gla_b_a"],
                                                             W["gla_norm_g"], dproj, t + "gla")
    G["gla_w_a2"] = dwa[:RANK]
    dproj, dkp, dvp, dbias = att_bwd(proj, sv["y"], dy, W["att_bias"], dproj, t + "att")
    dproj = att_shift_add(dkp, dvp, dproj, t + "att_kv")
    G["att_rel_bias"] = att_bias_grad(dbias, t + "att_bias")
    dz_c, G["conv_ln_g"], G["conv_ln_b"], G["conv_dw_b"] = conv_bwd_norm(proj, dy, W["conv_dw_w"], W["conv_dw_b"],
                                                                        W["conv_ln_g"], W["conv_ln_b"], t + "conv_norm")
    dproj, G["conv_dw_w"] = conv_bwd_taps(proj, dz_c, W["conv_dw_w"], dproj, t + "conv_taps")
    G["w_in_main"] = mm_tn(sv["xn"], dproj, name=t + "dw_in")[0]
    G["w_in_rank"] = mm_tn(sv["xn"], dpa, name=t + "dw_in_rank")[0]
    dxn = mm_nt(dpa, W["w_in_rank"], res=dxn, name=t + "dxn_rank")
    dxn = mm_nt(dproj, W["w_in_main"], res=dxn, name=t + "dxn_main")
    dh0, G["norm1_g"] = rms_bwd(dxn, sv["h0"], W["norm1_g"], dh1, t + "rms1")
    return dh0, G


def _prep_layer_weights(gathered, repl, li):
    W = {}
    full = {n: _merge_chips(g[li], SHARDED[n][1]) for n, g in gathered.items()}
    if "w_in" in full:
        main, rank = _from_ref_cols(full["w_in"])
        W["w_in_main"], W["w_in_rank"] = main[None], rank[None]
    if "w_branch" in full:
        W["w_branch"] = full["w_branch"]
    if "w_gate" in full:
        W["w_gate_all"] = jnp.transpose(full["w_gate"], (1, 0, 2)).reshape(1, D, 4 * D)
    if "b_gate" in full:
        W["b_gate_all"] = full["b_gate"].reshape(1, 4 * D)
    for n in ("w_out", "w_ff1", "w_ff2", "w_ple_gate", "w_ple"):
        if n in full:
            W[n] = full[n][None]
    if "gla_w_a2" in full:
        W["gla_wa"] = jnp.pad(full["gla_w_a2"], ((0, RANKP - RANK), (0, 0))).astype(bf16)
    if "att_rel_bias" in full:
        W["att_bias"] = att_bias_build(full["att_rel_bias"], f"l{li}_att_bias")
    if "conv_dw_w" in full:
        W["conv_dw_w"] = full["conv_dw_w"]
    if repl is not None:
        for n in ("norm1_g", "norm2_g", "norm3_g", "b_ple_gate", "sg_ln_g", "sg_ln_b", "gla_b_a", "gla_norm_g", "conv_dw_b",
                  "conv_ln_g", "conv_ln_b"):
            W[n] = repl[n][li].reshape(1, -1)
        pos = np.arange(128)
        mask = (pos[None, :] // CHUNK) <= (pos[:, None] // CHUNK)
        W["sg_maskf"] = jnp.asarray(mask, f32)
        W["sg_wm"] = jnp.where(mask[None], repl["sg_w"][li], 0.0).astype(bf16)
        W["sg_bsb"] = jnp.broadcast_to(repl["sg_b"][li][:, :, None], (4, 128, 128))
    return W


def _layer_grads_to_ref(G):
    out = {}
    out["w_in"] = _to_ref_cols(G["w_in_main"], G["w_in_rank"])
    out["w_gate"] = jnp.transpose(G["w_gate_all"].reshape(D, 4, D), (1, 0, 2))
    out["b_gate"] = G["b_gate_all"].reshape(4, D)
    for n in ("w_branch", "w_out", "w_ff1", "w_ff2", "w_ple_gate", "w_ple", "gla_w_a2", "att_rel_bias", "conv_dw_w", "sg_w",
              "sg_b"):
        out[n] = G[n]
    for n in ("norm1_g", "norm2_g", "norm3_g", "b_ple_gate", "sg_ln_g", "sg_ln_b", "gla_b_a", "gla_norm_g", "conv_dw_b",
              "conv_ln_g", "conv_ln_b"):
        out[n] = G[n].reshape(-1)
    return out


def kernel(x, p, norm1_g, w_in, sg_ln_g, sg_ln_b, sg_w, sg_b, gla_w_a2, gla_b_a, gla_norm_g, att_rel_bias, conv_dw_w, conv_dw_b, conv_ln_g, conv_ln_b, w_branch, w_gate, b_gate, w_out, norm2_g, w_ff1, w_ff2, norm3_g, w_ple_gate, b_ple_gate, w_ple, final_g, loss_target, m_norm1_g, m_w_in, m_sg_ln_g, m_sg_ln_b, m_sg_w, m_sg_b, m_gla_w_a2, m_gla_b_a, m_gla_norm_g, m_att_rel_bias, m_conv_dw_w, m_conv_dw_b, m_conv_ln_g, m_conv_ln_b, m_w_branch, m_w_gate, m_b_gate, m_w_out, m_norm2_g, m_w_ff1, m_w_ff2, m_norm3_g, m_w_ple_gate, m_b_ple_gate, m_w_ple, m_final_g, v_norm1_g, v_w_in, v_sg_ln_g, v_sg_ln_b, v_sg_w, v_sg_b, v_gla_w_a2, v_gla_b_a, v_gla_norm_g, v_att_rel_bias, v_conv_dw_w, v_conv_dw_b, v_conv_ln_g, v_conv_ln_b, v_w_branch, v_w_gate, v_b_gate, v_w_out, v_norm2_g, v_w_ff1, v_w_ff2, v_norm3_g, v_w_ple_gate, v_b_ple_gate, v_w_ple, v_final_g):
    args = dict(locals())
    weights = {n: args[n] for n in W_ORDER}
    moments_m = {n: args["m_" + n] for n in W_ORDER}
    moments_v = {n: args["v_" + n] for n in W_ORDER}
    c = lax.axis_index("c")
    sharded_names = BIG + SMALL

    early = ("w_in",) + SMALL
    shards = {n: (weights[n].astype(bf16) if n in BIG else weights[n]) for n in sharded_names}
    repl = {n: weights[n] for n in REPL}
    h = x[0]
    xn0, gathered = rms_fwd(h, norm1_g[0].reshape(1, D), "l0_rms1", gather=[shards[n] for n in early])
    Ws = [_prep_layer_weights(dict(zip(early, gathered)), repl, li) for li in range(DEPTH)]
    in_att, in_proj = ("w_ff1", "w_ff2", "w_ple_gate", "w_ple"), ("w_gate", "w_branch", "w_out")
    late = [(names, [shards[n] for n in names], Ws) for names in (in_att, in_proj)]

    saved = []
    for li in range(DEPTH):
        h, sv = _layer_fwd(h, p[li, 0], Ws[li], li, late if li == 0 else None, xn0 if li == 0 else None)
        saved.append(sv)
    loss_part, dh, dfinal = loss_head(h, final_g.reshape(1, D), loss_target[0], "loss_head")
    loss = lax.psum(loss_part[0, 0], ("x", "y", "c"))

    layer_grads = [None] * DEPTH
    for li in reversed(range(DEPTH)):
        dh, G = _layer_bwd(dh, saved[li], p[li, 0], Ws[li], li)
        layer_grads[li] = _layer_grads_to_ref(G)
    grad_x = dh[None]

    g0s, g1s = ([_split_chips(layer_grads[li][n], SHARDED[n][1]).astype(bf16) for n in BIG] for li in range(DEPTH))
    ras = sib_other_layer(g0s, g1s, "rs_sibling_layer")
    psums = [add_halves(a0, a1, r, c, "rs_add_" + n) for n, a0, a1, r in zip(BIG, g0s, g1s, ras)]
    rbs = a2a4(psums, "rs_all_to_all")
    chip = 2 * lax.axis_index("x") + lax.axis_index("y")
    reds = [reduce_chips(r, ps, c, chip, "rs_sum_" + n) for n, r, ps in zip(BIG, rbs, psums)]
    grads = dict(zip(BIG, ag2(reds, "rs_sibling_gather")))

    local = {n: jnp.stack([layer_grads[li][n] for li in range(DEPTH)]) for n in tuple(REPL)[:-1] + SMALL}
    local["final_g"] = dfinal.reshape(D)
    rnames = tuple(REPL) + SMALL
    rflat = jnp.concatenate([local[n].reshape(-1) for n in rnames])
    rflat = jnp.pad(rflat, (0, REPL_ROWS * PACK_W - rflat.shape[0])).reshape(REPL_ROWS, PACK_W)
    rall = ag8(rflat, "ar_gather").reshape(8, REPL_ROWS, PACK_W)
    rsum = sum_slots(rall, "ar_sum").reshape(-1)
    off = 0
    for n in rnames:
        shape = local[n].shape
        size = int(np.prod(shape))
        g = rsum[off:off + size].reshape(shape)
        off += size
        if n in SMALL:
            ax = SHARDED[n][1] + 1
            g = lax.dynamic_slice_in_dim(g, chip * (shape[ax] // 4), shape[ax] // 4, axis=ax)
        grads[n] = g

    deltas, new_m, new_v = {}, {}, {}
    for n in W_ORDER:
        deltas[n], new_m[n], new_v[n] = adamw(weights[n], grads[n], moments_m[n], moments_v[n], "adamw_" + n)
    return (loss, grad_x, *[grads[n] for n in W_ORDER], *[deltas[n] for n in W_ORDER], *[new_m[n] for n in W_ORDER],
            *[new_v[n] for n in W_ORDER])
```

```python
import functools

import jax
import jax.numpy as jnp
import numpy as np
from jax import lax
from jax.experimental import pallas as pl
from jax.experimental.pallas import tpu as pltpu

f32, bf16 = jnp.float32, jnp.bfloat16
HI = lax.Precision.HIGHEST
MESH = pl.DeviceIdType.MESH
SDS = jax.ShapeDtypeStruct
BS = pl.BlockSpec
ANY = pl.BlockSpec(memory_space=pl.ANY)

D = 1024
DEPTH = 2
CHUNK = 64
BW = 512
NP = 5120
RANK = 16
RANKP = 128
DFF = 4096
PLE = 256
CONV_K = 31
HALO = 32
TQ = 256
WIN = 768
REL_TABLE = 320
EPS = 1e-6
NEG_INF = -1e30
VMEM_LIMIT = 56 * 1024 * 1024

ADAM_LR, ADAM_B1, ADAM_B2, ADAM_EPS, ADAM_WD, ADAM_STEP = 0.001, 0.9, 0.999, 1e-08, 0.01, 10

OUR_COLS = dict(g_q=(0, 256), g_k=(256, 256), g_v=(512, 512), g_r=(1024, 512), a_q=(1536, 512), a_k=(2048, 512),
                a_v=(2560, 512), sg_u=(3072, 512), sg_v=(3584, 512), c_a=(4096, 512), c_g=(4608, 512))
REF_SPLITS = (("sg_u", 512), ("sg_v", 512), ("g_q", 256), ("g_k", 256), ("g_v", 512), ("g_r", 512), ("g_a", 16),
              ("a_q", 512), ("a_k", 512), ("a_v", 512), ("c_a", 512), ("c_g", 512))

SHARDED = dict(w_in=((1024, 5136), 1), w_branch=((4, 512, 1024), 2), w_gate=((4, 1024, 1024), 1), w_out=((1024, 1024), 0),
               w_ff1=((1024, 4096), 1), w_ff2=((4096, 1024), 0), w_ple_gate=((1024, 1024), 0), w_ple=((256, 1024), 1),
               gla_w_a2=((16, 256), 1), att_rel_bias=((8, 320), 1), conv_dw_w=((31, 512), 1), b_gate=((4, 1024), 1))
BIG = ("w_in", "w_branch", "w_gate", "w_out", "w_ff1", "w_ff2", "w_ple_gate", "w_ple")
SMALL = ("gla_w_a2", "att_rel_bias", "conv_dw_w", "b_gate")
REPL = dict(norm1_g=(2, 1024), sg_ln_g=(2, 512), sg_ln_b=(2, 512), sg_w=(2, 4, 128, 128), sg_b=(2, 4, 128), gla_b_a=(2, 256),
            gla_norm_g=(2, 512), conv_dw_b=(2, 512), conv_ln_g=(2, 512), conv_ln_b=(2, 512), norm2_g=(2, 1024),
            norm3_g=(2, 1024), b_ple_gate=(2, 1024), final_g=(1024,))
W_ORDER = ['norm1_g', 'w_in', 'sg_ln_g', 'sg_ln_b', 'sg_w', 'sg_b', 'gla_w_a2', 'gla_b_a', 'gla_norm_g', 'att_rel_bias',
           'conv_dw_w', 'conv_dw_b', 'conv_ln_g', 'conv_ln_b', 'w_branch', 'w_gate', 'b_gate', 'w_out', 'norm2_g', 'w_ff1',
           'w_ff2', 'norm3_g', 'w_ple_gate', 'b_ple_gate', 'w_ple', 'final_g']
PACK_W = 1024
REPL_ROWS = 200
AG_FORWARD_AT_END = 1.0
AG_FORWARD_INPROJ = 0.875
AG_FORWARD_ATTENTION = 0.75
TM = 1024
MM_VMEM_BUDGET = 44 * 1024 * 1024


def _tile(s):
    return 512 if s % 512 == 0 else s


def _token_tile(S, row_bytes, fixed_bytes):
    for t in (2048, 1024):
        if S % t == 0 and 2 * (t * row_bytes + fixed_bytes) <= MM_VMEM_BUDGET:
            return t
    return min(TM, S)


def _cp(*sem):
    return pltpu.CompilerParams(dimension_semantics=sem, vmem_limit_bytes=VMEM_LIMIT)


def rms_fwd(h, g, name, gather=None):
    S, Dm = h.shape
    T = _tile(S)
    ng = len(gather) if gather else 0

    def body(h_ref, g_ref, *rest):
        o_ref = rest[ng]
        if ng:
            _ag4_over_grid(rest[:ng], rest[ng + 1:2 * ng + 1], *rest[2 * ng + 1:], pl.program_id(0), S // T,
                           AG_FORWARD_AT_END)
        x = h_ref[...]
        r = lax.rsqrt(jnp.mean(x * x, axis=-1, keepdims=True) + EPS)
        o_ref[...] = (x * r * g_ref[...]).astype(bf16)

    outs = pl.pallas_call(
        body, out_shape=(SDS((S, Dm), bf16),) + tuple(_ag4_out_shapes(gather or [])), grid=(S // T,),
        in_specs=[BS((T, Dm), lambda i: (i, 0)), BS((1, Dm), lambda i: (0, 0))] + [ANY] * ng,
        out_specs=(BS((T, Dm), lambda i: (i, 0)),) + tuple(ANY for _ in range(ng)),
        scratch_shapes=_ag4_sems(ng) if ng else [], compiler_params=_cp("arbitrary" if ng else "parallel"),
        name=name)(h, g, *(gather or []))
    return (outs[0], list(outs[1:])) if ng else outs[0]


def mm_nn(x, w, *, name, bias=None, act=None, pre=None, mul=None, res=None, out_dtype=bf16, raw_out=False, gather=None):
    S = x.shape[0]
    G, K, N = w.shape
    tn = min(1024 if K <= 1024 else 512, N)
    nj = N // tn
    row_bytes = K * x.dtype.itemsize + tn * (jnp.dtype(out_dtype).itemsize + (2 if raw_out else 0)
                                             + sum(a.dtype.itemsize for a in (mul, res) if a is not None))
    T = _token_tile(S, row_bytes, K * tn * 2)
    extras = [a for a in (bias, mul, res) if a is not None]
    ng = len(gather) if gather else 0
    grid = (S // T, G, nj)

    def body(*refs):
        it = iter(refs)
        x_ref, w_ref = next(it), next(it)
        b_ref = next(it) if bias is not None else None
        m_ref = next(it) if mul is not None else None
        r_ref = next(it) if res is not None else None
        g_in = [next(it) for _ in range(ng)]
        o_ref = next(it)
        raw_ref = next(it) if raw_out else None
        if ng:
            g_out = [next(it) for _ in range(ng)]
            step = (pl.program_id(0) * G + pl.program_id(1)) * nj + pl.program_id(2)
            _ag4_over_grid(g_in, g_out, next(it), next(it), step, grid[0] * G * nj, AG_FORWARD_INPROJ)
        xv = x_ref[...]
        if pre == "relu2":
            xf = jnp.maximum(xv.astype(f32), 0.0)
            xv = xf * xf
        acc = jnp.dot(xv.astype(bf16), w_ref[0], preferred_element_type=f32)
        if raw_out:
            raw_ref[...] = acc.astype(bf16)
        if b_ref is not None:
            acc = acc + b_ref[...]
        if act == "sigmoid":
            acc = jax.nn.sigmoid(acc)
        if m_ref is not None:
            acc = acc * m_ref[...].astype(f32)
        if r_ref is not None:
            acc = r_ref[...].astype(f32) + acc
        o_ref[...] = acc.astype(out_dtype)

    in_specs = [BS((T, K), lambda i, g, j: (i, g)), BS((1, K, tn), lambda i, g, j: (g, 0, j))]
    if bias is not None:
        in_specs.append(BS((1, tn), lambda i, g, j: (0, g * nj + j)))
    for a in (mul, res):
        if a is not None:
            in_specs.append(BS((T, tn), lambda i, g, j: (i, g * nj + j)))
    ospec = BS((T, tn), lambda i, g, j: (i, g * nj + j))
    out_shape = SDS((S, G * N), out_dtype)
    if raw_out:
        out_shape, ospec = (out_shape, SDS((S, G * N), bf16)), (ospec, ospec)
    if not ng:
        return pl.pallas_call(
            body, out_shape=out_shape, grid=grid, in_specs=in_specs, out_specs=ospec,
            compiler_params=_cp("parallel", "parallel", "parallel"), name=name)(x, w, *extras)
    out_shape = (out_shape if raw_out else (out_shape,)) + tuple(_ag4_out_shapes(gather))
    ospec = (ospec if raw_out else (ospec,)) + tuple(ANY for _ in gather)
    outs = pl.pallas_call(
        body, out_shape=out_shape, grid=grid, in_specs=in_specs + [ANY] * ng, out_specs=ospec,
        scratch_shapes=_ag4_sems(ng), compiler_params=_cp("arbitrary", "arbitrary", "arbitrary"), name=name)(
            x, w, *extras, *gather)
    nres = 2 if raw_out else 1
    return (outs[0] if nres == 1 else outs[:2]), list(outs[nres:])


def mm_nt(dy, w, *, name, res=None, post_a=None, out_dtype=f32):
    S = dy.shape[0]
    G, K, N = w.shape
    tk = min(1024 if N <= 1024 else 512, K)
    nk = K // tk
    row_bytes = N * dy.dtype.itemsize + tk * (jnp.dtype(out_dtype).itemsize
                                              + sum(a.dtype.itemsize for a in (res, post_a) if a is not None))
    T = _token_tile(S, row_bytes, tk * N * 2)
    extras = [a for a in (res, post_a) if a is not None]

    def body(*refs):
        it = iter(refs)
        d_ref, w_ref = next(it), next(it)
        r_ref = next(it) if res is not None else None
        a_ref = next(it) if post_a is not None else None
        o_ref = next(it)
        acc = lax.dot_general(d_ref[...].astype(bf16), w_ref[0], (((1,), (1,)), ((), ())), preferred_element_type=f32)
        if r_ref is not None:
            acc = acc + r_ref[...].astype(f32)
        if a_ref is not None:
            acc = acc * (2.0 * jnp.maximum(a_ref[...].astype(f32), 0.0))
        o_ref[...] = acc.astype(out_dtype)

    in_specs = [BS((T, N), lambda i, g, j: (i, g)), BS((1, tk, N), lambda i, g, j: (g, j, 0))]
    for a in extras:
        in_specs.append(BS((T, tk), lambda i, g, j: (i, g * nk + j)))
    return pl.pallas_call(
        body, out_shape=SDS((S, G * K), out_dtype), grid=(S // T, G, nk), in_specs=in_specs,
        out_specs=BS((T, tk), lambda i, g, j: (i, g * nk + j)),
        compiler_params=_cp("parallel", "parallel", "parallel"), name=name)(dy, w, *extras)


def mm_tn(x, dy, *, name, G=1, pre=None, ts=None):
    S = x.shape[0]
    K, N = x.shape[1] // G, dy.shape[1] // G
    tk, tn = min(K, 1024), min(N, 1024)
    nk, nn = K // tk, N // tn
    if ts is None:
        ts = 2048 if (x.dtype == bf16 and dy.dtype == bf16 and pre is None and S % 2048 == 0) else 1024
    ts = min(ts, S)

    def body(x_ref, d_ref, o_ref, acc):
        @pl.when(pl.program_id(3) == 0)
        def _():
            acc[...] = jnp.zeros_like(acc)

        xv = x_ref[...]
        if pre == "relu2":
            xf = jnp.maximum(xv.astype(f32), 0.0)
            xv = xf * xf
        acc[...] += lax.dot_general(xv.astype(bf16), d_ref[...].astype(bf16), (((0,), (0,)), ((), ())),
                                    preferred_element_type=f32)

        @pl.when(pl.program_id(3) == S // ts - 1)
        def _():
            o_ref[0] = acc[...].astype(bf16)

    return pl.pallas_call(
        body, out_shape=SDS((G, K, N), bf16), grid=(G, nk, nn, S // ts),
        in_specs=[BS((ts, tk), lambda g, a, b, s: (s, g * nk + a)), BS((ts, tn), lambda g, a, b, s: (s, g * nn + b))],
        out_specs=BS((1, tk, tn), lambda g, a, b, s: (g, a, b)), scratch_shapes=[pltpu.VMEM((tk, tn), f32)],
        compiler_params=_cp("parallel", "parallel", "parallel", "arbitrary"), name=name)(x, dy)


def rms_bwd(dxn, x, g, dres, name):
    S, Dm = x.shape
    T = _tile(S)

    def body(*refs):
        if dres is not None:
            d_ref, x_ref, g_ref, r_ref, dx_ref, dg_ref = refs
        else:
            d_ref, x_ref, g_ref, dx_ref, dg_ref = refs
        xv = x_ref[...]
        d = d_ref[...].astype(f32)
        r = lax.rsqrt(jnp.mean(xv * xv, axis=-1, keepdims=True) + EPS)
        u = d * g_ref[...]
        dx = r * u - xv * ((r * r * r) * (1.0 / Dm)) * jnp.sum(u * xv, axis=-1, keepdims=True)
        if dres is not None:
            dx = r_ref[...] + dx
        dx_ref[...] = dx

        @pl.when(pl.program_id(0) == 0)
        def _():
            dg_ref[...] = jnp.zeros_like(dg_ref)

        dg_ref[...] += jnp.sum(d * xv * r, axis=0, keepdims=True)

    tok = BS((T, Dm), lambda i: (i, 0))
    vec = BS((1, Dm), lambda i: (0, 0))
    args = (dxn, x, g) + ((dres,) if dres is not None else ())
    return pl.pallas_call(
        body, out_shape=(SDS((S, Dm), f32), SDS((1, Dm), f32)), grid=(S // T,),
        in_specs=[tok, tok, vec] + ([tok] if dres is not None else []), out_specs=(tok, vec),
        compiler_params=_cp("arbitrary"), name=name)(*args)


def loss_head(h, g, target, name):
    S, Dm = h.shape
    T = _tile(S)

    def body(h_ref, g_ref, t_ref, loss_ref, dh_ref, dg_ref):
        @pl.when(pl.program_id(0) == 0)
        def _():
            loss_ref[...] = jnp.zeros_like(loss_ref)
            dg_ref[...] = jnp.zeros_like(dg_ref)

        xv = h_ref[...]
        gv = g_ref[...]
        r = lax.rsqrt(jnp.mean(xv * xv, axis=-1, keepdims=True) + EPS)
        diff = xv * r * gv - t_ref[...]
        loss_ref[...] += 0.5 * jnp.sum(jnp.mean(diff * diff, axis=-1, keepdims=True))
        d = diff * (1.0 / Dm)
        u = d * gv
        dh_ref[...] = r * u - xv * ((r * r * r) * (1.0 / Dm)) * jnp.sum(u * xv, axis=-1, keepdims=True)
        dg_ref[...] += jnp.sum(d * xv * r, axis=0, keepdims=True)

    tok = BS((T, Dm), lambda i: (i, 0))
    vec = BS((1, Dm), lambda i: (0, 0))
    return pl.pallas_call(
        body, out_shape=(SDS((1, 128), f32), SDS((S, Dm), f32), SDS((1, Dm), f32)), grid=(S // T,),
        in_specs=[tok, vec, tok], out_specs=(BS((1, 128), lambda i: (0, 0)), tok, vec),
        compiler_params=_cp("arbitrary"), name=name)(h, g, target)


def gate_merge_fwd(gate, z, name):
    S = gate.shape[0]
    T = _tile(S)

    def body(g_ref, z_ref, o_ref):
        acc = jnp.zeros((T, D), f32)
        for n in range(4):
            acc = acc + g_ref[:, n * D:(n + 1) * D].astype(f32) * z_ref[:, n * D:(n + 1) * D].astype(f32)
        o_ref[...] = acc.astype(bf16)

    wide = BS((T, 4 * D), lambda i: (i, 0))
    return pl.pallas_call(body, out_shape=SDS((S, D), bf16), grid=(S // T,), in_specs=[wide, wide],
                          out_specs=BS((T, D), lambda i: (i, 0)), compiler_params=_cp("parallel"), name=name)(gate, z)


def gate_merge_bwd(dm, gate, z, name):
    S = gate.shape[0]
    T = _tile(S)

    def body(dm_ref, g_ref, z_ref, dz_ref, dg_ref, db_ref):
        @pl.when(pl.program_id(0) == 0)
        def _():
            db_ref[...] = jnp.zeros_like(db_ref)

        dmv = dm_ref[...].astype(f32)
        for n in range(4):
            cols = slice(n * D, (n + 1) * D)
            gv = g_ref[:, cols].astype(f32)
            dz_ref[:, cols] = (dmv * gv).astype(bf16)
            dgp = dmv * z_ref[:, cols].astype(f32) * gv * (1.0 - gv)
            dg_ref[:, cols] = dgp.astype(bf16)
            db_ref[:, cols] += jnp.sum(dgp, axis=0, keepdims=True)

    wide = BS((T, 4 * D), lambda i: (i, 0))
    return pl.pallas_call(
        body, out_shape=(SDS((S, 4 * D), bf16), SDS((S, 4 * D), bf16), SDS((1, 4 * D), f32)), grid=(S // T,),
        in_specs=[BS((T, D), lambda i: (i, 0)), wide, wide], out_specs=(wide, wide, BS((1, 4 * D), lambda i: (0, 0))),
        compiler_params=_cp("arbitrary"), name=name)(dm, gate, z)


def ple_bwd_ew(dh, e, pg, name):
    S = dh.shape[0]
    T = _tile(S)

    def body(dh_ref, e_ref, pg_ref, dp_ref, de_ref, db_ref):
        @pl.when(pl.program_id(0) == 0)
        def _():
            db_ref[...] = jnp.zeros_like(db_ref)

        d = dh_ref[...]
        g = pg_ref[...].astype(f32)
        dpre = d * e_ref[...].astype(f32) * g * (1.0 - g)
        dp_ref[...] = dpre.astype(bf16)
        de_ref[...] = (d * g).astype(bf16)
        db_ref[...] += jnp.sum(dpre, axis=0, keepdims=True)

    tok = BS((T, D), lambda i: (i, 0))
    return pl.pallas_call(
        body, out_shape=(SDS((S, D), bf16), SDS((S, D), bf16), SDS((1, D), f32)), grid=(S // T,),
        in_specs=[tok, tok, tok], out_specs=(tok, tok, BS((1, D), lambda i: (0, 0))),
        compiler_params=_cp("arbitrary"), name=name)(dh, e, pg)


_GK = 0.7978845608028654
_GC = 0.044715


def _gelu(x):
    return 0.5 * x * (1.0 + jnp.tanh(_GK * (x + _GC * (x * x * x))))


def _gelu_grad(x):
    x2 = x * x
    t = jnp.tanh(_GK * (x + _GC * (x * x2)))
    return 0.5 * (1.0 + t) + 0.5 * x * (1.0 - t * t) * (_GK * (1.0 + 3.0 * _GC * x2))


def _ln_stats(v):
    mu = jnp.mean(v, axis=-1, keepdims=True)
    vc = v - mu
    rs = lax.rsqrt(jnp.mean(vc * vc, axis=-1, keepdims=True) + EPS)
    return vc * rs, rs


def _ln_bwd(dvh, vh, rs):
    return rs * (dvh - jnp.mean(dvh, axis=-1, keepdims=True) - vh * jnp.mean(dvh * vh, axis=-1, keepdims=True))


def sg_fwd(proj, lg, lb, wm, bsb, name):
    S = proj.shape[0]
    T = _tile(S)
    cu, cv = OUR_COLS["sg_u"][0] // BW, OUR_COLS["sg_v"][0] // BW

    def body(u_ref, v_ref, lg_ref, lb_ref, wm_ref, bsb_ref, o_ref):
        for b in range(T // 128):
            rows = slice(b * 128, (b + 1) * 128)
            u = _gelu(u_ref[rows, :].astype(f32))
            vh, _ = _ln_stats(_gelu(v_ref[rows, :].astype(f32)))
            vb = (vh * lg_ref[...] + lb_ref[...]).astype(bf16)
            outs = []
            for g in range(4):
                cols = slice(g * 128, (g + 1) * 128)
                mixed = jnp.dot(wm_ref[g], vb[:, cols], preferred_element_type=f32) + bsb_ref[g]
                outs.append(u[:, cols] * mixed)
            o_ref[rows, :] = jnp.concatenate(outs, axis=1).astype(bf16)

    vec = BS((1, BW), lambda i: (0, 0))
    cube = BS((4, 128, 128), lambda i: (0, 0, 0))
    return pl.pallas_call(
        body, out_shape=SDS((S, 4 * BW), bf16), grid=(S // T,),
        in_specs=[BS((T, BW), lambda i: (i, cu)), BS((T, BW), lambda i: (i, cv)), vec, vec, cube, cube],
        out_specs=BS((T, BW), lambda i: (i, 0)), compiler_params=_cp("parallel"), name=name)(proj, proj, lg, lb, wm, bsb)


def sg_bwd(proj, dy, lg, lb, wm, bsb, maskf, name):
    S = proj.shape[0]
    T = _tile(S)
    cu, cv = OUR_COLS["sg_u"][0] // BW, OUR_COLS["sg_v"][0] // BW
    creg = OUR_COLS["sg_u"][0] // (2 * BW)

    def body(u_ref, v_ref, dy_ref, lg_ref, lb_ref, wm_ref, bsb_ref, mk_ref, dp_ref, dwm_ref, dbs_ref, dlg_ref, dlb_ref):
        @pl.when(pl.program_id(0) == 0)
        def _():
            dwm_ref[...] = jnp.zeros_like(dwm_ref)
            dbs_ref[...] = jnp.zeros_like(dbs_ref)
            dlg_ref[...] = jnp.zeros_like(dlg_ref)
            dlb_ref[...] = jnp.zeros_like(dlb_ref)

        for b in range(T // 128):
            rows = slice(b * 128, (b + 1) * 128)
            su = u_ref[rows, :].astype(f32)
            sv = v_ref[rows, :].astype(f32)
            dya = dy_ref[rows, :].astype(f32)
            u = _gelu(su)
            vh, rs = _ln_stats(_gelu(sv))
            vb = (vh * lg_ref[...] + lb_ref[...]).astype(bf16)
            dus, dvls = [], []
            for g in range(4):
                cols = slice(g * 128, (g + 1) * 128)
                mixed = jnp.dot(wm_ref[g], vb[:, cols], preferred_element_type=f32) + bsb_ref[g]
                dus.append(dya[:, cols] * mixed)
                dmg = dya[:, cols] * u[:, cols]
                dmb = dmg.astype(bf16)
                dbs_ref[g] += jnp.broadcast_to(jnp.sum(dmg, axis=1, keepdims=True), (128, 128))
                dwm_ref[g] += mk_ref[...] * lax.dot_general(dmb, vb[:, cols], (((1,), (1,)), ((), ())),
                                                            preferred_element_type=f32)
                dvls.append(lax.dot_general(wm_ref[g], dmb, (((0,), (0,)), ((), ())), preferred_element_type=f32))
            du = jnp.concatenate(dus, axis=1)
            dvln = jnp.concatenate(dvls, axis=1)
            dlg_ref[...] += jnp.sum(dvln * vh, axis=0, keepdims=True)
            dlb_ref[...] += jnp.sum(dvln, axis=0, keepdims=True)
            dv = _ln_bwd(dvln * lg_ref[...], vh, rs)
            dp_ref[rows, 0:BW] = (du * _gelu_grad(su)).astype(bf16)
            dp_ref[rows, BW:2 * BW] = (dv * _gelu_grad(sv)).astype(bf16)

    vec = BS((1, BW), lambda i: (0, 0))
    cube = BS((4, 128, 128), lambda i: (0, 0, 0))
    return pl.pallas_call(
        body,
        out_shape=(SDS((S, NP), bf16), SDS((4, 128, 128), f32), SDS((4, 128, 128), f32), SDS((1, BW), f32), SDS((1, BW), f32)),
        grid=(S // T,),
        in_specs=[BS((T, BW), lambda i: (i, cu)), BS((T, BW), lambda i: (i, cv)), BS((T, BW), lambda i: (i, 0)), vec, vec,
                  cube, cube, BS((128, 128), lambda i: (0, 0))],
        out_specs=(BS((T, 2 * BW), lambda i: (i, creg)), cube, cube, vec, vec),
        compiler_params=_cp("arbitrary"), name=name)(proj, proj, dy, lg, lb, wm, bsb, maskf)


_SUB = 64


def _conv_specs(S, T):
    ca, cg = OUR_COLS["c_a"][0] // BW, OUR_COLS["c_g"][0] // BW
    hb = T // HALO
    prev = lambda i: jnp.maximum(i * hb - 1, 0)
    return [BS((T, BW), lambda i: (i, ca)), BS((T, BW), lambda i: (i, cg)),
            BS((HALO, BW), lambda i: (prev(i), ca)), BS((HALO, BW), lambda i: (prev(i), cg))]


def _fill_shifts(sh):
    n = sh.shape[1] - 8
    for s in range(1, 8):
        sh[s, pl.ds(0, n), :] = sh[0, pl.ds(s, n), :]


def _shifted(sh, off, rows):
    s = off % 8
    return sh[s, pl.ds(off - s, rows), :]


def _conv_fill_ybuf(a_ref, g_ref, ap_ref, gp_ref, ysh):
    T = a_ref.shape[0]
    ysh[0, pl.ds(HALO, T), :] = a_ref[...].astype(f32) * jax.nn.sigmoid(g_ref[...].astype(f32))
    first = (pl.program_id(0) == 0).astype(f32)
    ysh[0, pl.ds(0, HALO), :] = (1.0 - first) * (ap_ref[...].astype(f32) * jax.nn.sigmoid(gp_ref[...].astype(f32)))
    _fill_shifts(ysh)


def _conv_taps(w_ref, ysh, r0):
    acc = jnp.zeros((_SUB, BW), f32)
    for k in range(CONV_K):
        acc = acc + w_ref[k:k + 1, :] * _shifted(ysh, r0 + HALO - (CONV_K - 1) + k, _SUB)
    return acc


def conv_fwd(proj, w, b, lg, lb, y, name):
    S = proj.shape[0]
    T = _tile(S)

    def body(a_ref, g_ref, ap_ref, gp_ref, w_ref, b_ref, lg_ref, lb_ref, y_in, o_ref, ybuf):
        del y_in
        _conv_fill_ybuf(a_ref, g_ref, ap_ref, gp_ref, ybuf)
        for sb in range(T // _SUB):
            z = _conv_taps(w_ref, ybuf, sb * _SUB) + b_ref[...]
            zh, _ = _ln_stats(z)
            zl = zh * lg_ref[...] + lb_ref[...]
            o_ref[pl.ds(sb * _SUB, _SUB), :] = (zl * jax.nn.sigmoid(zl)).astype(bf16)

    vec = BS((1, BW), lambda i: (0, 0))
    return pl.pallas_call(
        body, out_shape=SDS(y.shape, bf16), grid=(S // T,),
        in_specs=_conv_specs(S, T) + [BS((CONV_K, BW), lambda i: (0, 0)), vec, vec, vec, ANY],
        out_specs=BS((T, BW), lambda i: (i, 3)), scratch_shapes=[pltpu.VMEM((8, T + HALO, BW), f32)],
        input_output_aliases={8: 0}, compiler_params=_cp("parallel"), name=name)(proj, proj, proj, proj, w, b, lg, lb, y)


def conv_bwd_norm(proj, dy, w, b, lg, lb, name):
    S = proj.shape[0]
    T = _tile(S)

    def body(a_ref, g_ref, ap_ref, gp_ref, dy_ref, w_ref, b_ref, lg_ref, lb_ref, dz_ref, dlg_ref, dlb_ref, db_ref, ybuf):
        @pl.when(pl.program_id(0) == 0)
        def _():
            dlg_ref[...] = jnp.zeros_like(dlg_ref)
            dlb_ref[...] = jnp.zeros_like(dlb_ref)
            db_ref[...] = jnp.zeros_like(db_ref)

        _conv_fill_ybuf(a_ref, g_ref, ap_ref, gp_ref, ybuf)
        for sb in range(T // _SUB):
            rows = pl.ds(sb * _SUB, _SUB)
            z = _conv_taps(w_ref, ybuf, sb * _SUB) + b_ref[...]
            zh, rs = _ln_stats(z)
            zl = zh * lg_ref[...] + lb_ref[...]
            sg = jax.nn.sigmoid(zl)
            dzl = dy_ref[rows, :].astype(f32) * sg * (1.0 + zl * (1.0 - sg))
            dlg_ref[...] += jnp.sum(dzl * zh, axis=0, keepdims=True)
            dlb_ref[...] += jnp.sum(dzl, axis=0, keepdims=True)
            dz = _ln_bwd(dzl * lg_ref[...], zh, rs)
            db_ref[...] += jnp.sum(dz, axis=0, keepdims=True)
            dz_ref[rows, :] = dz

    vec = BS((1, BW), lambda i: (0, 0))
    tok = BS((T, BW), lambda i: (i, 0))
    return pl.pallas_call(
        body, out_shape=(SDS((S, BW), f32), SDS((1, BW), f32), SDS((1, BW), f32), SDS((1, BW), f32)), grid=(S // T,),
        in_specs=_conv_specs(S, T) + [BS((T, BW), lambda i: (i, 3)), BS((CONV_K, BW), lambda i: (0, 0)), vec, vec, vec],
        out_specs=(tok, vec, vec, vec), scratch_shapes=[pltpu.VMEM((8, T + HALO, BW), f32)],
        compiler_params=_cp("arbitrary"), name=name)(proj, proj, proj, proj, dy, w, b, lg, lb)


def conv_bwd_taps(proj, dz, w, dproj, name):
    S = proj.shape[0]
    T = _tile(S)
    nT = S // T
    hb = T // HALO
    creg = OUR_COLS["c_a"][0] // (2 * BW)

    def body(a_ref, g_ref, ap_ref, gp_ref, dz_ref, dzn_ref, w_ref, dp_in, dp_ref, dw_ref, ybuf, dzbuf, dwacc):
        del dp_in
        i = pl.program_id(0)

        @pl.when(i == 0)
        def _():
            dwacc[...] = jnp.zeros_like(dwacc)

        _conv_fill_ybuf(a_ref, g_ref, ap_ref, gp_ref, ybuf)
        dzbuf[0, pl.ds(0, T), :] = dz_ref[...]
        dzbuf[0, pl.ds(T, HALO), :] = (i < nT - 1).astype(f32) * dzn_ref[...]
        _fill_shifts(dzbuf)
        for sb in range(T // _SUB):
            r0 = sb * _SUB
            rows = pl.ds(r0, _SUB)
            dzs = dz_ref[rows, :]
            dyg = jnp.zeros((_SUB, BW), f32)
            for k in range(CONV_K):
                ysl = _shifted(ybuf, r0 + HALO - (CONV_K - 1) + k, _SUB)
                dwacc[pl.ds(k * 8, 8), :] += jnp.sum((dzs * ysl).reshape(_SUB // 8, 8, BW), axis=0)
                dyg = dyg + w_ref[k:k + 1, :] * _shifted(dzbuf, r0 + (CONV_K - 1) - k, _SUB)
            av = a_ref[rows, :].astype(f32)
            sg = jax.nn.sigmoid(g_ref[rows, :].astype(f32))
            dp_ref[rows, 0:BW] = (dyg * sg).astype(bf16)
            dp_ref[rows, BW:2 * BW] = (dyg * av * sg * (1.0 - sg)).astype(bf16)

        @pl.when(i == nT - 1)
        def _():
            for k in range(CONV_K):
                dw_ref[k:k + 1, :] = jnp.sum(dwacc[pl.ds(k * 8, 8), :], axis=0, keepdims=True)

    nxt = lambda i: jnp.minimum((i + 1) * hb, S // HALO - 1)
    return pl.pallas_call(
        body, out_shape=(SDS((S, NP), bf16), SDS((CONV_K, BW), f32)), grid=(nT,),
        in_specs=_conv_specs(S, T) + [BS((T, BW), lambda i: (i, 0)), BS((HALO, BW), lambda i: (nxt(i), 0)),
                                      BS((CONV_K, BW), lambda i: (0, 0)), ANY],
        out_specs=(BS((T, 2 * BW), lambda i: (i, creg)), BS((CONV_K, BW), lambda i: (0, 0))),
        scratch_shapes=[pltpu.VMEM((8, T + HALO, BW), f32), pltpu.VMEM((8, T + HALO, BW), f32),
                        pltpu.VMEM((CONV_K * 8, BW), f32)],
        input_output_aliases={7: 0}, compiler_params=_cp("arbitrary"), name=name)(proj, proj, proj, proj, dz, dz, w, dproj)


def _toeplitz_index():
    j = lax.broadcasted_iota(jnp.int32, (REL_TABLE, 1024), 1)
    t = lax.broadcasted_iota(jnp.int32, (REL_TABLE, 1024), 0)
    e = ((WIN - 1) - j) & 1023
    tidx = jnp.clip(e - (TQ - 1), -(CHUNK - 1), 256) + (CHUNK - 1)
    return (tidx == t).astype(f32)


def att_bias_build(table, name):
    H = table.shape[0]

    def body(t_ref, o_ref):
        u = jnp.dot(t_ref[...], _toeplitz_index(), precision=HI, preferred_element_type=f32)
        row = lax.broadcasted_iota(jnp.int32, (TQ, 1024), 0)
        r = lax.broadcasted_iota(jnp.int32, (TQ, WIN), 0)
        n = lax.broadcasted_iota(jnp.int32, (TQ, WIN), 1)
        dchunk = (r // CHUNK + 8) - n // CHUNK
        band = (dchunk >= 0) & (dchunk <= 8)
        for h in range(H):
            x = jnp.broadcast_to(u[h:h + 1, :], (TQ, 1024))
            for b in range(8):
                x = jnp.where(((row >> b) & 1) == 1, pltpu.roll(x, 1 << b, 1), x)
            o_ref[h] = jnp.where(band, x[:, :WIN], NEG_INF)

    return pl.pallas_call(body, out_shape=SDS((H, TQ, WIN), f32), compiler_params=pltpu.CompilerParams(vmem_limit_bytes=VMEM_LIMIT),
                          name=name)(table)


def att_bias_grad(dbias, name):
    H = dbias.shape[0]

    def body(d_ref, o_ref):
        row = lax.broadcasted_iota(jnp.int32, (TQ, 1024), 0)
        rows = []
        for h in range(H):
            x = jnp.concatenate([d_ref[h], jnp.zeros((TQ, 1024 - WIN), f32)], axis=1)
            for b in range(8):
                x = jnp.where(((row >> b) & 1) == 1, pltpu.roll(x, 1024 - (1 << b), 1), x)
            rows.append(jnp.sum(x, axis=0, keepdims=True))
        du = jnp.concatenate(rows, axis=0)
        o_ref[...] = lax.dot_general(du, _toeplitz_index(), (((1,), (1,)), ((), ())), precision=HI,
                                     preferred_element_type=f32)

    return pl.pallas_call(body, out_shape=SDS((H, REL_TABLE), f32), compiler_params=pltpu.CompilerParams(vmem_limit_bytes=VMEM_LIMIT),
                          name=name)(dbias)


def _att_specs():
    cq, ck, cv = (OUR_COLS[n][0] // BW for n in ("a_q", "a_k", "a_v"))
    specs = [BS((TQ, BW), lambda i: (i, cq))]
    for col in (ck, cv):
        for back in (2, 1, 0):
            specs.append(BS((TQ, BW), functools.partial(lambda i, back, col: (jnp.maximum(i - back, 0), col), back=back, col=col)))
    return specs


def _att_pen(i):
    n = lax.broadcasted_iota(jnp.int32, (1, WIN), 1)
    return jnp.where(n + (i - 2) * TQ >= 0, 0.0, NEG_INF).astype(f32)


def _att_probs(qa, kp, bias_h, pen):
    s = lax.dot_general(qa, kp, (((1,), (1,)), ((), ())), preferred_element_type=f32) + bias_h + pen
    e = jnp.exp(s - jnp.max(s, axis=-1, keepdims=True))
    return e * (1.0 / jnp.sum(e, axis=-1, keepdims=True))


def att_fwd(proj, bias, y, name, gather=None):
    S = proj.shape[0]
    ng = len(gather) if gather else 0

    def body(q_ref, k2, k1, k0, v2, v1, v0, b_ref, y_in, *rest):
        del y_in
        g_in, o_ref, g_out = rest[:ng], rest[ng], rest[ng + 1:2 * ng + 1]
        i = pl.program_id(0)
        if ng:
            kwin, vwin, send_sems, recv_sems = rest[2 * ng + 1:]
            _ag4_over_grid(g_in, g_out, send_sems, recv_sems, i, S // TQ, AG_FORWARD_ATTENTION)
        else:
            kwin, vwin = rest[1:]
        for w, (kr, vr) in enumerate(((k2, v2), (k1, v1), (k0, v0))):
            kwin[pl.ds(w * TQ, TQ), :] = kr[...]
            vwin[pl.ds(w * TQ, TQ), :] = vr[...]
        pen = _att_pen(i)
        lo = lax.broadcasted_iota(jnp.int32, (TQ, 128), 1) < 64
        for hp in range(4):
            cols = slice(hp * 128, (hp + 1) * 128)
            qp, kp, vp = q_ref[:, cols] * jnp.asarray(0.125, bf16), kwin[:, cols], vwin[:, cols]
            outs = []
            for a in range(2):
                qa = jnp.where(lo if a == 0 else ~lo, qp, jnp.zeros_like(qp))
                p = _att_probs(qa, kp, b_ref[2 * hp + a], pen)
                outs.append(jnp.dot(p.astype(bf16), vp, preferred_element_type=f32))
            o_ref[:, cols] = jnp.where(lo, outs[0], outs[1]).astype(bf16)

    outs = pl.pallas_call(
        body, out_shape=(SDS(y.shape, bf16),) + tuple(_ag4_out_shapes(gather or [])), grid=(S // TQ,),
        in_specs=_att_specs() + [BS((8, TQ, WIN), lambda i: (0, 0, 0), pipeline_mode=pl.Buffered(1)), ANY] + [ANY] * ng,
        out_specs=(BS((TQ, BW), lambda i: (i, 2)),) + tuple(ANY for _ in range(ng)),
        scratch_shapes=[pltpu.VMEM((WIN, BW), bf16), pltpu.VMEM((WIN, BW), bf16)] + (_ag4_sems(ng) if ng else []),
        input_output_aliases={8: 0}, compiler_params=_cp("arbitrary" if ng else "parallel"), name=name)(
            proj, proj, proj, proj, proj, proj, proj, bias, y, *(gather or []))
    return (outs[0], list(outs[1:])) if ng else outs[0]


def att_bwd(proj, y, dy, bias, dproj, name):
    S = proj.shape[0]
    cq = OUR_COLS["a_q"][0] // BW

    def body(q_ref, k2, k1, k0, v2, v1, v0, b_ref, o_ref, do_ref, dp_in, dq_ref, dkp_ref, dvp_ref, db_ref, kwin, vwin):
        del dp_in
        i = pl.program_id(0)

        @pl.when(i == 0)
        def _():
            db_ref[...] = jnp.zeros_like(db_ref)

        for w, (kr, vr) in enumerate(((k2, v2), (k1, v1), (k0, v0))):
            kwin[pl.ds(w * TQ, TQ), :] = kr[...]
            vwin[pl.ds(w * TQ, TQ), :] = vr[...]
        pen = _att_pen(i)
        lo = lax.broadcasted_iota(jnp.int32, (TQ, 128), 1) < 64
        for hp in range(4):
            cols = slice(hp * 128, (hp + 1) * 128)
            qp, kp, vp = q_ref[:, cols] * jnp.asarray(0.125, bf16), kwin[:, cols], vwin[:, cols]
            dop, op = do_ref[:, cols], o_ref[:, cols]
            dqs = []
            dk = jnp.zeros((WIN, 128), f32)
            dv = jnp.zeros((WIN, 128), f32)
            for a in range(2):
                sel = lo if a == 0 else ~lo
                qa = jnp.where(sel, qp, jnp.zeros_like(qp))
                doa = jnp.where(sel, dop, jnp.zeros_like(dop))
                p = _att_probs(qa, kp, b_ref[2 * hp + a], pen)
                dpv = lax.dot_general(doa, vp, (((1,), (1,)), ((), ())), preferred_element_type=f32)
                delta = jnp.sum(doa.astype(f32) * op.astype(f32), axis=-1, keepdims=True)
                ds = p * (dpv - delta)
                db_ref[2 * hp + a] += ds
                dsb = ds.astype(bf16)
                dqs.append(jnp.dot(dsb, kp, preferred_element_type=f32))
                dk = dk + lax.dot_general(dsb, qa, (((0,), (0,)), ((), ())), preferred_element_type=f32)
                dv = dv + lax.dot_general(p.astype(bf16), doa, (((0,), (0,)), ((), ())), preferred_element_type=f32)
            dq_ref[:, cols] = (jnp.where(lo, dqs[0], dqs[1]) * 0.125).astype(bf16)
            for w in range(3):
                dkp_ref[w, :, cols] = dk[w * TQ:(w + 1) * TQ].astype(bf16)
                dvp_ref[w, :, cols] = dv[w * TQ:(w + 1) * TQ].astype(bf16)

    tok = BS((TQ, BW), lambda i: (i, 2))
    part = BS((3, TQ, BW), lambda i: (0, i, 0))
    full = BS((8, TQ, WIN), lambda i: (0, 0, 0))
    return pl.pallas_call(
        body, out_shape=(SDS((S, NP), bf16), SDS((3, S, BW), bf16), SDS((3, S, BW), bf16), SDS((8, TQ, WIN), f32)),
        grid=(S // TQ,),
        in_specs=_att_specs() + [BS((8, TQ, WIN), lambda i: (0, 0, 0), pipeline_mode=pl.Buffered(1)), tok, tok, ANY],
        out_specs=(BS((TQ, BW), lambda i: (i, cq)), part, part, full),
        scratch_shapes=[pltpu.VMEM((WIN, BW), bf16), pltpu.VMEM((WIN, BW), bf16)],
        input_output_aliases={10: 0}, compiler_params=_cp("arbitrary"), name=name)(
            proj, proj, proj, proj, proj, proj, proj, bias, y, dy, dproj)


def att_shift_add(dkp, dvp, dproj, name):
    S = dkp.shape[1]
    nT = S // TQ
    creg = OUR_COLS["a_k"][0] // (2 * BW)

    def body(k2, k1, k0, v2, v1, v0, dp_in, dp_ref):
        del dp_in
        j = pl.program_id(0)
        m1 = (j + 1 < nT).astype(f32)
        m0 = (j + 2 < nT).astype(f32)
        dp_ref[:, 0:BW] = (k2[0].astype(f32) + m1 * k1[0].astype(f32) + m0 * k0[0].astype(f32)).astype(bf16)
        dp_ref[:, BW:2 * BW] = (v2[0].astype(f32) + m1 * v1[0].astype(f32) + m0 * v0[0].astype(f32)).astype(bf16)

    def spec(w):
        return BS((1, TQ, BW), functools.partial(lambda j, w: (w, jnp.minimum(j + 2 - w, nT - 1), 0), w=w))

    return pl.pallas_call(
        body, out_shape=SDS(dproj.shape, bf16), grid=(nT,),
        in_specs=[spec(2), spec(1), spec(0), spec(2), spec(1), spec(0), ANY],
        out_specs=BS((TQ, 2 * BW), lambda j: (j, creg)),
        input_output_aliases={6: 0}, compiler_params=_cp("parallel"), name=name)(dkp, dkp, dkp, dvp, dvp, dvp, dproj)


GQ, GV = 256, 512
TGC = 8


def _bd_mask():
    r = lax.broadcasted_iota(jnp.int32, (GQ, GV), 0) // 64
    c = lax.broadcasted_iota(jnp.int32, (GQ, GV), 1) // 128
    return (r == c).astype(f32)


def _tri(strict):
    r = lax.broadcasted_iota(jnp.int32, (CHUNK, CHUNK), 0)
    c = lax.broadcasted_iota(jnp.int32, (CHUNK, CHUNK), 1)
    return ((c < r) if strict else (c <= r)).astype(f32)


def _compact(s_bd):
    return jnp.concatenate([s_bd[h * 64:(h + 1) * 64, h * 128:(h + 1) * 128] for h in range(4)], axis=0)


def _expand(comp, mask):
    return jnp.tile(comp, (1, 4)) * mask


def _gla_gates(alr, wa_ref, ba_ref, tri_incl, ones_col):
    a = jnp.dot(alr, wa_ref[...], preferred_element_type=f32) + ba_ref[...]
    la = (jnp.minimum(a, 0.0) - jnp.log(1.0 + jnp.exp(-jnp.abs(a)))) * (1.0 / 16.0)
    cum = jnp.dot(tri_incl, la, precision=HI, preferred_element_type=f32)
    tot_row = cum[CHUNK - 1:CHUNK, :]
    tot_col = lax.dot_general(la, ones_col, (((0,), (0,)), ((), ())), precision=HI, preferred_element_type=f32)
    return a, cum, tot_row, jnp.tile(jnp.exp(tot_col), (1, 4))


def _head_norm(o):
    rns, ons = [], []
    for h in range(4):
        oh = o[:, h * 128:(h + 1) * 128]
        rn = lax.rsqrt(jnp.mean(oh * oh, axis=-1, keepdims=True) + EPS)
        rns.append(rn)
        ons.append(oh * rn)
    return rns, ons


def _gla_in_specs(T, imap):
    cq, ck = OUR_COLS["g_q"][0] // GQ, OUR_COLS["g_k"][0] // GQ
    cv, cr = OUR_COLS["g_v"][0] // GV, OUR_COLS["g_r"][0] // GV
    return [BS((T, GQ), lambda i: (imap(i), cq)), BS((T, GQ), lambda i: (imap(i), ck)), BS((T, GV), lambda i: (imap(i), cv)),
            BS((T, GV), lambda i: (imap(i), cr)), BS((T, RANKP), lambda i: (imap(i), 0))]


def gla_fwd(proj, pa, wa, ba, ng, y, name):
    S = proj.shape[0]
    T = min(TGC * CHUNK, S)
    nch = T // CHUNK

    def body(q_ref, k_ref, v_ref, r_ref, a_ref, wa_ref, ba_ref, ng_ref, y_in, y_ref, st_ref, s_scr):
        del y_in

        @pl.when(pl.program_id(0) == 0)
        def _():
            s_scr[...] = jnp.zeros_like(s_scr)

        mask = _bd_mask()
        tri = _tri(False)
        ones_col = jnp.ones((CHUNK, 128), f32)

        s_bd = s_scr[...]
        for ci in range(nch):
            rows = pl.ds(ci * CHUNK, CHUNK)
            _, cum, tot_row, dec4 = _gla_gates(a_ref[rows, :], wa_ref, ba_ref, tri, ones_col)
            kd = (k_ref[rows, :].astype(f32) * jnp.exp(tot_row - cum)).astype(bf16)
            upd = lax.dot_general(kd, v_ref[rows, :], (((0,), (0,)), ((), ())), preferred_element_type=f32) * mask
            s_bd = dec4 * s_bd + upd
            st_ref[pl.ds(ci * GQ, GQ), :] = _compact(s_bd)
            qs = (q_ref[rows, :].astype(f32) * 0.125).astype(bf16)
            o = jnp.dot(qs, s_bd.astype(bf16), preferred_element_type=f32)
            _, ons = _head_norm(o)
            rv = r_ref[rows, :].astype(f32)
            y_ref[rows, :] = (jnp.concatenate(ons, axis=1) * ng_ref[...] * (rv * jax.nn.sigmoid(rv))).astype(bf16)
        s_scr[...] = s_bd

    return pl.pallas_call(
        body, out_shape=(SDS(y.shape, bf16), SDS((S // CHUNK * GQ, 128), f32)), grid=(S // T,),
        in_specs=_gla_in_specs(T, lambda i: i) + [BS((RANKP, GQ), lambda i: (0, 0)), BS((1, GQ), lambda i: (0, 0)),
                                                  BS((1, GV), lambda i: (0, 0)), ANY],
        out_specs=(BS((T, GV), lambda i: (i, 1)), BS((nch * GQ, 128), lambda i: (i, 0))),
        scratch_shapes=[pltpu.VMEM((GQ, GV), f32)], input_output_aliases={8: 0}, compiler_params=_cp("arbitrary"),
        name=name)(proj, proj, proj, proj, pa, wa, ba, ng, y)


def gla_bwd(proj, pa, states, dy, wa, ba, ng, dproj, name):
    S = proj.shape[0]
    T = min(TGC * CHUNK, S)
    nch = T // CHUNK
    nT = S // T
    rev = lambda i: nT - 1 - i

    def body(q_ref, k_ref, v_ref, r_ref, a_ref, st_ref, sp_ref, dy_ref, wa_ref, ba_ref, ng_ref, dp_in,
             dp_ref, da_ref, dwa_ref, dba_ref, dng_ref, g_scr):
        del dp_in
        i = pl.program_id(0)

        @pl.when(i == 0)
        def _():
            g_scr[...] = jnp.zeros_like(g_scr)
            dwa_ref[...] = jnp.zeros_like(dwa_ref)
            dba_ref[...] = jnp.zeros_like(dba_ref)
            dng_ref[...] = jnp.zeros_like(dng_ref)

        mask = _bd_mask()
        tri = _tri(False)
        tri_strict = _tri(True)
        ones_col = jnp.ones((CHUNK, 128), f32)
        ones_row = jnp.ones((8, 128), f32)
        first_tile = (i == nT - 1).astype(f32)

        g_carry = g_scr[...]
        for ci in reversed(range(nch)):
            rows = pl.ds(ci * CHUNK, CHUNK)
            alr = a_ref[rows, :]
            a, cum, tot_row, dec4 = _gla_gates(alr, wa_ref, ba_ref, tri, ones_col)
            wdec = jnp.exp(tot_row - cum)
            kdf = k_ref[rows, :].astype(f32) * wdec
            kd = kdf.astype(bf16)
            s_c = _expand(st_ref[pl.ds(ci * GQ, GQ), :], mask)
            prev = st_ref[pl.ds((ci - 1) * GQ, GQ), :] if ci > 0 else sp_ref[...] * (1.0 - first_tile)
            qs = (q_ref[rows, :].astype(f32) * 0.125).astype(bf16)
            s_cb = s_c.astype(bf16)
            o = jnp.dot(qs, s_cb, preferred_element_type=f32)
            rns, ons = _head_norm(o)
            on = jnp.concatenate(ons, axis=1)
            rv = r_ref[rows, :].astype(f32)
            sg = jax.nn.sigmoid(rv)
            sr = rv * sg
            dyv = dy_ref[rows, :].astype(f32)
            ngv = ng_ref[...]
            dng_ref[...] += jnp.sum(dyv * on * sr, axis=0, keepdims=True)
            d_on = dyv * ngv * sr
            dr = dyv * on * ngv * (sg * (1.0 + rv * (1.0 - sg)))
            dos = []
            for h in range(4):
                cols = slice(h * 128, (h + 1) * 128)
                dh_ = d_on[:, cols]
                dos.append(rns[h] * (dh_ - ons[h] * jnp.mean(dh_ * ons[h], axis=-1, keepdims=True)))
            do = jnp.concatenate(dos, axis=1).astype(bf16)
            dq = lax.dot_general(do, s_cb, (((1,), (1,)), ((), ())), preferred_element_type=f32) * 0.125
            ds = lax.dot_general(qs, do, (((0,), (0,)), ((), ())), preferred_element_type=f32) * mask + g_carry
            ddec_row = lax.dot_general(ones_row, _compact(ds) * prev, (((1,), (1,)), ((), ())), precision=HI,
                                       preferred_element_type=f32)[0:1, :]
            dsb = ds.astype(bf16)
            dkd = lax.dot_general(v_ref[rows, :], dsb, (((1,), (1,)), ((), ())), preferred_element_type=f32)
            dv = jnp.dot(kd, dsb, preferred_element_type=f32)
            g_carry = dec4 * ds
            dk = dkd * wdec
            dwlog = dkd * kdf
            dla = ddec_row * jnp.exp(tot_row) + jnp.dot(tri_strict, dwlog, precision=HI, preferred_element_type=f32)
            da = dla * (1.0 - jax.nn.sigmoid(a)) * (1.0 / 16.0)
            dab = da.astype(bf16)
            da_ref[rows, :] = lax.dot_general(dab, wa_ref[...], (((1,), (1,)), ((), ())),
                                              preferred_element_type=f32).astype(bf16)
            dwa_ref[...] += lax.dot_general(alr, dab, (((0,), (0,)), ((), ())), preferred_element_type=f32)
            dba_ref[...] += jnp.sum(da, axis=0, keepdims=True)
            dp_ref[rows, 0:GQ] = dq.astype(bf16)
            dp_ref[rows, GQ:2 * GQ] = dk.astype(bf16)
            dp_ref[rows, 2 * GQ:2 * GQ + GV] = dv.astype(bf16)
            dp_ref[rows, 2 * GQ + GV:2 * GQ + 2 * GV] = dr.astype(bf16)
        g_scr[...] = g_carry

    REG = 2 * GQ + 2 * GV
    return pl.pallas_call(
        body,
        out_shape=(SDS((S, NP), bf16), SDS((S, RANKP), bf16), SDS((RANKP, GQ), f32), SDS((1, GQ), f32), SDS((1, GV), f32)),
        grid=(nT,),
        in_specs=_gla_in_specs(T, rev) + [
            BS((nch * GQ, 128), lambda i: (rev(i), 0)),
            BS((GQ, 128), lambda i: (jnp.maximum(rev(i) * nch - 1, 0), 0)),
            BS((T, GV), lambda i: (rev(i), 1)),
            BS((RANKP, GQ), lambda i: (0, 0)), BS((1, GQ), lambda i: (0, 0)), BS((1, GV), lambda i: (0, 0)), ANY],
        out_specs=(BS((T, REG), lambda i: (rev(i), 0)), BS((T, RANKP), lambda i: (rev(i), 0)),
                   BS((RANKP, GQ), lambda i: (0, 0)), BS((1, GQ), lambda i: (0, 0)), BS((1, GV), lambda i: (0, 0))),
        scratch_shapes=[pltpu.VMEM((GQ, GV), f32)],
        input_output_aliases={11: 0}, compiler_params=_cp("arbitrary"), name=name)(
            proj, proj, proj, proj, pa, states, states, dy, wa, ba, ng, dproj)


def _as2d(a):
    if a.ndim == 1:
        return a.reshape(1, a.shape[0])
    return a.reshape(-1, a.shape[-1])


def adamw(w, g, m, v, name):
    shape = w.shape
    w2, g2, m2, v2 = (_as2d(a) for a in (w, g, m, v))
    R, C = w2.shape
    tr = R
    for cand in (512, 256, 128, 64, 32, 16, 8):
        if R % cand == 0 and cand * C * 4 * 7 * 2 <= 40 * 1024 * 1024:
            tr = cand
            break

    def body(w_ref, g_ref, m_ref, v_ref, d_ref, mo_ref, vo_ref):
        gv = g_ref[...]
        mn = ADAM_B1 * m_ref[...] + (1.0 - ADAM_B1) * gv
        vn = ADAM_B2 * v_ref[...] + (1.0 - ADAM_B2) * (gv * gv)
        m_hat = mn / (1.0 - ADAM_B1 ** ADAM_STEP)
        v_hat = vn / (1.0 - ADAM_B2 ** ADAM_STEP)
        d_ref[...] = -ADAM_LR * (m_hat / (jnp.sqrt(v_hat) + ADAM_EPS) + ADAM_WD * w_ref[...])
        mo_ref[...] = mn
        vo_ref[...] = vn

    blk = BS((tr, C), lambda i: (i, 0))
    outs = pl.pallas_call(body, out_shape=tuple(SDS((R, C), f32) for _ in range(3)), grid=(R // tr,),
                          in_specs=[blk] * 4, out_specs=(blk,) * 3, compiler_params=_cp("parallel"), name=name)(w2, g2, m2, v2)
    return tuple(o.reshape(shape) for o in outs)


def _row_tile(rows, row_bytes, budget=4 * 1024 * 1024):
    best = None
    for t in range(16, rows + 1, 16):
        if rows % t == 0 and t * row_bytes <= budget:
            best = t
    return best or rows


def add_halves(g0, g1, ra, c, name):
    shape = ra.shape
    cols = shape[-1]
    rows = int(np.prod(shape[:-1]))
    tr = _row_tile(rows, cols * 2)
    blk = lambda: BS((tr, cols), lambda i, c_ref: (i, 0))
    grid_spec = pltpu.PrefetchScalarGridSpec(num_scalar_prefetch=1, grid=(rows // tr,), in_specs=[blk(), blk(), blk()],
                                             out_specs=blk())

    def body(c_ref, a0_ref, a1_ref, b_ref, o_ref):
        mine = jnp.where(c_ref[0] == 0, a0_ref[...], a1_ref[...])
        o_ref[...] = (mine.astype(f32) + b_ref[...].astype(f32)).astype(bf16)

    out = pl.pallas_call(body, out_shape=SDS((rows, cols), bf16), grid_spec=grid_spec, compiler_params=_cp("parallel"),
                         name=name)(jnp.reshape(c, (1,)).astype(jnp.int32), g0.reshape(rows, cols), g1.reshape(rows, cols),
                                    ra.reshape(rows, cols))
    return out.reshape(shape)


def reduce_chips(rb, own, c, chip, name):
    shape = rb.shape[1:]
    cols = shape[-1]
    rows = int(np.prod(shape[:-1]))
    tr = _row_tile(rows, cols * 2 * 4)
    rb3, own3 = rb.reshape(4, rows, cols), own.reshape(4, rows, cols)

    def body(s_ref, own_ref, r1, r2, r3, o_ref):
        del s_ref
        o_ref[0] = ((own_ref[0].astype(f32) + r1[0].astype(f32)) + r2[0].astype(f32)) + r3[0].astype(f32)

    def slot(k):
        return BS((1, tr, cols), functools.partial(lambda i, s, k: ((s[1] + k) % 4, i, 0), k=k))

    grid_spec = pltpu.PrefetchScalarGridSpec(
        num_scalar_prefetch=1, grid=(rows // tr,), in_specs=[slot(0), slot(1), slot(2), slot(3)],
        out_specs=BS((1, tr, cols), lambda i, s: (s[0], i, 0)))
    out = pl.pallas_call(body, out_shape=SDS((2, rows, cols), f32), grid_spec=grid_spec, compiler_params=_cp("parallel"),
                         name=name)(jnp.stack([c, chip]).astype(jnp.int32), own3, rb3, rb3, rb3)
    return out.reshape((2,) + shape)


def sum_slots(x, name):
    N, shape = x.shape[0], x.shape[1:]
    cols = shape[-1]
    rows = int(np.prod(shape[:-1]))
    tr = _row_tile(rows, cols * x.dtype.itemsize * N)

    def body(x_ref, o_ref):
        acc = x_ref[0].astype(f32)
        for n in range(1, N):
            acc = acc + x_ref[n].astype(f32)
        o_ref[...] = acc

    out = pl.pallas_call(body, out_shape=SDS((rows, cols), f32), grid=(rows // tr,),
                         in_specs=[BS((N, tr, cols), lambda i: (0, i, 0))], out_specs=BS((tr, cols), lambda i: (i, 0)),
                         compiler_params=_cp("parallel"), name=name)(x.reshape(N, rows, cols))
    return out.reshape(shape)


def _me():
    return lax.axis_index("x"), lax.axis_index("y"), lax.axis_index("c")


def _rcopy(src, dst, send_sems, recv_sems, k, dev):
    return pltpu.make_async_remote_copy(src_ref=src, dst_ref=dst, send_sem=send_sems.at[k], recv_sem=recv_sems.at[k],
                                        device_id=dev, device_id_type=MESH)


def _comm_call(body, ins, out_shapes, n_remote, name, aliases=None):
    return pl.pallas_call(
        body, out_shape=tuple(out_shapes), in_specs=[ANY] * len(ins), out_specs=tuple(ANY for _ in out_shapes),
        scratch_shapes=[pltpu.SemaphoreType.DMA((n_remote,)), pltpu.SemaphoreType.DMA((n_remote,))],
        input_output_aliases=aliases or {}, name=name)(*ins)


def _ag4_out_shapes(bufs):
    return [SDS((2, 4) + b.shape[1:], b.dtype) for b in bufs]


def _ag4_sems(n):
    return [pltpu.SemaphoreType.DMA((8 * n,)), pltpu.SemaphoreType.DMA((8 * n,))]


def _ag4_phases(xs, os, send_sems, recv_sems):
    n = len(xs)

    def place():
        x, y, c = _me()
        return x, y, c, 2 * x + y, (x, y, 1 - c), [(1 - x, y), (x, 1 - y), (1 - x, 1 - y)]

    def sends():
        x, y, c, j, sib, chips = place()
        first = [_rcopy(xs[t].at[c], os[t].at[c, j], send_sems, recv_sems, 8 * t + k, (cx, cy, c))
                 for t in range(n) for k, (cx, cy) in enumerate(chips)]
        own = [_rcopy(xs[t].at[l], os[t].at[l, j], send_sems, recv_sems, 8 * t + 6 + l, sib) for t in range(n) for l in range(2)]
        return first + own

    def forwards():
        x, y, c, j, sib, chips = place()
        return [(_rcopy(os[t].at[c, 2 * cx + cy], os[t].at[c, 2 * cx + cy], send_sems, recv_sems, 8 * t + k, (x, y, c)),
                 _rcopy(os[t].at[c, 2 * cx + cy], os[t].at[c, 2 * cx + cy], send_sems, recv_sems, 8 * t + 3 + k, sib))
                for k, (cx, cy) in enumerate(chips) for t in range(n)]

    def start():
        for cp in sends():
            cp.start()

    def forward():
        for landed, fwd in forwards():
            landed.wait_recv()
            fwd.start()

    def finish():
        x, y, c, j, sib, chips = place()
        for t in range(n):
            for l in range(2):
                land = os[t].at[l, j]
                _rcopy(land, land, send_sems, recv_sems, 8 * t + 6 + l, (x, y, c)).wait_recv()
        for k, (cx, cy) in enumerate(chips):
            for t in range(n):
                land = os[t].at[1 - c, 2 * cx + cy]
                _rcopy(land, land, send_sems, recv_sems, 8 * t + 3 + k, (x, y, c)).wait_recv()
        for cp in sends() + [fwd for _, fwd in forwards()]:
            cp.wait_send()

    return start, forward, finish


def _ag4_over_grid(xs, os, send_sems, recv_sems, step, nsteps, forward_frac):
    start, forward, finish = _ag4_phases(xs, os, send_sems, recv_sems)
    pl.when(step == 0)(start)
    pl.when(step == min(nsteps - 1, int(nsteps * forward_frac)))(forward)
    pl.when(step == nsteps - 1)(finish)


def sib_other_layer(g0s, g1s, name):
    n = len(g0s)

    def body(*refs):
        layers, os = (refs[:n], refs[n:2 * n]), refs[2 * n:3 * n]
        send_sems, recv_sems = refs[3 * n:]
        x, y, c = _me()
        for mine in range(2):
            @pl.when(c == mine)
            def _():
                cps = [_rcopy(layers[1 - mine][t], os[t], send_sems, recv_sems, t, (x, y, 1 - c)) for t in range(n)]
                for cp in cps:
                    cp.start()
                for cp in cps:
                    cp.wait()

    return _comm_call(body, list(g0s) + list(g1s), [SDS(g.shape, g.dtype) for g in g0s], n, name)


def a2a4(ps, name):
    n = len(ps)

    def body(*refs):
        xs, os = refs[:n], refs[n:2 * n]
        send_sems, recv_sems = refs[2 * n:]
        x, y, c = _me()
        j = 2 * x + y
        chips = [(1 - x, y), (x, 1 - y), (1 - x, 1 - y)]
        sends = [_rcopy(xs[t].at[2 * cx + cy], os[t].at[j], send_sems, recv_sems, 3 * t + k, (cx, cy, c))
                 for t in range(n) for k, (cx, cy) in enumerate(chips)]
        for cp in sends:
            cp.start()
        for t in range(n):
            for k, (cx, cy) in enumerate(chips):
                land = os[t].at[2 * cx + cy]
                _rcopy(land, land, send_sems, recv_sems, 3 * t + k, (x, y, c)).wait_recv()
        for cp in sends:
            cp.wait_send()

    return _comm_call(body, ps, [SDS(p.shape, p.dtype) for p in ps], 3 * n, name)


def ag2(bufs, name):
    n = len(bufs)

    def body(*refs):
        xs, os = refs[:n], refs[n:2 * n]
        send_sems, recv_sems = refs[2 * n:]
        x, y, c = _me()
        cps = [_rcopy(xs[t].at[c], os[t].at[c], send_sems, recv_sems, t, (x, y, 1 - c)) for t in range(n)]
        for cp in cps:
            cp.start()
        for t in range(n):
            land = os[t].at[1 - c]
            _rcopy(land, land, send_sems, recv_sems, t, (x, y, c)).wait_recv()
        for cp in cps:
            cp.wait_send()

    return _comm_call(body, bufs, [SDS(b.shape, b.dtype) for b in bufs], n, name, aliases={t: t for t in range(n)})


def ag8(blk, name):
    m_per, n = blk.shape

    def body(x_ref, out_ref, send_sems, recv_sems, local_sem):
        x, y, c = _me()
        me, sibling = (x, y, c), (x, y, 1 - c)
        chips = [(1 - x, y), (x, 1 - y), (1 - x, 1 - y)]

        def rows(px, py, pc):
            return out_ref.at[pl.ds((4 * px + 2 * py + pc) * m_per, m_per), :]

        def copy(k, block, to, src=None):
            return pltpu.make_async_remote_copy(
                src_ref=rows(*block) if src is None else src, dst_ref=rows(*block), send_sem=send_sems.at[k],
                recv_sem=recv_sems.at[k], device_id=to, device_id_type=MESH)

        mine = pltpu.make_async_copy(x_ref, rows(*me), local_sem)
        mine.start()
        first = [copy(0, me, sibling, src=x_ref)]
        first += [copy(1 + j, me, (*chip, c), src=x_ref) for j, chip in enumerate(chips)]
        for cp in first:
            cp.start()
        passed = [copy(4 + j, (*chip, c), sibling) for j, chip in enumerate(chips)]
        for j, chip in enumerate(chips):
            copy(1 + j, (*chip, c), me).wait_recv()
            passed[j].start()
        copy(0, sibling, me).wait_recv()
        for j, chip in enumerate(chips):
            copy(4 + j, (*chip, 1 - c), me).wait_recv()
        for cp in first + passed:
            cp.wait_send()
        mine.wait()

    return pl.pallas_call(
        body, out_shape=SDS((8 * m_per, n), blk.dtype), in_specs=[pl.BlockSpec(memory_space=pltpu.VMEM)],
        out_specs=pl.BlockSpec(memory_space=pltpu.VMEM),
        scratch_shapes=[pltpu.SemaphoreType.DMA((7,)), pltpu.SemaphoreType.DMA((7,)), pltpu.SemaphoreType.DMA],
        name=name)(blk)


def _split_chips(full, axis):
    n = full.shape[axis] // 4
    parts = full.reshape(full.shape[:axis] + (4, n) + full.shape[axis + 1:])
    return jnp.moveaxis(parts, axis, 0)


def _merge_chips(gathered, axis):
    parts = jnp.moveaxis(gathered, 0, axis)
    return parts.reshape(parts.shape[:axis] + (parts.shape[axis] * parts.shape[axis + 1],) + parts.shape[axis + 2:])


def _to_ref_cols(main, rank):
    pieces = []
    for n, width in REF_SPLITS:
        if n == "g_a":
            pieces.append(rank[..., :RANK])
        else:
            off = OUR_COLS[n][0]
            pieces.append(main[..., off:off + width])
    return jnp.concatenate(pieces, axis=-1)


def _from_ref_cols(w):
    offs, o = {}, 0
    for n, width in REF_SPLITS:
        offs[n] = (o, width)
        o += width
    main = jnp.concatenate([w[..., offs[n][0]:offs[n][0] + offs[n][1]] for n in sorted(OUR_COLS, key=lambda k: OUR_COLS[k][0])],
                           axis=-1)
    ro = offs["g_a"][0]
    rank = jnp.pad(w[..., ro:ro + RANK], [(0, 0)] * (w.ndim - 1) + [(0, RANKP - RANK)])
    return main, rank


def _layer_fwd(h, p_i, W, li, late=None, xn=None):
    t = f"l{li}_"
    sv = {"h0": h}

    def arrived(names, gathered, Ws):
        for l, Wl in enumerate(Ws):
            Wl.update(_prep_layer_weights(dict(zip(names, gathered)), None, l))

    if xn is None:
        xn = rms_fwd(h, W["norm1_g"], t + "rms1")
    if late is None:
        proj = mm_nn(xn, W["w_in_main"], name=t + "inproj")
    else:
        (names, shards, Ws) = late[1]
        proj, gathered = mm_nn(xn, W["w_in_main"], name=t + "inproj", gather=shards)
        arrived(names, gathered, Ws)
    pa = mm_nn(xn, W["w_in_rank"], name=t + "inproj_rank")
    y = sg_fwd(proj, W["sg_ln_g"], W["sg_ln_b"], W["sg_wm"], W["sg_bsb"], t + "sg_fwd")
    y, states = gla_fwd(proj, pa, W["gla_wa"], W["gla_b_a"], W["gla_norm_g"], y, t + "gla_fwd")
    if late is None:
        y = att_fwd(proj, W["att_bias"], y, t + "att_fwd")
    else:
        (names, shards, Ws) = late[0]
        y, gathered = att_fwd(proj, W["att_bias"], y, t + "att_fwd", gather=shards)
        arrived(names, gathered, Ws)
    y = conv_fwd(proj, W["conv_dw_w"], W["conv_dw_b"], W["conv_ln_g"], W["conv_ln_b"], y, t + "conv_fwd")
    gate = mm_nn(xn, W["w_gate_all"], bias=W["b_gate_all"], act="sigmoid", name=t + "gate")
    z = mm_nn(y, W["w_branch"], name=t + "branch")
    m = gate_merge_fwd(gate, z, t + "merge")
    h1 = mm_nn(m, W["w_out"], res=h, out_dtype=f32, name=t + "outproj")
    hn = rms_fwd(h1, W["norm2_g"], t + "rms2")
    a = mm_nn(hn, W["w_ff1"], name=t + "ff1")
    h2 = mm_nn(a, W["w_ff2"], pre="relu2", res=h1, out_dtype=f32, name=t + "ff2")
    hg = rms_fwd(h2, W["norm3_g"], t + "rms3")
    pg = mm_nn(hg, W["w_ple_gate"], bias=W["b_ple_gate"], act="sigmoid", name=t + "ple_gate")
    h3, e = mm_nn(p_i, W["w_ple"], mul=pg, res=h2, out_dtype=f32, raw_out=True, name=t + "ple_out")
    sv.update(xn=xn, proj=proj, pa=pa, states=states, y=y, gate=gate, z=z, m=m, h1=h1, hn=hn, a=a, h2=h2, hg=hg, pg=pg, e=e)
    return h3, sv


def _layer_bwd(dh3, sv, p_i, W, li):
    t = f"l{li}_b_"
    G = {}
    dpg, de, G["b_ple_gate"] = ple_bwd_ew(dh3, sv["e"], sv["pg"], t + "ple_ew")
    G["w_ple_gate"] = mm_tn(sv["hg"], dpg, name=t + "dw_ple_gate")[0]
    G["w_ple"] = mm_tn(p_i, de, name=t + "dw_ple")[0]
    dhg = mm_nt(dpg, W["w_ple_gate"], name=t + "dhg")
    dh2, G["norm3_g"] = rms_bwd(dhg, sv["h2"], W["norm3_g"], dh3, t + "rms3")
    da = mm_nt(dh2, W["w_ff2"], post_a=sv["a"], out_dtype=bf16, name=t + "da")
    G["w_ff2"] = mm_tn(sv["a"], dh2, pre="relu2", name=t + "dw_ff2")[0]
    G["w_ff1"] = mm_tn(sv["hn"], da, name=t + "dw_ff1")[0]
    dhn = mm_nt(da, W["w_ff1"], name=t + "dhn")
    dh1, G["norm2_g"] = rms_bwd(dhn, sv["h1"], W["norm2_g"], dh2, t + "rms2")
    dm = mm_nt(dh1, W["w_out"], out_dtype=bf16, name=t + "dm")
    G["w_out"] = mm_tn(sv["m"], dh1, name=t + "dw_out")[0]
    dz, dgp, G["b_gate_all"] = gate_merge_bwd(dm, sv["gate"], sv["z"], t + "merge")
    G["w_branch"] = mm_tn(sv["y"], dz, G=4, name=t + "dw_branch")
    dy = mm_nt(dz, W["w_branch"], out_dtype=bf16, name=t + "dy")
    G["w_gate_all"] = mm_tn(sv["xn"], dgp, name=t + "dw_gate")[0]
    dxn = mm_nt(dgp, W["w_gate_all"], name=t + "dxn_gate")
    proj = sv["proj"]
    dproj, dwm, dbs, G["sg_ln_g"], G["sg_ln_b"] = sg_bwd(proj, dy, W["sg_ln_g"], W["sg_ln_b"], W["sg_wm"], W["sg_bsb"],
                                                          W["sg_maskf"], t + "sg")
    G["sg_w"], G["sg_b"] = dwm, dbs[:, :, 0]
    dproj, dpa, dwa, G["gla_b_a"], G["gla_norm_g"] = gla_bwd(proj, sv["pa"], sv["states"], dy, W["gla_wa"], W["gla_b_a"],
                                                             W["gla_norm_g"], dproj, t + "gla")
    G["gla_w_a2"] = dwa[:RANK]
    dproj, dkp, dvp, dbias = att_bwd(proj, sv["y"], dy, W["att_bias"], dproj, t + "att")
    dproj = att_shift_add(dkp, dvp, dproj, t + "att_kv")
    G["att_rel_bias"] = att_bias_grad(dbias, t + "att_bias")
    dz_c, G["conv_ln_g"], G["conv_ln_b"], G["conv_dw_b"] = conv_bwd_norm(proj, dy, W["conv_dw_w"], W["conv_dw_b"],
                                                                        W["conv_ln_g"], W["conv_ln_b"], t + "conv_norm")
    dproj, G["conv_dw_w"] = conv_bwd_taps(proj, dz_c, W["conv_dw_w"], dproj, t + "conv_taps")
    G["w_in_main"] = mm_tn(sv["xn"], dproj, name=t + "dw_in")[0]
    G["w_in_rank"] = mm_tn(sv["xn"], dpa, name=t + "dw_in_rank")[0]
    dxn = mm_nt(dpa, W["w_in_rank"], res=dxn, name=t + "dxn_rank")
    dxn = mm_nt(dproj, W["w_in_main"], res=dxn, name=t + "dxn_main")
    dh0, G["norm1_g"] = rms_bwd(dxn, sv["h0"], W["norm1_g"], dh1, t + "rms1")
    return dh0, G


def _prep_layer_weights(gathered, repl, li):
    W = {}
    full = {n: _merge_chips(g[li], SHARDED[n][1]) for n, g in gathered.items()}
    if "w_in" in full:
        main, rank = _from_ref_cols(full["w_in"])
        W["w_in_main"], W["w_in_rank"] = main[None], rank[None]
    if "w_branch" in full:
        W["w_branch"] = full["w_branch"]
    if "w_gate" in full:
        W["w_gate_all"] = jnp.transpose(full["w_gate"], (1, 0, 2)).reshape(1, D, 4 * D)
    if "b_gate" in full:
        W["b_gate_all"] = full["b_gate"].reshape(1, 4 * D)
    for n in ("w_out", "w_ff1", "w_ff2", "w_ple_gate", "w_ple"):
        if n in full:
            W[n] = full[n][None]
    if "gla_w_a2" in full:
        W["gla_wa"] = jnp.pad(full["gla_w_a2"], ((0, RANKP - RANK), (0, 0))).astype(bf16)
    if "att_rel_bias" in full:
        W["att_bias"] = att_bias_build(full["att_rel_bias"], f"l{li}_att_bias")
    if "conv_dw_w" in full:
        W["conv_dw_w"] = full["conv_dw_w"]
    if repl is not None:
        for n in ("norm1_g", "norm2_g", "norm3_g", "b_ple_gate", "sg_ln_g", "sg_ln_b", "gla_b_a", "gla_norm_g", "conv_dw_b",
                  "conv_ln_g", "conv_ln_b"):
            W[n] = repl[n][li].reshape(1, -1)
        pos = np.arange(128)
        mask = (pos[None, :] // CHUNK) <= (pos[:, None] // CHUNK)
        W["sg_maskf"] = jnp.asarray(mask, f32)
        W["sg_wm"] = jnp.where(mask[None], repl["sg_w"][li], 0.0).astype(bf16)
        W["sg_bsb"] = jnp.broadcast_to(repl["sg_b"][li][:, :, None], (4, 128, 128))
    return W


def _layer_grads_to_ref(G):
    out = {}
    out["w_in"] = _to_ref_cols(G["w_in_main"], G["w_in_rank"])
    out["w_gate"] = jnp.transpose(G["w_gate_all"].reshape(D, 4, D), (1, 0, 2))
    out["b_gate"] = G["b_gate_all"].reshape(4, D)
    for n in ("w_branch", "w_out", "w_ff1", "w_ff2", "w_ple_gate", "w_ple", "gla_w_a2", "att_rel_bias", "conv_dw_w", "sg_w",
              "sg_b"):
        out[n] = G[n]
    for n in ("norm1_g", "norm2_g", "norm3_g", "b_ple_gate", "sg_ln_g", "sg_ln_b", "gla_b_a", "gla_norm_g", "conv_dw_b",
              "conv_ln_g", "conv_ln_b"):
        out[n] = G[n].reshape(-1)
    return out


def kernel(x, p, norm1_g, w_in, sg_ln_g, sg_ln_b, sg_w, sg_b, gla_w_a2, gla_b_a, gla_norm_g, att_rel_bias, conv_dw_w, conv_dw_b, conv_ln_g, conv_ln_b, w_branch, w_gate, b_gate, w_out, norm2_g, w_ff1, w_ff2, norm3_g, w_ple_gate, b_ple_gate, w_ple, final_g, loss_target, m_norm1_g, m_w_in, m_sg_ln_g, m_sg_ln_b, m_sg_w, m_sg_b, m_gla_w_a2, m_gla_b_a, m_gla_norm_g, m_att_rel_bias, m_conv_dw_w, m_conv_dw_b, m_conv_ln_g, m_conv_ln_b, m_w_branch, m_w_gate, m_b_gate, m_w_out, m_norm2_g, m_w_ff1, m_w_ff2, m_norm3_g, m_w_ple_gate, m_b_ple_gate, m_w_ple, m_final_g, v_norm1_g, v_w_in, v_sg_ln_g, v_sg_ln_b, v_sg_w, v_sg_b, v_gla_w_a2, v_gla_b_a, v_gla_norm_g, v_att_rel_bias, v_conv_dw_w, v_conv_dw_b, v_conv_ln_g, v_conv_ln_b, v_w_branch, v_w_gate, v_b_gate, v_w_out, v_norm2_g, v_w_ff1, v_w_ff2, v_norm3_g, v_w_ple_gate, v_b_ple_gate, v_w_ple, v_final_g):
    args = dict(locals())
    weights = {n: args[n] for n in W_ORDER}
    moments_m = {n: args["m_" + n] for n in W_ORDER}
    moments_v = {n: args["v_" + n] for n in W_ORDER}
    c = lax.axis_index("c")
    sharded_names = BIG + SMALL

    early = ("w_in",) + SMALL
    shards = {n: (weights[n].astype(bf16) if n in BIG else weights[n]) for n in sharded_names}
    repl = {n: weights[n] for n in REPL}
    h = x[0]
    xn0, gathered = rms_fwd(h, norm1_g[0].reshape(1, D), "l0_rms1", gather=[shards[n] for n in early])
    Ws = [_prep_layer_weights(dict(zip(early, gathered)), repl, li) for li in range(DEPTH)]
    in_att, in_proj = ("w_ff1", "w_ff2", "w_ple_gate", "w_ple"), ("w_gate", "w_branch", "w_out")
    late = [(names, [shards[n] for n in names], Ws) for names in (in_att, in_proj)]

    saved = []
    for li in range(DEPTH):
        h, sv = _layer_fwd(h, p[li, 0], Ws[li], li, late if li == 0 else None, xn0 if li == 0 else None)
        saved.append(sv)
    loss_part, dh, dfinal = loss_head(h, final_g.reshape(1, D), loss_target[0], "loss_head")
    loss = lax.psum(loss_part[0, 0], ("x", "y", "c"))

    layer_grads = [None] * DEPTH
    for li in reversed(range(DEPTH)):
        dh, G = _layer_bwd(dh, saved[li], p[li, 0], Ws[li], li)
        layer_grads[li] = _layer_grads_to_ref(G)
    grad_x = dh[None]

    g0s, g1s = ([_split_chips(layer_grads[li][n], SHARDED[n][1]).astype(bf16) for n in BIG] for li in range(DEPTH))
    ras = sib_other_layer(g0s, g1s, "rs_sibling_layer")
    psums = [add_halves(a0, a1, r, c, "rs_add_" + n) for n, a0, a1, r in zip(BIG, g0s, g1s, ras)]
    rbs = a2a4(psums, "rs_all_to_all")
    chip = 2 * lax.axis_index("x") + lax.axis_index("y")
    reds = [reduce_chips(r, ps, c, chip, "rs_sum_" + n) for n, r, ps in zip(BIG, rbs, psums)]
    grads = dict(zip(BIG, ag2(reds, "rs_sibling_gather")))

    local = {n: jnp.stack([layer_grads[li][n] for li in range(DEPTH)]) for n in tuple(REPL)[:-1] + SMALL}
    local["final_g"] = dfinal.reshape(D)
    rnames = tuple(REPL) + SMALL
    rflat = jnp.concatenate([local[n].reshape(-1) for n in rnames])
    rflat = jnp.pad(rflat, (0, REPL_ROWS * PACK_W - rflat.shape[0])).reshape(REPL_ROWS, PACK_W)
    rall = ag8(rflat, "ar_gather").reshape(8, REPL_ROWS, PACK_W)
    rsum = sum_slots(rall, "ar_sum").reshape(-1)
    off = 0
    for n in rnames:
        shape = local[n].shape
        size = int(np.prod(shape))
        g = rsum[off:off + size].reshape(shape)
        off += size
        if n in SMALL:
            ax = SHARDED[n][1] + 1
            g = lax.dynamic_slice_in_dim(g, chip * (shape[ax] // 4), shape[ax] // 4, axis=ax)
        grads[n] = g

    deltas, new_m, new_v = {}, {}, {}
    for n in W_ORDER:
        deltas[n], new_m[n], new_v[n] = adamw(weights[n], grads[n], moments_m[n], moments_v[n], "adamw_" + n)
    return (loss, grad_x, *[grads[n] for n in W_ORDER], *[deltas[n] for n in W_ORDER], *[new_m[n] for n in W_ORDER],
            *[new_v[n] for n in W_ORDER])
```

```python
import functools

import jax
import jax.numpy as jnp
import numpy as np
from jax import lax
from jax.experimental import pallas as pl
from jax.experimental.pallas import tpu as pltpu

f32, bf16 = jnp.float32, jnp.bfloat16
HI = lax.Precision.HIGHEST
MESH = pl.DeviceIdType.MESH
SDS = jax.ShapeDtypeStruct
BS = pl.BlockSpec
ANY = pl.BlockSpec(memory_space=pl.ANY)

D = 1024
DEPTH = 2
CHUNK = 64
BW = 512
NP = 5120
RANK = 16
RANKP = 128
DFF = 4096
PLE = 256
CONV_K = 31
HALO = 32
TQ = 256
WIN = 768
REL_TABLE = 320
EPS = 1e-6
NEG_INF = -1e30
VMEM_LIMIT = 56 * 1024 * 1024

ADAM_LR, ADAM_B1, ADAM_B2, ADAM_EPS, ADAM_WD, ADAM_STEP = 0.001, 0.9, 0.999, 1e-08, 0.01, 10

OUR_COLS = dict(g_q=(0, 256), g_k=(256, 256), g_v=(512, 512), g_r=(1024, 512), a_q=(1536, 512), a_k=(2048, 512),
                a_v=(2560, 512), sg_u=(3072, 512), sg_v=(3584, 512), c_a=(4096, 512), c_g=(4608, 512))
REF_SPLITS = (("sg_u", 512), ("sg_v", 512), ("g_q", 256), ("g_k", 256), ("g_v", 512), ("g_r", 512), ("g_a", 16),
              ("a_q", 512), ("a_k", 512), ("a_v", 512), ("c_a", 512), ("c_g", 512))

SHARDED = dict(w_in=((1024, 5136), 1), w_branch=((4, 512, 1024), 2), w_gate=((4, 1024, 1024), 1), w_out=((1024, 1024), 0),
               w_ff1=((1024, 4096), 1), w_ff2=((4096, 1024), 0), w_ple_gate=((1024, 1024), 0), w_ple=((256, 1024), 1),
               gla_w_a2=((16, 256), 1), att_rel_bias=((8, 320), 1), conv_dw_w=((31, 512), 1), b_gate=((4, 1024), 1))
BIG = ("w_in", "w_branch", "w_gate", "w_out", "w_ff1", "w_ff2", "w_ple_gate", "w_ple")
SMALL = ("gla_w_a2", "att_rel_bias", "conv_dw_w", "b_gate")
REPL = dict(norm1_g=(2, 1024), sg_ln_g=(2, 512), sg_ln_b=(2, 512), sg_w=(2, 4, 128, 128), sg_b=(2, 4, 128), gla_b_a=(2, 256),
            gla_norm_g=(2, 512), conv_dw_b=(2, 512), conv_ln_g=(2, 512), conv_ln_b=(2, 512), norm2_g=(2, 1024),
            norm3_g=(2, 1024), b_ple_gate=(2, 1024), final_g=(1024,))
W_ORDER = ['norm1_g', 'w_in', 'sg_ln_g', 'sg_ln_b', 'sg_w', 'sg_b', 'gla_w_a2', 'gla_b_a', 'gla_norm_g', 'att_rel_bias',
           'conv_dw_w', 'conv_dw_b', 'conv_ln_g', 'conv_ln_b', 'w_branch', 'w_gate', 'b_gate', 'w_out', 'norm2_g', 'w_ff1',
           'w_ff2', 'norm3_g', 'w_ple_gate', 'b_ple_gate', 'w_ple', 'final_g']
PACK_W = 1024
REPL_ROWS = 200
AG_FORWARD_AT_END = 1.0
AG_FORWARD_INPROJ = 0.875
AG_FORWARD_ATTENTION = 0.75
TM = 1024
MM_VMEM_BUDGET = 44 * 1024 * 1024


def _tile(s):
    return 512 if s % 512 == 0 else s


def _token_tile(S, row_bytes, fixed_bytes):
    for t in (2048, 1024):
        if S % t == 0 and 2 * (t * row_bytes + fixed_bytes) <= MM_VMEM_BUDGET:
            return t
    return min(TM, S)


def _cp(*sem):
    return pltpu.CompilerParams(dimension_semantics=sem, vmem_limit_bytes=VMEM_LIMIT)


def rms_fwd(h, g, name, gather=None):
    S, Dm = h.shape
    T = _tile(S)
    ng = len(gather) if gather else 0

    def body(h_ref, g_ref, *rest):
        o_ref = rest[ng]
        if ng:
            _ag4_over_grid(rest[:ng], rest[ng + 1:2 * ng + 1], *rest[2 * ng + 1:], pl.program_id(0), S // T,
                           AG_FORWARD_AT_END)
        x = h_ref[...]
        r = lax.rsqrt(jnp.mean(x * x, axis=-1, keepdims=True) + EPS)
        o_ref[...] = (x * r * g_ref[...]).astype(bf16)

    outs = pl.pallas_call(
        body, out_shape=(SDS((S, Dm), bf16),) + tuple(_ag4_out_shapes(gather or [])), grid=(S // T,),
        in_specs=[BS((T, Dm), lambda i: (i, 0)), BS((1, Dm), lambda i: (0, 0))] + [ANY] * ng,
        out_specs=(BS((T, Dm), lambda i: (i, 0)),) + tuple(ANY for _ in range(ng)),
        scratch_shapes=_ag4_sems(ng) if ng else [], compiler_params=_cp("arbitrary" if ng else "parallel"),
        name=name)(h, g, *(gather or []))
    return (outs[0], list(outs[1:])) if ng else outs[0]


def mm_nn(x, w, *, name, bias=None, act=None, pre=None, mul=None, res=None, out_dtype=bf16, raw_out=False, gather=None):
    S = x.shape[0]
    G, K, N = w.shape
    tn = min(1024 if K <= 1024 else 512, N)
    nj = N // tn
    row_bytes = K * x.dtype.itemsize + tn * (jnp.dtype(out_dtype).itemsize + (2 if raw_out else 0)
                                             + sum(a.dtype.itemsize for a in (mul, res) if a is not None))
    T = _token_tile(S, row_bytes, K * tn * 2)
    extras = [a for a in (bias, mul, res) if a is not None]
    ng = len(gather) if gather else 0
    grid = (S // T, G, nj)

    def body(*refs):
        it = iter(refs)
        x_ref, w_ref = next(it), next(it)
        b_ref = next(it) if bias is not None else None
        m_ref = next(it) if mul is not None else None
        r_ref = next(it) if res is not None else None
        g_in = [next(it) for _ in range(ng)]
        o_ref = next(it)
        raw_ref = next(it) if raw_out else None
        if ng:
            g_out = [next(it) for _ in range(ng)]
            step = (pl.program_id(0) * G + pl.program_id(1)) * nj + pl.program_id(2)
            _ag4_over_grid(g_in, g_out, next(it), next(it), step, grid[0] * G * nj, AG_FORWARD_INPROJ)
        xv = x_ref[...]
        if pre == "relu2":
            xf = jnp.maximum(xv.astype(f32), 0.0)
            xv = xf * xf
        acc = jnp.dot(xv.astype(bf16), w_ref[0], preferred_element_type=f32)
        if raw_out:
            raw_ref[...] = acc.astype(bf16)
        if b_ref is not None:
            acc = acc + b_ref[...]
        if act == "sigmoid":
            acc = jax.nn.sigmoid(acc)
        if m_ref is not None:
            acc = acc * m_ref[...].astype(f32)
        if r_ref is not None:
            acc = r_ref[...].astype(f32) + acc
        o_ref[...] = acc.astype(out_dtype)

    in_specs = [BS((T, K), lambda i, g, j: (i, g)), BS((1, K, tn), lambda i, g, j: (g, 0, j))]
    if bias is not None:
        in_specs.append(BS((1, tn), lambda i, g, j: (0, g * nj + j)))
    for a in (mul, res):
        if a is not None:
            in_specs.append(BS((T, tn), lambda i, g, j: (i, g * nj + j)))
    ospec = BS((T, tn), lambda i, g, j: (i, g * nj + j))
    out_shape = SDS((S, G * N), out_dtype)
    if raw_out:
        out_shape, ospec = (out_shape, SDS((S, G * N), bf16)), (ospec, ospec)
    if not ng:
        return pl.pallas_call(
            body, out_shape=out_shape, grid=grid, in_specs=in_specs, out_specs=ospec,
            compiler_params=_cp("parallel", "parallel", "parallel"), name=name)(x, w, *extras)
    out_shape = (out_shape if raw_out else (out_shape,)) + tuple(_ag4_out_shapes(gather))
    ospec = (ospec if raw_out else (ospec,)) + tuple(ANY for _ in gather)
    outs = pl.pallas_call(
        body, out_shape=out_shape, grid=grid, in_specs=in_specs + [ANY] * ng, out_specs=ospec,
        scratch_shapes=_ag4_sems(ng), compiler_params=_cp("arbitrary", "arbitrary", "arbitrary"), name=name)(
            x, w, *extras, *gather)
    nres = 2 if raw_out else 1
    return (outs[0] if nres == 1 else outs[:2]), list(outs[nres:])


def mm_nt(dy, w, *, name, res=None, post_a=None, out_dtype=f32):
    S = dy.shape[0]
    G, K, N = w.shape
    tk = min(1024 if N <= 4096 else 512, K)
    nk = K // tk
    row_bytes = N * dy.dtype.itemsize + tk * (jnp.dtype(out_dtype).itemsize
                                              + sum(a.dtype.itemsize for a in (res, post_a) if a is not None))
    T = _token_tile(S, row_bytes, tk * N * 2)
    extras = [a for a in (res, post_a) if a is not None]

    def body(*refs):
        it = iter(refs)
        d_ref, w_ref = next(it), next(it)
        r_ref = next(it) if res is not None else None
        a_ref = next(it) if post_a is not None else None
        o_ref = next(it)
        acc = lax.dot_general(d_ref[...].astype(bf16), w_ref[0], (((1,), (1,)), ((), ())), preferred_element_type=f32)
        if r_ref is not None:
            acc = acc + r_ref[...].astype(f32)
        if a_ref is not None:
            acc = acc * (2.0 * jnp.maximum(a_ref[...].astype(f32), 0.0))
        o_ref[...] = acc.astype(out_dtype)

    in_specs = [BS((T, N), lambda i, g, j: (i, g)), BS((1, tk, N), lambda i, g, j: (g, j, 0))]
    for a in extras:
        in_specs.append(BS((T, tk), lambda i, g, j: (i, g * nk + j)))
    return pl.pallas_call(
        body, out_shape=SDS((S, G * K), out_dtype), grid=(S // T, G, nk), in_specs=in_specs,
        out_specs=BS((T, tk), lambda i, g, j: (i, g * nk + j)),
        compiler_params=_cp("parallel", "parallel", "parallel"), name=name)(dy, w, *extras)


def mm_tn(x, dy, *, name, G=1, pre=None, ts=None):
    S = x.shape[0]
    K, N = x.shape[1] // G, dy.shape[1] // G
    tk, tn = min(K, 1024), min(N, 1024)
    nk, nn = K // tk, N // tn
    if ts is None:
        ts = 2048 if (x.dtype == bf16 and dy.dtype == bf16 and pre is None and S % 2048 == 0) else 1024
    ts = min(ts, S)

    def body(x_ref, d_ref, o_ref, acc):
        @pl.when(pl.program_id(3) == 0)
        def _():
            acc[...] = jnp.zeros_like(acc)

        xv = x_ref[...]
        if pre == "relu2":
            xf = jnp.maximum(xv.astype(f32), 0.0)
            xv = xf * xf
        acc[...] += lax.dot_general(xv.astype(bf16), d_ref[...].astype(bf16), (((0,), (0,)), ((), ())),
                                    preferred_element_type=f32)

        @pl.when(pl.program_id(3) == S // ts - 1)
        def _():
            o_ref[0] = acc[...].astype(bf16)

    return pl.pallas_call(
        body, out_shape=SDS((G, K, N), bf16), grid=(G, nk, nn, S // ts),
        in_specs=[BS((ts, tk), lambda g, a, b, s: (s, g * nk + a)), BS((ts, tn), lambda g, a, b, s: (s, g * nn + b))],
        out_specs=BS((1, tk, tn), lambda g, a, b, s: (g, a, b)), scratch_shapes=[pltpu.VMEM((tk, tn), f32)],
        compiler_params=_cp("parallel", "parallel", "parallel", "arbitrary"), name=name)(x, dy)


def rms_bwd(dxn, x, g, dres, name):
    S, Dm = x.shape
    T = _tile(S)

    def body(*refs):
        if dres is not None:
            d_ref, x_ref, g_ref, r_ref, dx_ref, dg_ref = refs
        else:
            d_ref, x_ref, g_ref, dx_ref, dg_ref = refs
        xv = x_ref[...]
        d = d_ref[...].astype(f32)
        r = lax.rsqrt(jnp.mean(xv * xv, axis=-1, keepdims=True) + EPS)
        u = d * g_ref[...]
        dx = r * u - xv * ((r * r * r) * (1.0 / Dm)) * jnp.sum(u * xv, axis=-1, keepdims=True)
        if dres is not None:
            dx = r_ref[...] + dx
        dx_ref[...] = dx

        @pl.when(pl.program_id(0) == 0)
        def _():
            dg_ref[...] = jnp.zeros_like(dg_ref)

        dg_ref[...] += jnp.sum(d * xv * r, axis=0, keepdims=True)

    tok = BS((T, Dm), lambda i: (i, 0))
    vec = BS((1, Dm), lambda i: (0, 0))
    args = (dxn, x, g) + ((dres,) if dres is not None else ())
    return pl.pallas_call(
        body, out_shape=(SDS((S, Dm), f32), SDS((1, Dm), f32)), grid=(S // T,),
        in_specs=[tok, tok, vec] + ([tok] if dres is not None else []), out_specs=(tok, vec),
        compiler_params=_cp("arbitrary"), name=name)(*args)


def loss_head(h, g, target, name):
    S, Dm = h.shape
    T = _tile(S)

    def body(h_ref, g_ref, t_ref, loss_ref, dh_ref, dg_ref):
        @pl.when(pl.program_id(0) == 0)
        def _():
            loss_ref[...] = jnp.zeros_like(loss_ref)
            dg_ref[...] = jnp.zeros_like(dg_ref)

        xv = h_ref[...]
        gv = g_ref[...]
        r = lax.rsqrt(jnp.mean(xv * xv, axis=-1, keepdims=True) + EPS)
        diff = xv * r * gv - t_ref[...]
        loss_ref[...] += 0.5 * jnp.sum(jnp.mean(diff * diff, axis=-1, keepdims=True))
        d = diff * (1.0 / Dm)
        u = d * gv
        dh_ref[...] = r * u - xv * ((r * r * r) * (1.0 / Dm)) * jnp.sum(u * xv, axis=-1, keepdims=True)
        dg_ref[...] += jnp.sum(d * xv * r, axis=0, keepdims=True)

    tok = BS((T, Dm), lambda i: (i, 0))
    vec = BS((1, Dm), lambda i: (0, 0))
    return pl.pallas_call(
        body, out_shape=(SDS((1, 128), f32), SDS((S, Dm), f32), SDS((1, Dm), f32)), grid=(S // T,),
        in_specs=[tok, vec, tok], out_specs=(BS((1, 128), lambda i: (0, 0)), tok, vec),
        compiler_params=_cp("arbitrary"), name=name)(h, g, target)


def gate_merge_fwd(gate, z, name):
    S = gate.shape[0]
    T = _tile(S)

    def body(g_ref, z_ref, o_ref):
        acc = jnp.zeros((T, D), f32)
        for n in range(4):
            acc = acc + g_ref[:, n * D:(n + 1) * D].astype(f32) * z_ref[:, n * D:(n + 1) * D].astype(f32)
        o_ref[...] = acc.astype(bf16)

    wide = BS((T, 4 * D), lambda i: (i, 0))
    return pl.pallas_call(body, out_shape=SDS((S, D), bf16), grid=(S // T,), in_specs=[wide, wide],
                          out_specs=BS((T, D), lambda i: (i, 0)), compiler_params=_cp("parallel"), name=name)(gate, z)


def gate_merge_bwd(dm, gate, z, name):
    S = gate.shape[0]
    T = _tile(S)

    def body(dm_ref, g_ref, z_ref, dz_ref, dg_ref, db_ref):
        @pl.when(pl.program_id(0) == 0)
        def _():
            db_ref[...] = jnp.zeros_like(db_ref)

        dmv = dm_ref[...].astype(f32)
        for n in range(4):
            cols = slice(n * D, (n + 1) * D)
            gv = g_ref[:, cols].astype(f32)
            dz_ref[:, cols] = (dmv * gv).astype(bf16)
            dgp = dmv * z_ref[:, cols].astype(f32) * gv * (1.0 - gv)
            dg_ref[:, cols] = dgp.astype(bf16)
            db_ref[:, cols] += jnp.sum(dgp, axis=0, keepdims=True)

    wide = BS((T, 4 * D), lambda i: (i, 0))
    return pl.pallas_call(
        body, out_shape=(SDS((S, 4 * D), bf16), SDS((S, 4 * D), bf16), SDS((1, 4 * D), f32)), grid=(S // T,),
        in_specs=[BS((T, D), lambda i: (i, 0)), wide, wide], out_specs=(wide, wide, BS((1, 4 * D), lambda i: (0, 0))),
        compiler_params=_cp("arbitrary"), name=name)(dm, gate, z)


def ple_bwd_ew(dh, e, pg, name):
    S = dh.shape[0]
    T = _tile(S)

    def body(dh_ref, e_ref, pg_ref, dp_ref, de_ref, db_ref):
        @pl.when(pl.program_id(0) == 0)
        def _():
            db_ref[...] = jnp.zeros_like(db_ref)

        d = dh_ref[...]
        g = pg_ref[...].astype(f32)
        dpre = d * e_ref[...].astype(f32) * g * (1.0 - g)
        dp_ref[...] = dpre.astype(bf16)
        de_ref[...] = (d * g).astype(bf16)
        db_ref[...] += jnp.sum(dpre, axis=0, keepdims=True)

    tok = BS((T, D), lambda i: (i, 0))
    return pl.pallas_call(
        body, out_shape=(SDS((S, D), bf16), SDS((S, D), bf16), SDS((1, D), f32)), grid=(S // T,),
        in_specs=[tok, tok, tok], out_specs=(tok, tok, BS((1, D), lambda i: (0, 0))),
        compiler_params=_cp("arbitrary"), name=name)(dh, e, pg)


_GK = 0.7978845608028654
_GC = 0.044715


def _gelu(x):
    return 0.5 * x * (1.0 + jnp.tanh(_GK * (x + _GC * (x * x * x))))


def _gelu_grad(x):
    x2 = x * x
    t = jnp.tanh(_GK * (x + _GC * (x * x2)))
    return 0.5 * (1.0 + t) + 0.5 * x * (1.0 - t * t) * (_GK * (1.0 + 3.0 * _GC * x2))


def _ln_stats(v):
    mu = jnp.mean(v, axis=-1, keepdims=True)
    vc = v - mu
    rs = lax.rsqrt(jnp.mean(vc * vc, axis=-1, keepdims=True) + EPS)
    return vc * rs, rs


def _ln_bwd(dvh, vh, rs):
    return rs * (dvh - jnp.mean(dvh, axis=-1, keepdims=True) - vh * jnp.mean(dvh * vh, axis=-1, keepdims=True))


def sg_fwd(proj, lg, lb, wm, bsb, name):
    S = proj.shape[0]
    T = _tile(S)
    cu, cv = OUR_COLS["sg_u"][0] // BW, OUR_COLS["sg_v"][0] // BW

    def body(u_ref, v_ref, lg_ref, lb_ref, wm_ref, bsb_ref, o_ref):
        for b in range(T // 128):
            rows = slice(b * 128, (b + 1) * 128)
            u = _gelu(u_ref[rows, :].astype(f32))
            vh, _ = _ln_stats(_gelu(v_ref[rows, :].astype(f32)))
            vb = (vh * lg_ref[...] + lb_ref[...]).astype(bf16)
            outs = []
            for g in range(4):
                cols = slice(g * 128, (g + 1) * 128)
                mixed = jnp.dot(wm_ref[g], vb[:, cols], preferred_element_type=f32) + bsb_ref[g]
                outs.append(u[:, cols] * mixed)
            o_ref[rows, :] = jnp.concatenate(outs, axis=1).astype(bf16)

    vec = BS((1, BW), lambda i: (0, 0))
    cube = BS((4, 128, 128), lambda i: (0, 0, 0))
    return pl.pallas_call(
        body, out_shape=SDS((S, 4 * BW), bf16), grid=(S // T,),
        in_specs=[BS((T, BW), lambda i: (i, cu)), BS((T, BW), lambda i: (i, cv)), vec, vec, cube, cube],
        out_specs=BS((T, BW), lambda i: (i, 0)), compiler_params=_cp("parallel"), name=name)(proj, proj, lg, lb, wm, bsb)


def sg_bwd(proj, dy, lg, lb, wm, bsb, maskf, name):
    S = proj.shape[0]
    T = _tile(S)
    cu, cv = OUR_COLS["sg_u"][0] // BW, OUR_COLS["sg_v"][0] // BW
    creg = OUR_COLS["sg_u"][0] // (2 * BW)

    def body(u_ref, v_ref, dy_ref, lg_ref, lb_ref, wm_ref, bsb_ref, mk_ref, dp_ref, dwm_ref, dbs_ref, dlg_ref, dlb_ref):
        @pl.when(pl.program_id(0) == 0)
        def _():
            dwm_ref[...] = jnp.zeros_like(dwm_ref)
            dbs_ref[...] = jnp.zeros_like(dbs_ref)
            dlg_ref[...] = jnp.zeros_like(dlg_ref)
            dlb_ref[...] = jnp.zeros_like(dlb_ref)

        for b in range(T // 128):
            rows = slice(b * 128, (b + 1) * 128)
            su = u_ref[rows, :].astype(f32)
            sv = v_ref[rows, :].astype(f32)
            dya = dy_ref[rows, :].astype(f32)
            u = _gelu(su)
            vh, rs = _ln_stats(_gelu(sv))
            vb = (vh * lg_ref[...] + lb_ref[...]).astype(bf16)
            dus, dvls = [], []
            for g in range(4):
                cols = slice(g * 128, (g + 1) * 128)
                mixed = jnp.dot(wm_ref[g], vb[:, cols], preferred_element_type=f32) + bsb_ref[g]
                dus.append(dya[:, cols] * mixed)
                dmg = dya[:, cols] * u[:, cols]
                dmb = dmg.astype(bf16)
                dbs_ref[g] += jnp.broadcast_to(jnp.sum(dmg, axis=1, keepdims=True), (128, 128))
                dwm_ref[g] += mk_ref[...] * lax.dot_general(dmb, vb[:, cols], (((1,), (1,)), ((), ())),
                                                            preferred_element_type=f32)
                dvls.append(lax.dot_general(wm_ref[g], dmb, (((0,), (0,)), ((), ())), preferred_element_type=f32))
            du = jnp.concatenate(dus, axis=1)
            dvln = jnp.concatenate(dvls, axis=1)
            dlg_ref[...] += jnp.sum(dvln * vh, axis=0, keepdims=True)
            dlb_ref[...] += jnp.sum(dvln, axis=0, keepdims=True)
            dv = _ln_bwd(dvln * lg_ref[...], vh, rs)
            dp_ref[rows, 0:BW] = (du * _gelu_grad(su)).astype(bf16)
            dp_ref[rows, BW:2 * BW] = (dv * _gelu_grad(sv)).astype(bf16)

    vec = BS((1, BW), lambda i: (0, 0))
    cube = BS((4, 128, 128), lambda i: (0, 0, 0))
    return pl.pallas_call(
        body,
        out_shape=(SDS((S, NP), bf16), SDS((4, 128, 128), f32), SDS((4, 128, 128), f32), SDS((1, BW), f32), SDS((1, BW), f32)),
        grid=(S // T,),
        in_specs=[BS((T, BW), lambda i: (i, cu)), BS((T, BW), lambda i: (i, cv)), BS((T, BW), lambda i: (i, 0)), vec, vec,
                  cube, cube, BS((128, 128), lambda i: (0, 0))],
        out_specs=(BS((T, 2 * BW), lambda i: (i, creg)), cube, cube, vec, vec),
        compiler_params=_cp("arbitrary"), name=name)(proj, proj, dy, lg, lb, wm, bsb, maskf)


_SUB = 64


def _conv_specs(S, T):
    ca, cg = OUR_COLS["c_a"][0] // BW, OUR_COLS["c_g"][0] // BW
    hb = T // HALO
    prev = lambda i: jnp.maximum(i * hb - 1, 0)
    return [BS((T, BW), lambda i: (i, ca)), BS((T, BW), lambda i: (i, cg)),
            BS((HALO, BW), lambda i: (prev(i), ca)), BS((HALO, BW), lambda i: (prev(i), cg))]


def _fill_shifts(sh):
    n = sh.shape[1] - 8
    for s in range(1, 8):
        sh[s, pl.ds(0, n), :] = sh[0, pl.ds(s, n), :]


def _shifted(sh, off, rows):
    s = off % 8
    return sh[s, pl.ds(off - s, rows), :]


def _conv_fill_ybuf(a_ref, g_ref, ap_ref, gp_ref, ysh):
    T = a_ref.shape[0]
    ysh[0, pl.ds(HALO, T), :] = a_ref[...].astype(f32) * jax.nn.sigmoid(g_ref[...].astype(f32))
    first = (pl.program_id(0) == 0).astype(f32)
    ysh[0, pl.ds(0, HALO), :] = (1.0 - first) * (ap_ref[...].astype(f32) * jax.nn.sigmoid(gp_ref[...].astype(f32)))
    _fill_shifts(ysh)


def _conv_taps(w_ref, ysh, r0):
    acc = jnp.zeros((_SUB, BW), f32)
    for k in range(CONV_K):
        acc = acc + w_ref[k:k + 1, :] * _shifted(ysh, r0 + HALO - (CONV_K - 1) + k, _SUB)
    return acc


def conv_fwd(proj, w, b, lg, lb, y, name):
    S = proj.shape[0]
    T = _tile(S)

    def body(a_ref, g_ref, ap_ref, gp_ref, w_ref, b_ref, lg_ref, lb_ref, y_in, o_ref, ybuf):
        del y_in
        _conv_fill_ybuf(a_ref, g_ref, ap_ref, gp_ref, ybuf)
        for sb in range(T // _SUB):
            z = _conv_taps(w_ref, ybuf, sb * _SUB) + b_ref[...]
            zh, _ = _ln_stats(z)
            zl = zh * lg_ref[...] + lb_ref[...]
            o_ref[pl.ds(sb * _SUB, _SUB), :] = (zl * jax.nn.sigmoid(zl)).astype(bf16)

    vec = BS((1, BW), lambda i: (0, 0))
    return pl.pallas_call(
        body, out_shape=SDS(y.shape, bf16), grid=(S // T,),
        in_specs=_conv_specs(S, T) + [BS((CONV_K, BW), lambda i: (0, 0)), vec, vec, vec, ANY],
        out_specs=BS((T, BW), lambda i: (i, 3)), scratch_shapes=[pltpu.VMEM((8, T + HALO, BW), f32)],
        input_output_aliases={8: 0}, compiler_params=_cp("parallel"), name=name)(proj, proj, proj, proj, w, b, lg, lb, y)


def conv_bwd_norm(proj, dy, w, b, lg, lb, name):
    S = proj.shape[0]
    T = _tile(S)

    def body(a_ref, g_ref, ap_ref, gp_ref, dy_ref, w_ref, b_ref, lg_ref, lb_ref, dz_ref, dlg_ref, dlb_ref, db_ref, ybuf):
        @pl.when(pl.program_id(0) == 0)
        def _():
            dlg_ref[...] = jnp.zeros_like(dlg_ref)
            dlb_ref[...] = jnp.zeros_like(dlb_ref)
            db_ref[...] = jnp.zeros_like(db_ref)

        _conv_fill_ybuf(a_ref, g_ref, ap_ref, gp_ref, ybuf)
        for sb in range(T // _SUB):
            rows = pl.ds(sb * _SUB, _SUB)
            z = _conv_taps(w_ref, ybuf, sb * _SUB) + b_ref[...]
            zh, rs = _ln_stats(z)
            zl = zh * lg_ref[...] + lb_ref[...]
            sg = jax.nn.sigmoid(zl)
            dzl = dy_ref[rows, :].astype(f32) * sg * (1.0 + zl * (1.0 - sg))
            dlg_ref[...] += jnp.sum(dzl * zh, axis=0, keepdims=True)
            dlb_ref[...] += jnp.sum(dzl, axis=0, keepdims=True)
            dz = _ln_bwd(dzl * lg_ref[...], zh, rs)
            db_ref[...] += jnp.sum(dz, axis=0, keepdims=True)
            dz_ref[rows, :] = dz

    vec = BS((1, BW), lambda i: (0, 0))
    tok = BS((T, BW), lambda i: (i, 0))
    return pl.pallas_call(
        body, out_shape=(SDS((S, BW), f32), SDS((1, BW), f32), SDS((1, BW), f32), SDS((1, BW), f32)), grid=(S // T,),
        in_specs=_conv_specs(S, T) + [BS((T, BW), lambda i: (i, 3)), BS((CONV_K, BW), lambda i: (0, 0)), vec, vec, vec],
        out_specs=(tok, vec, vec, vec), scratch_shapes=[pltpu.VMEM((8, T + HALO, BW), f32)],
        compiler_params=_cp("arbitrary"), name=name)(proj, proj, proj, proj, dy, w, b, lg, lb)


def conv_bwd_taps(proj, dz, w, dproj, name):
    S = proj.shape[0]
    T = _tile(S)
    nT = S // T
    hb = T // HALO
    creg = OUR_COLS["c_a"][0] // (2 * BW)

    def body(a_ref, g_ref, ap_ref, gp_ref, dz_ref, dzn_ref, w_ref, dp_in, dp_ref, dw_ref, ybuf, dzbuf, dwacc):
        del dp_in
        i = pl.program_id(0)

        @pl.when(i == 0)
        def _():
            dwacc[...] = jnp.zeros_like(dwacc)

        _conv_fill_ybuf(a_ref, g_ref, ap_ref, gp_ref, ybuf)
        dzbuf[0, pl.ds(0, T), :] = dz_ref[...]
        dzbuf[0, pl.ds(T, HALO), :] = (i < nT - 1).astype(f32) * dzn_ref[...]
        _fill_shifts(dzbuf)
        for sb in range(T // _SUB):
            r0 = sb * _SUB
            rows = pl.ds(r0, _SUB)
            dzs = dz_ref[rows, :]
            dyg = jnp.zeros((_SUB, BW), f32)
            for k in range(CONV_K):
                ysl = _shifted(ybuf, r0 + HALO - (CONV_K - 1) + k, _SUB)
                dwacc[pl.ds(k * 8, 8), :] += jnp.sum((dzs * ysl).reshape(_SUB // 8, 8, BW), axis=0)
                dyg = dyg + w_ref[k:k + 1, :] * _shifted(dzbuf, r0 + (CONV_K - 1) - k, _SUB)
            av = a_ref[rows, :].astype(f32)
            sg = jax.nn.sigmoid(g_ref[rows, :].astype(f32))
            dp_ref[rows, 0:BW] = (dyg * sg).astype(bf16)
            dp_ref[rows, BW:2 * BW] = (dyg * av * sg * (1.0 - sg)).astype(bf16)

        @pl.when(i == nT - 1)
        def _():
            for k in range(CONV_K):
                dw_ref[k:k + 1, :] = jnp.sum(dwacc[pl.ds(k * 8, 8), :], axis=0, keepdims=True)

    nxt = lambda i: jnp.minimum((i + 1) * hb, S // HALO - 1)
    return pl.pallas_call(
        body, out_shape=(SDS((S, NP), bf16), SDS((CONV_K, BW), f32)), grid=(nT,),
        in_specs=_conv_specs(S, T) + [BS((T, BW), lambda i: (i, 0)), BS((HALO, BW), lambda i: (nxt(i), 0)),
                                      BS((CONV_K, BW), lambda i: (0, 0)), ANY],
        out_specs=(BS((T, 2 * BW), lambda i: (i, creg)), BS((CONV_K, BW), lambda i: (0, 0))),
        scratch_shapes=[pltpu.VMEM((8, T + HALO, BW), f32), pltpu.VMEM((8, T + HALO, BW), f32),
                        pltpu.VMEM((CONV_K * 8, BW), f32)],
        input_output_aliases={7: 0}, compiler_params=_cp("arbitrary"), name=name)(proj, proj, proj, proj, dz, dz, w, dproj)


def _toeplitz_index():
    j = lax.broadcasted_iota(jnp.int32, (REL_TABLE, 1024), 1)
    t = lax.broadcasted_iota(jnp.int32, (REL_TABLE, 1024), 0)
    e = ((WIN - 1) - j) & 1023
    tidx = jnp.clip(e - (TQ - 1), -(CHUNK - 1), 256) + (CHUNK - 1)
    return (tidx == t).astype(f32)


def att_bias_build(table, name):
    H = table.shape[0]

    def body(t_ref, o_ref):
        u = jnp.dot(t_ref[...], _toeplitz_index(), precision=HI, preferred_element_type=f32)
        row = lax.broadcasted_iota(jnp.int32, (TQ, 1024), 0)
        r = lax.broadcasted_iota(jnp.int32, (TQ, WIN), 0)
        n = lax.broadcasted_iota(jnp.int32, (TQ, WIN), 1)
        dchunk = (r // CHUNK + 8) - n // CHUNK
        band = (dchunk >= 0) & (dchunk <= 8)
        for h in range(H):
            x = jnp.broadcast_to(u[h:h + 1, :], (TQ, 1024))
            for b in range(8):
                x = jnp.where(((row >> b) & 1) == 1, pltpu.roll(x, 1 << b, 1), x)
            o_ref[h] = jnp.where(band, x[:, :WIN], NEG_INF)

    return pl.pallas_call(body, out_shape=SDS((H, TQ, WIN), f32), compiler_params=pltpu.CompilerParams(vmem_limit_bytes=VMEM_LIMIT),
                          name=name)(table)


def att_bias_grad(dbias, name):
    H = dbias.shape[0]

    def body(d_ref, o_ref):
        row = lax.broadcasted_iota(jnp.int32, (TQ, 1024), 0)
        rows = []
        for h in range(H):
            x = jnp.concatenate([d_ref[h], jnp.zeros((TQ, 1024 - WIN), f32)], axis=1)
            for b in range(8):
                x = jnp.where(((row >> b) & 1) == 1, pltpu.roll(x, 1024 - (1 << b), 1), x)
            rows.append(jnp.sum(x, axis=0, keepdims=True))
        du = jnp.concatenate(rows, axis=0)
        o_ref[...] = lax.dot_general(du, _toeplitz_index(), (((1,), (1,)), ((), ())), precision=HI,
                                     preferred_element_type=f32)

    return pl.pallas_call(body, out_shape=SDS((H, REL_TABLE), f32), compiler_params=pltpu.CompilerParams(vmem_limit_bytes=VMEM_LIMIT),
                          name=name)(dbias)


def _att_specs():
    cq, ck, cv = (OUR_COLS[n][0] // BW for n in ("a_q", "a_k", "a_v"))
    specs = [BS((TQ, BW), lambda i: (i, cq))]
    for col in (ck, cv):
        for back in (2, 1, 0):
            specs.append(BS((TQ, BW), functools.partial(lambda i, back, col: (jnp.maximum(i - back, 0), col), back=back, col=col)))
    return specs


def _att_pen(i):
    n = lax.broadcasted_iota(jnp.int32, (1, WIN), 1)
    return jnp.where(n + (i - 2) * TQ >= 0, 0.0, NEG_INF).astype(f32)


def _att_probs(qa, kp, bias_h, pen):
    s = lax.dot_general(qa, kp, (((1,), (1,)), ((), ())), preferred_element_type=f32) + bias_h + pen
    e = jnp.exp(s - jnp.max(s, axis=-1, keepdims=True))
    return e * (1.0 / jnp.sum(e, axis=-1, keepdims=True))


def att_fwd(proj, bias, y, name, gather=None):
    S = proj.shape[0]
    ng = len(gather) if gather else 0

    def body(q_ref, k2, k1, k0, v2, v1, v0, b_ref, y_in, *rest):
        del y_in
        g_in, o_ref, g_out = rest[:ng], rest[ng], rest[ng + 1:2 * ng + 1]
        i = pl.program_id(0)
        if ng:
            kwin, vwin, send_sems, recv_sems = rest[2 * ng + 1:]
            _ag4_over_grid(g_in, g_out, send_sems, recv_sems, i, S // TQ, AG_FORWARD_ATTENTION)
        else:
            kwin, vwin = rest[1:]
        for w, (kr, vr) in enumerate(((k2, v2), (k1, v1), (k0, v0))):
            kwin[pl.ds(w * TQ, TQ), :] = kr[...]
            vwin[pl.ds(w * TQ, TQ), :] = vr[...]
        pen = _att_pen(i)
        lo = lax.broadcasted_iota(jnp.int32, (TQ, 128), 1) < 64
        for hp in range(4):
            cols = slice(hp * 128, (hp + 1) * 128)
            qp, kp, vp = q_ref[:, cols] * jnp.asarray(0.125, bf16), kwin[:, cols], vwin[:, cols]
            outs = []
            for a in range(2):
                qa = jnp.where(lo if a == 0 else ~lo, qp, jnp.zeros_like(qp))
                p = _att_probs(qa, kp, b_ref[2 * hp + a], pen)
                outs.append(jnp.dot(p.astype(bf16), vp, preferred_element_type=f32))
            o_ref[:, cols] = jnp.where(lo, outs[0], outs[1]).astype(bf16)

    outs = pl.pallas_call(
        body, out_shape=(SDS(y.shape, bf16),) + tuple(_ag4_out_shapes(gather or [])), grid=(S // TQ,),
        in_specs=_att_specs() + [BS((8, TQ, WIN), lambda i: (0, 0, 0), pipeline_mode=pl.Buffered(1)), ANY] + [ANY] * ng,
        out_specs=(BS((TQ, BW), lambda i: (i, 2)),) + tuple(ANY for _ in range(ng)),
        scratch_shapes=[pltpu.VMEM((WIN, BW), bf16), pltpu.VMEM((WIN, BW), bf16)] + (_ag4_sems(ng) if ng else []),
        input_output_aliases={8: 0}, compiler_params=_cp("arbitrary" if ng else "parallel"), name=name)(
            proj, proj, proj, proj, proj, proj, proj, bias, y, *(gather or []))
    return (outs[0], list(outs[1:])) if ng else outs[0]


def att_bwd(proj, y, dy, bias, dproj, name):
    S = proj.shape[0]
    cq = OUR_COLS["a_q"][0] // BW

    def body(q_ref, k2, k1, k0, v2, v1, v0, b_ref, o_ref, do_ref, dp_in, dq_ref, dkp_ref, dvp_ref, db_ref, kwin, vwin):
        del dp_in
        i = pl.program_id(0)

        @pl.when(i == 0)
        def _():
            db_ref[...] = jnp.zeros_like(db_ref)

        for w, (kr, vr) in enumerate(((k2, v2), (k1, v1), (k0, v0))):
            kwin[pl.ds(w * TQ, TQ), :] = kr[...]
            vwin[pl.ds(w * TQ, TQ), :] = vr[...]
        pen = _att_pen(i)
        lo = lax.broadcasted_iota(jnp.int32, (TQ, 128), 1) < 64
        for hp in range(4):
            cols = slice(hp * 128, (hp + 1) * 128)
            qp, kp, vp = q_ref[:, cols] * jnp.asarray(0.125, bf16), kwin[:, cols], vwin[:, cols]
            dop, op = do_ref[:, cols], o_ref[:, cols]
            dqs = []
            dk = jnp.zeros((WIN, 128), f32)
            dv = jnp.zeros((WIN, 128), f32)
            for a in range(2):
                sel = lo if a == 0 else ~lo
                qa = jnp.where(sel, qp, jnp.zeros_like(qp))
                doa = jnp.where(sel, dop, jnp.zeros_like(dop))
                p = _att_probs(qa, kp, b_ref[2 * hp + a], pen)
                dpv = lax.dot_general(doa, vp, (((1,), (1,)), ((), ())), preferred_element_type=f32)
                delta = jnp.sum(doa.astype(f32) * op.astype(f32), axis=-1, keepdims=True)
                ds = p * (dpv - delta)
                db_ref[2 * hp + a] += ds
                dsb = ds.astype(bf16)
                dqs.append(jnp.dot(dsb, kp, preferred_element_type=f32))
                dk = dk + lax.dot_general(dsb, qa, (((0,), (0,)), ((), ())), preferred_element_type=f32)
                dv = dv + lax.dot_general(p.astype(bf16), doa, (((0,), (0,)), ((), ())), preferred_element_type=f32)
            dq_ref[:, cols] = (jnp.where(lo, dqs[0], dqs[1]) * 0.125).astype(bf16)
            for w in range(3):
                dkp_ref[w, :, cols] = dk[w * TQ:(w + 1) * TQ].astype(bf16)
                dvp_ref[w, :, cols] = dv[w * TQ:(w + 1) * TQ].astype(bf16)

    tok = BS((TQ, BW), lambda i: (i, 2))
    part = BS((3, TQ, BW), lambda i: (0, i, 0))
    full = BS((8, TQ, WIN), lambda i: (0, 0, 0))
    return pl.pallas_call(
        body, out_shape=(SDS((S, NP), bf16), SDS((3, S, BW), bf16), SDS((3, S, BW), bf16), SDS((8, TQ, WIN), f32)),
        grid=(S // TQ,),
        in_specs=_att_specs() + [BS((8, TQ, WIN), lambda i: (0, 0, 0), pipeline_mode=pl.Buffered(1)), tok, tok, ANY],
        out_specs=(BS((TQ, BW), lambda i: (i, cq)), part, part, full),
        scratch_shapes=[pltpu.VMEM((WIN, BW), bf16), pltpu.VMEM((WIN, BW), bf16)],
        input_output_aliases={10: 0}, compiler_params=_cp("arbitrary"), name=name)(
            proj, proj, proj, proj, proj, proj, proj, bias, y, dy, dproj)


def att_shift_add(dkp, dvp, dproj, name):
    S = dkp.shape[1]
    nT = S // TQ
    creg = OUR_COLS["a_k"][0] // (2 * BW)

    def body(k2, k1, k0, v2, v1, v0, dp_in, dp_ref):
        del dp_in
        j = pl.program_id(0)
        m1 = (j + 1 < nT).astype(f32)
        m0 = (j + 2 < nT).astype(f32)
        dp_ref[:, 0:BW] = (k2[0].astype(f32) + m1 * k1[0].astype(f32) + m0 * k0[0].astype(f32)).astype(bf16)
        dp_ref[:, BW:2 * BW] = (v2[0].astype(f32) + m1 * v1[0].astype(f32) + m0 * v0[0].astype(f32)).astype(bf16)

    def spec(w):
        return BS((1, TQ, BW), functools.partial(lambda j, w: (w, jnp.minimum(j + 2 - w, nT - 1), 0), w=w))

    return pl.pallas_call(
        body, out_shape=SDS(dproj.shape, bf16), grid=(nT,),
        in_specs=[spec(2), spec(1), spec(0), spec(2), spec(1), spec(0), ANY],
        out_specs=BS((TQ, 2 * BW), lambda j: (j, creg)),
        input_output_aliases={6: 0}, compiler_params=_cp("parallel"), name=name)(dkp, dkp, dkp, dvp, dvp, dvp, dproj)


GQ, GV = 256, 512
TGC = 8


def _bd_mask():
    r = lax.broadcasted_iota(jnp.int32, (GQ, GV), 0) // 64
    c = lax.broadcasted_iota(jnp.int32, (GQ, GV), 1) // 128
    return (r == c).astype(f32)


def _tri(strict):
    r = lax.broadcasted_iota(jnp.int32, (CHUNK, CHUNK), 0)
    c = lax.broadcasted_iota(jnp.int32, (CHUNK, CHUNK), 1)
    return ((c < r) if strict else (c <= r)).astype(f32)


def _compact(s_bd):
    return jnp.concatenate([s_bd[h * 64:(h + 1) * 64, h * 128:(h + 1) * 128] for h in range(4)], axis=0)


def _expand(comp, mask):
    return jnp.tile(comp, (1, 4)) * mask


def _gla_gates(alr, wa_ref, ba_ref, tri_incl, ones_col):
    a = jnp.dot(alr, wa_ref[...], preferred_element_type=f32) + ba_ref[...]
    la = (jnp.minimum(a, 0.0) - jnp.log(1.0 + jnp.exp(-jnp.abs(a)))) * (1.0 / 16.0)
    cum = jnp.dot(tri_incl, la, precision=HI, preferred_element_type=f32)
    tot_row = cum[CHUNK - 1:CHUNK, :]
    tot_col = lax.dot_general(la, ones_col, (((0,), (0,)), ((), ())), precision=HI, preferred_element_type=f32)
    return a, cum, tot_row, jnp.tile(jnp.exp(tot_col), (1, 4))


def _head_norm(o):
    rns, ons = [], []
    for h in range(4):
        oh = o[:, h * 128:(h + 1) * 128]
        rn = lax.rsqrt(jnp.mean(oh * oh, axis=-1, keepdims=True) + EPS)
        rns.append(rn)
        ons.append(oh * rn)
    return rns, ons


def _gla_in_specs(T, imap):
    cq, ck = OUR_COLS["g_q"][0] // GQ, OUR_COLS["g_k"][0] // GQ
    cv, cr = OUR_COLS["g_v"][0] // GV, OUR_COLS["g_r"][0] // GV
    return [BS((T, GQ), lambda i: (imap(i), cq)), BS((T, GQ), lambda i: (imap(i), ck)), BS((T, GV), lambda i: (imap(i), cv)),
            BS((T, GV), lambda i: (imap(i), cr)), BS((T, RANKP), lambda i: (imap(i), 0))]


def gla_fwd(proj, pa, wa, ba, ng, y, name):
    S = proj.shape[0]
    T = min(TGC * CHUNK, S)
    nch = T // CHUNK

    def body(q_ref, k_ref, v_ref, r_ref, a_ref, wa_ref, ba_ref, ng_ref, y_in, y_ref, st_ref, s_scr):
        del y_in

        @pl.when(pl.program_id(0) == 0)
        def _():
            s_scr[...] = jnp.zeros_like(s_scr)

        mask = _bd_mask()
        tri = _tri(False)
        ones_col = jnp.ones((CHUNK, 128), f32)

        s_bd = s_scr[...]
        for ci in range(nch):
            rows = pl.ds(ci * CHUNK, CHUNK)
            _, cum, tot_row, dec4 = _gla_gates(a_ref[rows, :], wa_ref, ba_ref, tri, ones_col)
            kd = (k_ref[rows, :].astype(f32) * jnp.exp(tot_row - cum)).astype(bf16)
            upd = lax.dot_general(kd, v_ref[rows, :], (((0,), (0,)), ((), ())), preferred_element_type=f32) * mask
            s_bd = dec4 * s_bd + upd
            st_ref[pl.ds(ci * GQ, GQ), :] = _compact(s_bd)
            qs = (q_ref[rows, :].astype(f32) * 0.125).astype(bf16)
            o = jnp.dot(qs, s_bd.astype(bf16), preferred_element_type=f32)
            _, ons = _head_norm(o)
            rv = r_ref[rows, :].astype(f32)
            y_ref[rows, :] = (jnp.concatenate(ons, axis=1) * ng_ref[...] * (rv * jax.nn.sigmoid(rv))).astype(bf16)
        s_scr[...] = s_bd

    return pl.pallas_call(
        body, out_shape=(SDS(y.shape, bf16), SDS((S // CHUNK * GQ, 128), f32)), grid=(S // T,),
        in_specs=_gla_in_specs(T, lambda i: i) + [BS((RANKP, GQ), lambda i: (0, 0)), BS((1, GQ), lambda i: (0, 0)),
                                                  BS((1, GV), lambda i: (0, 0)), ANY],
        out_specs=(BS((T, GV), lambda i: (i, 1)), BS((nch * GQ, 128), lambda i: (i, 0))),
        scratch_shapes=[pltpu.VMEM((GQ, GV), f32)], input_output_aliases={8: 0}, compiler_params=_cp("arbitrary"),
        name=name)(proj, proj, proj, proj, pa, wa, ba, ng, y)


def gla_bwd(proj, pa, states, dy, wa, ba, ng, dproj, name):
    S = proj.shape[0]
    T = min(TGC * CHUNK, S)
    nch = T // CHUNK
    nT = S // T
    rev = lambda i: nT - 1 - i

    def body(q_ref, k_ref, v_ref, r_ref, a_ref, st_ref, sp_ref, dy_ref, wa_ref, ba_ref, ng_ref, dp_in,
             dp_ref, da_ref, dwa_ref, dba_ref, dng_ref, g_scr):
        del dp_in
        i = pl.program_id(0)

        @pl.when(i == 0)
        def _():
            g_scr[...] = jnp.zeros_like(g_scr)
            dwa_ref[...] = jnp.zeros_like(dwa_ref)
            dba_ref[...] = jnp.zeros_like(dba_ref)
            dng_ref[...] = jnp.zeros_like(dng_ref)

        mask = _bd_mask()
        tri = _tri(False)
        tri_strict = _tri(True)
        ones_col = jnp.ones((CHUNK, 128), f32)
        ones_row = jnp.ones((8, 128), f32)
        first_tile = (i == nT - 1).astype(f32)

        g_carry = g_scr[...]
        for ci in reversed(range(nch)):
            rows = pl.ds(ci * CHUNK, CHUNK)
            alr = a_ref[rows, :]
            a, cum, tot_row, dec4 = _gla_gates(alr, wa_ref, ba_ref, tri, ones_col)
            wdec = jnp.exp(tot_row - cum)
            kdf = k_ref[rows, :].astype(f32) * wdec
            kd = kdf.astype(bf16)
            s_c = _expand(st_ref[pl.ds(ci * GQ, GQ), :], mask)
            prev = st_ref[pl.ds((ci - 1) * GQ, GQ), :] if ci > 0 else sp_ref[...] * (1.0 - first_tile)
            qs = (q_ref[rows, :].astype(f32) * 0.125).astype(bf16)
            s_cb = s_c.astype(bf16)
            o = jnp.dot(qs, s_cb, preferred_element_type=f32)
            rns, ons = _head_norm(o)
            on = jnp.concatenate(ons, axis=1)
            rv = r_ref[rows, :].astype(f32)
            sg = jax.nn.sigmoid(rv)
            sr = rv * sg
            dyv = dy_ref[rows, :].astype(f32)
            ngv = ng_ref[...]
            dng_ref[...] += jnp.sum(dyv * on * sr, axis=0, keepdims=True)
            d_on = dyv * ngv * sr
            dr = dyv * on * ngv * (sg * (1.0 + rv * (1.0 - sg)))
            dos = []
            for h in range(4):
                cols = slice(h * 128, (h + 1) * 128)
                dh_ = d_on[:, cols]
                dos.append(rns[h] * (dh_ - ons[h] * jnp.mean(dh_ * ons[h], axis=-1, keepdims=True)))
            do = jnp.concatenate(dos, axis=1).astype(bf16)
            dq = lax.dot_general(do, s_cb, (((1,), (1,)), ((), ())), preferred_element_type=f32) * 0.125
            ds = lax.dot_general(qs, do, (((0,), (0,)), ((), ())), preferred_element_type=f32) * mask + g_carry
            ddec_row = lax.dot_general(ones_row, _compact(ds) * prev, (((1,), (1,)), ((), ())), precision=HI,
                                       preferred_element_type=f32)[0:1, :]
            dsb = ds.astype(bf16)
            dkd = lax.dot_general(v_ref[rows, :], dsb, (((1,), (1,)), ((), ())), preferred_element_type=f32)
            dv = jnp.dot(kd, dsb, preferred_element_type=f32)
            g_carry = dec4 * ds
            dk = dkd * wdec
            dwlog = dkd * kdf
            dla = ddec_row * jnp.exp(tot_row) + jnp.dot(tri_strict, dwlog, precision=HI, preferred_element_type=f32)
            da = dla * (1.0 - jax.nn.sigmoid(a)) * (1.0 / 16.0)
            dab = da.astype(bf16)
            da_ref[rows, :] = lax.dot_general(dab, wa_ref[...], (((1,), (1,)), ((), ())),
                                              preferred_element_type=f32).astype(bf16)
            dwa_ref[...] += lax.dot_general(alr, dab, (((0,), (0,)), ((), ())), preferred_element_type=f32)
            dba_ref[...] += jnp.sum(da, axis=0, keepdims=True)
            dp_ref[rows, 0:GQ] = dq.astype(bf16)
            dp_ref[rows, GQ:2 * GQ] = dk.astype(bf16)
            dp_ref[rows, 2 * GQ:2 * GQ + GV] = dv.astype(bf16)
            dp_ref[rows, 2 * GQ + GV:2 * GQ + 2 * GV] = dr.astype(bf16)
        g_scr[...] = g_carry

    REG = 2 * GQ + 2 * GV
    return pl.pallas_call(
        body,
        out_shape=(SDS((S, NP), bf16), SDS((S, RANKP), bf16), SDS((RANKP, GQ), f32), SDS((1, GQ), f32), SDS((1, GV), f32)),
        grid=(nT,),
        in_specs=_gla_in_specs(T, rev) + [
            BS((nch * GQ, 128), lambda i: (rev(i), 0)),
            BS((GQ, 128), lambda i: (jnp.maximum(rev(i) * nch - 1, 0), 0)),
            BS((T, GV), lambda i: (rev(i), 1)),
            BS((RANKP, GQ), lambda i: (0, 0)), BS((1, GQ), lambda i: (0, 0)), BS((1, GV), lambda i: (0, 0)), ANY],
        out_specs=(BS((T, REG), lambda i: (rev(i), 0)), BS((T, RANKP), lambda i: (rev(i), 0)),
                   BS((RANKP, GQ), lambda i: (0, 0)), BS((1, GQ), lambda i: (0, 0)), BS((1, GV), lambda i: (0, 0))),
        scratch_shapes=[pltpu.VMEM((GQ, GV), f32)],
        input_output_aliases={11: 0}, compiler_params=_cp("arbitrary"), name=name)(
            proj, proj, proj, proj, pa, states, states, dy, wa, ba, ng, dproj)


def _as2d(a):
    if a.ndim == 1:
        return a.reshape(1, a.shape[0])
    return a.reshape(-1, a.shape[-1])


def adamw(w, g, m, v, name):
    shape = w.shape
    w2, g2, m2, v2 = (_as2d(a) for a in (w, g, m, v))
    R, C = w2.shape
    tr = R
    for cand in (512, 256, 128, 64, 32, 16, 8):
        if R % cand == 0 and cand * C * 4 * 7 * 2 <= 40 * 1024 * 1024:
            tr = cand
            break

    def body(w_ref, g_ref, m_ref, v_ref, d_ref, mo_ref, vo_ref):
        gv = g_ref[...]
        mn = ADAM_B1 * m_ref[...] + (1.0 - ADAM_B1) * gv
        vn = ADAM_B2 * v_ref[...] + (1.0 - ADAM_B2) * (gv * gv)
        m_hat = mn / (1.0 - ADAM_B1 ** ADAM_STEP)
        v_hat = vn / (1.0 - ADAM_B2 ** ADAM_STEP)
        d_ref[...] = -ADAM_LR * (m_hat / (jnp.sqrt(v_hat) + ADAM_EPS) + ADAM_WD * w_ref[...])
        mo_ref[...] = mn
        vo_ref[...] = vn

    blk = BS((tr, C), lambda i: (i, 0))
    outs = pl.pallas_call(body, out_shape=tuple(SDS((R, C), f32) for _ in range(3)), grid=(R // tr,),
                          in_specs=[blk] * 4, out_specs=(blk,) * 3, compiler_params=_cp("parallel"), name=name)(w2, g2, m2, v2)
    return tuple(o.reshape(shape) for o in outs)


def _row_tile(rows, row_bytes, budget=4 * 1024 * 1024):
    best = None
    for t in range(16, rows + 1, 16):
        if rows % t == 0 and t * row_bytes <= budget:
            best = t
    return best or rows


def add_halves(g0, g1, ra, c, name):
    shape = ra.shape
    cols = shape[-1]
    rows = int(np.prod(shape[:-1]))
    tr = _row_tile(rows, cols * 2)
    blk = lambda: BS((tr, cols), lambda i, c_ref: (i, 0))
    grid_spec = pltpu.PrefetchScalarGridSpec(num_scalar_prefetch=1, grid=(rows // tr,), in_specs=[blk(), blk(), blk()],
                                             out_specs=blk())

    def body(c_ref, a0_ref, a1_ref, b_ref, o_ref):
        mine = jnp.where(c_ref[0] == 0, a0_ref[...], a1_ref[...])
        o_ref[...] = (mine.astype(f32) + b_ref[...].astype(f32)).astype(bf16)

    out = pl.pallas_call(body, out_shape=SDS((rows, cols), bf16), grid_spec=grid_spec, compiler_params=_cp("parallel"),
                         name=name)(jnp.reshape(c, (1,)).astype(jnp.int32), g0.reshape(rows, cols), g1.reshape(rows, cols),
                                    ra.reshape(rows, cols))
    return out.reshape(shape)


def reduce_chips(rb, own, c, chip, name):
    shape = rb.shape[1:]
    cols = shape[-1]
    rows = int(np.prod(shape[:-1]))
    tr = _row_tile(rows, cols * 2 * 4)
    rb3, own3 = rb.reshape(4, rows, cols), own.reshape(4, rows, cols)

    def body(s_ref, own_ref, r1, r2, r3, o_ref):
        del s_ref
        o_ref[0] = ((own_ref[0].astype(f32) + r1[0].astype(f32)) + r2[0].astype(f32)) + r3[0].astype(f32)

    def slot(k):
        return BS((1, tr, cols), functools.partial(lambda i, s, k: ((s[1] + k) % 4, i, 0), k=k))

    grid_spec = pltpu.PrefetchScalarGridSpec(
        num_scalar_prefetch=1, grid=(rows // tr,), in_specs=[slot(0), slot(1), slot(2), slot(3)],
        out_specs=BS((1, tr, cols), lambda i, s: (s[0], i, 0)))
    out = pl.pallas_call(body, out_shape=SDS((2, rows, cols), f32), grid_spec=grid_spec, compiler_params=_cp("parallel"),
                         name=name)(jnp.stack([c, chip]).astype(jnp.int32), own3, rb3, rb3, rb3)
    return out.reshape((2,) + shape)


def sum_slots(x, name):
    N, shape = x.shape[0], x.shape[1:]
    cols = shape[-1]
    rows = int(np.prod(shape[:-1]))
    tr = _row_tile(rows, cols * x.dtype.itemsize * N)

    def body(x_ref, o_ref):
        acc = x_ref[0].astype(f32)
        for n in range(1, N):
            acc = acc + x_ref[n].astype(f32)
        o_ref[...] = acc

    out = pl.pallas_call(body, out_shape=SDS((rows, cols), f32), grid=(rows // tr,),
                         in_specs=[BS((N, tr, cols), lambda i: (0, i, 0))], out_specs=BS((tr, cols), lambda i: (i, 0)),
                         compiler_params=_cp("parallel"), name=name)(x.reshape(N, rows, cols))
    return out.reshape(shape)


def _me():
    return lax.axis_index("x"), lax.axis_index("y"), lax.axis_index("c")


def _rcopy(src, dst, send_sems, recv_sems, k, dev):
    return pltpu.make_async_remote_copy(src_ref=src, dst_ref=dst, send_sem=send_sems.at[k], recv_sem=recv_sems.at[k],
                                        device_id=dev, device_id_type=MESH)


def _comm_call(body, ins, out_shapes, n_remote, name, aliases=None):
    return pl.pallas_call(
        body, out_shape=tuple(out_shapes), in_specs=[ANY] * len(ins), out_specs=tuple(ANY for _ in out_shapes),
        scratch_shapes=[pltpu.SemaphoreType.DMA((n_remote,)), pltpu.SemaphoreType.DMA((n_remote,))],
        input_output_aliases=aliases or {}, name=name)(*ins)


def _ag4_out_shapes(bufs):
    return [SDS((2, 4) + b.shape[1:], b.dtype) for b in bufs]


def _ag4_sems(n):
    return [pltpu.SemaphoreType.DMA((8 * n,)), pltpu.SemaphoreType.DMA((8 * n,))]


def _ag4_phases(xs, os, send_sems, recv_sems):
    n = len(xs)

    def place():
        x, y, c = _me()
        return x, y, c, 2 * x + y, (x, y, 1 - c), [(1 - x, y), (x, 1 - y), (1 - x, 1 - y)]

    def sends():
        x, y, c, j, sib, chips = place()
        first = [_rcopy(xs[t].at[c], os[t].at[c, j], send_sems, recv_sems, 8 * t + k, (cx, cy, c))
                 for t in range(n) for k, (cx, cy) in enumerate(chips)]
        own = [_rcopy(xs[t].at[l], os[t].at[l, j], send_sems, recv_sems, 8 * t + 6 + l, sib) for t in range(n) for l in range(2)]
        return first + own

    def forwards():
        x, y, c, j, sib, chips = place()
        return [(_rcopy(os[t].at[c, 2 * cx + cy], os[t].at[c, 2 * cx + cy], send_sems, recv_sems, 8 * t + k, (x, y, c)),
                 _rcopy(os[t].at[c, 2 * cx + cy], os[t].at[c, 2 * cx + cy], send_sems, recv_sems, 8 * t + 3 + k, sib))
                for k, (cx, cy) in enumerate(chips) for t in range(n)]

    def start():
        for cp in sends():
            cp.start()

    def forward():
        for landed, fwd in forwards():
            landed.wait_recv()
            fwd.start()

    def finish():
        x, y, c, j, sib, chips = place()
        for t in range(n):
            for l in range(2):
                land = os[t].at[l, j]
                _rcopy(land, land, send_sems, recv_sems, 8 * t + 6 + l, (x, y, c)).wait_recv()
        for k, (cx, cy) in enumerate(chips):
            for t in range(n):
                land = os[t].at[1 - c, 2 * cx + cy]
                _rcopy(land, land, send_sems, recv_sems, 8 * t + 3 + k, (x, y, c)).wait_recv()
        for cp in sends() + [fwd for _, fwd in forwards()]:
            cp.wait_send()

    return start, forward, finish


def _ag4_over_grid(xs, os, send_sems, recv_sems, step, nsteps, forward_frac):
    start, forward, finish = _ag4_phases(xs, os, send_sems, recv_sems)
    pl.when(step == 0)(start)
    pl.when(step == min(nsteps - 1, int(nsteps * forward_frac)))(forward)
    pl.when(step == nsteps - 1)(finish)


def sib_other_layer(g0s, g1s, name):
    n = len(g0s)

    def body(*refs):
        layers, os = (refs[:n], refs[n:2 * n]), refs[2 * n:3 * n]
        send_sems, recv_sems = refs[3 * n:]
        x, y, c = _me()
        for mine in range(2):
            @pl.when(c == mine)
            def _():
                cps = [_rcopy(layers[1 - mine][t], os[t], send_sems, recv_sems, t, (x, y, 1 - c)) for t in range(n)]
                for cp in cps:
                    cp.start()
                for cp in cps:
                    cp.wait()

    return _comm_call(body, list(g0s) + list(g1s), [SDS(g.shape, g.dtype) for g in g0s], n, name)


def a2a4(ps, name):
    n = len(ps)

    def body(*refs):
        xs, os = refs[:n], refs[n:2 * n]
        send_sems, recv_sems = refs[2 * n:]
        x, y, c = _me()
        j = 2 * x + y
        chips = [(1 - x, y), (x, 1 - y), (1 - x, 1 - y)]
        sends = [_rcopy(xs[t].at[2 * cx + cy], os[t].at[j], send_sems, recv_sems, 3 * t + k, (cx, cy, c))
                 for t in range(n) for k, (cx, cy) in enumerate(chips)]
        for cp in sends:
            cp.start()
        for t in range(n):
            for k, (cx, cy) in enumerate(chips):
                land = os[t].at[2 * cx + cy]
                _rcopy(land, land, send_sems, recv_sems, 3 * t + k, (x, y, c)).wait_recv()
        for cp in sends:
            cp.wait_send()

    return _comm_call(body, ps, [SDS(p.shape, p.dtype) for p in ps], 3 * n, name)


def ag2(bufs, name):
    n = len(bufs)

    def body(*refs):
        xs, os = refs[:n], refs[n:2 * n]
        send_sems, recv_sems = refs[2 * n:]
        x, y, c = _me()
        cps = [_rcopy(xs[t].at[c], os[t].at[c], send_sems, recv_sems, t, (x, y, 1 - c)) for t in range(n)]
        for cp in cps:
            cp.start()
        for t in range(n):
            land = os[t].at[1 - c]
            _rcopy(land, land, send_sems, recv_sems, t, (x, y, c)).wait_recv()
        for cp in cps:
            cp.wait_send()

    return _comm_call(body, bufs, [SDS(b.shape, b.dtype) for b in bufs], n, name, aliases={t: t for t in range(n)})


def ag8(blk, name):
    m_per, n = blk.shape

    def body(x_ref, out_ref, send_sems, recv_sems, local_sem):
        x, y, c = _me()
        me, sibling = (x, y, c), (x, y, 1 - c)
        chips = [(1 - x, y), (x, 1 - y), (1 - x, 1 - y)]

        def rows(px, py, pc):
            return out_ref.at[pl.ds((4 * px + 2 * py + pc) * m_per, m_per), :]

        def copy(k, block, to, src=None):
            return pltpu.make_async_remote_copy(
                src_ref=rows(*block) if src is None else src, dst_ref=rows(*block), send_sem=send_sems.at[k],
                recv_sem=recv_sems.at[k], device_id=to, device_id_type=MESH)

        mine = pltpu.make_async_copy(x_ref, rows(*me), local_sem)
        mine.start()
        first = [copy(0, me, sibling, src=x_ref)]
        first += [copy(1 + j, me, (*chip, c), src=x_ref) for j, chip in enumerate(chips)]
        for cp in first:
            cp.start()
        passed = [copy(4 + j, (*chip, c), sibling) for j, chip in enumerate(chips)]
        for j, chip in enumerate(chips):
            copy(1 + j, (*chip, c), me).wait_recv()
            passed[j].start()
        copy(0, sibling, me).wait_recv()
        for j, chip in enumerate(chips):
            copy(4 + j, (*chip, 1 - c), me).wait_recv()
        for cp in first + passed:
            cp.wait_send()
        mine.wait()

    return pl.pallas_call(
        body, out_shape=SDS((8 * m_per, n), blk.dtype), in_specs=[pl.BlockSpec(memory_space=pltpu.VMEM)],
        out_specs=pl.BlockSpec(memory_space=pltpu.VMEM),
        scratch_shapes=[pltpu.SemaphoreType.DMA((7,)), pltpu.SemaphoreType.DMA((7,)), pltpu.SemaphoreType.DMA],
        name=name)(blk)


def _split_chips(full, axis):
    n = full.shape[axis] // 4
    parts = full.reshape(full.shape[:axis] + (4, n) + full.shape[axis + 1:])
    return jnp.moveaxis(parts, axis, 0)


def _merge_chips(gathered, axis):
    parts = jnp.moveaxis(gathered, 0, axis)
    return parts.reshape(parts.shape[:axis] + (parts.shape[axis] * parts.shape[axis + 1],) + parts.shape[axis + 2:])


def _to_ref_cols(main, rank):
    pieces = []
    for n, width in REF_SPLITS:
        if n == "g_a":
            pieces.append(rank[..., :RANK])
        else:
            off = OUR_COLS[n][0]
            pieces.append(main[..., off:off + width])
    return jnp.concatenate(pieces, axis=-1)


def _from_ref_cols(w):
    offs, o = {}, 0
    for n, width in REF_SPLITS:
        offs[n] = (o, width)
        o += width
    main = jnp.concatenate([w[..., offs[n][0]:offs[n][0] + offs[n][1]] for n in sorted(OUR_COLS, key=lambda k: OUR_COLS[k][0])],
                           axis=-1)
    ro = offs["g_a"][0]
    rank = jnp.pad(w[..., ro:ro + RANK], [(0, 0)] * (w.ndim - 1) + [(0, RANKP - RANK)])
    return main, rank


def _layer_fwd(h, p_i, W, li, late=None, xn=None):
    t = f"l{li}_"
    sv = {"h0": h}

    def arrived(names, gathered, Ws):
        for l, Wl in enumerate(Ws):
            Wl.update(_prep_layer_weights(dict(zip(names, gathered)), None, l))

    if xn is None:
        xn = rms_fwd(h, W["norm1_g"], t + "rms1")
    if late is None:
        proj = mm_nn(xn, W["w_in_main"], name=t + "inproj")
    else:
        (names, shards, Ws) = late[1]
        proj, gathered = mm_nn(xn, W["w_in_main"], name=t + "inproj", gather=shards)
        arrived(names, gathered, Ws)
    pa = mm_nn(xn, W["w_in_rank"], name=t + "inproj_rank")
    y = sg_fwd(proj, W["sg_ln_g"], W["sg_ln_b"], W["sg_wm"], W["sg_bsb"], t + "sg_fwd")
    y, states = gla_fwd(proj, pa, W["gla_wa"], W["gla_b_a"], W["gla_norm_g"], y, t + "gla_fwd")
    if late is None:
        y = att_fwd(proj, W["att_bias"], y, t + "att_fwd")
    else:
        (names, shards, Ws) = late[0]
        y, gathered = att_fwd(proj, W["att_bias"], y, t + "att_fwd", gather=shards)
        arrived(names, gathered, Ws)
    y = conv_fwd(proj, W["conv_dw_w"], W["conv_dw_b"], W["conv_ln_g"], W["conv_ln_b"], y, t + "conv_fwd")
    gate = mm_nn(xn, W["w_gate_all"], bias=W["b_gate_all"], act="sigmoid", name=t + "gate")
    z = mm_nn(y, W["w_branch"], name=t + "branch")
    m = gate_merge_fwd(gate, z, t + "merge")
    h1 = mm_nn(m, W["w_out"], res=h, out_dtype=f32, name=t + "outproj")
    hn = rms_fwd(h1, W["norm2_g"], t + "rms2")
    a = mm_nn(hn, W["w_ff1"], name=t + "ff1")
    h2 = mm_nn(a, W["w_ff2"], pre="relu2", res=h1, out_dtype=f32, name=t + "ff2")
    hg = rms_fwd(h2, W["norm3_g"], t + "rms3")
    pg = mm_nn(hg, W["w_ple_gate"], bias=W["b_ple_gate"], act="sigmoid", name=t + "ple_gate")
    h3, e = mm_nn(p_i, W["w_ple"], mul=pg, res=h2, out_dtype=f32, raw_out=True, name=t + "ple_out")
    sv.update(xn=xn, proj=proj, pa=pa, states=states, y=y, gate=gate, z=z, m=m, h1=h1, hn=hn, a=a, h2=h2, hg=hg, pg=pg, e=e)
    return h3, sv


def _layer_bwd(dh3, sv, p_i, W, li):
    t = f"l{li}_b_"
    G = {}
    dpg, de, G["b_ple_gate"] = ple_bwd_ew(dh3, sv["e"], sv["pg"], t + "ple_ew")
    G["w_ple_gate"] = mm_tn(sv["hg"], dpg, name=t + "dw_ple_gate")[0]
    G["w_ple"] = mm_tn(p_i, de, name=t + "dw_ple")[0]
    dhg = mm_nt(dpg, W["w_ple_gate"], name=t + "dhg")
    dh2, G["norm3_g"] = rms_bwd(dhg, sv["h2"], W["norm3_g"], dh3, t + "rms3")
    da = mm_nt(dh2, W["w_ff2"], post_a=sv["a"], out_dtype=bf16, name=t + "da")
    G["w_ff2"] = mm_tn(sv["a"], dh2, pre="relu2", name=t + "dw_ff2")[0]
    G["w_ff1"] = mm_tn(sv["hn"], da, name=t + "dw_ff1")[0]
    dhn = mm_nt(da, W["w_ff1"], name=t + "dhn")
    dh1, G["norm2_g"] = rms_bwd(dhn, sv["h1"], W["norm2_g"], dh2, t + "rms2")
    dm = mm_nt(dh1, W["w_out"], out_dtype=bf16, name=t + "dm")
    G["w_out"] = mm_tn(sv["m"], dh1, name=t + "dw_out")[0]
    dz, dgp, G["b_gate_all"] = gate_merge_bwd(dm, sv["gate"], sv["z"], t + "merge")
    G["w_branch"] = mm_tn(sv["y"], dz, G=4, name=t + "dw_branch")
    dy = mm_nt(dz, W["w_branch"], out_dtype=bf16, name=t + "dy")
    G["w_gate_all"] = mm_tn(sv["xn"], dgp, name=t + "dw_gate")[0]
    dxn = mm_nt(dgp, W["w_gate_all"], name=t + "dxn_gate")
    proj = sv["proj"]
    dproj, dwm, dbs, G["sg_ln_g"], G["sg_ln_b"] = sg_bwd(proj, dy, W["sg_ln_g"], W["sg_ln_b"], W["sg_wm"], W["sg_bsb"],
                                                          W["sg_maskf"], t + "sg")
    G["sg_w"], G["sg_b"] = dwm, dbs[:, :, 0]
    dproj, dpa, dwa, G["gla_b_a"], G["gla_norm_g"] = gla_bwd(proj, sv["pa"], sv["states"], dy, W["gla_wa"], W["gla_b_a"],
                                                             W["gla_norm_g"], dproj, t + "gla")
    G["gla_w_a2"] = dwa[:RANK]
    dproj, dkp, dvp, dbias = att_bwd(proj, sv["y"], dy, W["att_bias"], dproj, t + "att")
    dproj = att_shift_add(dkp, dvp, dproj, t + "att_kv")
    G["att_rel_bias"] = att_bias_grad(dbias, t + "att_bias")
    dz_c, G["conv_ln_g"], G["conv_ln_b"], G["conv_dw_b"] = conv_bwd_norm(proj, dy, W["conv_dw_w"], W["conv_dw_b"],
                                                                        W["conv_ln_g"], W["conv_ln_b"], t + "conv_norm")
    dproj, G["conv_dw_w"] = conv_bwd_taps(proj, dz_c, W["conv_dw_w"], dproj, t + "conv_taps")
    G["w_in_main"] = mm_tn(sv["xn"], dproj, name=t + "dw_in")[0]
    G["w_in_rank"] = mm_tn(sv["xn"], dpa, name=t + "dw_in_rank")[0]
    dxn = mm_nt(dpa, W["w_in_rank"], res=dxn, name=t + "dxn_rank")
    dxn = mm_nt(dproj, W["w_in_main"], res=dxn, name=t + "dxn_main")
    dh0, G["norm1_g"] = rms_bwd(dxn, sv["h0"], W["norm1_g"], dh1, t + "rms1")
    return dh0, G


def _prep_layer_weights(gathered, repl, li):
    W = {}
    full = {n: _merge_chips(g[li], SHARDED[n][1]) for n, g in gathered.items()}
    if "w_in" in full:
        main, rank = _from_ref_cols(full["w_in"])
        W["w_in_main"], W["w_in_rank"] = main[None], rank[None]
    if "w_branch" in full:
        W["w_branch"] = full["w_branch"]
    if "w_gate" in full:
        W["w_gate_all"] = jnp.transpose(full["w_gate"], (1, 0, 2)).reshape(1, D, 4 * D)
    if "b_gate" in full:
        W["b_gate_all"] = full["b_gate"].reshape(1, 4 * D)
    for n in ("w_out", "w_ff1", "w_ff2", "w_ple_gate", "w_ple"):
        if n in full:
            W[n] = full[n][None]
    if "gla_w_a2" in full:
        W["gla_wa"] = jnp.pad(full["gla_w_a2"], ((0, RANKP - RANK), (0, 0))).astype(bf16)
    if "att_rel_bias" in full:
        W["att_bias"] = att_bias_build(full["att_rel_bias"], f"l{li}_att_bias")
    if "conv_dw_w" in full:
        W["conv_dw_w"] = full["conv_dw_w"]
    if repl is not None:
        for n in ("norm1_g", "norm2_g", "norm3_g", "b_ple_gate", "sg_ln_g", "sg_ln_b", "gla_b_a", "gla_norm_g", "conv_dw_b",
                  "conv_ln_g", "conv_ln_b"):
            W[n] = repl[n][li].reshape(1, -1)
        pos = np.arange(128)
        mask = (pos[None, :] // CHUNK) <= (pos[:, None] // CHUNK)
        W["sg_maskf"] = jnp.asarray(mask, f32)
        W["sg_wm"] = jnp.where(mask[None], repl["sg_w"][li], 0.0).astype(bf16)
        W["sg_bsb"] = jnp.broadcast_to(repl["sg_b"][li][:, :, None], (4, 128, 128))
    return W


def _layer_grads_to_ref(G):
    out = {}
    out["w_in"] = _to_ref_cols(G["w_in_main"], G["w_in_rank"])
    out["w_gate"] = jnp.transpose(G["w_gate_all"].reshape(D, 4, D), (1, 0, 2))
    out["b_gate"] = G["b_gate_all"].reshape(4, D)
    for n in ("w_branch", "w_out", "w_ff1", "w_ff2", "w_ple_gate", "w_ple", "gla_w_a2", "att_rel_bias", "conv_dw_w", "sg_w",
              "sg_b"):
        out[n] = G[n]
    for n in ("norm1_g", "norm2_g", "norm3_g", "b_ple_gate", "sg_ln_g", "sg_ln_b", "gla_b_a", "gla_norm_g", "conv_dw_b",
              "conv_ln_g", "conv_ln_b"):
        out[n] = G[n].reshape(-1)
    return out


def kernel(x, p, norm1_g, w_in, sg_ln_g, sg_ln_b, sg_w, sg_b, gla_w_a2, gla_b_a, gla_norm_g, att_rel_bias, conv_dw_w, conv_dw_b, conv_ln_g, conv_ln_b, w_branch, w_gate, b_gate, w_out, norm2_g, w_ff1, w_ff2, norm3_g, w_ple_gate, b_ple_gate, w_ple, final_g, loss_target, m_norm1_g, m_w_in, m_sg_ln_g, m_sg_ln_b, m_sg_w, m_sg_b, m_gla_w_a2, m_gla_b_a, m_gla_norm_g, m_att_rel_bias, m_conv_dw_w, m_conv_dw_b, m_conv_ln_g, m_conv_ln_b, m_w_branch, m_w_gate, m_b_gate, m_w_out, m_norm2_g, m_w_ff1, m_w_ff2, m_norm3_g, m_w_ple_gate, m_b_ple_gate, m_w_ple, m_final_g, v_norm1_g, v_w_in, v_sg_ln_g, v_sg_ln_b, v_sg_w, v_sg_b, v_gla_w_a2, v_gla_b_a, v_gla_norm_g, v_att_rel_bias, v_conv_dw_w, v_conv_dw_b, v_conv_ln_g, v_conv_ln_b, v_w_branch, v_w_gate, v_b_gate, v_w_out, v_norm2_g, v_w_ff1, v_w_ff2, v_norm3_g, v_w_ple_gate, v_b_ple_gate, v_w_ple, v_final_g):
    args = dict(locals())
    weights = {n: args[n] for n in W_ORDER}
    moments_m = {n: args["m_" + n] for n in W_ORDER}
    moments_v = {n: args["v_" + n] for n in W_ORDER}
    c = lax.axis_index("c")
    sharded_names = BIG + SMALL

    early = ("w_in",) + SMALL
    shards = {n: (weights[n].astype(bf16) if n in BIG else weights[n]) for n in sharded_names}
    repl = {n: weights[n] for n in REPL}
    h = x[0]
    xn0, gathered = rms_fwd(h, norm1_g[0].reshape(1, D), "l0_rms1", gather=[shards[n] for n in early])
    Ws = [_prep_layer_weights(dict(zip(early, gathered)), repl, li) for li in range(DEPTH)]
    in_att, in_proj = ("w_ff1", "w_ff2", "w_ple_gate", "w_ple"), ("w_gate", "w_branch", "w_out")
    late = [(names, [shards[n] for n in names], Ws) for names in (in_att, in_proj)]

    saved = []
    for li in range(DEPTH):
        h, sv = _layer_fwd(h, p[li, 0], Ws[li], li, late if li == 0 else None, xn0 if li == 0 else None)
        saved.append(sv)
    loss_part, dh, dfinal = loss_head(h, final_g.reshape(1, D), loss_target[0], "loss_head")
    loss = lax.psum(loss_part[0, 0], ("x", "y", "c"))

    layer_grads = [None] * DEPTH
    for li in reversed(range(DEPTH)):
        dh, G = _layer_bwd(dh, saved[li], p[li, 0], Ws[li], li)
        layer_grads[li] = _layer_grads_to_ref(G)
    grad_x = dh[None]

    g0s, g1s = ([_split_chips(layer_grads[li][n], SHARDED[n][1]).astype(bf16) for n in BIG] for li in range(DEPTH))
    ras = sib_other_layer(g0s, g1s, "rs_sibling_layer")
    psums = [add_halves(a0, a1, r, c, "rs_add_" + n) for n, a0, a1, r in zip(BIG, g0s, g1s, ras)]
    rbs = a2a4(psums, "rs_all_to_all")
    chip = 2 * lax.axis_index("x") + lax.axis_index("y")
    reds = [reduce_chips(r, ps, c, chip, "rs_sum_" + n) for n, r, ps in zip(BIG, rbs, psums)]
    grads = dict(zip(BIG, ag2(reds, "rs_sibling_gather")))

    local = {n: jnp.stack([layer_grads[li][n] for li in range(DEPTH)]) for n in tuple(REPL)[:-1] + SMALL}
    local["final_g"] = dfinal.reshape(D)
    rnames = tuple(REPL) + SMALL
    rflat = jnp.concatenate([local[n].reshape(-1) for n in rnames])
    rflat = jnp.pad(rflat, (0, REPL_ROWS * PACK_W - rflat.shape[0])).reshape(REPL_ROWS, PACK_W)
    rall = ag8(rflat, "ar_gather").reshape(8, REPL_ROWS, PACK_W)
    rsum = sum_slots(rall, "ar_sum").reshape(-1)
    off = 0
    for n in rnames:
        shape = local[n].shape
        size = int(np.prod(shape))
        g = rsum[off:off + size].reshape(shape)
        off += size
        if n in SMALL:
            ax = SHARDED[n][1] + 1
            g = lax.dynamic_slice_in_dim(g, chip * (shape[ax] // 4), shape[ax] // 4, axis=ax)
        grads[n] = g

    deltas, new_m, new_v = {}, {}, {}
    for n in W_ORDER:
        deltas[n], new_m[n], new_v[n] = adamw(weights[n], grads[n], moments_m[n], moments_v[n], "adamw_" + n)
    return (loss, grad_x, *[grads[n] for n in W_ORDER], *[deltas[n] for n in W_ORDER], *[new_m[n] for n in W_ORDER],
            *[new_v[n] for n in W_ORDER])
```
